```python
import math
import jax
import jax.numpy as jnp
from jax import lax
import numpy as np

D_MODEL = 2048
BATCH = 4
SEQ = 2048
DEPTH = 1
DEC_BATCH = 16
DEC_SEQ = 16
PAST_LEN = 1024

CHUNK = 64
QBLK = 128
N_DIFF_HEADS = 8
DIFF_HEAD_DIM = 128
DIFF_VDIM = 2 * DIFF_HEAD_DIM
DIFF_WIDTH = N_DIFF_HEADS * 2 * DIFF_HEAD_DIM
N_GDN_HEADS = 16
GDN_DK = 128
GDN_DV = 128
GDN_QKWIDTH = N_GDN_HEADS * GDN_DK
GDN_VWIDTH = N_GDN_HEADS * GDN_DV
CONV_W = 4
CONV_CH = 2 * GDN_QKWIDTH + GDN_VWIDTH
N_MEM = 256
N_XHEADS = 4
XHEAD_DIM = 128
XWIDTH = N_XHEADS * XHEAD_DIM
D_FF = ((8 * D_MODEL // 3 + 255) // 256) * 256
ALPHA = (2.0 * DEPTH) ** 0.25
BETA = (8.0 * DEPTH) ** -0.25
LN_EPS = 1e-5
NORM_EPS = 1e-6
IN_SIZES = (DIFF_WIDTH, DIFF_WIDTH, N_DIFF_HEADS * DIFF_VDIM,
            GDN_QKWIDTH, GDN_QKWIDTH, GDN_VWIDTH, GDN_VWIDTH,
            N_GDN_HEADS, N_GDN_HEADS, D_MODEL, D_MODEL)
D_IN = sum(IN_SIZES)

kernel_name = "hybrid_streaming_encoder_step"


def layer_norm(x, g, b):
    xf = x.astype(jnp.float32)
    mu = jnp.mean(xf, -1, keepdims=True)
    var = jnp.mean(jnp.square(xf - mu), -1, keepdims=True)
    return ((xf - mu) * lax.rsqrt(var + LN_EPS) * g + b).astype(x.dtype)


def rms_norm(x, g):
    xf = x.astype(jnp.float32)
    return (xf * lax.rsqrt(jnp.mean(xf * xf, -1, keepdims=True) + NORM_EPS) * g).astype(x.dtype)


def l2norm(x):
    xf = x.astype(jnp.float32)
    return (xf * lax.rsqrt(jnp.sum(xf * xf, -1, keepdims=True) + NORM_EPS)).astype(x.dtype)


def alibi_slopes(n):
    return jnp.exp2(-8.0 * jnp.arange(1, n + 1, dtype=jnp.float32) / n)


def split_in(proj):
    return jnp.split(proj, np.cumsum(IN_SIZES)[:-1].tolist(), axis=-1)


def causal_conv(x, buf, w):
    T = x.shape[1]
    xp = jnp.concatenate([buf, x], axis=1)
    y = sum(w[j] * xp[:, j:j + T] for j in range(CONV_W))
    return y, xp[:, -(CONV_W - 1):]


def diff_attention_block(q, k, v, q_pos, k_pos, lam):
    s = jnp.einsum("bqhmd,bkhmd->bhmqk", q, k).astype(jnp.float32) * (DIFF_HEAD_DIM ** -0.5)
    dist = jnp.abs(q_pos[:, None] - k_pos[None, :]).astype(jnp.float32)
    bias = -alibi_slopes(N_DIFF_HEADS)[:, None, None, None] * dist
    allowed = (k_pos[None, :] // CHUNK) <= (q_pos[:, None] // CHUNK)
    p = jax.nn.softmax(jnp.where(allowed, s + bias, -jnp.inf), axis=-1)
    wts = p[:, :, 0] - lam * p[:, :, 1]
    return jnp.einsum("bhqk,bkhe->bqhe", wts.astype(v.dtype), v)


def diff_attention_prompt(q, k, v, lam):
    B, S = q.shape[0], q.shape[1]
    nb = S // QBLK
    k_pos = jnp.arange(S)
    qb = jnp.moveaxis(q.reshape(B, nb, QBLK, N_DIFF_HEADS, 2, DIFF_HEAD_DIM), 1, 0)

    def one_block(args):
        i, qi = args
        return diff_attention_block(qi, k, v, i * QBLK + jnp.arange(QBLK), k_pos, lam)

    o = lax.map(one_block, (jnp.arange(nb), qb))
    return jnp.moveaxis(o, 0, 1).reshape(B, S, N_DIFF_HEADS, DIFF_VDIM)


def gated_delta_chunked(q, k, v, g, beta, s0, c):
    f32 = jnp.float32
    B, T, H, DK = q.shape
    DV = v.shape[-1]
    n = T // c

    def chunks(a):
        return jnp.swapaxes(a.astype(f32).reshape(B, n, c, *a.shape[2:]), 2, 3)

    qc = chunks(q) * (DK ** -0.5)
    kc = chunks(k)
    vc = chunks(v)
    gc = jnp.cumsum(chunks(g), axis=-1)
    bc = chunks(beta)
    tri = jnp.tril(jnp.ones((c, c), dtype=bool))
    strict = jnp.tril(jnp.ones((c, c), dtype=bool), -1)
    decay = jnp.exp(jnp.where(tri, gc[..., :, None] - gc[..., None, :], -jnp.inf))
    kb = kc * bc[..., None]
    m = jnp.where(strict, jnp.einsum("bnhid,bnhjd->bnhij", kb, kc) * decay, 0.0)
    a = m + jnp.eye(c, dtype=f32)
    rhs = jnp.concatenate([vc * bc[..., None], kb * jnp.exp(gc)[..., None]], axis=-1)
    sol = lax.linalg.triangular_solve(a, rhs, left_side=True, lower=True, unit_diagonal=True)
    u, w = sol[..., :DV], sol[..., DV:]
    qk = jnp.where(tri, jnp.einsum("bnhid,bnhjd->bnhij", qc, kc) * decay, 0.0)

    def step(S, xs):
        q_i, k_i, u_i, w_i, qk_i, g_i = xs
        v_new = u_i - jnp.einsum("bhcd,bhde->bhce", w_i, S)
        o = (jnp.einsum("bhcd,bhde->bhce", q_i * jnp.exp(g_i)[..., None], S)
             + jnp.einsum("bhij,bhje->bhie", qk_i, v_new))
        g_last = g_i[..., -1]
        S = (S * jnp.exp(g_last)[..., None, None]
             + jnp.einsum("bhcd,bhce->bhde", k_i * jnp.exp(g_last[..., None] - g_i)[..., None], v_new))
        return S, o

    xs = tuple(jnp.swapaxes(t, 0, 1) for t in (qc, kc, u, w, qk, gc))
    s_fin, o = lax.scan(step, s0.astype(f32), xs)
    o = jnp.swapaxes(jnp.swapaxes(o, 0, 1), 2, 3).reshape(B, T, H, DV)
    return o.astype(v.dtype), s_fin.astype(s0.dtype)


def encoder_layer(x, mem_k, mem_v, past, lam_init,
                  w_in, conv_w, lam_q1, lam_k1, lam_q2, lam_k2, diff_subln_g,
                  gdn_a_log, gdn_dt_bias, gdn_norm_g, w_pa, w_pb, w_o, ln1_g, ln1_b,
                  w_xq, w_xo, ln2_g, ln2_b, w_ff1, w_ff3, w_ff2, ln3_g, ln3_b):
    B, T, _ = x.shape
    proj = jnp.einsum("btd,de->bte", x, w_in)
    dq, dk, dv, gq, gk, gv, gz, ga, gb, gate_a, gate_b = split_in(proj)

    qd = dq.reshape(B, T, N_DIFF_HEADS, 2, DIFF_HEAD_DIM)
    kd = dk.reshape(B, T, N_DIFF_HEADS, 2, DIFF_HEAD_DIM)
    vd = dv.reshape(B, T, N_DIFF_HEADS, DIFF_VDIM)
    f32 = jnp.float32
    lam = (jnp.exp(jnp.sum(lam_q1.astype(f32) * lam_k1.astype(f32)))
           - jnp.exp(jnp.sum(lam_q2.astype(f32) * lam_k2.astype(f32))) + lam_init)
    new_k = kd.reshape(B, T, N_DIFF_HEADS, 2 * DIFF_HEAD_DIM)
    if past is None:
        o_a = diff_attention_prompt(qd, kd, vd, lam)
        buf0 = jnp.zeros((B, CONV_W - 1, CONV_CH), proj.dtype)
        s0 = jnp.zeros((B, N_GDN_HEADS, GDN_DK, GDN_DV), x.dtype)
        c = CHUNK
    else:
        cache_k, cache_v, s0, buf0 = past
        P = cache_k.shape[1]
        k_all = jnp.concatenate([cache_k, new_k], axis=1).reshape(B, P + T, N_DIFF_HEADS, 2, DIFF_HEAD_DIM)
        v_all = jnp.concatenate([cache_v, vd], axis=1)
        o_a = diff_attention_block(qd, k_all, v_all, P + jnp.arange(T), jnp.arange(P + T), lam)
        c = T
    o_a = (rms_norm(o_a, diff_subln_g) * (1.0 - lam_init)).reshape(B, T, DIFF_WIDTH)

    qkv, new_buf = causal_conv(jnp.concatenate([gq, gk, gv], axis=-1), buf0, conv_w)
    qkv = jax.nn.silu(qkv)
    q_g, k_g, v_g = jnp.split(qkv, [GDN_QKWIDTH, 2 * GDN_QKWIDTH], axis=-1)
    q_g = l2norm(q_g.reshape(B, T, N_GDN_HEADS, GDN_DK))
    k_g = l2norm(k_g.reshape(B, T, N_GDN_HEADS, GDN_DK))
    v_g = v_g.reshape(B, T, N_GDN_HEADS, GDN_DV)
    g = -jnp.exp(gdn_a_log.astype(f32)) * jax.nn.softplus(ga.astype(f32) + gdn_dt_bias.astype(f32))
    beta = jax.nn.sigmoid(gb.astype(f32))
    o_b, s_new = gated_delta_chunked(q_g, k_g, v_g, g, beta, s0, c)
    o_b = (rms_norm(o_b, gdn_norm_g) * jax.nn.silu(gz.reshape(B, T, N_GDN_HEADS, GDN_DV))).reshape(B, T, GDN_VWIDTH)

    mix = (jax.nn.sigmoid(gate_a) * jnp.einsum("bte,ed->btd", o_a, w_pa)
           + jax.nn.sigmoid(gate_b) * jnp.einsum("bte,ed->btd", o_b, w_pb))
    h1 = layer_norm(ALPHA * x + jnp.einsum("btd,de->bte", mix, w_o), ln1_g, ln1_b)

    qx = jnp.einsum("btd,de->bte", h1, w_xq).reshape(B, T, N_XHEADS, XHEAD_DIM)
    sx = jnp.einsum("bthd,bmhd->bhtm", qx, mem_k).astype(f32) * (XHEAD_DIM ** -0.5)
    px = jax.nn.softmax(sx, axis=-1)
    ox = jnp.einsum("bhtm,bmhd->bthd", px.astype(mem_v.dtype), mem_v).reshape(B, T, XWIDTH)
    h2 = layer_norm(ALPHA * h1 + jnp.einsum("bte,ed->btd", ox, w_xo), ln2_g, ln2_b)

    f = jax.nn.silu(jnp.einsum("btd,df->btf", h2, w_ff1)) * jnp.einsum("btd,df->btf", h2, w_ff3)
    y = layer_norm(ALPHA * h2 + jnp.einsum("btf,fd->btd", f, w_ff2), ln3_g, ln3_b)
    return y, new_k, vd, s_new, new_buf


def setup_inputs(seed: int = 0) -> dict:
    key = jax.random.key(seed)
    ks = iter(jax.random.split(key, 64))
    L, D = DEPTH, D_MODEL

    def nrm(shape, s):
        return jax.random.normal(next(ks), shape, jnp.float32) * s

    def gain(n):
        return 1.0 + nrm((L, n), 0.02)

    x_prompt = nrm((BATCH, SEQ, D), 1.0)
    x_sample = nrm((DEC_BATCH, DEC_SEQ, D), 1.0)
    mem_prompt = nrm((BATCH, N_MEM, D), 1.0)
    cache_diff_k = nrm((L, DEC_BATCH, PAST_LEN, N_DIFF_HEADS, 2 * DIFF_HEAD_DIM), 1.0)
    cache_diff_v = nrm((L, DEC_BATCH, PAST_LEN, N_DIFF_HEADS, DIFF_VDIM), 0.5)
    state_gdn = nrm((L, DEC_BATCH, N_GDN_HEADS, GDN_DK, GDN_DV), 0.1)
    state_gdn_conv = nrm((L, DEC_BATCH, CONV_W - 1, CONV_CH), 1.0)
    cache_mem_k = nrm((L, DEC_BATCH, N_MEM, N_XHEADS, XHEAD_DIM), 1.0)
    cache_mem_v = nrm((L, DEC_BATCH, N_MEM, N_XHEADS, XHEAD_DIM), 0.5)

    s_in = D ** -0.5
    w_in = jnp.concatenate([
        nrm((L, D, DIFF_WIDTH), s_in),
        nrm((L, D, DIFF_WIDTH), s_in),
        nrm((L, D, N_DIFF_HEADS * DIFF_VDIM), s_in * BETA),
        nrm((L, D, GDN_QKWIDTH), s_in),
        nrm((L, D, GDN_QKWIDTH), s_in),
        nrm((L, D, GDN_VWIDTH), s_in * BETA),
        nrm((L, D, GDN_VWIDTH), s_in),
        nrm((L, D, N_GDN_HEADS), s_in),
        nrm((L, D, N_GDN_HEADS), s_in),
        nrm((L, D, D), s_in),
        nrm((L, D, D), s_in),
    ], axis=-1)
    conv_w = nrm((L, CONV_W, CONV_CH), CONV_W ** -0.5)
    lam_q1 = nrm((L, DIFF_HEAD_DIM), 0.1)
    lam_k1 = nrm((L, DIFF_HEAD_DIM), 0.1)
    lam_q2 = nrm((L, DIFF_HEAD_DIM), 0.1)
    lam_k2 = nrm((L, DIFF_HEAD_DIM), 0.1)
    diff_subln_g = gain(DIFF_VDIM)
    gdn_a_log = jnp.log(jax.random.uniform(next(ks), (L, N_GDN_HEADS), jnp.float32, 1.0, 16.0))
    dt = jnp.exp(jax.random.uniform(next(ks), (L, N_GDN_HEADS), jnp.float32,
                                    math.log(1e-3), math.log(1e-1)))
    gdn_dt_bias = dt + jnp.log(-jnp.expm1(-dt))
    gdn_norm_g = gain(GDN_DV)
    w_pa = nrm((L, DIFF_WIDTH, D), DIFF_WIDTH ** -0.5 * BETA)
    w_pb = nrm((L, GDN_VWIDTH, D), GDN_VWIDTH ** -0.5 * BETA)
    w_o = nrm((L, D, D), D ** -0.5 * BETA)
    ln1_g = gain(D)
    ln1_b = nrm((L, D), 0.02)
    w_xq = nrm((L, D, XWIDTH), s_in)
    w_xk = nrm((L, D, XWIDTH), s_in)
    w_xv = nrm((L, D, XWIDTH), s_in * BETA)
    w_xo = nrm((L, XWIDTH, D), XWIDTH ** -0.5 * BETA)
    ln2_g = gain(D)
    ln2_b = nrm((L, D), 0.02)
    w_ff1 = nrm((L, D, D_FF), s_in * BETA)
    w_ff3 = nrm((L, D, D_FF), s_in * BETA)
    w_ff2 = nrm((L, D_FF, D), D_FF ** -0.5 * BETA)
    ln3_g = gain(D)
    ln3_b = nrm((L, D), 0.02)
    return {
        "x_prompt": x_prompt, "x_sample": x_sample, "mem_prompt": mem_prompt,
        "cache_diff_k": cache_diff_k, "cache_diff_v": cache_diff_v,
        "state_gdn": state_gdn, "state_gdn_conv": state_gdn_conv,
        "cache_mem_k": cache_mem_k, "cache_mem_v": cache_mem_v,
        "w_in": w_in, "conv_w": conv_w,
        "lam_q1": lam_q1, "lam_k1": lam_k1, "lam_q2": lam_q2, "lam_k2": lam_k2,
        "diff_subln_g": diff_subln_g, "gdn_a_log": gdn_a_log, "gdn_dt_bias": gdn_dt_bias,
        "gdn_norm_g": gdn_norm_g, "w_pa": w_pa, "w_pb": w_pb, "w_o": w_o,
        "ln1_g": ln1_g, "ln1_b": ln1_b,
        "w_xq": w_xq, "w_xk": w_xk, "w_xv": w_xv, "w_xo": w_xo,
        "ln2_g": ln2_g, "ln2_b": ln2_b,
        "w_ff1": w_ff1, "w_ff3": w_ff3, "w_ff2": w_ff2,
        "ln3_g": ln3_g, "ln3_b": ln3_b,
    }


def reference(x_prompt, x_sample, mem_prompt, cache_diff_k, cache_diff_v, state_gdn, state_gdn_conv,
              cache_mem_k, cache_mem_v, w_in, conv_w, lam_q1, lam_k1, lam_q2, lam_k2, diff_subln_g,
              gdn_a_log, gdn_dt_bias, gdn_norm_g, w_pa, w_pb, w_o, ln1_g, ln1_b,
              w_xq, w_xk, w_xv, w_xo, ln2_g, ln2_b, w_ff1, w_ff3, w_ff2, ln3_g, ln3_b):
    yp, ys = x_prompt, x_sample
    Bp = x_prompt.shape[0]
    pk_l, pv_l, ps_l, pc_l, mk_l, mv_l = [], [], [], [], [], []
    sk_l, sv_l, ss_l, sc_l = [], [], [], []
    for l in range(DEPTH):
        lam_init = 0.8 - 0.6 * math.exp(-0.3 * l)
        lw = (w_in[l], conv_w[l], lam_q1[l], lam_k1[l], lam_q2[l], lam_k2[l], diff_subln_g[l],
              gdn_a_log[l], gdn_dt_bias[l], gdn_norm_g[l], w_pa[l], w_pb[l], w_o[l], ln1_g[l], ln1_b[l],
              w_xq[l], w_xo[l], ln2_g[l], ln2_b[l], w_ff1[l], w_ff3[l], w_ff2[l], ln3_g[l], ln3_b[l])
        mem_k = jnp.einsum("bmd,de->bme", mem_prompt, w_xk[l]).reshape(Bp, N_MEM, N_XHEADS, XHEAD_DIM)
        mem_v = jnp.einsum("bmd,de->bme", mem_prompt, w_xv[l]).reshape(Bp, N_MEM, N_XHEADS, XHEAD_DIM)
        yp, pk, pv, ps, pc = encoder_layer(yp, mem_k, mem_v, None, lam_init, *lw)
        past = (cache_diff_k[l], cache_diff_v[l], state_gdn[l], state_gdn_conv[l])
        ys, sk, sv, ss, sc = encoder_layer(ys, cache_mem_k[l], cache_mem_v[l], past, lam_init, *lw)
        pk_l.append(pk); pv_l.append(pv); ps_l.append(ps); pc_l.append(pc)
        mk_l.append(mem_k); mv_l.append(mem_v)
        sk_l.append(sk); sv_l.append(sv); ss_l.append(ss); sc_l.append(sc)
    return (yp, ys,
            jnp.stack(pk_l), jnp.stack(pv_l), jnp.stack(ps_l), jnp.stack(pc_l),
            jnp.stack(mk_l), jnp.stack(mv_l),
            jnp.stack(sk_l), jnp.stack(sv_l), jnp.stack(ss_l), jnp.stack(sc_l))
```

```python
import functools
import math

import jax
import jax.numpy as jnp
from jax import lax
from jax.experimental import pallas as pl
from jax.experimental.pallas import tpu as pltpu

D_MODEL = 2048
CHUNK = 64
N_DIFF_HEADS = 8
DIFF_HEAD_DIM = 128
DIFF_VDIM = 2 * DIFF_HEAD_DIM
DIFF_WIDTH = N_DIFF_HEADS * DIFF_VDIM
N_GDN_HEADS = 16
GDN_DK = 128
GDN_DV = 128
GDN_WIDTH = N_GDN_HEADS * GDN_DK
CONV_W = 4
N_MEM = 256
N_XHEADS = 4
XHEAD_DIM = 128
XWIDTH = N_XHEADS * XHEAD_DIM
D_FF = 5632
DEPTH = 1
ALPHA = (2.0 * DEPTH) ** 0.25
LN_EPS = 1e-5
NORM_EPS = 1e-6
LAM_INIT = 0.8 - 0.6 * math.exp(-0.3 * 0)

OFF_DQ = 0
OFF_DK = 2048
OFF_DV = 4096
OFF_GQKV = 6144
OFF_GZ = 12288
OFF_GAB = 14336
OFF_GATES = 14368

VMEM_LIMIT = 56 * 1024 * 1024
BF16 = jnp.bfloat16
F32 = jnp.float32
HI = lax.Precision.HIGHEST


def _cparams(sem):
    return pltpu.CompilerParams(dimension_semantics=sem, vmem_limit_bytes=VMEM_LIMIT)


def _sigmoid(x):
    return 1.0 / (1.0 + jnp.exp(-x))


def _silu(x):
    return x * _sigmoid(x)


def _layer_norm(x, g, b):
    mu = jnp.mean(x, axis=-1, keepdims=True)
    xc = x - mu
    var = jnp.mean(xc * xc, axis=-1, keepdims=True)
    return xc * lax.rsqrt(var + LN_EPS) * g + b


def _proj_kernel(x_ref, w_ref, *o_refs, act, scale):
    acc = jnp.dot(x_ref[...], w_ref[...], preferred_element_type=F32)
    if scale != 1.0:
        acc = acc * scale
    if act == "sigmoid":
        acc = _sigmoid(acc)
    elif act == "silu":
        acc = _silu(acc)
    for o in o_refs:
        o[...] = acc.astype(o.dtype)


def _proj(x, w, col_off, n_cols, out_dtypes, *, act=None, scale=1.0, tm=1024, tn=1024, name="proj"):
    M, K = x.shape
    tm = min(tm, M)
    tn = min(tn, n_cols)
    assert M % tm == 0 and n_cols % tn == 0 and col_off % tn == 0
    cb = col_off // tn
    outs = pl.pallas_call(
        functools.partial(_proj_kernel, act=act, scale=scale),
        out_shape=tuple(jax.ShapeDtypeStruct((M, n_cols), dt) for dt in out_dtypes),
        grid=(M // tm, n_cols // tn),
        in_specs=[pl.BlockSpec((tm, K), lambda i, j: (i, 0)),
                  pl.BlockSpec((K, tn), lambda i, j: (0, j + cb))],
        out_specs=tuple(pl.BlockSpec((tm, tn), lambda i, j: (i, j)) for _ in out_dtypes),
        compiler_params=_cparams(("parallel", "parallel")),
        name=name,
    )(x, w)
    return outs


def _lam_value(lq1, lk1, lq2, lk2):
    a = jnp.sum(lq1 * lk1, axis=-1, keepdims=True)
    b = jnp.sum(lq2 * lk2, axis=-1, keepdims=True)
    return jnp.exp(a) - jnp.exp(b) + LAM_INIT


def _subln(o, g):
    ms = jnp.mean(o * o, axis=-1, keepdims=True)
    return o * lax.rsqrt(ms + NORM_EPS) * g * (1.0 - LAM_INIT)


def _head_slope(h):
    e = (h + 1).astype(F32) * (-8.0 / N_DIFF_HEADS)
    return jnp.exp2(jnp.full((1, 1), e, F32))


def _nt_dot(a, b):
    return lax.dot_general(a, b, (((1,), (1,)), ((), ())), preferred_element_type=F32)


def _diff_prompt_kernel(lq1_ref, lk1_ref, lq2_ref, lk2_ref, g_ref, q_ref, k_ref, v_ref, o_ref,
                        m_ref, l_ref, acc_ref, *, tq):
    h = pl.program_id(1)
    qi = pl.program_id(2)
    slope = _head_slope(h)
    lam = _lam_value(lq1_ref[...], lk1_ref[...], lq2_ref[...], lk2_ref[...])

    q = q_ref[0]
    qs = (q[:, :DIFF_HEAD_DIM], q[:, DIFF_HEAD_DIM:])
    rows = lax.broadcasted_iota(jnp.int32, (tq, tq), 0)
    cols = lax.broadcasted_iota(jnp.int32, (tq, tq), 1)
    rel = (rows - cols).astype(F32)
    bias_off = -slope * rel

    m_ref[...] = jnp.full(m_ref.shape, -jnp.inf, F32)
    l_ref[...] = jnp.zeros(l_ref.shape, F32)
    acc_ref[...] = jnp.zeros(acc_ref.shape, F32)

    def update(kblk, vblk, bias):
        for m in range(2):
            s = _nt_dot(qs[m], kblk[:, m * DIFF_HEAD_DIM:(m + 1) * DIFF_HEAD_DIM]) + bias
            m_old = m_ref[m]
            m_new = jnp.maximum(m_old, jnp.max(s, axis=-1, keepdims=True))
            a = jnp.exp(m_old - m_new)
            p = jnp.exp(s - m_new)
            l_ref[m] = a * l_ref[m] + jnp.sum(p, axis=-1, keepdims=True)
            acc_ref[m] = a * acc_ref[m] + jnp.dot(p.astype(BF16), vblk, preferred_element_type=F32)
            m_ref[m] = m_new

    def body(j, carry):
        start = pl.multiple_of(j * tq, tq)
        kblk = k_ref[0, pl.ds(start, tq), :]
        vblk = v_ref[0, pl.ds(start, tq), :]
        gap = ((qi - j) * tq).astype(F32)
        update(kblk, vblk, bias_off - slope * gap)
        return carry

    lax.fori_loop(0, qi, body, 0)

    start = pl.multiple_of(qi * tq, tq)
    kblk = k_ref[0, pl.ds(start, tq), :]
    vblk = v_ref[0, pl.ds(start, tq), :]
    allowed = (cols // CHUNK) <= (rows // CHUNK)
    bias_diag = jnp.where(allowed, -slope * jnp.abs(rel), -jnp.inf)
    update(kblk, vblk, bias_diag)

    o = acc_ref[0] / l_ref[0] - lam * (acc_ref[1] / l_ref[1])
    o_ref[0] = _subln(o, g_ref[...]).astype(o_ref.dtype)


def _diff_attention_prompt(q, k, v, lam_rows, subln_g, *, tq=256):
    B, T, _ = q.shape
    tq = min(tq, T)
    assert T % tq == 0 and tq % CHUNK == 0
    vec = pl.BlockSpec((1, DIFF_HEAD_DIM), lambda b, h, i: (0, 0))
    return pl.pallas_call(
        functools.partial(_diff_prompt_kernel, tq=tq),
        out_shape=jax.ShapeDtypeStruct((B, T, DIFF_WIDTH), BF16),
        grid=(B, N_DIFF_HEADS, T // tq),
        in_specs=[vec, vec, vec, vec,
                  pl.BlockSpec((1, DIFF_VDIM), lambda b, h, i: (0, 0)),
                  pl.BlockSpec((1, tq, DIFF_VDIM), lambda b, h, i: (b, i, h)),
                  pl.BlockSpec((1, T, DIFF_VDIM), lambda b, h, i: (b, 0, h)),
                  pl.BlockSpec((1, T, DIFF_VDIM), lambda b, h, i: (b, 0, h))],
        out_specs=pl.BlockSpec((1, tq, DIFF_VDIM), lambda b, h, i: (b, i, h)),
        scratch_shapes=[pltpu.VMEM((2, tq, 1), F32), pltpu.VMEM((2, tq, 1), F32),
                        pltpu.VMEM((2, tq, DIFF_VDIM), F32)],
        compiler_params=_cparams(("parallel", "parallel", "parallel")),
        name="diff_attn_prompt",
    )(*lam_rows, subln_g, q, k, v)


def _diff_sample_kernel(lq1_ref, lk1_ref, lq2_ref, lk2_ref, g_ref, q_ref, kc_ref, vc_ref, kn_ref, vn_ref,
                        o_ref, *, past):
    h = pl.program_id(1)
    slope = _head_slope(h)
    lam = _lam_value(lq1_ref[...], lk1_ref[...], lq2_ref[...], lk2_ref[...])
    q = q_ref[0]
    T = q.shape[0]
    kc = kc_ref[0].astype(BF16)
    vc = vc_ref[0].astype(BF16)
    kn = kn_ref[0]
    vn = vn_ref[0]

    def bias(n_keys, key0):
        qpos = past + lax.broadcasted_iota(jnp.int32, (T, n_keys), 0)
        kpos = key0 + lax.broadcasted_iota(jnp.int32, (T, n_keys), 1)
        allowed = (kpos // CHUNK) <= (qpos // CHUNK)
        dist = jnp.abs(qpos - kpos).astype(F32)
        return jnp.where(allowed, -slope * dist, -jnp.inf)

    bias_c = bias(past, 0)
    bias_n = bias(T, past)
    wc, wn = [], []
    for m in range(2):
        sl = slice(m * DIFF_HEAD_DIM, (m + 1) * DIFF_HEAD_DIM)
        sc = _nt_dot(q[:, sl], kc[:, sl]) + bias_c
        sn = _nt_dot(q[:, sl], kn[:, sl]) + bias_n
        mx = jnp.maximum(jnp.max(sc, axis=-1, keepdims=True), jnp.max(sn, axis=-1, keepdims=True))
        pc = jnp.exp(sc - mx)
        pn = jnp.exp(sn - mx)
        inv = 1.0 / (jnp.sum(pc, axis=-1, keepdims=True) + jnp.sum(pn, axis=-1, keepdims=True))
        wc.append(pc * inv)
        wn.append(pn * inv)
    w_c = (wc[0] - lam * wc[1]).astype(BF16)
    w_n = (wn[0] - lam * wn[1]).astype(BF16)
    o = (jnp.dot(w_c, vc, preferred_element_type=F32) + jnp.dot(w_n, vn, preferred_element_type=F32))
    o_ref[0] = _subln(o, g_ref[...]).astype(o_ref.dtype)


def _diff_attention_sample(q, cache_k, cache_v, k_new, v_new, lam_rows, subln_g):
    B, T, _ = q.shape
    P = cache_k.shape[1]
    vec = pl.BlockSpec((1, DIFF_HEAD_DIM), lambda b, h: (0, 0))
    new = pl.BlockSpec((1, T, DIFF_VDIM), lambda b, h: (b, 0, h))
    old = pl.BlockSpec((1, P, DIFF_VDIM), lambda b, h: (b, 0, h))
    return pl.pallas_call(
        functools.partial(_diff_sample_kernel, past=P),
        out_shape=jax.ShapeDtypeStruct((B, T, DIFF_WIDTH), BF16),
        grid=(B, N_DIFF_HEADS),
        in_specs=[vec, vec, vec, vec, pl.BlockSpec((1, DIFF_VDIM), lambda b, h: (0, 0)),
                  new, old, old, new, new],
        out_specs=new,
        compiler_params=_cparams(("parallel", "parallel")),
        name="diff_attn_sample",
    )(*lam_rows, subln_g, q, cache_k, cache_v, k_new, v_new)


def _gdn_kernel(*refs, c, G, has_state):
    if has_state:
        (xq_ref, xk_ref, xv_ref, bq_ref, bk_ref, bv_ref, cwq_ref, cwk_ref, cwv_ref, ab_ref, alog_ref, dtb_ref,
         z_ref, ng_ref, s0_ref, o_ref, s_ref, pad_ref) = refs
    else:
        (xq_ref, xk_ref, xv_ref, bq_ref, bk_ref, bv_ref, cwq_ref, cwk_ref, cwv_ref, ab_ref, alog_ref, dtb_ref,
         z_ref, ng_ref, o_ref, s_ref, pad_ref) = refs
        s0_ref = None
    hg = pl.program_id(1)
    n = pl.program_id(2)
    H = N_GDN_HEADS
    PADR = 8

    @pl.when(n == 0)
    def _init():
        for t, b_ref in enumerate((bq_ref, bk_ref, bv_ref)):
            pad_ref[t, PADR - (CONV_W - 1):PADR, :] = b_ref[0]
        if has_state:
            s_ref[0] = s0_ref[0]
        else:
            s_ref[...] = jnp.zeros(s_ref.shape, F32)

    conv = []
    for t, (x_ref, cw_ref) in enumerate(((xq_ref, cwq_ref), (xk_ref, cwk_ref), (xv_ref, cwv_ref))):
        x = x_ref[0]
        pad_ref[t, PADR:PADR + c, :] = x
        y = cw_ref[CONV_W - 1:CONV_W, :] * x
        for j in range(CONV_W - 1):
            lo = PADR - (CONV_W - 1) + j
            y = y + cw_ref[j:j + 1, :] * pad_ref[t, lo:lo + c, :]
        pad_ref[t, 0:PADR, :] = x[c - PADR:c, :]
        conv.append(_silu(y))

    ab = ab_ref[0]
    a_in = ab[:, 0:H] + dtb_ref[...]
    softplus = jnp.maximum(a_in, 0.0) + jnp.log1p(jnp.exp(-jnp.abs(a_in)))
    g_col = -jnp.exp(alog_ref[...]) * softplus
    beta_col = _sigmoid(ab[:, H:2 * H])
    r = lax.broadcasted_iota(jnp.int32, (c, c), 0)
    s = lax.broadcasted_iota(jnp.int32, (c, c), 1)
    tri = r >= s
    strict = r > s
    tri_f = tri.astype(F32)
    eye_c = (r == s).astype(F32)
    gc_col = jnp.dot(tri_f, g_col, preferred_element_type=F32, precision=HI)
    eye_h = (lax.broadcasted_iota(jnp.int32, (H, H), 0) == lax.broadcasted_iota(jnp.int32, (H, H), 1)).astype(F32)
    gc_row = lax.dot_general(eye_h, gc_col, (((1,), (1,)), ((), ())), preferred_element_type=F32,
                             precision=HI)

    for hh in range(G):
        lanes = slice(hh * GDN_DK, (hh + 1) * GDN_DK)
        head = hg * G + hh
        sel_c = (lax.broadcasted_iota(jnp.int32, (c, H), 1) == head).astype(F32)
        sel_r = (lax.broadcasted_iota(jnp.int32, (H, c), 0) == head).astype(F32)
        gcc = jnp.sum(gc_col * sel_c, axis=1, keepdims=True)
        bet = jnp.sum(beta_col * sel_c, axis=1, keepdims=True)
        gcr = jnp.sum(gc_row * sel_r, axis=0, keepdims=True)
        g_last = gcc[c - 1:c, :]

        qh = conv[0][:, lanes]
        kh = conv[1][:, lanes]
        vh = conv[2][:, lanes]
        qh = qh * lax.rsqrt(jnp.sum(qh * qh, axis=-1, keepdims=True) + NORM_EPS) * (GDN_DK ** -0.5)
        kh = kh * lax.rsqrt(jnp.sum(kh * kh, axis=-1, keepdims=True) + NORM_EPS)

        decay = jnp.exp(jnp.where(tri, gcc - gcr, -jnp.inf))
        kb = kh * bet
        k16 = kh.astype(BF16)
        mmat = jnp.where(strict, _nt_dot(kb.astype(BF16), k16) * decay, 0.0)
        pw = -mmat
        tinv = eye_c + pw
        levels = int(math.log2(c)) - 1
        for lvl in range(levels):
            pw = jnp.dot(pw, pw, preferred_element_type=F32, precision=HI)
            tinv = tinv + jnp.dot(pw, tinv, preferred_element_type=F32, precision=HI)
        rhs = jnp.concatenate([vh * bet, kb * jnp.exp(gcc)], axis=1)
        sol = jnp.dot(tinv, rhs, preferred_element_type=F32, precision=HI)
        u = sol[:, :GDN_DV]
        w = sol[:, GDN_DV:]
        qk = jnp.where(tri, _nt_dot(qh.astype(BF16), k16) * decay, 0.0)

        S = s_ref[0, hh]
        S16 = S.astype(BF16)
        v_new = u - jnp.dot(w.astype(BF16), S16, preferred_element_type=F32)
        v16 = v_new.astype(BF16)
        o = (jnp.dot((qh * jnp.exp(gcc)).astype(BF16), S16, preferred_element_type=F32)
             + jnp.dot(qk.astype(BF16), v16, preferred_element_type=F32))
        kd = (kh * jnp.exp(g_last - gcc)).astype(BF16)
        s_ref[0, hh] = S * jnp.exp(g_last) + lax.dot_general(kd, v16, (((0,), (0,)), ((), ())),
                                                             preferred_element_type=F32)

        ms = jnp.mean(o * o, axis=-1, keepdims=True)
        o = o * lax.rsqrt(ms + NORM_EPS) * ng_ref[...] * z_ref[0][:, lanes].astype(F32)
        o_ref[0, :, lanes] = o.astype(o_ref.dtype)


def _gdn(gqkv, buf0, conv_w, gab, a_log, dt_bias, z_silu, norm_g, s0, *, c, G=4):
    B, T, _ = gqkv.shape
    H = N_GDN_HEADS
    assert T % c == 0 and c % 8 == 0 and (c & (c - 1)) == 0 and H % G == 0
    nG = H // G
    W = G * GDN_DK

    def stream(t):
        return pl.BlockSpec((1, c, W), lambda b, g, n, t=t: (b, n, t * nG + g))

    def hist(t):
        return pl.BlockSpec((1, CONV_W - 1, W), lambda b, g, n, t=t: (b, 0, t * nG + g))

    def cw(t):
        return pl.BlockSpec((CONV_W, W), lambda b, g, n, t=t: (0, t * nG + g))

    small = pl.BlockSpec((1, H), lambda b, g, n: (0, 0))
    state = pl.BlockSpec((1, G, GDN_DK, GDN_DV), lambda b, g, n: (b, g, 0, 0))
    in_specs = [stream(0), stream(1), stream(2), hist(0), hist(1), hist(2), cw(0), cw(1), cw(2),
                pl.BlockSpec((1, c, 128), lambda b, g, n: (b, n, 0)), small, small,
                pl.BlockSpec((1, c, W), lambda b, g, n: (b, n, g)),
                pl.BlockSpec((1, GDN_DV), lambda b, g, n: (0, 0))]
    args = [gqkv, gqkv, gqkv, buf0, buf0, buf0, conv_w, conv_w, conv_w, gab, a_log, dt_bias, z_silu, norm_g]
    if s0 is not None:
        in_specs.append(state)
        args.append(s0)
    return pl.pallas_call(
        functools.partial(_gdn_kernel, c=c, G=G, has_state=s0 is not None),
        out_shape=(jax.ShapeDtypeStruct((B, T, GDN_WIDTH), BF16),
                   jax.ShapeDtypeStruct((B, H, GDN_DK, GDN_DV), F32)),
        grid=(B, nG, T // c),
        in_specs=in_specs,
        out_specs=(pl.BlockSpec((1, c, W), lambda b, g, n: (b, n, g)), state),
        scratch_shapes=[pltpu.VMEM((3, 8 + c, W), F32)],
        compiler_params=_cparams(("parallel", "parallel", "arbitrary")),
        name="gdn",
    )(*args)


def _mix_kernel(oa_ref, ob_ref, wa_ref, wb_ref, sa_ref, sb_ref, o_ref):
    a = jnp.dot(oa_ref[...], wa_ref[...], preferred_element_type=F32)
    b = jnp.dot(ob_ref[...], wb_ref[...], preferred_element_type=F32)
    o_ref[...] = (sa_ref[...].astype(F32) * a + sb_ref[...].astype(F32) * b).astype(o_ref.dtype)


def _mix(o_a, o_b, w_pa, w_pb, gates, *, tm=512, tn=1024):
    M = o_a.shape[0]
    tm = min(tm, M)
    nb = D_MODEL // tn
    return pl.pallas_call(
        _mix_kernel,
        out_shape=jax.ShapeDtypeStruct((M, D_MODEL), BF16),
        grid=(M // tm, nb),
        in_specs=[pl.BlockSpec((tm, DIFF_WIDTH), lambda i, j: (i, 0)),
                  pl.BlockSpec((tm, GDN_WIDTH), lambda i, j: (i, 0)),
                  pl.BlockSpec((DIFF_WIDTH, tn), lambda i, j: (0, j)),
                  pl.BlockSpec((GDN_WIDTH, tn), lambda i, j: (0, j)),
                  pl.BlockSpec((tm, tn), lambda i, j: (i, j)),
                  pl.BlockSpec((tm, tn), lambda i, j: (i, j + nb))],
        out_specs=pl.BlockSpec((tm, tn), lambda i, j: (i, j)),
        compiler_params=_cparams(("parallel", "parallel")),
        name="mix",
    )(o_a, o_b, w_pa, w_pb, gates, gates)


def _post_kernel(x_ref, mix_ref, wo_ref, g1_ref, b1_ref, wxq_ref, mk_ref, mv_ref, wxo_ref, g2_ref, b2_ref,
                 h2_ref, h2b_ref):
    h1 = ALPHA * x_ref[0] + jnp.dot(mix_ref[0], wo_ref[...], preferred_element_type=F32)
    h1 = _layer_norm(h1, g1_ref[...], b1_ref[...])
    qx = jnp.dot(h1.astype(BF16), wxq_ref[...], preferred_element_type=F32) * (XHEAD_DIM ** -0.5)
    qx = qx.astype(BF16)
    mk = mk_ref[0]
    mv = mv_ref[0]
    heads = []
    for hh in range(N_XHEADS):
        sl = slice(hh * XHEAD_DIM, (hh + 1) * XHEAD_DIM)
        s = _nt_dot(qx[:, sl], mk[:, sl])
        p = jnp.exp(s - jnp.max(s, axis=-1, keepdims=True))
        p = p / jnp.sum(p, axis=-1, keepdims=True)
        heads.append(jnp.dot(p.astype(BF16), mv[:, sl], preferred_element_type=F32))
    ox = jnp.concatenate(heads, axis=1).astype(BF16)
    h2 = ALPHA * h1 + jnp.dot(ox, wxo_ref[...], preferred_element_type=F32)
    h2 = _layer_norm(h2, g2_ref[...], b2_ref[...])
    h2_ref[0] = h2
    h2b_ref[0] = h2.astype(BF16)


def _post(x, mix, w_o, ln1_g, ln1_b, w_xq, mem_k, mem_v, w_xo, ln2_g, ln2_b, *, tm=256):
    B, T, D = x.shape
    tm = min(tm, T)
    const = lambda shape: pl.BlockSpec(shape, lambda b, i: (0, 0))
    rows = lambda: pl.BlockSpec((1, tm, D), lambda b, i: (b, i, 0))
    mem = lambda: pl.BlockSpec((1, N_MEM, XWIDTH), lambda b, i: (b, 0, 0))
    return pl.pallas_call(
        _post_kernel,
        out_shape=(jax.ShapeDtypeStruct((B, T, D), F32), jax.ShapeDtypeStruct((B, T, D), BF16)),
        grid=(B, T // tm),
        in_specs=[rows(), rows(), const((D, D)), const((1, D)), const((1, D)), const((D, XWIDTH)),
                  mem(), mem(), const((XWIDTH, D)), const((1, D)), const((1, D))],
        out_specs=(rows(), rows()),
        compiler_params=_cparams(("parallel", "parallel")),
        name="post_attn",
    )(x, mix, w_o, ln1_g, ln1_b, w_xq, mem_k, mem_v, w_xo, ln2_g, ln2_b)


def _ffn_kernel(hb_ref, h_ref, w1_ref, w3_ref, w2_ref, g_ref, b_ref, y_ref, acc_ref):
    f = pl.program_id(1)

    @pl.when(f == 0)
    def _():
        acc_ref[...] = jnp.zeros(acc_ref.shape, F32)

    hb = hb_ref[...]
    a = jnp.dot(hb, w1_ref[...], preferred_element_type=F32)
    b = jnp.dot(hb, w3_ref[...], preferred_element_type=F32)
    act = (_silu(a) * b).astype(BF16)
    acc_ref[...] += jnp.dot(act, w2_ref[...], preferred_element_type=F32)

    @pl.when(f == pl.num_programs(1) - 1)
    def _():
        y_ref[...] = _layer_norm(ALPHA * h_ref[...] + acc_ref[...], g_ref[...], b_ref[...])


def _ffn(h2b, h2, w1, w3, w2, ln_g, ln_b, *, tm=512, tf=512):
    M, D = h2.shape
    tm = min(tm, M)
    assert M % tm == 0 and D_FF % tf == 0
    return pl.pallas_call(
        _ffn_kernel,
        out_shape=jax.ShapeDtypeStruct((M, D), F32),
        grid=(M // tm, D_FF // tf),
        in_specs=[pl.BlockSpec((tm, D), lambda i, f: (i, 0)),
                  pl.BlockSpec((tm, D), lambda i, f: (i, 0)),
                  pl.BlockSpec((D, tf), lambda i, f: (0, f)),
                  pl.BlockSpec((D, tf), lambda i, f: (0, f)),
                  pl.BlockSpec((tf, D), lambda i, f: (f, 0)),
                  pl.BlockSpec((1, D), lambda i, f: (0, 0)),
                  pl.BlockSpec((1, D), lambda i, f: (0, 0))],
        out_specs=pl.BlockSpec((tm, D), lambda i, f: (i, 0)),
        scratch_shapes=[pltpu.VMEM((tm, D), F32)],
        compiler_params=_cparams(("parallel", "arbitrary")),
        name="ffn",
    )(h2b, h2, w1, w3, w2, ln_g, ln_b)


def _encoder_layer(x, mem_k, mem_v, past, W):
    B, T, D = x.shape
    M = B * T
    xb = x.reshape(M, D).astype(BF16)
    w_in = W["w_in"]

    (dq,) = _proj(xb, w_in, OFF_DQ, DIFF_WIDTH, (BF16,), scale=DIFF_HEAD_DIM ** -0.5, name="proj_dq")
    dk, dkb = _proj(xb, w_in, OFF_DK, DIFF_WIDTH, (F32, BF16), name="proj_dk")
    dv, dvb = _proj(xb, w_in, OFF_DV, DIFF_WIDTH, (F32, BF16), name="proj_dv")
    (gqkv,) = _proj(xb, w_in, OFF_GQKV, 3 * GDN_WIDTH, (F32,), name="proj_gqkv")
    (gz,) = _proj(xb, w_in, OFF_GZ, GDN_WIDTH, (BF16,), act="silu", name="proj_gz")
    (gab,) = _proj(xb, w_in, OFF_GAB, 128, (F32,), tn=128, name="proj_gab")
    (gates,) = _proj(xb, W["w_gates"], 0, 2 * D_MODEL, (BF16,), act="sigmoid", name="proj_gates")

    dq = dq.reshape(B, T, DIFF_WIDTH)
    dkb = dkb.reshape(B, T, DIFF_WIDTH)
    dvb = dvb.reshape(B, T, DIFF_WIDTH)
    gqkv = gqkv.reshape(B, T, 3 * GDN_WIDTH)
    lam_rows = W["lam_rows"]
    if past is None:
        o_a = _diff_attention_prompt(dq, dkb, dvb, lam_rows, W["diff_subln_g"])
        buf0 = jnp.zeros((B, CONV_W - 1, 3 * GDN_WIDTH), F32)
        s0 = None
        c = CHUNK
    else:
        cache_k, cache_v, s0, buf0 = past
        P = cache_k.shape[1]
        o_a = _diff_attention_sample(dq, cache_k.reshape(B, P, DIFF_WIDTH), cache_v.reshape(B, P, DIFF_WIDTH),
                                     dkb, dvb, lam_rows, W["diff_subln_g"])
        c = T
    o_b, s_new = _gdn(gqkv, buf0, W["conv_w"], gab.reshape(B, T, 128), W["gdn_a_log"], W["gdn_dt_bias"],
                      gz.reshape(B, T, GDN_WIDTH), W["gdn_norm_g"], s0, c=c)
    new_buf = gqkv[:, T - (CONV_W - 1):, :]

    mix = _mix(o_a.reshape(M, DIFF_WIDTH), o_b.reshape(M, GDN_WIDTH), W["w_pa"], W["w_pb"], gates)
    h2, h2b = _post(x, mix.reshape(B, T, D), W["w_o"], W["ln1_g"], W["ln1_b"], W["w_xq"], mem_k, mem_v,
                    W["w_xo"], W["ln2_g"], W["ln2_b"])
    y = _ffn(h2b.reshape(M, D), h2.reshape(M, D), W["w_ff1"], W["w_ff3"], W["w_ff2"], W["ln3_g"], W["ln3_b"])
    return (y.reshape(B, T, D), dk.reshape(B, T, N_DIFF_HEADS, DIFF_VDIM), dv.reshape(B, T, N_DIFF_HEADS, DIFF_VDIM),
            s_new, new_buf)


def kernel(x_prompt, x_sample, mem_prompt, cache_diff_k, cache_diff_v, state_gdn, state_gdn_conv, cache_mem_k, cache_mem_v, w_in, conv_w, lam_q1, lam_k1, lam_q2, lam_k2, diff_subln_g, gdn_a_log, gdn_dt_bias, gdn_norm_g, w_pa, w_pb, w_o, ln1_g, ln1_b, w_xq, w_xk, w_xv, w_xo, ln2_g, ln2_b, w_ff1, w_ff3, w_ff2, ln3_g, ln3_b):
    l = 0
    w_in_b = w_in[l].astype(BF16)
    W = {
        "w_in": w_in_b,
        "w_gates": w_in_b[:, OFF_GATES:],
        "conv_w": conv_w[l],
        "lam_rows": tuple(v[l].reshape(1, DIFF_HEAD_DIM) for v in (lam_q1, lam_k1, lam_q2, lam_k2)),
        "diff_subln_g": diff_subln_g[l].reshape(1, DIFF_VDIM),
        "gdn_a_log": gdn_a_log[l].reshape(1, N_GDN_HEADS),
        "gdn_dt_bias": gdn_dt_bias[l].reshape(1, N_GDN_HEADS),
        "gdn_norm_g": gdn_norm_g[l].reshape(1, GDN_DV),
        "w_pa": w_pa[l].astype(BF16), "w_pb": w_pb[l].astype(BF16), "w_o": w_o[l].astype(BF16),
        "ln1_g": ln1_g[l].reshape(1, D_MODEL), "ln1_b": ln1_b[l].reshape(1, D_MODEL),
        "w_xq": w_xq[l].astype(BF16), "w_xo": w_xo[l].astype(BF16),
        "ln2_g": ln2_g[l].reshape(1, D_MODEL), "ln2_b": ln2_b[l].reshape(1, D_MODEL),
        "w_ff1": w_ff1[l].astype(BF16), "w_ff3": w_ff3[l].astype(BF16), "w_ff2": w_ff2[l].astype(BF16),
        "ln3_g": ln3_g[l].reshape(1, D_MODEL), "ln3_b": ln3_b[l].reshape(1, D_MODEL),
    }
    Bp = x_prompt.shape[0]
    memb = mem_prompt.reshape(Bp * N_MEM, D_MODEL).astype(BF16)
    mem_k, mem_kb = _proj(memb, w_xk[l].astype(BF16), 0, XWIDTH, (F32, BF16), tn=XWIDTH, name="proj_mem_k")
    mem_v, mem_vb = _proj(memb, w_xv[l].astype(BF16), 0, XWIDTH, (F32, BF16), tn=XWIDTH, name="proj_mem_v")

    yp, pk, pv, ps, pc = _encoder_layer(x_prompt, mem_kb.reshape(Bp, N_MEM, XWIDTH),
                                        mem_vb.reshape(Bp, N_MEM, XWIDTH), None, W)
    Bs = x_sample.shape[0]
    past = (cache_diff_k[l], cache_diff_v[l], state_gdn[l], state_gdn_conv[l])
    ys, sk, sv, ss, sc = _encoder_layer(x_sample, cache_mem_k[l].reshape(Bs, N_MEM, XWIDTH).astype(BF16),
                                        cache_mem_v[l].reshape(Bs, N_MEM, XWIDTH).astype(BF16), past, W)
    st = lambda a: a[None]
    return (yp, ys, st(pk), st(pv), st(ps), st(pc),
            st(mem_k.reshape(Bp, N_MEM, N_XHEADS, XHEAD_DIM)), st(mem_v.reshape(Bp, N_MEM, N_XHEADS, XHEAD_DIM)),
            st(sk), st(sv), st(ss), st(sc))
```

```python
import functools
import math

import jax
import jax.numpy as jnp
from jax import lax
from jax.experimental import pallas as pl
from jax.experimental.pallas import tpu as pltpu

D_MODEL = 2048
CHUNK = 64
N_DIFF_HEADS = 8
DIFF_HEAD_DIM = 128
DIFF_VDIM = 2 * DIFF_HEAD_DIM
DIFF_WIDTH = N_DIFF_HEADS * DIFF_VDIM
N_GDN_HEADS = 16
GDN_DK = 128
GDN_DV = 128
GDN_WIDTH = N_GDN_HEADS * GDN_DK
CONV_W = 4
N_MEM = 256
N_XHEADS = 4
XHEAD_DIM = 128
XWIDTH = N_XHEADS * XHEAD_DIM
D_FF = 5632
DEPTH = 1
ALPHA = (2.0 * DEPTH) ** 0.25
LN_EPS = 1e-5
NORM_EPS = 1e-6
LAM_INIT = 0.8 - 0.6 * math.exp(-0.3 * 0)

OFF_DQ = 0
OFF_DK = 2048
OFF_DV = 4096
OFF_GQKV = 6144
OFF_GZ = 12288
OFF_GAB = 14336
OFF_GATES = 14368

VMEM_LIMIT = 56 * 1024 * 1024
BF16 = jnp.bfloat16
F32 = jnp.float32
HI = lax.Precision.HIGHEST


def _cparams(sem):
    return pltpu.CompilerParams(dimension_semantics=sem, vmem_limit_bytes=VMEM_LIMIT)


def _sigmoid(x):
    return 1.0 / (1.0 + jnp.exp(-x))


def _silu(x):
    return x * _sigmoid(x)


def _layer_norm(x, g, b):
    mu = jnp.mean(x, axis=-1, keepdims=True)
    xc = x - mu
    var = jnp.mean(xc * xc, axis=-1, keepdims=True)
    return xc * lax.rsqrt(var + LN_EPS) * g + b


def _proj_kernel(x_ref, w_ref, *o_refs, act, scale):
    acc = jnp.dot(x_ref[...], w_ref[...], preferred_element_type=F32)
    if scale != 1.0:
        acc = acc * scale
    if act == "sigmoid":
        acc = _sigmoid(acc)
    elif act == "silu":
        acc = _silu(acc)
    for o in o_refs:
        o[...] = acc.astype(o.dtype)


def _proj(x, w, col_off, n_cols, out_dtypes, *, act=None, scale=1.0, tm=1024, tn=1024, name="proj"):
    M, K = x.shape
    tm = min(tm, M)
    tn = min(tn, n_cols)
    assert M % tm == 0 and n_cols % tn == 0 and col_off % tn == 0
    cb = col_off // tn
    outs = pl.pallas_call(
        functools.partial(_proj_kernel, act=act, scale=scale),
        out_shape=tuple(jax.ShapeDtypeStruct((M, n_cols), dt) for dt in out_dtypes),
        grid=(M // tm, n_cols // tn),
        in_specs=[pl.BlockSpec((tm, K), lambda i, j: (i, 0)),
                  pl.BlockSpec((K, tn), lambda i, j: (0, j + cb))],
        out_specs=tuple(pl.BlockSpec((tm, tn), lambda i, j: (i, j)) for _ in out_dtypes),
        compiler_params=_cparams(("parallel", "parallel")),
        name=name,
    )(x, w)
    return outs


def _lam_value(lq1, lk1, lq2, lk2):
    a = jnp.sum(lq1 * lk1, axis=-1, keepdims=True)
    b = jnp.sum(lq2 * lk2, axis=-1, keepdims=True)
    return jnp.exp(a) - jnp.exp(b) + LAM_INIT


def _subln(o, g):
    ms = jnp.mean(o * o, axis=-1, keepdims=True)
    return o * lax.rsqrt(ms + NORM_EPS) * g * (1.0 - LAM_INIT)


def _head_slope(h):
    e = (h + 1).astype(F32) * (-8.0 / N_DIFF_HEADS)
    return jnp.exp2(jnp.full((1, 1), e, F32))


def _nt_dot(a, b):
    return lax.dot_general(a, b, (((1,), (1,)), ((), ())), preferred_element_type=F32)


def _diff_prompt_kernel(lq1_ref, lk1_ref, lq2_ref, lk2_ref, g_ref, q_ref, k_ref, v_ref, o_ref,
                        m_ref, l_ref, acc_ref, *, tq):
    h = pl.program_id(1)
    qi = pl.program_id(2)
    slope = _head_slope(h)
    lam = _lam_value(lq1_ref[...], lk1_ref[...], lq2_ref[...], lk2_ref[...])

    q = q_ref[0]
    qs = (q[:, :DIFF_HEAD_DIM], q[:, DIFF_HEAD_DIM:])
    rows = lax.broadcasted_iota(jnp.int32, (tq, tq), 0)
    cols = lax.broadcasted_iota(jnp.int32, (tq, tq), 1)
    rel = (rows - cols).astype(F32)
    bias_off = -slope * rel

    m_ref[...] = jnp.full(m_ref.shape, -jnp.inf, F32)
    l_ref[...] = jnp.zeros(l_ref.shape, F32)
    acc_ref[...] = jnp.zeros(acc_ref.shape, F32)

    def update(kblk, vblk, bias):
        for m in range(2):
            s = _nt_dot(qs[m], kblk[:, m * DIFF_HEAD_DIM:(m + 1) * DIFF_HEAD_DIM]) + bias
            m_old = m_ref[m]
            m_new = jnp.maximum(m_old, jnp.max(s, axis=-1, keepdims=True))
            a = jnp.exp(m_old - m_new)
            p = jnp.exp(s - m_new)
            l_ref[m] = a * l_ref[m] + jnp.sum(p, axis=-1, keepdims=True)
            acc_ref[m] = a * acc_ref[m] + jnp.dot(p.astype(BF16), vblk, preferred_element_type=F32)
            m_ref[m] = m_new

    def body(j, carry):
        start = pl.multiple_of(j * tq, tq)
        kblk = k_ref[0, pl.ds(start, tq), :]
        vblk = v_ref[0, pl.ds(start, tq), :]
        gap = ((qi - j) * tq).astype(F32)
        update(kblk, vblk, bias_off - slope * gap)
        return carry

    lax.fori_loop(0, qi, body, 0)

    start = pl.multiple_of(qi * tq, tq)
    kblk = k_ref[0, pl.ds(start, tq), :]
    vblk = v_ref[0, pl.ds(start, tq), :]
    allowed = (cols // CHUNK) <= (rows // CHUNK)
    bias_diag = jnp.where(allowed, -slope * jnp.abs(rel), -jnp.inf)
    update(kblk, vblk, bias_diag)

    o = acc_ref[0] / l_ref[0] - lam * (acc_ref[1] / l_ref[1])
    o_ref[0] = _subln(o, g_ref[...]).astype(o_ref.dtype)


def _diff_attention_prompt(q, k, v, lam_rows, subln_g, *, tq=256):
    B, T, _ = q.shape
    tq = min(tq, T)
    assert T % tq == 0 and tq % CHUNK == 0
    vec = pl.BlockSpec((1, DIFF_HEAD_DIM), lambda b, h, i: (0, 0))
    return pl.pallas_call(
        functools.partial(_diff_prompt_kernel, tq=tq),
        out_shape=jax.ShapeDtypeStruct((B, T, DIFF_WIDTH), BF16),
        grid=(B, N_DIFF_HEADS, T // tq),
        in_specs=[vec, vec, vec, vec,
                  pl.BlockSpec((1, DIFF_VDIM), lambda b, h, i: (0, 0)),
                  pl.BlockSpec((1, tq, DIFF_VDIM), lambda b, h, i: (b, i, h)),
                  pl.BlockSpec((1, T, DIFF_VDIM), lambda b, h, i: (b, 0, h)),
                  pl.BlockSpec((1, T, DIFF_VDIM), lambda b, h, i: (b, 0, h))],
        out_specs=pl.BlockSpec((1, tq, DIFF_VDIM), lambda b, h, i: (b, i, h)),
        scratch_shapes=[pltpu.VMEM((2, tq, 1), F32), pltpu.VMEM((2, tq, 1), F32),
                        pltpu.VMEM((2, tq, DIFF_VDIM), F32)],
        compiler_params=_cparams(("parallel", "parallel", "parallel")),
        name="diff_attn_prompt",
    )(*lam_rows, subln_g, q, k, v)


def _diff_sample_kernel(lq1_ref, lk1_ref, lq2_ref, lk2_ref, g_ref, q_ref, kc_ref, vc_ref, kn_ref, vn_ref,
                        o_ref, *, past):
    h = pl.program_id(1)
    slope = _head_slope(h)
    lam = _lam_value(lq1_ref[...], lk1_ref[...], lq2_ref[...], lk2_ref[...])
    q = q_ref[0]
    T = q.shape[0]
    kc = kc_ref[0].astype(BF16)
    vc = vc_ref[0].astype(BF16)
    kn = kn_ref[0]
    vn = vn_ref[0]

    def bias(n_keys, key0):
        qpos = past + lax.broadcasted_iota(jnp.int32, (T, n_keys), 0)
        kpos = key0 + lax.broadcasted_iota(jnp.int32, (T, n_keys), 1)
        allowed = (kpos // CHUNK) <= (qpos // CHUNK)
        dist = jnp.abs(qpos - kpos).astype(F32)
        return jnp.where(allowed, -slope * dist, -jnp.inf)

    bias_c = bias(past, 0)
    bias_n = bias(T, past)
    wc, wn = [], []
    for m in range(2):
        sl = slice(m * DIFF_HEAD_DIM, (m + 1) * DIFF_HEAD_DIM)
        sc = _nt_dot(q[:, sl], kc[:, sl]) + bias_c
        sn = _nt_dot(q[:, sl], kn[:, sl]) + bias_n
        mx = jnp.maximum(jnp.max(sc, axis=-1, keepdims=True), jnp.max(sn, axis=-1, keepdims=True))
        pc = jnp.exp(sc - mx)
        pn = jnp.exp(sn - mx)
        inv = 1.0 / (jnp.sum(pc, axis=-1, keepdims=True) + jnp.sum(pn, axis=-1, keepdims=True))
        wc.append(pc * inv)
        wn.append(pn * inv)
    w_c = (wc[0] - lam * wc[1]).astype(BF16)
    w_n = (wn[0] - lam * wn[1]).astype(BF16)
    o = (jnp.dot(w_c, vc, preferred_element_type=F32) + jnp.dot(w_n, vn, preferred_element_type=F32))
    o_ref[0] = _subln(o, g_ref[...]).astype(o_ref.dtype)


def _diff_attention_sample(q, cache_k, cache_v, k_new, v_new, lam_rows, subln_g):
    B, T, _ = q.shape
    P = cache_k.shape[1]
    vec = pl.BlockSpec((1, DIFF_HEAD_DIM), lambda b, h: (0, 0))
    new = pl.BlockSpec((1, T, DIFF_VDIM), lambda b, h: (b, 0, h))
    old = pl.BlockSpec((1, P, DIFF_VDIM), lambda b, h: (b, 0, h))
    return pl.pallas_call(
        functools.partial(_diff_sample_kernel, past=P),
        out_shape=jax.ShapeDtypeStruct((B, T, DIFF_WIDTH), BF16),
        grid=(B, N_DIFF_HEADS),
        in_specs=[vec, vec, vec, vec, pl.BlockSpec((1, DIFF_VDIM), lambda b, h: (0, 0)),
                  new, old, old, new, new],
        out_specs=new,
        compiler_params=_cparams(("parallel", "parallel")),
        name="diff_attn_sample",
    )(*lam_rows, subln_g, q, cache_k, cache_v, k_new, v_new)


def _gdn_kernel(*refs, c, G, has_state):
    if has_state:
        (xq_ref, xk_ref, xv_ref, bq_ref, bk_ref, bv_ref, cwq_ref, cwk_ref, cwv_ref, ab_ref, alog_ref, dtb_ref,
         z_ref, ng_ref, s0_ref, o_ref, s_ref, pad_ref) = refs
    else:
        (xq_ref, xk_ref, xv_ref, bq_ref, bk_ref, bv_ref, cwq_ref, cwk_ref, cwv_ref, ab_ref, alog_ref, dtb_ref,
         z_ref, ng_ref, o_ref, s_ref, pad_ref) = refs
        s0_ref = None
    hg = pl.program_id(1)
    n = pl.program_id(2)
    H = N_GDN_HEADS
    PADR = 8

    @pl.when(n == 0)
    def _init():
        for t, b_ref in enumerate((bq_ref, bk_ref, bv_ref)):
            pad_ref[t, PADR - (CONV_W - 1):PADR, :] = b_ref[0]
        if has_state:
            s_ref[0] = s0_ref[0]
        else:
            s_ref[...] = jnp.zeros(s_ref.shape, F32)

    conv = []
    for t, (x_ref, cw_ref) in enumerate(((xq_ref, cwq_ref), (xk_ref, cwk_ref), (xv_ref, cwv_ref))):
        x = x_ref[0]
        pad_ref[t, PADR:PADR + c, :] = x
        y = cw_ref[CONV_W - 1:CONV_W, :] * x
        for j in range(CONV_W - 1):
            lo = PADR - (CONV_W - 1) + j
            y = y + cw_ref[j:j + 1, :] * pad_ref[t, lo:lo + c, :]
        pad_ref[t, 0:PADR, :] = x[c - PADR:c, :]
        conv.append(_silu(y))

    ab = ab_ref[0]
    a_in = ab[:, 0:H] + dtb_ref[...]
    softplus = jnp.maximum(a_in, 0.0) + jnp.log1p(jnp.exp(-jnp.abs(a_in)))
    g_col = -jnp.exp(alog_ref[...]) * softplus
    beta_col = _sigmoid(ab[:, H:2 * H])
    r = lax.broadcasted_iota(jnp.int32, (c, c), 0)
    s = lax.broadcasted_iota(jnp.int32, (c, c), 1)
    tri = r >= s
    strict = r > s
    tri_f = tri.astype(F32)
    eye_c = (r == s).astype(F32)
    gc_col = jnp.dot(tri_f, g_col, preferred_element_type=F32, precision=HI)
    eye_h = (lax.broadcasted_iota(jnp.int32, (H, H), 0) == lax.broadcasted_iota(jnp.int32, (H, H), 1)).astype(F32)
    gc_row = lax.dot_general(eye_h, gc_col, (((1,), (1,)), ((), ())), preferred_element_type=F32,
                             precision=HI)

    def bf(a):
        return a.astype(BF16)

    def split(a):
        hi = a.astype(BF16)
        return hi, (a - hi.astype(F32)).astype(BF16)

    def dot3(a, b):
        (ah, al), (bh, bl) = a, b
        return (jnp.dot(al, bh, preferred_element_type=F32) + jnp.dot(ah, bl, preferred_element_type=F32)
                + jnp.dot(ah, bh, preferred_element_type=F32))

    blocks = []
    b_ = 1
    while b_ < c:
        blocks.append((((r // (2 * b_)) == (s // (2 * b_))) & ((r // b_) != (s // b_)) & strict).astype(F32))
        b_ *= 2

    heads = range(G)
    lanes = [slice(hh * GDN_DK, (hh + 1) * GDN_DK) for hh in heads]
    if G == H:
        gcc = [gc_col[:, hh:hh + 1] for hh in heads]
        bet = [beta_col[:, hh:hh + 1] for hh in heads]
        gcr = [gc_row[hh:hh + 1, :] for hh in heads]
    else:
        gcc, bet, gcr = [], [], []
        for hh in heads:
            head = hg * G + hh
            sel_c = (lax.broadcasted_iota(jnp.int32, (c, H), 1) == head).astype(F32)
            sel_r = (lax.broadcasted_iota(jnp.int32, (H, c), 0) == head).astype(F32)
            gcc.append(jnp.sum(gc_col * sel_c, axis=1, keepdims=True))
            bet.append(jnp.sum(beta_col * sel_c, axis=1, keepdims=True))
            gcr.append(jnp.sum(gc_row * sel_r, axis=0, keepdims=True))
    g_last = [g[c - 1:c, :] for g in gcc]

    q, k, v = [], [], []
    for hh in heads:
        qh = conv[0][:, lanes[hh]]
        kh = conv[1][:, lanes[hh]]
        q.append(qh * (lax.rsqrt(jnp.sum(qh * qh, axis=-1, keepdims=True) + NORM_EPS) * (GDN_DK ** -0.5)))
        k.append(kh * lax.rsqrt(jnp.sum(kh * kh, axis=-1, keepdims=True) + NORM_EPS))
        v.append(conv[2][:, lanes[hh]])

    decay = [jnp.exp(jnp.where(tri, gcc[hh] - gcr[hh], -jnp.inf)) for hh in heads]
    kb = [k[hh] * bet[hh] for hh in heads]
    k16 = [bf(k[hh]) for hh in heads]
    mmat = [jnp.where(strict, _nt_dot(bf(kb[hh]), k16[hh]) * decay[hh], 0.0) for hh in heads]
    qk = [jnp.where(tri, _nt_dot(bf(q[hh]), k16[hh]) * decay[hh], 0.0) for hh in heads]

    tinv = [eye_c - mmat[hh] * blocks[0] for hh in heads]
    for lvl in range(1, len(blocks)):
        d16 = [bf(t) for t in tinv]
        x = [jnp.dot(bf(mmat[hh] * blocks[lvl]), d16[hh], preferred_element_type=F32) for hh in heads]
        tinv = [tinv[hh] - jnp.dot(d16[hh], bf(x[hh]), preferred_element_type=F32) for hh in heads]
    m_s = [split(m) for m in mmat]
    t_s = [split(t) for t in tinv]
    res = [(eye_c - tinv[hh]) - dot3(m_s[hh], t_s[hh]) for hh in heads]
    tinv = [tinv[hh] + jnp.dot(t_s[hh][0], bf(res[hh]), preferred_element_type=F32) for hh in heads]

    rhs = [jnp.concatenate([v[hh] * bet[hh], kb[hh] * jnp.exp(gcc[hh])], axis=1) for hh in heads]
    sol = [dot3(split(tinv[hh]), split(rhs[hh])) for hh in heads]
    u = [x_[:, :GDN_DV] for x_ in sol]
    w = [x_[:, GDN_DV:] for x_ in sol]

    S = [s_ref[0, hh] for hh in heads]
    S16 = [bf(x_) for x_ in S]
    v_new = [u[hh] - jnp.dot(bf(w[hh]), S16[hh], preferred_element_type=F32) for hh in heads]
    v16 = [bf(x_) for x_ in v_new]
    o = [jnp.dot(bf(q[hh] * jnp.exp(gcc[hh])), S16[hh], preferred_element_type=F32)
         + jnp.dot(bf(qk[hh]), v16[hh], preferred_element_type=F32) for hh in heads]
    for hh in heads:
        kd = bf(k[hh] * jnp.exp(g_last[hh] - gcc[hh]))
        s_ref[0, hh] = S[hh] * jnp.exp(g_last[hh]) + lax.dot_general(kd, v16[hh], (((0,), (0,)), ((), ())),
                                                                     preferred_element_type=F32)
    for hh in heads:
        ms = jnp.mean(o[hh] * o[hh], axis=-1, keepdims=True)
        y = o[hh] * lax.rsqrt(ms + NORM_EPS) * ng_ref[...] * z_ref[0][:, lanes[hh]].astype(F32)
        o_ref[0, :, lanes[hh]] = y.astype(o_ref.dtype)


def _gdn(gqkv, buf0, conv_w, gab, a_log, dt_bias, z_silu, norm_g, s0, *, c, G=16):
    B, T, _ = gqkv.shape
    H = N_GDN_HEADS
    assert T % c == 0 and c % 8 == 0 and (c & (c - 1)) == 0 and H % G == 0
    nG = H // G
    W = G * GDN_DK

    def stream(t):
        return pl.BlockSpec((1, c, W), lambda b, g, n, t=t: (b, n, t * nG + g))

    def hist(t):
        return pl.BlockSpec((1, CONV_W - 1, W), lambda b, g, n, t=t: (b, 0, t * nG + g))

    def cw(t):
        return pl.BlockSpec((CONV_W, W), lambda b, g, n, t=t: (0, t * nG + g))

    small = pl.BlockSpec((1, H), lambda b, g, n: (0, 0))
    state = pl.BlockSpec((1, G, GDN_DK, GDN_DV), lambda b, g, n: (b, g, 0, 0))
    in_specs = [stream(0), stream(1), stream(2), hist(0), hist(1), hist(2), cw(0), cw(1), cw(2),
                pl.BlockSpec((1, c, 128), lambda b, g, n: (b, n, 0)), small, small,
                pl.BlockSpec((1, c, W), lambda b, g, n: (b, n, g)),
                pl.BlockSpec((1, GDN_DV), lambda b, g, n: (0, 0))]
    args = [gqkv, gqkv, gqkv, buf0, buf0, buf0, conv_w, conv_w, conv_w, gab, a_log, dt_bias, z_silu, norm_g]
    if s0 is not None:
        in_specs.append(state)
        args.append(s0)
    return pl.pallas_call(
        functools.partial(_gdn_kernel, c=c, G=G, has_state=s0 is not None),
        out_shape=(jax.ShapeDtypeStruct((B, T, GDN_WIDTH), BF16),
                   jax.ShapeDtypeStruct((B, H, GDN_DK, GDN_DV), F32)),
        grid=(B, nG, T // c),
        in_specs=in_specs,
        out_specs=(pl.BlockSpec((1, c, W), lambda b, g, n: (b, n, g)), state),
        scratch_shapes=[pltpu.VMEM((3, 8 + c, W), F32)],
        compiler_params=_cparams(("parallel", "parallel", "arbitrary")),
        name="gdn",
    )(*args)


def _mix_kernel(oa_ref, ob_ref, wa_ref, wb_ref, sa_ref, sb_ref, o_ref):
    a = jnp.dot(oa_ref[...], wa_ref[...], preferred_element_type=F32)
    b = jnp.dot(ob_ref[...], wb_ref[...], preferred_element_type=F32)
    o_ref[...] = (sa_ref[...].astype(F32) * a + sb_ref[...].astype(F32) * b).astype(o_ref.dtype)


def _mix(o_a, o_b, w_pa, w_pb, gates, *, tm=512, tn=1024):
    M = o_a.shape[0]
    tm = min(tm, M)
    nb = D_MODEL // tn
    return pl.pallas_call(
        _mix_kernel,
        out_shape=jax.ShapeDtypeStruct((M, D_MODEL), BF16),
        grid=(M // tm, nb),
        in_specs=[pl.BlockSpec((tm, DIFF_WIDTH), lambda i, j: (i, 0)),
                  pl.BlockSpec((tm, GDN_WIDTH), lambda i, j: (i, 0)),
                  pl.BlockSpec((DIFF_WIDTH, tn), lambda i, j: (0, j)),
                  pl.BlockSpec((GDN_WIDTH, tn), lambda i, j: (0, j)),
                  pl.BlockSpec((tm, tn), lambda i, j: (i, j)),
                  pl.BlockSpec((tm, tn), lambda i, j: (i, j + nb))],
        out_specs=pl.BlockSpec((tm, tn), lambda i, j: (i, j)),
        compiler_params=_cparams(("parallel", "parallel")),
        name="mix",
    )(o_a, o_b, w_pa, w_pb, gates, gates)


def _post_kernel(x_ref, mix_ref, wo_ref, g1_ref, b1_ref, wxq_ref, mk_ref, mv_ref, wxo_ref, g2_ref, b2_ref,
                 h2_ref, h2b_ref):
    h1 = ALPHA * x_ref[0] + jnp.dot(mix_ref[0], wo_ref[...], preferred_element_type=F32)
    h1 = _layer_norm(h1, g1_ref[...], b1_ref[...])
    qx = jnp.dot(h1.astype(BF16), wxq_ref[...], preferred_element_type=F32) * (XHEAD_DIM ** -0.5)
    qx = qx.astype(BF16)
    mk = mk_ref[0]
    mv = mv_ref[0]
    heads = []
    for hh in range(N_XHEADS):
        sl = slice(hh * XHEAD_DIM, (hh + 1) * XHEAD_DIM)
        s = _nt_dot(qx[:, sl], mk[:, sl])
        p = jnp.exp(s - jnp.max(s, axis=-1, keepdims=True))
        p = p / jnp.sum(p, axis=-1, keepdims=True)
        heads.append(jnp.dot(p.astype(BF16), mv[:, sl], preferred_element_type=F32))
    ox = jnp.concatenate(heads, axis=1).astype(BF16)
    h2 = ALPHA * h1 + jnp.dot(ox, wxo_ref[...], preferred_element_type=F32)
    h2 = _layer_norm(h2, g2_ref[...], b2_ref[...])
    h2_ref[0] = h2
    h2b_ref[0] = h2.astype(BF16)


def _post(x, mix, w_o, ln1_g, ln1_b, w_xq, mem_k, mem_v, w_xo, ln2_g, ln2_b, *, tm=256):
    B, T, D = x.shape
    tm = min(tm, T)
    const = lambda shape: pl.BlockSpec(shape, lambda b, i: (0, 0))
    rows = lambda: pl.BlockSpec((1, tm, D), lambda b, i: (b, i, 0))
    mem = lambda: pl.BlockSpec((1, N_MEM, XWIDTH), lambda b, i: (b, 0, 0))
    return pl.pallas_call(
        _post_kernel,
        out_shape=(jax.ShapeDtypeStruct((B, T, D), F32), jax.ShapeDtypeStruct((B, T, D), BF16)),
        grid=(B, T // tm),
        in_specs=[rows(), rows(), const((D, D)), const((1, D)), const((1, D)), const((D, XWIDTH)),
                  mem(), mem(), const((XWIDTH, D)), const((1, D)), const((1, D))],
        out_specs=(rows(), rows()),
        compiler_params=_cparams(("parallel", "parallel")),
        name="post_attn",
    )(x, mix, w_o, ln1_g, ln1_b, w_xq, mem_k, mem_v, w_xo, ln2_g, ln2_b)


def _ffn_kernel(hb_ref, h_ref, w1_ref, w3_ref, w2_ref, g_ref, b_ref, y_ref, acc_ref):
    f = pl.program_id(1)

    @pl.when(f == 0)
    def _():
        acc_ref[...] = jnp.zeros(acc_ref.shape, F32)

    hb = hb_ref[...]
    a = jnp.dot(hb, w1_ref[...], preferred_element_type=F32)
    b = jnp.dot(hb, w3_ref[...], preferred_element_type=F32)
    act = (_silu(a) * b).astype(BF16)
    acc_ref[...] += jnp.dot(act, w2_ref[...], preferred_element_type=F32)

    @pl.when(f == pl.num_programs(1) - 1)
    def _():
        y_ref[...] = _layer_norm(ALPHA * h_ref[...] + acc_ref[...], g_ref[...], b_ref[...])


def _ffn(h2b, h2, w1, w3, w2, ln_g, ln_b, *, tm=512, tf=512):
    M, D = h2.shape
    tm = min(tm, M)
    assert M % tm == 0 and D_FF % tf == 0
    return pl.pallas_call(
        _ffn_kernel,
        out_shape=jax.ShapeDtypeStruct((M, D), F32),
        grid=(M // tm, D_FF // tf),
        in_specs=[pl.BlockSpec((tm, D), lambda i, f: (i, 0)),
                  pl.BlockSpec((tm, D), lambda i, f: (i, 0)),
                  pl.BlockSpec((D, tf), lambda i, f: (0, f)),
                  pl.BlockSpec((D, tf), lambda i, f: (0, f)),
                  pl.BlockSpec((tf, D), lambda i, f: (f, 0)),
                  pl.BlockSpec((1, D), lambda i, f: (0, 0)),
                  pl.BlockSpec((1, D), lambda i, f: (0, 0))],
        out_specs=pl.BlockSpec((tm, D), lambda i, f: (i, 0)),
        scratch_shapes=[pltpu.VMEM((tm, D), F32)],
        compiler_params=_cparams(("parallel", "arbitrary")),
        name="ffn",
    )(h2b, h2, w1, w3, w2, ln_g, ln_b)


def _encoder_layer(x, mem_k, mem_v, past, W):
    B, T, D = x.shape
    M = B * T
    xb = x.reshape(M, D).astype(BF16)
    w_in = W["w_in"]

    (dq,) = _proj(xb, w_in, OFF_DQ, DIFF_WIDTH, (BF16,), scale=DIFF_HEAD_DIM ** -0.5, name="proj_dq")
    dk, dkb = _proj(xb, w_in, OFF_DK, DIFF_WIDTH, (F32, BF16), name="proj_dk")
    dv, dvb = _proj(xb, w_in, OFF_DV, DIFF_WIDTH, (F32, BF16), name="proj_dv")
    (gqkv,) = _proj(xb, w_in, OFF_GQKV, 3 * GDN_WIDTH, (F32,), name="proj_gqkv")
    (gz,) = _proj(xb, w_in, OFF_GZ, GDN_WIDTH, (BF16,), act="silu", name="proj_gz")
    (gab,) = _proj(xb, w_in, OFF_GAB, 128, (F32,), tn=128, name="proj_gab")
    (gates,) = _proj(xb, W["w_gates"], 0, 2 * D_MODEL, (BF16,), act="sigmoid", name="proj_gates")

    dq = dq.reshape(B, T, DIFF_WIDTH)
    dkb = dkb.reshape(B, T, DIFF_WIDTH)
    dvb = dvb.reshape(B, T, DIFF_WIDTH)
    gqkv = gqkv.reshape(B, T, 3 * GDN_WIDTH)
    lam_rows = W["lam_rows"]
    if past is None:
        o_a = _diff_attention_prompt(dq, dkb, dvb, lam_rows, W["diff_subln_g"])
        buf0 = jnp.zeros((B, CONV_W - 1, 3 * GDN_WIDTH), F32)
        s0 = None
        c = CHUNK
    else:
        cache_k, cache_v, s0, buf0 = past
        P = cache_k.shape[1]
        o_a = _diff_attention_sample(dq, cache_k.reshape(B, P, DIFF_WIDTH), cache_v.reshape(B, P, DIFF_WIDTH),
                                     dkb, dvb, lam_rows, W["diff_subln_g"])
        c = T
    o_b, s_new = _gdn(gqkv, buf0, W["conv_w"], gab.reshape(B, T, 128), W["gdn_a_log"], W["gdn_dt_bias"],
                      gz.reshape(B, T, GDN_WIDTH), W["gdn_norm_g"], s0, c=c)
    new_buf = gqkv[:, T - (CONV_W - 1):, :]

    mix = _mix(o_a.reshape(M, DIFF_WIDTH), o_b.reshape(M, GDN_WIDTH), W["w_pa"], W["w_pb"], gates)
    h2, h2b = _post(x, mix.reshape(B, T, D), W["w_o"], W["ln1_g"], W["ln1_b"], W["w_xq"], mem_k, mem_v,
                    W["w_xo"], W["ln2_g"], W["ln2_b"])
    y = _ffn(h2b.reshape(M, D), h2.reshape(M, D), W["w_ff1"], W["w_ff3"], W["w_ff2"], W["ln3_g"], W["ln3_b"])
    return (y.reshape(B, T, D), dk.reshape(B, T, N_DIFF_HEADS, DIFF_VDIM), dv.reshape(B, T, N_DIFF_HEADS, DIFF_VDIM),
            s_new, new_buf)


def kernel(x_prompt, x_sample, mem_prompt, cache_diff_k, cache_diff_v, state_gdn, state_gdn_conv, cache_mem_k, cache_mem_v, w_in, conv_w, lam_q1, lam_k1, lam_q2, lam_k2, diff_subln_g, gdn_a_log, gdn_dt_bias, gdn_norm_g, w_pa, w_pb, w_o, ln1_g, ln1_b, w_xq, w_xk, w_xv, w_xo, ln2_g, ln2_b, w_ff1, w_ff3, w_ff2, ln3_g, ln3_b):
    l = 0
    w_in_b = w_in[l].astype(BF16)
    W = {
        "w_in": w_in_b,
        "w_gates": w_in_b[:, OFF_GATES:],
        "conv_w": conv_w[l],
        "lam_rows": tuple(v[l].reshape(1, DIFF_HEAD_DIM) for v in (lam_q1, lam_k1, lam_q2, lam_k2)),
        "diff_subln_g": diff_subln_g[l].reshape(1, DIFF_VDIM),
        "gdn_a_log": gdn_a_log[l].reshape(1, N_GDN_HEADS),
        "gdn_dt_bias": gdn_dt_bias[l].reshape(1, N_GDN_HEADS),
        "gdn_norm_g": gdn_norm_g[l].reshape(1, GDN_DV),
        "w_pa": w_pa[l].astype(BF16), "w_pb": w_pb[l].astype(BF16), "w_o": w_o[l].astype(BF16),
        "ln1_g": ln1_g[l].reshape(1, D_MODEL), "ln1_b": ln1_b[l].reshape(1, D_MODEL),
        "w_xq": w_xq[l].astype(BF16), "w_xo": w_xo[l].astype(BF16),
        "ln2_g": ln2_g[l].reshape(1, D_MODEL), "ln2_b": ln2_b[l].reshape(1, D_MODEL),
        "w_ff1": w_ff1[l].astype(BF16), "w_ff3": w_ff3[l].astype(BF16), "w_ff2": w_ff2[l].astype(BF16),
        "ln3_g": ln3_g[l].reshape(1, D_MODEL), "ln3_b": ln3_b[l].reshape(1, D_MODEL),
    }
    Bp = x_prompt.shape[0]
    memb = mem_prompt.reshape(Bp * N_MEM, D_MODEL).astype(BF16)
    mem_k, mem_kb = _proj(memb, w_xk[l].astype(BF16), 0, XWIDTH, (F32, BF16), tn=XWIDTH, name="proj_mem_k")
    mem_v, mem_vb = _proj(memb, w_xv[l].astype(BF16), 0, XWIDTH, (F32, BF16), tn=XWIDTH, name="proj_mem_v")

    yp, pk, pv, ps, pc = _encoder_layer(x_prompt, mem_kb.reshape(Bp, N_MEM, XWIDTH),
                                        mem_vb.reshape(Bp, N_MEM, XWIDTH), None, W)
    Bs = x_sample.shape[0]
    past = (cache_diff_k[l], cache_diff_v[l], state_gdn[l], state_gdn_conv[l])
    ys, sk, sv, ss, sc = _encoder_layer(x_sample, cache_mem_k[l].reshape(Bs, N_MEM, XWIDTH).astype(BF16),
                                        cache_mem_v[l].reshape(Bs, N_MEM, XWIDTH).astype(BF16), past, W)
    st = lambda a: a[None]
    return (yp, ys, st(pk), st(pv), st(ps), st(pc),
            st(mem_k.reshape(Bp, N_MEM, N_XHEADS, XHEAD_DIM)), st(mem_v.reshape(Bp, N_MEM, N_XHEADS, XHEAD_DIM)),
            st(sk), st(sv), st(ss), st(sc))
```

```python
import functools
import math

import jax
import jax.numpy as jnp
from jax import lax
from jax.experimental import pallas as pl
from jax.experimental.pallas import tpu as pltpu

D_MODEL = 2048
CHUNK = 64
N_DIFF_HEADS = 8
DIFF_HEAD_DIM = 128
DIFF_VDIM = 2 * DIFF_HEAD_DIM
DIFF_WIDTH = N_DIFF_HEADS * DIFF_VDIM
N_GDN_HEADS = 16
GDN_DK = 128
GDN_DV = 128
GDN_WIDTH = N_GDN_HEADS * GDN_DK
CONV_W = 4
N_MEM = 256
N_XHEADS = 4
XHEAD_DIM = 128
XWIDTH = N_XHEADS * XHEAD_DIM
D_FF = 5632
DEPTH = 1
ALPHA = (2.0 * DEPTH) ** 0.25
LN_EPS = 1e-5
NORM_EPS = 1e-6
LAM_INIT = 0.8 - 0.6 * math.exp(-0.3 * 0)

OFF_DQ = 0
OFF_DK = 2048
OFF_DV = 4096
OFF_GQKV = 6144
OFF_GZ = 12288
OFF_GAB = 14336
OFF_GATES = 14368

VMEM_LIMIT = 56 * 1024 * 1024
BF16 = jnp.bfloat16
F32 = jnp.float32
HI = lax.Precision.HIGHEST


def _cparams(sem):
    return pltpu.CompilerParams(dimension_semantics=sem, vmem_limit_bytes=VMEM_LIMIT)


def _sigmoid(x):
    return 1.0 / (1.0 + jnp.exp(-x))


def _silu(x):
    return x * _sigmoid(x)


def _layer_norm(x, g, b):
    mu = jnp.mean(x, axis=-1, keepdims=True)
    xc = x - mu
    var = jnp.mean(xc * xc, axis=-1, keepdims=True)
    return xc * lax.rsqrt(var + LN_EPS) * g + b


def _proj_kernel(x_ref, w_ref, *o_refs, act, scale):
    acc = jnp.dot(x_ref[...], w_ref[...], preferred_element_type=F32)
    if scale != 1.0:
        acc = acc * scale
    if act == "sigmoid":
        acc = _sigmoid(acc)
    elif act == "silu":
        acc = _silu(acc)
    for o in o_refs:
        o[...] = acc.astype(o.dtype)


def _proj(x, w, col_off, n_cols, out_dtypes, *, act=None, scale=1.0, tm=1024, tn=1024, name="proj"):
    M, K = x.shape
    tm = min(tm, M)
    tn = min(tn, n_cols)
    assert M % tm == 0 and n_cols % tn == 0 and col_off % tn == 0
    cb = col_off // tn
    outs = pl.pallas_call(
        functools.partial(_proj_kernel, act=act, scale=scale),
        out_shape=tuple(jax.ShapeDtypeStruct((M, n_cols), dt) for dt in out_dtypes),
        grid=(M // tm, n_cols // tn),
        in_specs=[pl.BlockSpec((tm, K), lambda i, j: (i, 0)),
                  pl.BlockSpec((K, tn), lambda i, j: (0, j + cb))],
        out_specs=tuple(pl.BlockSpec((tm, tn), lambda i, j: (i, j)) for _ in out_dtypes),
        compiler_params=_cparams(("parallel", "parallel")),
        name=name,
    )(x, w)
    return outs


def _lam_value(lq1, lk1, lq2, lk2):
    a = jnp.sum(lq1 * lk1, axis=-1, keepdims=True)
    b = jnp.sum(lq2 * lk2, axis=-1, keepdims=True)
    return jnp.exp(a) - jnp.exp(b) + LAM_INIT


def _subln(o, g):
    ms = jnp.mean(o * o, axis=-1, keepdims=True)
    return o * lax.rsqrt(ms + NORM_EPS) * g * (1.0 - LAM_INIT)


def _head_slope(h):
    e = (h + 1).astype(F32) * (-8.0 / N_DIFF_HEADS)
    return jnp.exp2(jnp.full((1, 1), e, F32))


def _nt_dot(a, b):
    return lax.dot_general(a, b, (((1,), (1,)), ((), ())), preferred_element_type=F32)


def _lane_tile(x, n):
    return x if n == 1 else jnp.concatenate([x] * n, axis=1)


def _diff_prompt_kernel(lq1_ref, lk1_ref, lq2_ref, lk2_ref, g_ref, q_ref, k_ref, v_ref, o_ref,
                        m_ref, l_ref, acc_ref, *, tq):
    LANES = 128
    h = pl.program_id(1)
    qi = pl.program_id(2)
    slope = _head_slope(h)
    lam = _lam_value(lq1_ref[...], lk1_ref[...], lq2_ref[...], lk2_ref[...])

    q = q_ref[0]
    qs = (q[:, :DIFF_HEAD_DIM], q[:, DIFF_HEAD_DIM:])
    col_bias = slope * lax.broadcasted_iota(jnp.int32, (1, tq), 1).astype(F32)

    m_ref[...] = jnp.full(m_ref.shape, -jnp.inf, F32)
    l_ref[...] = jnp.zeros(l_ref.shape, F32)
    acc_ref[...] = jnp.zeros(acc_ref.shape, F32)

    def update(kblk, vblk, bias, shift):
        maps = range(2)
        t = [_nt_dot(qs[m], kblk[:, m * DIFF_HEAD_DIM:(m + 1) * DIFF_HEAD_DIM]) + bias for m in maps]
        m_old = [m_ref[m] for m in maps]
        m_new = [jnp.maximum(m_old[m], jnp.max(t[m], axis=-1, keepdims=True) + shift) for m in maps]
        a = [jnp.exp(m_old[m] - m_new[m]) for m in maps]
        p = [jnp.exp(t[m] - _lane_tile(m_new[m] - shift, tq // LANES)) for m in maps]
        pv = [jnp.dot(p[m].astype(BF16), vblk, preferred_element_type=F32) for m in maps]
        for m in maps:
            psum = p[m][:, :LANES]
            for i in range(1, tq // LANES):
                psum = psum + p[m][:, i * LANES:(i + 1) * LANES]
            l_ref[m] = a[m] * l_ref[m] + psum
            acc_ref[m] = _lane_tile(a[m], DIFF_VDIM // LANES) * acc_ref[m] + pv[m]
            m_ref[m] = m_new[m]

    def body(j, carry):
        start = pl.multiple_of(j * tq, tq)
        gap = ((qi - j) * tq).astype(F32)
        update(k_ref[0, pl.ds(start, tq), :], v_ref[0, pl.ds(start, tq), :], col_bias, -slope * gap)
        return carry

    lax.fori_loop(0, qi, body, 0)

    rows = lax.broadcasted_iota(jnp.int32, (tq, tq), 0)
    cols = lax.broadcasted_iota(jnp.int32, (tq, tq), 1)
    allowed = (cols // CHUNK) <= (rows // CHUNK)
    dist = jnp.abs(rows - cols).astype(F32)
    bias_diag = jnp.where(allowed, slope * (rows.astype(F32) - dist), -jnp.inf)
    start = pl.multiple_of(qi * tq, tq)
    update(k_ref[0, pl.ds(start, tq), :], v_ref[0, pl.ds(start, tq), :], bias_diag, jnp.zeros((1, 1), F32))

    outs = []
    for m in range(2):
        inv = 1.0 / jnp.sum(l_ref[m], axis=-1, keepdims=True)
        outs.append(acc_ref[m] * inv)
    o = outs[0] - lam * outs[1]
    o_ref[0] = _subln(o, g_ref[...]).astype(o_ref.dtype)


def _diff_attention_prompt(q, k, v, lam_rows, subln_g, *, tq=512):
    B, T, _ = q.shape
    tq = min(tq, T)
    assert T % tq == 0 and tq % CHUNK == 0
    vec = pl.BlockSpec((1, DIFF_HEAD_DIM), lambda b, h, i: (0, 0))
    return pl.pallas_call(
        functools.partial(_diff_prompt_kernel, tq=tq),
        out_shape=jax.ShapeDtypeStruct((B, T, DIFF_WIDTH), BF16),
        grid=(B, N_DIFF_HEADS, T // tq),
        in_specs=[vec, vec, vec, vec,
                  pl.BlockSpec((1, DIFF_VDIM), lambda b, h, i: (0, 0)),
                  pl.BlockSpec((1, tq, DIFF_VDIM), lambda b, h, i: (b, i, h)),
                  pl.BlockSpec((1, T, DIFF_VDIM), lambda b, h, i: (b, 0, h)),
                  pl.BlockSpec((1, T, DIFF_VDIM), lambda b, h, i: (b, 0, h))],
        out_specs=pl.BlockSpec((1, tq, DIFF_VDIM), lambda b, h, i: (b, i, h)),
        scratch_shapes=[pltpu.VMEM((2, tq, 128), F32), pltpu.VMEM((2, tq, 128), F32),
                        pltpu.VMEM((2, tq, DIFF_VDIM), F32)],
        compiler_params=_cparams(("parallel", "parallel", "parallel")),
        name="diff_attn_prompt",
    )(*lam_rows, subln_g, q, k, v)


def _diff_sample_kernel(lq1_ref, lk1_ref, lq2_ref, lk2_ref, g_ref, q_ref, kc_ref, vc_ref, kn_ref, vn_ref,
                        o_ref, *, past, pc):
    H = N_DIFF_HEADS
    T = q_ref.shape[1]
    L = H * T
    rows = pc * H
    n_chunks = past // pc
    lam = _lam_value(lq1_ref[...], lk1_ref[...], lq2_ref[...], lk2_ref[...])
    maps = range(2)

    def map_cols(m):
        return slice(m * DIFF_HEAD_DIM, (m + 1) * DIFF_HEAD_DIM)

    qm = [jnp.concatenate([q_ref[0, :, h * DIFF_VDIM + m * DIFF_HEAD_DIM:h * DIFF_VDIM + (m + 1) * DIFF_HEAD_DIM]
                           for h in range(H)], axis=0) for m in maps]
    lane = lax.broadcasted_iota(jnp.int32, (1, L), 1)
    h_lane = lane // T
    t_lane = lane % T
    slope = jnp.exp2((h_lane + 1).astype(F32) * (-8.0 / H))
    base = slope * (lax.broadcasted_iota(jnp.int32, (rows, L), 0) // H).astype(F32)
    valid = lax.broadcasted_iota(jnp.int32, (H, L), 0) == h_lane

    def chunk_shift(c):
        return slope * ((c * pc - past) - t_lane).astype(F32)

    def chunk_scores(c):
        r0 = pl.multiple_of(c * rows, rows)
        kblk = kc_ref[0, pl.ds(r0, rows), :].astype(BF16)
        return [(_nt_dot(kblk[:, map_cols(m)], qm[m]) + base).reshape(pc, H, L) for m in maps]

    def stats_step(c, carry):
        shift = chunk_shift(c)
        s = chunk_scores(c)
        out = []
        for m in maps:
            m_old, l_old = carry[2 * m], carry[2 * m + 1]
            m_new = jnp.maximum(m_old, jnp.max(s[m], axis=0) + shift)
            l_new = l_old * jnp.exp(m_old - m_new) + jnp.sum(jnp.exp(s[m] - (m_new - shift)[None]), axis=0)
            out += [m_new, l_new]
        return tuple(out)

    init = (jnp.full((H, L), -jnp.inf, F32), jnp.zeros((H, L), F32)) * 2
    stats = lax.fori_loop(0, n_chunks, stats_step, init)

    kn = kn_ref[0].astype(BF16)
    vn = vn_ref[0].astype(BF16)
    tk = lax.broadcasted_iota(jnp.int32, (T * H, L), 0) // H
    allowed = ((past + tk) // CHUNK) <= ((past + t_lane) // CHUNK)
    bias_n = jnp.where(allowed, -slope * jnp.abs(tk - t_lane).astype(F32), -jnp.inf)
    s_n = [(_nt_dot(kn[:, map_cols(m)], qm[m]) + bias_n).reshape(T, H, L) for m in maps]
    m_fin, coef = [], []
    for m in maps:
        m_run, l_run = stats[2 * m], stats[2 * m + 1]
        mf = jnp.maximum(m_run, jnp.max(s_n[m], axis=0))
        lf = l_run * jnp.exp(m_run - mf) + jnp.sum(jnp.exp(s_n[m] - mf[None]), axis=0)
        m_fin.append(mf)
        coef.append(jnp.where(valid, (1.0 if m == 0 else -lam) / lf, 0.0))

    def tn_dot(w, v):
        return lax.dot_general(w, v, (((0,), (0,)), ((), ())), preferred_element_type=F32)

    def out_step(c, acc):
        shift = chunk_shift(c)
        s = chunk_scores(c)
        r0 = pl.multiple_of(c * rows, rows)
        vblk = vc_ref[0, pl.ds(r0, rows), :].astype(BF16)
        w = sum(jnp.exp(s[m] - (m_fin[m] - shift)[None]) * coef[m][None] for m in maps)
        return acc + tn_dot(w.reshape(rows, L).astype(BF16), vblk)

    acc = lax.fori_loop(0, n_chunks, out_step, jnp.zeros((L, DIFF_VDIM), F32))
    w_n = sum(jnp.exp(s_n[m] - m_fin[m][None]) * coef[m][None] for m in maps)
    acc = acc + tn_dot(w_n.reshape(T * H, L).astype(BF16), vn)
    o = _subln(acc, g_ref[...]).astype(o_ref.dtype)
    for h in range(H):
        o_ref[0, :, h * DIFF_VDIM:(h + 1) * DIFF_VDIM] = o[h * T:(h + 1) * T]


def _diff_attention_sample(q, cache_k, cache_v, k_new, v_new, lam_rows, subln_g):
    B, T, _ = q.shape
    P = cache_k.shape[1]
    H = N_DIFF_HEADS
    pc = min(P, 128)
    assert H * T == 128 and P > 0 and P % pc == 0
    vec = pl.BlockSpec((1, DIFF_HEAD_DIM), lambda b: (0, 0))
    tok = pl.BlockSpec((1, T, DIFF_WIDTH), lambda b: (b, 0, 0))
    new = pl.BlockSpec((1, T * H, DIFF_VDIM), lambda b: (b, 0, 0))
    old = pl.BlockSpec((1, P * H, DIFF_VDIM), lambda b: (b, 0, 0))
    flat = lambda a: a.reshape(B, a.shape[1] * H, DIFF_VDIM)
    return pl.pallas_call(
        functools.partial(_diff_sample_kernel, past=P, pc=pc),
        out_shape=jax.ShapeDtypeStruct((B, T, DIFF_WIDTH), BF16),
        grid=(B,),
        in_specs=[vec, vec, vec, vec, pl.BlockSpec((1, DIFF_VDIM), lambda b: (0, 0)),
                  tok, old, old, new, new],
        out_specs=tok,
        compiler_params=_cparams(("parallel",)),
        name="diff_attn_sample",
    )(*lam_rows, subln_g, q, flat(cache_k), flat(cache_v), flat(k_new), flat(v_new))


def _gdn_kernel(*refs, c, G, has_state):
    if has_state:
        (xq_ref, xk_ref, xv_ref, bq_ref, bk_ref, bv_ref, cwq_ref, cwk_ref, cwv_ref, ab_ref, alog_ref, dtb_ref,
         z_ref, ng_ref, s0_ref, o_ref, s_ref, pad_ref) = refs
    else:
        (xq_ref, xk_ref, xv_ref, bq_ref, bk_ref, bv_ref, cwq_ref, cwk_ref, cwv_ref, ab_ref, alog_ref, dtb_ref,
         z_ref, ng_ref, o_ref, s_ref, pad_ref) = refs
        s0_ref = None
    hg = pl.program_id(1)
    n = pl.program_id(2)
    H = N_GDN_HEADS
    PADR = 8

    @pl.when(n == 0)
    def _init():
        for t, b_ref in enumerate((bq_ref, bk_ref, bv_ref)):
            pad_ref[t, PADR - (CONV_W - 1):PADR, :] = b_ref[0]
        if has_state:
            s_ref[0] = s0_ref[0]
        else:
            s_ref[...] = jnp.zeros(s_ref.shape, F32)

    conv = []
    for t, (x_ref, cw_ref) in enumerate(((xq_ref, cwq_ref), (xk_ref, cwk_ref), (xv_ref, cwv_ref))):
        x = x_ref[0]
        pad_ref[t, PADR:PADR + c, :] = x
        y = cw_ref[CONV_W - 1:CONV_W, :] * x
        for j in range(CONV_W - 1):
            lo = PADR - (CONV_W - 1) + j
            y = y + cw_ref[j:j + 1, :] * pad_ref[t, lo:lo + c, :]
        pad_ref[t, 0:PADR, :] = x[c - PADR:c, :]
        conv.append(_silu(y))

    ab = ab_ref[0]
    a_in = ab[:, 0:H] + dtb_ref[...]
    softplus = jnp.maximum(a_in, 0.0) + jnp.log1p(jnp.exp(-jnp.abs(a_in)))
    g_col = -jnp.exp(alog_ref[...]) * softplus
    beta_col = _sigmoid(ab[:, H:2 * H])
    r = lax.broadcasted_iota(jnp.int32, (c, c), 0)
    s = lax.broadcasted_iota(jnp.int32, (c, c), 1)
    tri = r >= s
    strict = r > s
    tri_f = tri.astype(F32)
    eye_c = (r == s).astype(F32)
    gc_col = jnp.dot(tri_f, g_col, preferred_element_type=F32, precision=HI)
    eye_h = (lax.broadcasted_iota(jnp.int32, (H, H), 0) == lax.broadcasted_iota(jnp.int32, (H, H), 1)).astype(F32)
    gc_row = lax.dot_general(eye_h, gc_col, (((1,), (1,)), ((), ())), preferred_element_type=F32,
                             precision=HI)

    def bf(a):
        return a.astype(BF16)

    def split(a):
        hi = a.astype(BF16)
        return hi, (a - hi.astype(F32)).astype(BF16)

    def dot3(a, b):
        (ah, al), (bh, bl) = a, b
        return (jnp.dot(al, bh, preferred_element_type=F32) + jnp.dot(ah, bl, preferred_element_type=F32)
                + jnp.dot(ah, bh, preferred_element_type=F32))

    blocks = []
    b_ = 1
    while b_ < c:
        blocks.append((((r // (2 * b_)) == (s // (2 * b_))) & ((r // b_) != (s // b_)) & strict).astype(F32))
        b_ *= 2

    heads = range(G)
    lanes = [slice(hh * GDN_DK, (hh + 1) * GDN_DK) for hh in heads]
    if G == H:
        gcc = [gc_col[:, hh:hh + 1] for hh in heads]
        bet = [beta_col[:, hh:hh + 1] for hh in heads]
        gcr = [gc_row[hh:hh + 1, :] for hh in heads]
    else:
        gcc, bet, gcr = [], [], []
        for hh in heads:
            head = hg * G + hh
            sel_c = (lax.broadcasted_iota(jnp.int32, (c, H), 1) == head).astype(F32)
            sel_r = (lax.broadcasted_iota(jnp.int32, (H, c), 0) == head).astype(F32)
            gcc.append(jnp.sum(gc_col * sel_c, axis=1, keepdims=True))
            bet.append(jnp.sum(beta_col * sel_c, axis=1, keepdims=True))
            gcr.append(jnp.sum(gc_row * sel_r, axis=0, keepdims=True))
    g_last = [g[c - 1:c, :] for g in gcc]

    q, k, v = [], [], []
    for hh in heads:
        qh = conv[0][:, lanes[hh]]
        kh = conv[1][:, lanes[hh]]
        q.append(qh * (lax.rsqrt(jnp.sum(qh * qh, axis=-1, keepdims=True) + NORM_EPS) * (GDN_DK ** -0.5)))
        k.append(kh * lax.rsqrt(jnp.sum(kh * kh, axis=-1, keepdims=True) + NORM_EPS))
        v.append(conv[2][:, lanes[hh]])

    decay = [jnp.exp(jnp.where(tri, gcc[hh] - gcr[hh], -jnp.inf)) for hh in heads]
    kb = [k[hh] * bet[hh] for hh in heads]
    k16 = [bf(k[hh]) for hh in heads]
    mmat = [jnp.where(strict, _nt_dot(bf(kb[hh]), k16[hh]) * decay[hh], 0.0) for hh in heads]
    qk = [jnp.where(tri, _nt_dot(bf(q[hh]), k16[hh]) * decay[hh], 0.0) for hh in heads]

    tinv = [eye_c - mmat[hh] * blocks[0] for hh in heads]
    for lvl in range(1, len(blocks)):
        d16 = [bf(t) for t in tinv]
        x = [jnp.dot(bf(mmat[hh] * blocks[lvl]), d16[hh], preferred_element_type=F32) for hh in heads]
        tinv = [tinv[hh] - jnp.dot(d16[hh], bf(x[hh]), preferred_element_type=F32) for hh in heads]
    m_s = [split(m) for m in mmat]
    t_s = [split(t) for t in tinv]
    res = [(eye_c - tinv[hh]) - dot3(m_s[hh], t_s[hh]) for hh in heads]
    tinv = [tinv[hh] + jnp.dot(t_s[hh][0], bf(res[hh]), preferred_element_type=F32) for hh in heads]

    rhs = [jnp.concatenate([v[hh] * bet[hh], kb[hh] * jnp.exp(gcc[hh])], axis=1) for hh in heads]
    sol = [dot3(split(tinv[hh]), split(rhs[hh])) for hh in heads]
    u = [x_[:, :GDN_DV] for x_ in sol]
    w = [x_[:, GDN_DV:] for x_ in sol]

    S = [s_ref[0, hh] for hh in heads]
    S16 = [bf(x_) for x_ in S]
    v_new = [u[hh] - jnp.dot(bf(w[hh]), S16[hh], preferred_element_type=F32) for hh in heads]
    v16 = [bf(x_) for x_ in v_new]
    o = [jnp.dot(bf(q[hh] * jnp.exp(gcc[hh])), S16[hh], preferred_element_type=F32)
         + jnp.dot(bf(qk[hh]), v16[hh], preferred_element_type=F32) for hh in heads]
    for hh in heads:
        kd = bf(k[hh] * jnp.exp(g_last[hh] - gcc[hh]))
        s_ref[0, hh] = S[hh] * jnp.exp(g_last[hh]) + lax.dot_general(kd, v16[hh], (((0,), (0,)), ((), ())),
                                                                     preferred_element_type=F32)
    for hh in heads:
        ms = jnp.mean(o[hh] * o[hh], axis=-1, keepdims=True)
        y = o[hh] * lax.rsqrt(ms + NORM_EPS) * ng_ref[...] * z_ref[0][:, lanes[hh]].astype(F32)
        o_ref[0, :, lanes[hh]] = y.astype(o_ref.dtype)


def _gdn(gqkv, buf0, conv_w, gab, a_log, dt_bias, z_silu, norm_g, s0, *, c, G=16):
    B, T, _ = gqkv.shape
    H = N_GDN_HEADS
    assert T % c == 0 and c % 8 == 0 and (c & (c - 1)) == 0 and H % G == 0
    nG = H // G
    W = G * GDN_DK

    def stream(t):
        return pl.BlockSpec((1, c, W), lambda b, g, n, t=t: (b, n, t * nG + g))

    def hist(t):
        return pl.BlockSpec((1, CONV_W - 1, W), lambda b, g, n, t=t: (b, 0, t * nG + g))

    def cw(t):
        return pl.BlockSpec((CONV_W, W), lambda b, g, n, t=t: (0, t * nG + g))

    small = pl.BlockSpec((1, H), lambda b, g, n: (0, 0))
    state = pl.BlockSpec((1, G, GDN_DK, GDN_DV), lambda b, g, n: (b, g, 0, 0))
    in_specs = [stream(0), stream(1), stream(2), hist(0), hist(1), hist(2), cw(0), cw(1), cw(2),
                pl.BlockSpec((1, c, 128), lambda b, g, n: (b, n, 0)), small, small,
                pl.BlockSpec((1, c, W), lambda b, g, n: (b, n, g)),
                pl.BlockSpec((1, GDN_DV), lambda b, g, n: (0, 0))]
    args = [gqkv, gqkv, gqkv, buf0, buf0, buf0, conv_w, conv_w, conv_w, gab, a_log, dt_bias, z_silu, norm_g]
    if s0 is not None:
        in_specs.append(state)
        args.append(s0)
    return pl.pallas_call(
        functools.partial(_gdn_kernel, c=c, G=G, has_state=s0 is not None),
        out_shape=(jax.ShapeDtypeStruct((B, T, GDN_WIDTH), BF16),
                   jax.ShapeDtypeStruct((B, H, GDN_DK, GDN_DV), F32)),
        grid=(B, nG, T // c),
        in_specs=in_specs,
        out_specs=(pl.BlockSpec((1, c, W), lambda b, g, n: (b, n, g)), state),
        scratch_shapes=[pltpu.VMEM((3, 8 + c, W), F32)],
        compiler_params=_cparams(("parallel", "parallel", "arbitrary")),
        name="gdn",
    )(*args)


def _mix_kernel(oa_ref, ob_ref, wa_ref, wb_ref, sa_ref, sb_ref, o_ref):
    a = jnp.dot(oa_ref[...], wa_ref[...], preferred_element_type=F32)
    b = jnp.dot(ob_ref[...], wb_ref[...], preferred_element_type=F32)
    o_ref[...] = (sa_ref[...].astype(F32) * a + sb_ref[...].astype(F32) * b).astype(o_ref.dtype)


def _mix(o_a, o_b, w_pa, w_pb, gates, *, tm=512, tn=1024):
    M = o_a.shape[0]
    tm = min(tm, M)
    nb = D_MODEL // tn
    return pl.pallas_call(
        _mix_kernel,
        out_shape=jax.ShapeDtypeStruct((M, D_MODEL), BF16),
        grid=(M // tm, nb),
        in_specs=[pl.BlockSpec((tm, DIFF_WIDTH), lambda i, j: (i, 0)),
                  pl.BlockSpec((tm, GDN_WIDTH), lambda i, j: (i, 0)),
                  pl.BlockSpec((DIFF_WIDTH, tn), lambda i, j: (0, j)),
                  pl.BlockSpec((GDN_WIDTH, tn), lambda i, j: (0, j)),
                  pl.BlockSpec((tm, tn), lambda i, j: (i, j)),
                  pl.BlockSpec((tm, tn), lambda i, j: (i, j + nb))],
        out_specs=pl.BlockSpec((tm, tn), lambda i, j: (i, j)),
        compiler_params=_cparams(("parallel", "parallel")),
        name="mix",
    )(o_a, o_b, w_pa, w_pb, gates, gates)


def _post_kernel(x_ref, mix_ref, wo_ref, g1_ref, b1_ref, wxq_ref, mk_ref, mv_ref, wxo_ref, g2_ref, b2_ref,
                 h2_ref, h2b_ref):
    h1 = ALPHA * x_ref[0] + jnp.dot(mix_ref[0], wo_ref[...], preferred_element_type=F32)
    h1 = _layer_norm(h1, g1_ref[...], b1_ref[...])
    qx = jnp.dot(h1.astype(BF16), wxq_ref[...], preferred_element_type=F32) * (XHEAD_DIM ** -0.5)
    qx = qx.astype(BF16)
    mk = mk_ref[0]
    mv = mv_ref[0]
    heads = []
    for hh in range(N_XHEADS):
        sl = slice(hh * XHEAD_DIM, (hh + 1) * XHEAD_DIM)
        s = _nt_dot(qx[:, sl], mk[:, sl])
        p = jnp.exp(s - jnp.max(s, axis=-1, keepdims=True))
        p = p / jnp.sum(p, axis=-1, keepdims=True)
        heads.append(jnp.dot(p.astype(BF16), mv[:, sl], preferred_element_type=F32))
    ox = jnp.concatenate(heads, axis=1).astype(BF16)
    h2 = ALPHA * h1 + jnp.dot(ox, wxo_ref[...], preferred_element_type=F32)
    h2 = _layer_norm(h2, g2_ref[...], b2_ref[...])
    h2_ref[0] = h2
    h2b_ref[0] = h2.astype(BF16)


def _post(x, mix, w_o, ln1_g, ln1_b, w_xq, mem_k, mem_v, w_xo, ln2_g, ln2_b, *, tm=256):
    B, T, D = x.shape
    tm = min(tm, T)
    const = lambda shape: pl.BlockSpec(shape, lambda b, i: (0, 0))
    rows = lambda: pl.BlockSpec((1, tm, D), lambda b, i: (b, i, 0))
    mem = lambda: pl.BlockSpec((1, N_MEM, XWIDTH), lambda b, i: (b, 0, 0))
    return pl.pallas_call(
        _post_kernel,
        out_shape=(jax.ShapeDtypeStruct((B, T, D), F32), jax.ShapeDtypeStruct((B, T, D), BF16)),
        grid=(B, T // tm),
        in_specs=[rows(), rows(), const((D, D)), const((1, D)), const((1, D)), const((D, XWIDTH)),
                  mem(), mem(), const((XWIDTH, D)), const((1, D)), const((1, D))],
        out_specs=(rows(), rows()),
        compiler_params=_cparams(("parallel", "parallel")),
        name="post_attn",
    )(x, mix, w_o, ln1_g, ln1_b, w_xq, mem_k, mem_v, w_xo, ln2_g, ln2_b)


def _ffn_kernel(hb_ref, h_ref, w1_ref, w3_ref, w2_ref, g_ref, b_ref, y_ref, acc_ref):
    f = pl.program_id(1)

    @pl.when(f == 0)
    def _():
        acc_ref[...] = jnp.zeros(acc_ref.shape, F32)

    hb = hb_ref[...]
    a = jnp.dot(hb, w1_ref[...], preferred_element_type=F32)
    b = jnp.dot(hb, w3_ref[...], preferred_element_type=F32)
    act = (_silu(a) * b).astype(BF16)
    acc_ref[...] += jnp.dot(act, w2_ref[...], preferred_element_type=F32)

    @pl.when(f == pl.num_programs(1) - 1)
    def _():
        y_ref[...] = _layer_norm(ALPHA * h_ref[...] + acc_ref[...], g_ref[...], b_ref[...])


def _ffn(h2b, h2, w1, w3, w2, ln_g, ln_b, *, tm=512, tf=512):
    M, D = h2.shape
    tm = min(tm, M)
    assert M % tm == 0 and D_FF % tf == 0
    return pl.pallas_call(
        _ffn_kernel,
        out_shape=jax.ShapeDtypeStruct((M, D), F32),
        grid=(M // tm, D_FF // tf),
        in_specs=[pl.BlockSpec((tm, D), lambda i, f: (i, 0)),
                  pl.BlockSpec((tm, D), lambda i, f: (i, 0)),
                  pl.BlockSpec((D, tf), lambda i, f: (0, f)),
                  pl.BlockSpec((D, tf), lambda i, f: (0, f)),
                  pl.BlockSpec((tf, D), lambda i, f: (f, 0)),
                  pl.BlockSpec((1, D), lambda i, f: (0, 0)),
                  pl.BlockSpec((1, D), lambda i, f: (0, 0))],
        out_specs=pl.BlockSpec((tm, D), lambda i, f: (i, 0)),
        scratch_shapes=[pltpu.VMEM((tm, D), F32)],
        compiler_params=_cparams(("parallel", "arbitrary")),
        name="ffn",
    )(h2b, h2, w1, w3, w2, ln_g, ln_b)


def _encoder_layer(x, mem_k, mem_v, past, W):
    B, T, D = x.shape
    M = B * T
    xb = x.reshape(M, D).astype(BF16)
    w_in = W["w_in"]

    (dq,) = _proj(xb, w_in, OFF_DQ, DIFF_WIDTH, (BF16,), scale=DIFF_HEAD_DIM ** -0.5, name="proj_dq")
    kv_dtypes = (F32, BF16) if past is None else (F32,)
    dk, *dkb = _proj(xb, w_in, OFF_DK, DIFF_WIDTH, kv_dtypes, name="proj_dk")
    dv, *dvb = _proj(xb, w_in, OFF_DV, DIFF_WIDTH, kv_dtypes, name="proj_dv")
    (gqkv,) = _proj(xb, w_in, OFF_GQKV, 3 * GDN_WIDTH, (F32,), name="proj_gqkv")
    (gz,) = _proj(xb, w_in, OFF_GZ, GDN_WIDTH, (BF16,), act="silu", name="proj_gz")
    (gab,) = _proj(xb, w_in, OFF_GAB, 128, (F32,), tn=128, name="proj_gab")
    (gates,) = _proj(xb, W["w_gates"], 0, 2 * D_MODEL, (BF16,), act="sigmoid", name="proj_gates")

    dq = dq.reshape(B, T, DIFF_WIDTH)
    dk = dk.reshape(B, T, N_DIFF_HEADS, DIFF_VDIM)
    dv = dv.reshape(B, T, N_DIFF_HEADS, DIFF_VDIM)
    gqkv = gqkv.reshape(B, T, 3 * GDN_WIDTH)
    lam_rows = W["lam_rows"]
    if past is None:
        o_a = _diff_attention_prompt(dq, dkb[0].reshape(B, T, DIFF_WIDTH), dvb[0].reshape(B, T, DIFF_WIDTH),
                                     lam_rows, W["diff_subln_g"])
        buf0 = jnp.zeros((B, CONV_W - 1, 3 * GDN_WIDTH), F32)
        s0 = None
        c = CHUNK
    else:
        cache_k, cache_v, s0, buf0 = past
        o_a = _diff_attention_sample(dq, cache_k, cache_v, dk, dv, lam_rows, W["diff_subln_g"])
        c = T
    o_b, s_new = _gdn(gqkv, buf0, W["conv_w"], gab.reshape(B, T, 128), W["gdn_a_log"], W["gdn_dt_bias"],
                      gz.reshape(B, T, GDN_WIDTH), W["gdn_norm_g"], s0, c=c)
    new_buf = gqkv[:, T - (CONV_W - 1):, :]

    mix = _mix(o_a.reshape(M, DIFF_WIDTH), o_b.reshape(M, GDN_WIDTH), W["w_pa"], W["w_pb"], gates)
    h2, h2b = _post(x, mix.reshape(B, T, D), W["w_o"], W["ln1_g"], W["ln1_b"], W["w_xq"], mem_k, mem_v,
                    W["w_xo"], W["ln2_g"], W["ln2_b"])
    y = _ffn(h2b.reshape(M, D), h2.reshape(M, D), W["w_ff1"], W["w_ff3"], W["w_ff2"], W["ln3_g"], W["ln3_b"])
    return y.reshape(B, T, D), dk, dv, s_new, new_buf


def kernel(x_prompt, x_sample, mem_prompt, cache_diff_k, cache_diff_v, state_gdn, state_gdn_conv, cache_mem_k, cache_mem_v, w_in, conv_w, lam_q1, lam_k1, lam_q2, lam_k2, diff_subln_g, gdn_a_log, gdn_dt_bias, gdn_norm_g, w_pa, w_pb, w_o, ln1_g, ln1_b, w_xq, w_xk, w_xv, w_xo, ln2_g, ln2_b, w_ff1, w_ff3, w_ff2, ln3_g, ln3_b):
    l = 0
    w_in_b = w_in[l].astype(BF16)
    W = {
        "w_in": w_in_b,
        "w_gates": w_in_b[:, OFF_GATES:],
        "conv_w": conv_w[l],
        "lam_rows": tuple(v[l].reshape(1, DIFF_HEAD_DIM) for v in (lam_q1, lam_k1, lam_q2, lam_k2)),
        "diff_subln_g": diff_subln_g[l].reshape(1, DIFF_VDIM),
        "gdn_a_log": gdn_a_log[l].reshape(1, N_GDN_HEADS),
        "gdn_dt_bias": gdn_dt_bias[l].reshape(1, N_GDN_HEADS),
        "gdn_norm_g": gdn_norm_g[l].reshape(1, GDN_DV),
        "w_pa": w_pa[l].astype(BF16), "w_pb": w_pb[l].astype(BF16), "w_o": w_o[l].astype(BF16),
        "ln1_g": ln1_g[l].reshape(1, D_MODEL), "ln1_b": ln1_b[l].reshape(1, D_MODEL),
        "w_xq": w_xq[l].astype(BF16), "w_xo": w_xo[l].astype(BF16),
        "ln2_g": ln2_g[l].reshape(1, D_MODEL), "ln2_b": ln2_b[l].reshape(1, D_MODEL),
        "w_ff1": w_ff1[l].astype(BF16), "w_ff3": w_ff3[l].astype(BF16), "w_ff2": w_ff2[l].astype(BF16),
        "ln3_g": ln3_g[l].reshape(1, D_MODEL), "ln3_b": ln3_b[l].reshape(1, D_MODEL),
    }
    Bp = x_prompt.shape[0]
    memb = mem_prompt.reshape(Bp * N_MEM, D_MODEL).astype(BF16)
    mem_k, mem_kb = _proj(memb, w_xk[l].astype(BF16), 0, XWIDTH, (F32, BF16), tn=XWIDTH, name="proj_mem_k")
    mem_v, mem_vb = _proj(memb, w_xv[l].astype(BF16), 0, XWIDTH, (F32, BF16), tn=XWIDTH, name="proj_mem_v")

    yp, pk, pv, ps, pc = _encoder_layer(x_prompt, mem_kb.reshape(Bp, N_MEM, XWIDTH),
                                        mem_vb.reshape(Bp, N_MEM, XWIDTH), None, W)
    Bs = x_sample.shape[0]
    past = (cache_diff_k[l], cache_diff_v[l], state_gdn[l], state_gdn_conv[l])
    ys, sk, sv, ss, sc = _encoder_layer(x_sample, cache_mem_k[l].reshape(Bs, N_MEM, XWIDTH).astype(BF16),
                                        cache_mem_v[l].reshape(Bs, N_MEM, XWIDTH).astype(BF16), past, W)
    st = lambda a: a[None]
    return (yp, ys, st(pk), st(pv), st(ps), st(pc),
            st(mem_k.reshape(Bp, N_MEM, N_XHEADS, XHEAD_DIM)), st(mem_v.reshape(Bp, N_MEM, N_XHEADS, XHEAD_DIM)),
            st(sk), st(sv), st(ss), st(sc))
```

```python
import functools
import math

import jax
import jax.numpy as jnp
from jax import lax
from jax.experimental import pallas as pl
from jax.experimental.pallas import tpu as pltpu

D_MODEL = 2048
CHUNK = 64
N_DIFF_HEADS = 8
DIFF_HEAD_DIM = 128
DIFF_VDIM = 2 * DIFF_HEAD_DIM
DIFF_WIDTH = N_DIFF_HEADS * DIFF_VDIM
N_GDN_HEADS = 16
GDN_DK = 128
GDN_DV = 128
GDN_WIDTH = N_GDN_HEADS * GDN_DK
CONV_W = 4
N_MEM = 256
N_XHEADS = 4
XHEAD_DIM = 128
XWIDTH = N_XHEADS * XHEAD_DIM
D_FF = 5632
DEPTH = 1
ALPHA = (2.0 * DEPTH) ** 0.25
LN_EPS = 1e-5
NORM_EPS = 1e-6
LAM_INIT = 0.8 - 0.6 * math.exp(-0.3 * 0)

OFF_DQ = 0
OFF_DK = 2048
OFF_DV = 4096
OFF_GQKV = 6144
OFF_GZ = 12288
OFF_GAB = 14336
OFF_GATES = 14368

VMEM_LIMIT = 56 * 1024 * 1024
BF16 = jnp.bfloat16
F32 = jnp.float32
HI = lax.Precision.HIGHEST


def _cparams(sem):
    return pltpu.CompilerParams(dimension_semantics=sem, vmem_limit_bytes=VMEM_LIMIT)


def _sigmoid(x):
    return 1.0 / (1.0 + jnp.exp(-x))


def _silu(x):
    return x * _sigmoid(x)


def _layer_norm(x, g, b):
    mu = jnp.mean(x, axis=-1, keepdims=True)
    xc = x - mu
    var = jnp.mean(xc * xc, axis=-1, keepdims=True)
    return xc * lax.rsqrt(var + LN_EPS) * g + b


def _proj_kernel(x_ref, w_ref, *o_refs, act, scale):
    acc = jnp.dot(x_ref[...], w_ref[...], preferred_element_type=F32)
    if scale != 1.0:
        acc = acc * scale
    if act == "sigmoid":
        acc = _sigmoid(acc)
    elif act == "silu":
        acc = _silu(acc)
    for o in o_refs:
        o[...] = acc.astype(o.dtype)


def _proj(x, w, col_off, n_cols, out_dtypes, *, act=None, scale=1.0, tm=1024, tn=1024, name="proj"):
    M, K = x.shape
    tm = min(tm, M)
    tn = min(tn, n_cols)
    assert M % tm == 0 and n_cols % tn == 0 and col_off % tn == 0
    cb = col_off // tn
    outs = pl.pallas_call(
        functools.partial(_proj_kernel, act=act, scale=scale),
        out_shape=tuple(jax.ShapeDtypeStruct((M, n_cols), dt) for dt in out_dtypes),
        grid=(M // tm, n_cols // tn),
        in_specs=[pl.BlockSpec((tm, K), lambda i, j: (i, 0)),
                  pl.BlockSpec((K, tn), lambda i, j: (0, j + cb))],
        out_specs=tuple(pl.BlockSpec((tm, tn), lambda i, j: (i, j)) for _ in out_dtypes),
        compiler_params=_cparams(("parallel", "parallel")),
        name=name,
    )(x, w)
    return outs


CONV_PAD = 8


def _proj_conv_kernel(x_ref, w_ref, hist_ref, cw_ref, y_ref, raw_ref, pad_ref, *, nb, T, sub, tr_max, norm, scale):
    tn = w_ref.shape[1]
    tr = min(T, tr_max) if nb == 1 else T
    units = [(s, r) for s in range(tn // sub) for r in range(T // tr)]

    def rows_of(r):
        return slice(r * tr, (r + 1) * tr) if nb == 1 else slice(None)

    def matmul(s, r, gate):
        rows = rows_of(r)
        if gate is None:
            lhs = x_ref[rows, :]
        else:
            n_rows = nb * tr
            first = x_ref[rows, 0:256] + jnp.concatenate([jnp.concatenate([gate] * 2, axis=1)] * (n_rows // 16), axis=0)
            lhs = jnp.concatenate([first, x_ref[rows, 256:]], axis=1)
        return jnp.dot(lhs, w_ref[:, s * sub:(s + 1) * sub], preferred_element_type=F32)

    def epilogue(s, r, acc):
        cols = slice(s * sub, (s + 1) * sub)
        slot = s % 2
        r0 = r * tr
        a3 = acc.reshape(nb, tr, sub)
        pad_ref[slot, :, CONV_PAD + r0:CONV_PAD + r0 + tr, :] = a3
        if r == 0:
            pad_ref[slot, :, CONV_PAD - (CONV_W - 1):CONV_PAD, :] = hist_ref[:, :, cols]
        if r == T // tr - 1:
            raw_ref[:, :, cols] = a3[:, tr - CONV_PAD:, :]
        y = cw_ref[CONV_W - 1:CONV_W, cols] * a3
        for j in range(CONV_W - 1):
            lo = CONV_PAD - (CONV_W - 1) + j + r0
            y = y + cw_ref[j:j + 1, cols] * pad_ref[slot, :, lo:lo + tr, :]
        y = _silu(y)
        if norm:
            heads = []
            for hh in range(sub // GDN_DK):
                yh = y[:, :, hh * GDN_DK:(hh + 1) * GDN_DK]
                heads.append(yh * (lax.rsqrt(jnp.sum(yh * yh, axis=-1, keepdims=True) + NORM_EPS) * scale))
            y = jnp.concatenate(heads, axis=-1)
        y2 = y.reshape(nb * tr, sub)
        y_ref[rows_of(r), cols] = y2.astype(y_ref.dtype)
        bits = pltpu.bitcast(y2[nb * tr - 16:, 0:128], jnp.uint32)
        zero = lax.shift_right_logical(lax.shift_right_logical(bits, jnp.uint32(16)), jnp.uint32(16))
        return zero.astype(F32).astype(BF16)

    gates = [None, None]
    acc = matmul(*units[0], None)
    for i, u in enumerate(units):
        nxt = matmul(*units[i + 1], gates[i + 1]) if i + 1 < len(units) else None
        gates.append(epilogue(*u, acc))
        acc = nxt


def _proj_conv(x, w, col_off, hist, conv_w, *, B, T, norm, scale=1.0, tn=1024, sub=512, tr=512, name="proj_conv"):
    M, K = x.shape
    n_cols = GDN_WIDTH
    nb = max(1, 256 // T) if T < 256 else 1
    nb = min(nb, B)
    tm = nb * T
    gq = OFF_GQKV
    assert B % nb == 0 and n_cols % tn == 0 and col_off % tn == 0 and (col_off - gq) % tn == 0 and T >= CONV_PAD
    cb = col_off // tn
    hb = (col_off - gq) // tn
    return pl.pallas_call(
        functools.partial(_proj_conv_kernel, nb=nb, T=T, sub=sub, tr_max=tr, norm=norm, scale=scale),
        out_shape=(jax.ShapeDtypeStruct((M, n_cols), BF16), jax.ShapeDtypeStruct((B, CONV_PAD, n_cols), F32)),
        grid=(B // nb, n_cols // tn),
        in_specs=[pl.BlockSpec((tm, K), lambda i, j: (i, 0)),
                  pl.BlockSpec((K, tn), lambda i, j: (0, j + cb)),
                  pl.BlockSpec((nb, CONV_W - 1, tn), lambda i, j: (i, 0, j + hb)),
                  pl.BlockSpec((CONV_W, tn), lambda i, j: (0, j + hb))],
        out_specs=(pl.BlockSpec((tm, tn), lambda i, j: (i, j)),
                   pl.BlockSpec((nb, CONV_PAD, tn), lambda i, j: (i, 0, j))),
        scratch_shapes=[pltpu.VMEM((2, nb, T + CONV_PAD, sub), F32)],
        compiler_params=_cparams(("parallel", "parallel")),
        name=name,
    )(x, w, hist, conv_w)


def _lam_value(lq1, lk1, lq2, lk2):
    a = jnp.sum(lq1 * lk1, axis=-1, keepdims=True)
    b = jnp.sum(lq2 * lk2, axis=-1, keepdims=True)
    return jnp.exp(a) - jnp.exp(b) + LAM_INIT


def _subln(o, g):
    ms = jnp.mean(o * o, axis=-1, keepdims=True)
    return o * lax.rsqrt(ms + NORM_EPS) * g * (1.0 - LAM_INIT)


def _head_slope(h):
    e = (h + 1).astype(F32) * (-8.0 / N_DIFF_HEADS)
    return jnp.exp2(jnp.full((1, 1), e, F32))


def _nt_dot(a, b):
    return lax.dot_general(a, b, (((1,), (1,)), ((), ())), preferred_element_type=F32)


def _lane_tile(x, n):
    return x if n == 1 else jnp.concatenate([x] * n, axis=1)


def _diff_prompt_kernel(lq1_ref, lk1_ref, lq2_ref, lk2_ref, g_ref, q_ref, k_ref, v_ref, o_ref,
                        m_ref, l_ref, acc_ref, *, tq):
    LANES = 128
    h = pl.program_id(1)
    qi = pl.program_id(2)
    slope = _head_slope(h)
    lam = _lam_value(lq1_ref[...], lk1_ref[...], lq2_ref[...], lk2_ref[...])

    q = q_ref[0]
    qs = (q[:, :DIFF_HEAD_DIM], q[:, DIFF_HEAD_DIM:])
    col_bias = slope * lax.broadcasted_iota(jnp.int32, (1, tq), 1).astype(F32)

    m_ref[...] = jnp.full(m_ref.shape, -jnp.inf, F32)
    l_ref[...] = jnp.zeros(l_ref.shape, F32)
    acc_ref[...] = jnp.zeros(acc_ref.shape, F32)

    def update(kblk, vblk, bias, shift):
        maps = range(2)
        t = [_nt_dot(qs[m], kblk[:, m * DIFF_HEAD_DIM:(m + 1) * DIFF_HEAD_DIM]) + bias for m in maps]
        m_old = [m_ref[m] for m in maps]
        m_new = [jnp.maximum(m_old[m], jnp.max(t[m], axis=-1, keepdims=True) + shift) for m in maps]
        a = [jnp.exp(m_old[m] - m_new[m]) for m in maps]
        p = [jnp.exp(t[m] - _lane_tile(m_new[m] - shift, tq // LANES)) for m in maps]
        pv = [jnp.dot(p[m].astype(BF16), vblk, preferred_element_type=F32) for m in maps]
        for m in maps:
            psum = p[m][:, :LANES]
            for i in range(1, tq // LANES):
                psum = psum + p[m][:, i * LANES:(i + 1) * LANES]
            l_ref[m] = a[m] * l_ref[m] + psum
            acc_ref[m] = _lane_tile(a[m], DIFF_VDIM // LANES) * acc_ref[m] + pv[m]
            m_ref[m] = m_new[m]

    def body(j, carry):
        start = pl.multiple_of(j * tq, tq)
        gap = ((qi - j) * tq).astype(F32)
        update(k_ref[0, pl.ds(start, tq), :], v_ref[0, pl.ds(start, tq), :], col_bias, -slope * gap)
        return carry

    lax.fori_loop(0, qi, body, 0)

    rows = lax.broadcasted_iota(jnp.int32, (tq, tq), 0)
    cols = lax.broadcasted_iota(jnp.int32, (tq, tq), 1)
    allowed = (cols // CHUNK) <= (rows // CHUNK)
    dist = jnp.abs(rows - cols).astype(F32)
    bias_diag = jnp.where(allowed, slope * (rows.astype(F32) - dist), -jnp.inf)
    start = pl.multiple_of(qi * tq, tq)
    update(k_ref[0, pl.ds(start, tq), :], v_ref[0, pl.ds(start, tq), :], bias_diag, jnp.zeros((1, 1), F32))

    outs = []
    for m in range(2):
        inv = 1.0 / jnp.sum(l_ref[m], axis=-1, keepdims=True)
        outs.append(acc_ref[m] * inv)
    o = outs[0] - lam * outs[1]
    o_ref[0] = _subln(o, g_ref[...]).astype(o_ref.dtype)


def _diff_attention_prompt(q, k, v, lam_rows, subln_g, *, tq=512):
    B, T, _ = q.shape
    tq = min(tq, T)
    assert T % tq == 0 and tq % CHUNK == 0
    vec = pl.BlockSpec((1, DIFF_HEAD_DIM), lambda b, h, i: (0, 0))
    return pl.pallas_call(
        functools.partial(_diff_prompt_kernel, tq=tq),
        out_shape=jax.ShapeDtypeStruct((B, T, DIFF_WIDTH), BF16),
        grid=(B, N_DIFF_HEADS, T // tq),
        in_specs=[vec, vec, vec, vec,
                  pl.BlockSpec((1, DIFF_VDIM), lambda b, h, i: (0, 0)),
                  pl.BlockSpec((1, tq, DIFF_VDIM), lambda b, h, i: (b, i, h)),
                  pl.BlockSpec((1, T, DIFF_VDIM), lambda b, h, i: (b, 0, h)),
                  pl.BlockSpec((1, T, DIFF_VDIM), lambda b, h, i: (b, 0, h))],
        out_specs=pl.BlockSpec((1, tq, DIFF_VDIM), lambda b, h, i: (b, i, h)),
        scratch_shapes=[pltpu.VMEM((2, tq, 128), F32), pltpu.VMEM((2, tq, 128), F32),
                        pltpu.VMEM((2, tq, DIFF_VDIM), F32)],
        compiler_params=_cparams(("parallel", "parallel", "parallel")),
        name="diff_attn_prompt",
    )(*lam_rows, subln_g, q, k, v)


def _diff_sample_kernel(lq1_ref, lk1_ref, lq2_ref, lk2_ref, g_ref, q_ref, kc_ref, vc_ref, kn_ref, vn_ref,
                        o_ref, *, past, pc):
    H = N_DIFF_HEADS
    T = q_ref.shape[1]
    L = H * T
    rows = pc * H
    n_chunks = past // pc
    lam = _lam_value(lq1_ref[...], lk1_ref[...], lq2_ref[...], lk2_ref[...])
    maps = range(2)

    def map_cols(m):
        return slice(m * DIFF_HEAD_DIM, (m + 1) * DIFF_HEAD_DIM)

    qm = [jnp.concatenate([q_ref[0, :, h * DIFF_VDIM + m * DIFF_HEAD_DIM:h * DIFF_VDIM + (m + 1) * DIFF_HEAD_DIM]
                           for h in range(H)], axis=0) for m in maps]
    lane = lax.broadcasted_iota(jnp.int32, (1, L), 1)
    h_lane = lane // T
    t_lane = lane % T
    slope = jnp.exp2((h_lane + 1).astype(F32) * (-8.0 / H))
    base = slope * (lax.broadcasted_iota(jnp.int32, (rows, L), 0) // H).astype(F32)
    valid = lax.broadcasted_iota(jnp.int32, (H, L), 0) == h_lane

    def chunk_shift(c):
        return slope * ((c * pc - past) - t_lane).astype(F32)

    def chunk_scores(c):
        r0 = pl.multiple_of(c * rows, rows)
        kblk = kc_ref[0, pl.ds(r0, rows), :].astype(BF16)
        return [(_nt_dot(kblk[:, map_cols(m)], qm[m]) + base).reshape(pc, H, L) for m in maps]

    def stats_step(c, carry):
        shift = chunk_shift(c)
        s = chunk_scores(c)
        out = []
        for m in maps:
            m_old, l_old = carry[2 * m], carry[2 * m + 1]
            m_new = jnp.maximum(m_old, jnp.max(s[m], axis=0) + shift)
            l_new = l_old * jnp.exp(m_old - m_new) + jnp.sum(jnp.exp(s[m] - (m_new - shift)[None]), axis=0)
            out += [m_new, l_new]
        return tuple(out)

    init = (jnp.full((H, L), -jnp.inf, F32), jnp.zeros((H, L), F32)) * 2
    stats = lax.fori_loop(0, n_chunks, stats_step, init)

    kn = kn_ref[0].astype(BF16)
    vn = vn_ref[0].astype(BF16)
    tk = lax.broadcasted_iota(jnp.int32, (T * H, L), 0) // H
    allowed = ((past + tk) // CHUNK) <= ((past + t_lane) // CHUNK)
    bias_n = jnp.where(allowed, -slope * jnp.abs(tk - t_lane).astype(F32), -jnp.inf)
    s_n = [(_nt_dot(kn[:, map_cols(m)], qm[m]) + bias_n).reshape(T, H, L) for m in maps]
    m_fin, coef = [], []
    for m in maps:
        m_run, l_run = stats[2 * m], stats[2 * m + 1]
        mf = jnp.maximum(m_run, jnp.max(s_n[m], axis=0))
        lf = l_run * jnp.exp(m_run - mf) + jnp.sum(jnp.exp(s_n[m] - mf[None]), axis=0)
        m_fin.append(mf)
        coef.append(jnp.where(valid, (1.0 if m == 0 else -lam) / lf, 0.0))

    def tn_dot(w, v):
        return lax.dot_general(w, v, (((0,), (0,)), ((), ())), preferred_element_type=F32)

    def out_step(c, acc):
        shift = chunk_shift(c)
        s = chunk_scores(c)
        r0 = pl.multiple_of(c * rows, rows)
        vblk = vc_ref[0, pl.ds(r0, rows), :].astype(BF16)
        w = sum(jnp.exp(s[m] - (m_fin[m] - shift)[None]) * coef[m][None] for m in maps)
        return acc + tn_dot(w.reshape(rows, L).astype(BF16), vblk)

    acc = lax.fori_loop(0, n_chunks, out_step, jnp.zeros((L, DIFF_VDIM), F32))
    w_n = sum(jnp.exp(s_n[m] - m_fin[m][None]) * coef[m][None] for m in maps)
    acc = acc + tn_dot(w_n.reshape(T * H, L).astype(BF16), vn)
    o = _subln(acc, g_ref[...]).astype(o_ref.dtype)
    for h in range(H):
        o_ref[0, :, h * DIFF_VDIM:(h + 1) * DIFF_VDIM] = o[h * T:(h + 1) * T]


def _diff_attention_sample(q, cache_k, cache_v, k_new, v_new, lam_rows, subln_g):
    B, T, _ = q.shape
    P = cache_k.shape[1]
    H = N_DIFF_HEADS
    pc = min(P, 128)
    assert H * T == 128 and P > 0 and P % pc == 0
    vec = pl.BlockSpec((1, DIFF_HEAD_DIM), lambda b: (0, 0))
    tok = pl.BlockSpec((1, T, DIFF_WIDTH), lambda b: (b, 0, 0))
    new = pl.BlockSpec((1, T * H, DIFF_VDIM), lambda b: (b, 0, 0))
    old = pl.BlockSpec((1, P * H, DIFF_VDIM), lambda b: (b, 0, 0))
    flat = lambda a: a.reshape(B, a.shape[1] * H, DIFF_VDIM)
    return pl.pallas_call(
        functools.partial(_diff_sample_kernel, past=P, pc=pc),
        out_shape=jax.ShapeDtypeStruct((B, T, DIFF_WIDTH), BF16),
        grid=(B,),
        in_specs=[vec, vec, vec, vec, pl.BlockSpec((1, DIFF_VDIM), lambda b: (0, 0)),
                  tok, old, old, new, new],
        out_specs=tok,
        compiler_params=_cparams(("parallel",)),
        name="diff_attn_sample",
    )(*lam_rows, subln_g, q, flat(cache_k), flat(cache_v), flat(k_new), flat(v_new))


def _gdn_kernel(*refs, c, G, has_state):
    if has_state:
        (xq_ref, xk_ref, xv_ref, ab_ref, alog_ref, dtb_ref, z_ref, ng_ref, s0_ref, o_ref, s_ref) = refs
    else:
        (xq_ref, xk_ref, xv_ref, ab_ref, alog_ref, dtb_ref, z_ref, ng_ref, o_ref, s_ref) = refs
        s0_ref = None
    hg = pl.program_id(1)
    n = pl.program_id(2)
    H = N_GDN_HEADS

    @pl.when(n == 0)
    def _init():
        if has_state:
            s_ref[0] = s0_ref[0]
        else:
            s_ref[...] = jnp.zeros(s_ref.shape, F32)

    conv = [xq_ref[0], xk_ref[0], xv_ref[0]]

    ab = ab_ref[0]
    a_in = ab[:, 0:H] + dtb_ref[...]
    softplus = jnp.maximum(a_in, 0.0) + jnp.log1p(jnp.exp(-jnp.abs(a_in)))
    g_col = -jnp.exp(alog_ref[...]) * softplus
    beta_col = _sigmoid(ab[:, H:2 * H])
    r = lax.broadcasted_iota(jnp.int32, (c, c), 0)
    s = lax.broadcasted_iota(jnp.int32, (c, c), 1)
    tri = r >= s
    strict = r > s
    tri_f = tri.astype(F32)
    eye_c = (r == s).astype(F32)
    gc_col = jnp.dot(tri_f, g_col, preferred_element_type=F32, precision=HI)
    eye_h = (lax.broadcasted_iota(jnp.int32, (H, H), 0) == lax.broadcasted_iota(jnp.int32, (H, H), 1)).astype(F32)
    gc_row = lax.dot_general(eye_h, gc_col, (((1,), (1,)), ((), ())), preferred_element_type=F32,
                             precision=HI)

    def bf(a):
        return a.astype(BF16)

    def split(a):
        hi = a.astype(BF16)
        return hi, (a - hi.astype(F32)).astype(BF16)

    def dot3(a, b):
        (ah, al), (bh, bl) = a, b
        return (jnp.dot(al, bh, preferred_element_type=F32) + jnp.dot(ah, bl, preferred_element_type=F32)
                + jnp.dot(ah, bh, preferred_element_type=F32))

    blocks = []
    b_ = 1
    while b_ < c:
        blocks.append((((r // (2 * b_)) == (s // (2 * b_))) & ((r // b_) != (s // b_)) & strict).astype(F32))
        b_ *= 2

    heads = range(G)
    lanes = [slice(hh * GDN_DK, (hh + 1) * GDN_DK) for hh in heads]
    if G == H:
        gcc = [gc_col[:, hh:hh + 1] for hh in heads]
        bet = [beta_col[:, hh:hh + 1] for hh in heads]
        gcr = [gc_row[hh:hh + 1, :] for hh in heads]
    else:
        gcc, bet, gcr = [], [], []
        for hh in heads:
            head = hg * G + hh
            sel_c = (lax.broadcasted_iota(jnp.int32, (c, H), 1) == head).astype(F32)
            sel_r = (lax.broadcasted_iota(jnp.int32, (H, c), 0) == head).astype(F32)
            gcc.append(jnp.sum(gc_col * sel_c, axis=1, keepdims=True))
            bet.append(jnp.sum(beta_col * sel_c, axis=1, keepdims=True))
            gcr.append(jnp.sum(gc_row * sel_r, axis=0, keepdims=True))
    g_last = [g[c - 1:c, :] for g in gcc]

    q = [conv[0][:, lanes[hh]].astype(F32) for hh in heads]
    k = [conv[1][:, lanes[hh]].astype(F32) for hh in heads]
    v = [conv[2][:, lanes[hh]].astype(F32) for hh in heads]

    decay = [jnp.exp(jnp.where(tri, gcc[hh] - gcr[hh], -jnp.inf)) for hh in heads]
    kb = [k[hh] * bet[hh] for hh in heads]
    k16 = [bf(k[hh]) for hh in heads]
    mmat = [jnp.where(strict, _nt_dot(bf(kb[hh]), k16[hh]) * decay[hh], 0.0) for hh in heads]
    qk = [jnp.where(tri, _nt_dot(bf(q[hh]), k16[hh]) * decay[hh], 0.0) for hh in heads]

    tinv = [eye_c - mmat[hh] * blocks[0] for hh in heads]
    for lvl in range(1, len(blocks)):
        d16 = [bf(t) for t in tinv]
        x = [jnp.dot(bf(mmat[hh] * blocks[lvl]), d16[hh], preferred_element_type=F32) for hh in heads]
        tinv = [tinv[hh] - jnp.dot(d16[hh], bf(x[hh]), preferred_element_type=F32) for hh in heads]
    m_s = [split(m) for m in mmat]
    t_s = [split(t) for t in tinv]
    res = [(eye_c - tinv[hh]) - dot3(m_s[hh], t_s[hh]) for hh in heads]
    tinv = [tinv[hh] + jnp.dot(t_s[hh][0], bf(res[hh]), preferred_element_type=F32) for hh in heads]

    rhs = [jnp.concatenate([v[hh] * bet[hh], kb[hh] * jnp.exp(gcc[hh])], axis=1) for hh in heads]
    sol = [dot3(split(tinv[hh]), split(rhs[hh])) for hh in heads]
    u = [x_[:, :GDN_DV] for x_ in sol]
    w = [x_[:, GDN_DV:] for x_ in sol]

    S = [s_ref[0, hh] for hh in heads]
    S16 = [bf(x_) for x_ in S]
    v_new = [u[hh] - jnp.dot(bf(w[hh]), S16[hh], preferred_element_type=F32) for hh in heads]
    v16 = [bf(x_) for x_ in v_new]
    o = [jnp.dot(bf(q[hh] * jnp.exp(gcc[hh])), S16[hh], preferred_element_type=F32)
         + jnp.dot(bf(qk[hh]), v16[hh], preferred_element_type=F32) for hh in heads]
    for hh in heads:
        kd = bf(k[hh] * jnp.exp(g_last[hh] - gcc[hh]))
        s_ref[0, hh] = S[hh] * jnp.exp(g_last[hh]) + lax.dot_general(kd, v16[hh], (((0,), (0,)), ((), ())),
                                                                     preferred_element_type=F32)
    for hh in heads:
        ms = jnp.mean(o[hh] * o[hh], axis=-1, keepdims=True)
        y = o[hh] * lax.rsqrt(ms + NORM_EPS) * ng_ref[...] * z_ref[0][:, lanes[hh]].astype(F32)
        o_ref[0, :, lanes[hh]] = y.astype(o_ref.dtype)


def _gdn(q, k, v, gab, a_log, dt_bias, z_silu, norm_g, s0, *, c, G=16):
    B, T, _ = q.shape
    H = N_GDN_HEADS
    assert T % c == 0 and c % 8 == 0 and (c & (c - 1)) == 0 and H % G == 0
    nG = H // G
    W = G * GDN_DK
    stream = pl.BlockSpec((1, c, W), lambda b, g, n: (b, n, g))
    small = pl.BlockSpec((1, H), lambda b, g, n: (0, 0))
    state = pl.BlockSpec((1, G, GDN_DK, GDN_DV), lambda b, g, n: (b, g, 0, 0))
    in_specs = [stream, stream, stream,
                pl.BlockSpec((1, c, 128), lambda b, g, n: (b, n, 0)), small, small,
                stream, pl.BlockSpec((1, GDN_DV), lambda b, g, n: (0, 0))]
    args = [q, k, v, gab, a_log, dt_bias, z_silu, norm_g]
    if s0 is not None:
        in_specs.append(state)
        args.append(s0)
    return pl.pallas_call(
        functools.partial(_gdn_kernel, c=c, G=G, has_state=s0 is not None),
        out_shape=(jax.ShapeDtypeStruct((B, T, GDN_WIDTH), BF16),
                   jax.ShapeDtypeStruct((B, H, GDN_DK, GDN_DV), F32)),
        grid=(B, nG, T // c),
        in_specs=in_specs,
        out_specs=(stream, state),
        compiler_params=_cparams(("parallel", "parallel", "arbitrary")),
        name="gdn",
    )(*args)


def _mix_kernel(oa_ref, ob_ref, wa_ref, wb_ref, sa_ref, sb_ref, o_ref):
    a = jnp.dot(oa_ref[...], wa_ref[...], preferred_element_type=F32)
    b = jnp.dot(ob_ref[...], wb_ref[...], preferred_element_type=F32)
    o_ref[...] = (sa_ref[...].astype(F32) * a + sb_ref[...].astype(F32) * b).astype(o_ref.dtype)


def _mix(o_a, o_b, w_pa, w_pb, gates, *, tm=512, tn=1024):
    M = o_a.shape[0]
    tm = min(tm, M)
    nb = D_MODEL // tn
    return pl.pallas_call(
        _mix_kernel,
        out_shape=jax.ShapeDtypeStruct((M, D_MODEL), BF16),
        grid=(M // tm, nb),
        in_specs=[pl.BlockSpec((tm, DIFF_WIDTH), lambda i, j: (i, 0)),
                  pl.BlockSpec((tm, GDN_WIDTH), lambda i, j: (i, 0)),
                  pl.BlockSpec((DIFF_WIDTH, tn), lambda i, j: (0, j)),
                  pl.BlockSpec((GDN_WIDTH, tn), lambda i, j: (0, j)),
                  pl.BlockSpec((tm, tn), lambda i, j: (i, j)),
                  pl.BlockSpec((tm, tn), lambda i, j: (i, j + nb))],
        out_specs=pl.BlockSpec((tm, tn), lambda i, j: (i, j)),
        compiler_params=_cparams(("parallel", "parallel")),
        name="mix",
    )(o_a, o_b, w_pa, w_pb, gates, gates)


def _post_kernel(x_ref, mix_ref, wo_ref, g1_ref, b1_ref, wxq_ref, mk_ref, mv_ref, wxo_ref, g2_ref, b2_ref,
                 h2_ref, h2b_ref):
    h1 = ALPHA * x_ref[0] + jnp.dot(mix_ref[0], wo_ref[...], preferred_element_type=F32)
    h1 = _layer_norm(h1, g1_ref[...], b1_ref[...])
    qx = jnp.dot(h1.astype(BF16), wxq_ref[...], preferred_element_type=F32) * (XHEAD_DIM ** -0.5)
    qx = qx.astype(BF16)
    mk = mk_ref[0]
    mv = mv_ref[0]
    heads = []
    for hh in range(N_XHEADS):
        sl = slice(hh * XHEAD_DIM, (hh + 1) * XHEAD_DIM)
        s = _nt_dot(qx[:, sl], mk[:, sl])
        p = jnp.exp(s - jnp.max(s, axis=-1, keepdims=True))
        p = p / jnp.sum(p, axis=-1, keepdims=True)
        heads.append(jnp.dot(p.astype(BF16), mv[:, sl], preferred_element_type=F32))
    ox = jnp.concatenate(heads, axis=1).astype(BF16)
    h2 = ALPHA * h1 + jnp.dot(ox, wxo_ref[...], preferred_element_type=F32)
    h2 = _layer_norm(h2, g2_ref[...], b2_ref[...])
    h2_ref[0] = h2
    h2b_ref[0] = h2.astype(BF16)


def _post(x, mix, w_o, ln1_g, ln1_b, w_xq, mem_k, mem_v, w_xo, ln2_g, ln2_b, *, tm=256):
    B, T, D = x.shape
    tm = min(tm, T)
    const = lambda shape: pl.BlockSpec(shape, lambda b, i: (0, 0))
    rows = lambda: pl.BlockSpec((1, tm, D), lambda b, i: (b, i, 0))
    mem = lambda: pl.BlockSpec((1, N_MEM, XWIDTH), lambda b, i: (b, 0, 0))
    return pl.pallas_call(
        _post_kernel,
        out_shape=(jax.ShapeDtypeStruct((B, T, D), F32), jax.ShapeDtypeStruct((B, T, D), BF16)),
        grid=(B, T // tm),
        in_specs=[rows(), rows(), const((D, D)), const((1, D)), const((1, D)), const((D, XWIDTH)),
                  mem(), mem(), const((XWIDTH, D)), const((1, D)), const((1, D))],
        out_specs=(rows(), rows()),
        compiler_params=_cparams(("parallel", "parallel")),
        name="post_attn",
    )(x, mix, w_o, ln1_g, ln1_b, w_xq, mem_k, mem_v, w_xo, ln2_g, ln2_b)


def _ffn_kernel(hb_ref, h_ref, w1_ref, w3_ref, w2_ref, g_ref, b_ref, y_ref, acc_ref):
    f = pl.program_id(1)

    @pl.when(f == 0)
    def _():
        acc_ref[...] = jnp.zeros(acc_ref.shape, F32)

    hb = hb_ref[...]
    a = jnp.dot(hb, w1_ref[...], preferred_element_type=F32)
    b = jnp.dot(hb, w3_ref[...], preferred_element_type=F32)
    act = (_silu(a) * b).astype(BF16)
    acc_ref[...] += jnp.dot(act, w2_ref[...], preferred_element_type=F32)

    @pl.when(f == pl.num_programs(1) - 1)
    def _():
        y_ref[...] = _layer_norm(ALPHA * h_ref[...] + acc_ref[...], g_ref[...], b_ref[...])


def _ffn(h2b, h2, w1, w3, w2, ln_g, ln_b, *, tm=512, tf=512):
    M, D = h2.shape
    tm = min(tm, M)
    assert M % tm == 0 and D_FF % tf == 0
    return pl.pallas_call(
        _ffn_kernel,
        out_shape=jax.ShapeDtypeStruct((M, D), F32),
        grid=(M // tm, D_FF // tf),
        in_specs=[pl.BlockSpec((tm, D), lambda i, f: (i, 0)),
                  pl.BlockSpec((tm, D), lambda i, f: (i, 0)),
                  pl.BlockSpec((D, tf), lambda i, f: (0, f)),
                  pl.BlockSpec((D, tf), lambda i, f: (0, f)),
                  pl.BlockSpec((tf, D), lambda i, f: (f, 0)),
                  pl.BlockSpec((1, D), lambda i, f: (0, 0)),
                  pl.BlockSpec((1, D), lambda i, f: (0, 0))],
        out_specs=pl.BlockSpec((tm, D), lambda i, f: (i, 0)),
        scratch_shapes=[pltpu.VMEM((tm, D), F32)],
        compiler_params=_cparams(("parallel", "arbitrary")),
        name="ffn",
    )(h2b, h2, w1, w3, w2, ln_g, ln_b)


def _encoder_layer(x, mem_k, mem_v, past, W):
    B, T, D = x.shape
    M = B * T
    xb = x.reshape(M, D).astype(BF16)
    w_in = W["w_in"]

    (dq,) = _proj(xb, w_in, OFF_DQ, DIFF_WIDTH, (BF16,), scale=DIFF_HEAD_DIM ** -0.5, name="proj_dq")
    kv_dtypes = (F32, BF16) if past is None else (F32,)
    dk, *dkb = _proj(xb, w_in, OFF_DK, DIFF_WIDTH, kv_dtypes, name="proj_dk")
    dv, *dvb = _proj(xb, w_in, OFF_DV, DIFF_WIDTH, kv_dtypes, name="proj_dv")
    buf0 = jnp.zeros((B, CONV_W - 1, 3 * GDN_WIDTH), F32) if past is None else past[3]
    streams, tails = [], []
    for t, (norm, scale) in enumerate(((True, GDN_DK ** -0.5), (True, 1.0), (False, 1.0))):
        y, tail = _proj_conv(xb, w_in, OFF_GQKV + t * GDN_WIDTH, buf0, W["conv_w"], B=B, T=T, norm=norm,
                             scale=scale, name="proj_conv_" + "qkv"[t])
        streams.append(y.reshape(B, T, GDN_WIDTH))
        tails.append(tail[:, CONV_PAD - (CONV_W - 1):, :])
    (gz,) = _proj(xb, w_in, OFF_GZ, GDN_WIDTH, (BF16,), act="silu", name="proj_gz")
    (gab,) = _proj(xb, w_in, OFF_GAB, 128, (F32,), tn=128, name="proj_gab")
    (gates,) = _proj(xb, W["w_gates"], 0, 2 * D_MODEL, (BF16,), act="sigmoid", name="proj_gates")

    dq = dq.reshape(B, T, DIFF_WIDTH)
    dk = dk.reshape(B, T, N_DIFF_HEADS, DIFF_VDIM)
    dv = dv.reshape(B, T, N_DIFF_HEADS, DIFF_VDIM)
    lam_rows = W["lam_rows"]
    if past is None:
        o_a = _diff_attention_prompt(dq, dkb[0].reshape(B, T, DIFF_WIDTH), dvb[0].reshape(B, T, DIFF_WIDTH),
                                     lam_rows, W["diff_subln_g"])
        s0 = None
        c = CHUNK
    else:
        cache_k, cache_v, s0, _ = past
        o_a = _diff_attention_sample(dq, cache_k, cache_v, dk, dv, lam_rows, W["diff_subln_g"])
        c = T
    o_b, s_new = _gdn(*streams, gab.reshape(B, T, 128), W["gdn_a_log"], W["gdn_dt_bias"],
                      gz.reshape(B, T, GDN_WIDTH), W["gdn_norm_g"], s0, c=c)
    new_buf = jnp.concatenate(tails, axis=-1)

    mix = _mix(o_a.reshape(M, DIFF_WIDTH), o_b.reshape(M, GDN_WIDTH), W["w_pa"], W["w_pb"], gates)
    h2, h2b = _post(x, mix.reshape(B, T, D), W["w_o"], W["ln1_g"], W["ln1_b"], W["w_xq"], mem_k, mem_v,
                    W["w_xo"], W["ln2_g"], W["ln2_b"])
    y = _ffn(h2b.reshape(M, D), h2.reshape(M, D), W["w_ff1"], W["w_ff3"], W["w_ff2"], W["ln3_g"], W["ln3_b"])
    return y.reshape(B, T, D), dk, dv, s_new, new_buf


def kernel(x_prompt, x_sample, mem_prompt, cache_diff_k, cache_diff_v, state_gdn, state_gdn_conv, cache_mem_k, cache_mem_v, w_in, conv_w, lam_q1, lam_k1, lam_q2, lam_k2, diff_subln_g, gdn_a_log, gdn_dt_bias, gdn_norm_g, w_pa, w_pb, w_o, ln1_g, ln1_b, w_xq, w_xk, w_xv, w_xo, ln2_g, ln2_b, w_ff1, w_ff3, w_ff2, ln3_g, ln3_b):
    l = 0
    w_in_b = w_in[l].astype(BF16)
    W = {
        "w_in": w_in_b,
        "w_gates": w_in_b[:, OFF_GATES:],
        "conv_w": conv_w[l],
        "lam_rows": tuple(v[l].reshape(1, DIFF_HEAD_DIM) for v in (lam_q1, lam_k1, lam_q2, lam_k2)),
        "diff_subln_g": diff_subln_g[l].reshape(1, DIFF_VDIM),
        "gdn_a_log": gdn_a_log[l].reshape(1, N_GDN_HEADS),
        "gdn_dt_bias": gdn_dt_bias[l].reshape(1, N_GDN_HEADS),
        "gdn_norm_g": gdn_norm_g[l].reshape(1, GDN_DV),
        "w_pa": w_pa[l].astype(BF16), "w_pb": w_pb[l].astype(BF16), "w_o": w_o[l].astype(BF16),
        "ln1_g": ln1_g[l].reshape(1, D_MODEL), "ln1_b": ln1_b[l].reshape(1, D_MODEL),
        "w_xq": w_xq[l].astype(BF16), "w_xo": w_xo[l].astype(BF16),
        "ln2_g": ln2_g[l].reshape(1, D_MODEL), "ln2_b": ln2_b[l].reshape(1, D_MODEL),
        "w_ff1": w_ff1[l].astype(BF16), "w_ff3": w_ff3[l].astype(BF16), "w_ff2": w_ff2[l].astype(BF16),
        "ln3_g": ln3_g[l].reshape(1, D_MODEL), "ln3_b": ln3_b[l].reshape(1, D_MODEL),
    }
    Bp = x_prompt.shape[0]
    memb = mem_prompt.reshape(Bp * N_MEM, D_MODEL).astype(BF16)
    mem_k, mem_kb = _proj(memb, w_xk[l].astype(BF16), 0, XWIDTH, (F32, BF16), tn=XWIDTH, name="proj_mem_k")
    mem_v, mem_vb = _proj(memb, w_xv[l].astype(BF16), 0, XWIDTH, (F32, BF16), tn=XWIDTH, name="proj_mem_v")

    yp, pk, pv, ps, pc = _encoder_layer(x_prompt, mem_kb.reshape(Bp, N_MEM, XWIDTH),
                                        mem_vb.reshape(Bp, N_MEM, XWIDTH), None, W)
    Bs = x_sample.shape[0]
    past = (cache_diff_k[l], cache_diff_v[l], state_gdn[l], state_gdn_conv[l])
    ys, sk, sv, ss, sc = _encoder_layer(x_sample, cache_mem_k[l].reshape(Bs, N_MEM, XWIDTH).astype(BF16),
                                        cache_mem_v[l].reshape(Bs, N_MEM, XWIDTH).astype(BF16), past, W)
    st = lambda a: a[None]
    return (yp, ys, st(pk), st(pv), st(ps), st(pc),
            st(mem_k.reshape(Bp, N_MEM, N_XHEADS, XHEAD_DIM)), st(mem_v.reshape(Bp, N_MEM, N_XHEADS, XHEAD_DIM)),
            st(sk), st(sv), st(ss), st(sc))
```

```python
import functools
import math

import jax
import jax.numpy as jnp
from jax import lax
from jax.experimental import pallas as pl
from jax.experimental.pallas import tpu as pltpu

D_MODEL = 2048
CHUNK = 64
N_DIFF_HEADS = 8
DIFF_HEAD_DIM = 128
DIFF_VDIM = 2 * DIFF_HEAD_DIM
DIFF_WIDTH = N_DIFF_HEADS * DIFF_VDIM
N_GDN_HEADS = 16
GDN_DK = 128
GDN_DV = 128
GDN_WIDTH = N_GDN_HEADS * GDN_DK
CONV_W = 4
N_MEM = 256
N_XHEADS = 4
XHEAD_DIM = 128
XWIDTH = N_XHEADS * XHEAD_DIM
D_FF = 5632
DEPTH = 1
ALPHA = (2.0 * DEPTH) ** 0.25
LN_EPS = 1e-5
NORM_EPS = 1e-6
LAM_INIT = 0.8 - 0.6 * math.exp(-0.3 * 0)

OFF_DQ = 0
OFF_DK = 2048
OFF_DV = 4096
OFF_GQKV = 6144
OFF_GZ = 12288
OFF_GAB = 14336
OFF_GATES = 14368

VMEM_LIMIT = 56 * 1024 * 1024
BF16 = jnp.bfloat16
F32 = jnp.float32
HI = lax.Precision.HIGHEST


def _cparams(sem):
    return pltpu.CompilerParams(dimension_semantics=sem, vmem_limit_bytes=VMEM_LIMIT)


def _sigmoid(x):
    return 1.0 / (1.0 + jnp.exp(-x))


def _silu(x):
    return x * _sigmoid(x)


def _layer_norm(x, g, b):
    mu = jnp.mean(x, axis=-1, keepdims=True)
    xc = x - mu
    var = jnp.mean(xc * xc, axis=-1, keepdims=True)
    return xc * lax.rsqrt(var + LN_EPS) * g + b


def _proj_kernel(x_ref, w_ref, *o_refs, act, scale):
    acc = jnp.dot(x_ref[...], w_ref[...].astype(BF16), preferred_element_type=F32)
    if scale != 1.0:
        acc = acc * scale
    if act == "sigmoid":
        acc = _sigmoid(acc)
    elif act == "silu":
        acc = _silu(acc)
    for o in o_refs:
        o[...] = acc.astype(o.dtype)


def _proj(x, w, col_off, n_cols, out_dtypes, *, act=None, scale=1.0, tm=1024, tn=1024, rows_inner=False,
          name="proj"):
    M, K = x.shape
    tm = min(tm, M)
    tn = min(tn, n_cols)
    assert M % tm == 0 and n_cols % tn == 0 and col_off % tn == 0
    cb = col_off // tn
    if rows_inner:
        grid = (n_cols // tn, M // tm)
        ij = lambda a, b: (b, a)
    else:
        grid = (M // tm, n_cols // tn)
        ij = lambda a, b: (a, b)
    outs = pl.pallas_call(
        functools.partial(_proj_kernel, act=act, scale=scale),
        out_shape=tuple(jax.ShapeDtypeStruct((M, n_cols), dt) for dt in out_dtypes),
        grid=grid,
        in_specs=[pl.BlockSpec((tm, K), lambda a, b: (ij(a, b)[0], 0)),
                  pl.BlockSpec((K, tn), lambda a, b: (0, ij(a, b)[1] + cb))],
        out_specs=tuple(pl.BlockSpec((tm, tn), lambda a, b: ij(a, b)) for _ in out_dtypes),
        compiler_params=_cparams(("parallel", "parallel")),
        name=name,
    )(x, w)
    return outs


CONV_PAD = 8


def _proj_conv_kernel(x_ref, w_ref, hist_ref, cw_ref, y_ref, raw_ref, pad_ref, *, nb, T, sub, tr_max, norm, scale):
    tn = w_ref.shape[1]
    tr = min(T, tr_max) if nb == 1 else T
    units = [(s, r) for s in range(tn // sub) for r in range(T // tr)]
    w_slabs = [w_ref[:, s * sub:(s + 1) * sub].astype(BF16) for s in range(tn // sub)]

    def rows_of(r):
        return slice(r * tr, (r + 1) * tr) if nb == 1 else slice(None)

    def matmul(s, r, gate):
        rows = rows_of(r)
        if gate is None:
            lhs = x_ref[rows, :]
        else:
            n_rows = nb * tr
            first = x_ref[rows, 0:256] + jnp.concatenate([jnp.concatenate([gate] * 2, axis=1)] * (n_rows // 16), axis=0)
            lhs = jnp.concatenate([first, x_ref[rows, 256:]], axis=1)
        return jnp.dot(lhs, w_slabs[s], preferred_element_type=F32)

    def epilogue(s, r, acc):
        cols = slice(s * sub, (s + 1) * sub)
        slot = s % 2
        r0 = r * tr
        a3 = acc.reshape(nb, tr, sub)
        pad_ref[slot, :, CONV_PAD + r0:CONV_PAD + r0 + tr, :] = a3
        if r == 0:
            pad_ref[slot, :, CONV_PAD - (CONV_W - 1):CONV_PAD, :] = hist_ref[:, :, cols]
        if r == T // tr - 1:
            raw_ref[:, :, cols] = a3[:, tr - CONV_PAD:, :]
        y = cw_ref[CONV_W - 1:CONV_W, cols] * a3
        for j in range(CONV_W - 1):
            lo = CONV_PAD - (CONV_W - 1) + j + r0
            y = y + cw_ref[j:j + 1, cols] * pad_ref[slot, :, lo:lo + tr, :]
        y = _silu(y)
        if norm:
            heads = []
            for hh in range(sub // GDN_DK):
                yh = y[:, :, hh * GDN_DK:(hh + 1) * GDN_DK]
                heads.append(yh * (lax.rsqrt(jnp.sum(yh * yh, axis=-1, keepdims=True) + NORM_EPS) * scale))
            y = jnp.concatenate(heads, axis=-1)
        y2 = y.reshape(nb * tr, sub)
        y_ref[rows_of(r), cols] = y2.astype(y_ref.dtype)
        bits = pltpu.bitcast(y2[nb * tr - 16:, 0:128], jnp.uint32)
        zero = lax.shift_right_logical(lax.shift_right_logical(bits, jnp.uint32(16)), jnp.uint32(16))
        return zero.astype(F32).astype(BF16)

    gates = [None, None]
    acc = matmul(*units[0], None)
    for i, u in enumerate(units):
        nxt = matmul(*units[i + 1], gates[i + 1]) if i + 1 < len(units) else None
        gates.append(epilogue(*u, acc))
        acc = nxt


def _proj_conv(x, w, col_off, hist, conv_w, *, B, T, norm, scale=1.0, tn=512, sub=512, tr=512, name="proj_conv"):
    M, K = x.shape
    n_cols = GDN_WIDTH
    nb = max(1, 256 // T) if T < 256 else 1
    nb = min(nb, B)
    tm = nb * T
    gq = OFF_GQKV
    assert B % nb == 0 and n_cols % tn == 0 and col_off % tn == 0 and (col_off - gq) % tn == 0 and T >= CONV_PAD
    cb = col_off // tn
    hb = (col_off - gq) // tn
    return pl.pallas_call(
        functools.partial(_proj_conv_kernel, nb=nb, T=T, sub=sub, tr_max=tr, norm=norm, scale=scale),
        out_shape=(jax.ShapeDtypeStruct((M, n_cols), BF16), jax.ShapeDtypeStruct((B, CONV_PAD, n_cols), F32)),
        grid=(B // nb, n_cols // tn),
        in_specs=[pl.BlockSpec((tm, K), lambda i, j: (i, 0)),
                  pl.BlockSpec((K, tn), lambda i, j: (0, j + cb)),
                  pl.BlockSpec((nb, CONV_W - 1, tn), lambda i, j: (i, 0, j + hb)),
                  pl.BlockSpec((CONV_W, tn), lambda i, j: (0, j + hb))],
        out_specs=(pl.BlockSpec((tm, tn), lambda i, j: (i, j)),
                   pl.BlockSpec((nb, CONV_PAD, tn), lambda i, j: (i, 0, j))),
        scratch_shapes=[pltpu.VMEM((min(2, tn // sub), nb, T + CONV_PAD, sub), F32)],
        compiler_params=_cparams(("parallel", "parallel")),
        name=name,
    )(x, w, hist, conv_w)


def _lam_value(lq1, lk1, lq2, lk2):
    a = jnp.sum(lq1 * lk1, axis=-1, keepdims=True)
    b = jnp.sum(lq2 * lk2, axis=-1, keepdims=True)
    return jnp.exp(a) - jnp.exp(b) + LAM_INIT


def _subln(o, g):
    ms = jnp.mean(o * o, axis=-1, keepdims=True)
    return o * lax.rsqrt(ms + NORM_EPS) * g * (1.0 - LAM_INIT)


def _head_slope(h):
    e = (h + 1).astype(F32) * (-8.0 / N_DIFF_HEADS)
    return jnp.exp2(jnp.full((1, 1), e, F32))


def _nt_dot(a, b):
    return lax.dot_general(a, b, (((1,), (1,)), ((), ())), preferred_element_type=F32)


def _lane_tile(x, n):
    return x if n == 1 else jnp.concatenate([x] * n, axis=1)


def _diff_prompt_kernel(lq1_ref, lk1_ref, lq2_ref, lk2_ref, g_ref, base_ref, q_ref, k_ref, v_ref, o_ref,
                        m_ref, l_ref, acc_ref, *, tq):
    LANES = 128
    half = tq // 2
    h = pl.program_id(1)
    qi = pl.program_id(2)
    slope = _head_slope(h)
    lam = _lam_value(lq1_ref[...], lk1_ref[...], lq2_ref[...], lk2_ref[...])
    col_bias = slope * lax.broadcasted_iota(jnp.int32, (1, tq), 1).astype(F32)
    maps = range(2)

    m_ref[...] = jnp.full(m_ref.shape, -jnp.inf, F32)
    l_ref[...] = jnp.zeros(l_ref.shape, F32)
    acc_ref[...] = jnp.zeros(acc_ref.shape, F32)

    def update(*jobs):
        todo = [(job, m) for job in jobs for m in maps]
        t = [_nt_dot(q_ref[0, rows, m * DIFF_HEAD_DIM:(m + 1) * DIFF_HEAD_DIM],
                     kblk[:, m * DIFF_HEAD_DIM:(m + 1) * DIFF_HEAD_DIM]) + bias
             for (rows, kblk, _, bias, _), m in todo]
        m_old = [m_ref[m, rows] for (rows, *_), m in todo]
        m_new = [jnp.maximum(m_old[i], jnp.max(t[i], axis=-1, keepdims=True) + job[4])
                 for i, (job, m) in enumerate(todo)]
        a = [jnp.exp(m_old[i] - m_new[i]) for i in range(len(todo))]
        p = [jnp.exp(t[i] - _lane_tile(m_new[i] - job[4], job[1].shape[0] // LANES))
             for i, (job, m) in enumerate(todo)]
        pv = [jnp.dot(p[i].astype(BF16), job[2], preferred_element_type=F32) for i, (job, m) in enumerate(todo)]
        for i, ((rows, kblk, *_), m) in enumerate(todo):
            psum = p[i][:, :LANES]
            for c in range(1, kblk.shape[0] // LANES):
                psum = psum + p[i][:, c * LANES:(c + 1) * LANES]
            l_ref[m, rows] = a[i] * l_ref[m, rows] + psum
            acc_ref[m, rows] = _lane_tile(a[i], DIFF_VDIM // LANES) * acc_ref[m, rows] + pv[i]
            m_ref[m, rows] = m_new[i]

    every = slice(0, tq)

    def body(j, carry):
        start = pl.multiple_of(j * tq, tq)
        gap = ((qi - j) * tq).astype(F32)
        update((every, k_ref[0, pl.ds(start, tq), :], v_ref[0, pl.ds(start, tq), :], col_bias, -slope * gap))
        return carry

    lax.fori_loop(0, qi, body, 0)

    lo, hi = slice(0, half), slice(half, tq)
    start = pl.multiple_of(qi * tq, tq)
    zero = jnp.zeros((1, 1), F32)
    update((lo, k_ref[0, pl.ds(start, half), :], v_ref[0, pl.ds(start, half), :], slope * base_ref[:, half:],
            -slope * float(half)),
           (hi, k_ref[0, pl.ds(start, tq), :], v_ref[0, pl.ds(start, tq), :], slope * base_ref[...], zero))

    outs = []
    for m in maps:
        inv = 1.0 / jnp.sum(l_ref[m], axis=-1, keepdims=True)
        outs.append(acc_ref[m] * inv)
    o = outs[0] - lam * outs[1]
    o_ref[0] = _subln(o, g_ref[...]).astype(o_ref.dtype)


def _diff_attention_prompt(q, k, v, lam_rows, subln_g, *, tq=512):
    B, T, _ = q.shape
    tq = min(tq, T)
    half = tq // 2
    assert T % tq == 0 and half % CHUNK == 0
    ii = lax.broadcasted_iota(jnp.int32, (half, tq), 0)
    jj = lax.broadcasted_iota(jnp.int32, (half, tq), 1)
    jh = jj - half
    diag = jnp.where((jh // CHUNK) <= (ii // CHUNK), (half + ii - jnp.abs(ii - jh)).astype(F32), -jnp.inf)
    base = jnp.where(jj < half, jj.astype(F32), diag)
    vec = pl.BlockSpec((1, DIFF_HEAD_DIM), lambda b, h, i: (0, 0))
    return pl.pallas_call(
        functools.partial(_diff_prompt_kernel, tq=tq),
        out_shape=jax.ShapeDtypeStruct((B, T, DIFF_WIDTH), BF16),
        grid=(B, N_DIFF_HEADS, T // tq),
        in_specs=[vec, vec, vec, vec,
                  pl.BlockSpec((1, DIFF_VDIM), lambda b, h, i: (0, 0)),
                  pl.BlockSpec((half, tq), lambda b, h, i: (0, 0)),
                  pl.BlockSpec((1, tq, DIFF_VDIM), lambda b, h, i: (b, i, h)),
                  pl.BlockSpec((1, T, DIFF_VDIM), lambda b, h, i: (b, 0, h)),
                  pl.BlockSpec((1, T, DIFF_VDIM), lambda b, h, i: (b, 0, h))],
        out_specs=pl.BlockSpec((1, tq, DIFF_VDIM), lambda b, h, i: (b, i, h)),
        scratch_shapes=[pltpu.VMEM((2, tq, 128), F32), pltpu.VMEM((2, tq, 128), F32),
                        pltpu.VMEM((2, tq, DIFF_VDIM), F32)],
        compiler_params=_cparams(("parallel", "parallel", "parallel")),
        name="diff_attn_prompt",
    )(*lam_rows, subln_g, base, q, k, v)


def _diff_sample_kernel(lq1_ref, lk1_ref, lq2_ref, lk2_ref, g_ref, q_ref, kc_ref, vc_ref, kn_ref, vn_ref,
                        o_ref, *, past, pc):
    H = N_DIFF_HEADS
    T = q_ref.shape[1]
    L = H * T
    rows = pc * H
    n_chunks = past // pc
    lam = _lam_value(lq1_ref[...], lk1_ref[...], lq2_ref[...], lk2_ref[...])
    maps = range(2)

    def map_cols(m):
        return slice(m * DIFF_HEAD_DIM, (m + 1) * DIFF_HEAD_DIM)

    qm = [jnp.concatenate([q_ref[0, :, h * DIFF_VDIM + m * DIFF_HEAD_DIM:h * DIFF_VDIM + (m + 1) * DIFF_HEAD_DIM]
                           for h in range(H)], axis=0) for m in maps]
    lane = lax.broadcasted_iota(jnp.int32, (1, L), 1)
    h_lane = lane // T
    t_lane = lane % T
    slope = jnp.exp2((h_lane + 1).astype(F32) * (-8.0 / H))
    base = slope * (lax.broadcasted_iota(jnp.int32, (rows, L), 0) // H).astype(F32)
    valid = lax.broadcasted_iota(jnp.int32, (H, L), 0) == h_lane

    def chunk_shift(c):
        return slope * ((c * pc - past) - t_lane).astype(F32)

    def chunk_scores(c):
        r0 = pl.multiple_of(c * rows, rows)
        kblk = kc_ref[0, pl.ds(r0, rows), :].astype(BF16)
        return [(_nt_dot(kblk[:, map_cols(m)], qm[m]) + base).reshape(pc, H, L) for m in maps]

    def stats_step(c, carry):
        shift = chunk_shift(c)
        s = chunk_scores(c)
        out = []
        for m in maps:
            m_old, l_old = carry[2 * m], carry[2 * m + 1]
            m_new = jnp.maximum(m_old, jnp.max(s[m], axis=0) + shift)
            l_new = l_old * jnp.exp(m_old - m_new) + jnp.sum(jnp.exp(s[m] - (m_new - shift)[None]), axis=0)
            out += [m_new, l_new]
        return tuple(out)

    init = (jnp.full((H, L), -jnp.inf, F32), jnp.zeros((H, L), F32)) * 2
    stats = lax.fori_loop(0, n_chunks, stats_step, init)

    kn = kn_ref[0].astype(BF16)
    vn = vn_ref[0].astype(BF16)
    tk = lax.broadcasted_iota(jnp.int32, (T * H, L), 0) // H
    allowed = ((past + tk) // CHUNK) <= ((past + t_lane) // CHUNK)
    bias_n = jnp.where(allowed, -slope * jnp.abs(tk - t_lane).astype(F32), -jnp.inf)
    s_n = [(_nt_dot(kn[:, map_cols(m)], qm[m]) + bias_n).reshape(T, H, L) for m in maps]
    m_fin, coef = [], []
    for m in maps:
        m_run, l_run = stats[2 * m], stats[2 * m + 1]
        mf = jnp.maximum(m_run, jnp.max(s_n[m], axis=0))
        lf = l_run * jnp.exp(m_run - mf) + jnp.sum(jnp.exp(s_n[m] - mf[None]), axis=0)
        m_fin.append(mf)
        coef.append(jnp.where(valid, (1.0 if m == 0 else -lam) / lf, 0.0))

    def tn_dot(w, v):
        return lax.dot_general(w, v, (((0,), (0,)), ((), ())), preferred_element_type=F32)

    def out_step(c, acc):
        shift = chunk_shift(c)
        s = chunk_scores(c)
        r0 = pl.multiple_of(c * rows, rows)
        vblk = vc_ref[0, pl.ds(r0, rows), :].astype(BF16)
        w = sum(jnp.exp(s[m] - (m_fin[m] - shift)[None]) * coef[m][None] for m in maps)
        return acc + tn_dot(w.reshape(rows, L).astype(BF16), vblk)

    acc = lax.fori_loop(0, n_chunks, out_step, jnp.zeros((L, DIFF_VDIM), F32))
    w_n = sum(jnp.exp(s_n[m] - m_fin[m][None]) * coef[m][None] for m in maps)
    acc = acc + tn_dot(w_n.reshape(T * H, L).astype(BF16), vn)
    o = _subln(acc, g_ref[...]).astype(o_ref.dtype)
    for h in range(H):
        o_ref[0, :, h * DIFF_VDIM:(h + 1) * DIFF_VDIM] = o[h * T:(h + 1) * T]


def _diff_attention_sample(q, cache_k, cache_v, k_new, v_new, lam_rows, subln_g):
    B, T, _ = q.shape
    P = cache_k.shape[1]
    H = N_DIFF_HEADS
    pc = min(P, 128)
    assert H * T == 128 and P > 0 and P % pc == 0
    vec = pl.BlockSpec((1, DIFF_HEAD_DIM), lambda b: (0, 0))
    tok = pl.BlockSpec((1, T, DIFF_WIDTH), lambda b: (b, 0, 0))
    new = pl.BlockSpec((1, T * H, DIFF_VDIM), lambda b: (b, 0, 0))
    old = pl.BlockSpec((1, P * H, DIFF_VDIM), lambda b: (b, 0, 0))
    flat = lambda a: a.reshape(B, a.shape[1] * H, DIFF_VDIM)
    return pl.pallas_call(
        functools.partial(_diff_sample_kernel, past=P, pc=pc),
        out_shape=jax.ShapeDtypeStruct((B, T, DIFF_WIDTH), BF16),
        grid=(B,),
        in_specs=[vec, vec, vec, vec, pl.BlockSpec((1, DIFF_VDIM), lambda b: (0, 0)),
                  tok, old, old, new, new],
        out_specs=tok,
        compiler_params=_cparams(("parallel",)),
        name="diff_attn_sample",
    )(*lam_rows, subln_g, q, flat(cache_k), flat(cache_v), flat(k_new), flat(v_new))


def _gdn_kernel(*refs, c, G, has_state):
    if has_state:
        (xq_ref, xk_ref, xv_ref, ab_ref, alog_ref, dtb_ref, z_ref, ng_ref, s0_ref, o_ref, s_ref) = refs
    else:
        (xq_ref, xk_ref, xv_ref, ab_ref, alog_ref, dtb_ref, z_ref, ng_ref, o_ref, s_ref) = refs
        s0_ref = None
    hg = pl.program_id(1)
    n = pl.program_id(2)
    H = N_GDN_HEADS

    @pl.when(n == 0)
    def _init():
        if has_state:
            s_ref[0] = s0_ref[0]
        else:
            s_ref[...] = jnp.zeros(s_ref.shape, F32)

    conv = [xq_ref[0], xk_ref[0], xv_ref[0]]

    ab = ab_ref[0]
    a_in = ab[:, 0:H] + dtb_ref[...]
    softplus = jnp.maximum(a_in, 0.0) + jnp.log1p(jnp.exp(-jnp.abs(a_in)))
    g_col = -jnp.exp(alog_ref[...]) * softplus
    beta_col = _sigmoid(ab[:, H:2 * H])
    r = lax.broadcasted_iota(jnp.int32, (c, c), 0)
    s = lax.broadcasted_iota(jnp.int32, (c, c), 1)
    tri = r >= s
    strict = r > s
    tri_f = tri.astype(F32)
    eye_c = (r == s).astype(F32)
    gc_col = jnp.dot(tri_f, g_col, preferred_element_type=F32, precision=HI)
    eye_h = (lax.broadcasted_iota(jnp.int32, (H, H), 0) == lax.broadcasted_iota(jnp.int32, (H, H), 1)).astype(F32)
    gc_row = lax.dot_general(eye_h, gc_col, (((1,), (1,)), ((), ())), preferred_element_type=F32,
                             precision=HI)

    def bf(a):
        return a.astype(BF16)

    def split(a):
        hi = a.astype(BF16)
        return hi, (a - hi.astype(F32)).astype(BF16)

    def dot3(a, b):
        (ah, al), (bh, bl) = a, b
        return (jnp.dot(al, bh, preferred_element_type=F32) + jnp.dot(ah, bl, preferred_element_type=F32)
                + jnp.dot(ah, bh, preferred_element_type=F32))

    blocks = []
    b_ = 1
    while b_ < c:
        blocks.append((((r // (2 * b_)) == (s // (2 * b_))) & ((r // b_) != (s // b_)) & strict).astype(F32))
        b_ *= 2

    heads = range(G)
    lanes = [slice(hh * GDN_DK, (hh + 1) * GDN_DK) for hh in heads]
    if G == H:
        gcc = [gc_col[:, hh:hh + 1] for hh in heads]
        bet = [beta_col[:, hh:hh + 1] for hh in heads]
        gcr = [gc_row[hh:hh + 1, :] for hh in heads]
    else:
        gcc, bet, gcr = [], [], []
        for hh in heads:
            head = hg * G + hh
            sel_c = (lax.broadcasted_iota(jnp.int32, (c, H), 1) == head).astype(F32)
            sel_r = (lax.broadcasted_iota(jnp.int32, (H, c), 0) == head).astype(F32)
            gcc.append(jnp.sum(gc_col * sel_c, axis=1, keepdims=True))
            bet.append(jnp.sum(beta_col * sel_c, axis=1, keepdims=True))
            gcr.append(jnp.sum(gc_row * sel_r, axis=0, keepdims=True))
    g_last = [g[c - 1:c, :] for g in gcc]

    q = [conv[0][:, lanes[hh]].astype(F32) for hh in heads]
    k = [conv[1][:, lanes[hh]].astype(F32) for hh in heads]
    v = [conv[2][:, lanes[hh]].astype(F32) for hh in heads]

    decay = [jnp.exp(jnp.where(tri, gcc[hh] - gcr[hh], -jnp.inf)) for hh in heads]
    kb = [k[hh] * bet[hh] for hh in heads]
    k16 = [bf(k[hh]) for hh in heads]
    mmat = [jnp.where(strict, _nt_dot(bf(kb[hh]), k16[hh]) * decay[hh], 0.0) for hh in heads]
    qk = [jnp.where(tri, _nt_dot(bf(q[hh]), k16[hh]) * decay[hh], 0.0) for hh in heads]

    tinv = [eye_c - mmat[hh] * blocks[0] for hh in heads]
    for lvl in range(1, len(blocks)):
        d16 = [bf(t) for t in tinv]
        x = [jnp.dot(bf(mmat[hh] * blocks[lvl]), d16[hh], preferred_element_type=F32) for hh in heads]
        tinv = [tinv[hh] - jnp.dot(d16[hh], bf(x[hh]), preferred_element_type=F32) for hh in heads]
    m_s = [split(m) for m in mmat]
    t_s = [split(t) for t in tinv]
    res = [(eye_c - tinv[hh]) - dot3(m_s[hh], t_s[hh]) for hh in heads]
    tinv = [tinv[hh] + jnp.dot(t_s[hh][0], bf(res[hh]), preferred_element_type=F32) for hh in heads]

    rhs = [jnp.concatenate([v[hh] * bet[hh], kb[hh] * jnp.exp(gcc[hh])], axis=1) for hh in heads]
    sol = [dot3(split(tinv[hh]), split(rhs[hh])) for hh in heads]
    u = [x_[:, :GDN_DV] for x_ in sol]
    w = [x_[:, GDN_DV:] for x_ in sol]

    S = [s_ref[0, hh] for hh in heads]
    S16 = [bf(x_) for x_ in S]
    v_new = [u[hh] - jnp.dot(bf(w[hh]), S16[hh], preferred_element_type=F32) for hh in heads]
    v16 = [bf(x_) for x_ in v_new]
    o = [jnp.dot(bf(q[hh] * jnp.exp(gcc[hh])), S16[hh], preferred_element_type=F32)
         + jnp.dot(bf(qk[hh]), v16[hh], preferred_element_type=F32) for hh in heads]
    for hh in heads:
        kd = bf(k[hh] * jnp.exp(g_last[hh] - gcc[hh]))
        s_ref[0, hh] = S[hh] * jnp.exp(g_last[hh]) + lax.dot_general(kd, v16[hh], (((0,), (0,)), ((), ())),
                                                                     preferred_element_type=F32)
    for hh in heads:
        ms = jnp.mean(o[hh] * o[hh], axis=-1, keepdims=True)
        y = o[hh] * lax.rsqrt(ms + NORM_EPS) * ng_ref[...] * z_ref[0][:, lanes[hh]].astype(F32)
        o_ref[0, :, lanes[hh]] = y.astype(o_ref.dtype)


def _gdn(q, k, v, gab, a_log, dt_bias, z_silu, norm_g, s0, *, c, G=16):
    B, T, _ = q.shape
    H = N_GDN_HEADS
    assert T % c == 0 and c % 8 == 0 and (c & (c - 1)) == 0 and H % G == 0
    nG = H // G
    W = G * GDN_DK
    stream = pl.BlockSpec((1, c, W), lambda b, g, n: (b, n, g))
    small = pl.BlockSpec((1, H), lambda b, g, n: (0, 0))
    state = pl.BlockSpec((1, G, GDN_DK, GDN_DV), lambda b, g, n: (b, g, 0, 0))
    in_specs = [stream, stream, stream,
                pl.BlockSpec((1, c, 128), lambda b, g, n: (b, n, 0)), small, small,
                stream, pl.BlockSpec((1, GDN_DV), lambda b, g, n: (0, 0))]
    args = [q, k, v, gab, a_log, dt_bias, z_silu, norm_g]
    if s0 is not None:
        in_specs.append(state)
        args.append(s0)
    return pl.pallas_call(
        functools.partial(_gdn_kernel, c=c, G=G, has_state=s0 is not None),
        out_shape=(jax.ShapeDtypeStruct((B, T, GDN_WIDTH), BF16),
                   jax.ShapeDtypeStruct((B, H, GDN_DK, GDN_DV), F32)),
        grid=(B, nG, T // c),
        in_specs=in_specs,
        out_specs=(stream, state),
        compiler_params=_cparams(("parallel", "parallel", "arbitrary")),
        name="gdn",
    )(*args)


def _mix_kernel(oa_ref, ob_ref, wa_ref, wb_ref, sa_ref, sb_ref, o_ref):
    a = jnp.dot(oa_ref[...], wa_ref[...], preferred_element_type=F32)
    b = jnp.dot(ob_ref[...], wb_ref[...], preferred_element_type=F32)
    o_ref[...] = (sa_ref[...].astype(F32) * a + sb_ref[...].astype(F32) * b).astype(o_ref.dtype)


def _mix(o_a, o_b, w_pa, w_pb, gates, *, tm=512, tn=1024):
    M = o_a.shape[0]
    tm = min(tm, M)
    nb = D_MODEL // tn
    return pl.pallas_call(
        _mix_kernel,
        out_shape=jax.ShapeDtypeStruct((M, D_MODEL), BF16),
        grid=(M // tm, nb),
        in_specs=[pl.BlockSpec((tm, DIFF_WIDTH), lambda i, j: (i, 0)),
                  pl.BlockSpec((tm, GDN_WIDTH), lambda i, j: (i, 0)),
                  pl.BlockSpec((DIFF_WIDTH, tn), lambda i, j: (0, j)),
                  pl.BlockSpec((GDN_WIDTH, tn), lambda i, j: (0, j)),
                  pl.BlockSpec((tm, tn), lambda i, j: (i, j)),
                  pl.BlockSpec((tm, tn), lambda i, j: (i, j + nb))],
        out_specs=pl.BlockSpec((tm, tn), lambda i, j: (i, j)),
        compiler_params=_cparams(("parallel", "parallel")),
        name="mix",
    )(o_a, o_b, w_pa, w_pb, gates, gates)


def _post_kernel(x_ref, mix_ref, wo_ref, g1_ref, b1_ref, wxq_ref, mk_ref, mv_ref, wxo_ref, g2_ref, b2_ref,
                 h2_ref, h2b_ref):
    h1 = ALPHA * x_ref[0] + jnp.dot(mix_ref[0], wo_ref[...], preferred_element_type=F32)
    h1 = _layer_norm(h1, g1_ref[...], b1_ref[...])
    qx = jnp.dot(h1.astype(BF16), wxq_ref[...], preferred_element_type=F32) * (XHEAD_DIM ** -0.5)
    qx = qx.astype(BF16)
    mk = mk_ref[0]
    mv = mv_ref[0]
    heads = []
    for hh in range(N_XHEADS):
        sl = slice(hh * XHEAD_DIM, (hh + 1) * XHEAD_DIM)
        s = _nt_dot(qx[:, sl], mk[:, sl])
        p = jnp.exp(s - jnp.max(s, axis=-1, keepdims=True))
        p = p / jnp.sum(p, axis=-1, keepdims=True)
        heads.append(jnp.dot(p.astype(BF16), mv[:, sl], preferred_element_type=F32))
    ox = jnp.concatenate(heads, axis=1).astype(BF16)
    h2 = ALPHA * h1 + jnp.dot(ox, wxo_ref[...], preferred_element_type=F32)
    h2 = _layer_norm(h2, g2_ref[...], b2_ref[...])
    h2_ref[0] = h2
    h2b_ref[0] = h2.astype(BF16)


def _post(x, mix, w_o, ln1_g, ln1_b, w_xq, mem_k, mem_v, w_xo, ln2_g, ln2_b, *, tm=512):
    B, T, D = x.shape
    tm = min(tm, T)
    const = lambda shape: pl.BlockSpec(shape, lambda b, i: (0, 0), pipeline_mode=pl.Buffered(1))
    rows = lambda: pl.BlockSpec((1, tm, D), lambda b, i: (b, i, 0))
    mem = lambda: pl.BlockSpec((1, N_MEM, XWIDTH), lambda b, i: (b, 0, 0))
    return pl.pallas_call(
        _post_kernel,
        out_shape=(jax.ShapeDtypeStruct((B, T, D), F32), jax.ShapeDtypeStruct((B, T, D), BF16)),
        grid=(B, T // tm),
        in_specs=[rows(), rows(), const((D, D)), const((1, D)), const((1, D)), const((D, XWIDTH)),
                  mem(), mem(), const((XWIDTH, D)), const((1, D)), const((1, D))],
        out_specs=(rows(), rows()),
        compiler_params=_cparams(("parallel", "parallel")),
        name="post_attn",
    )(x, mix, w_o, ln1_g, ln1_b, w_xq, mem_k, mem_v, w_xo, ln2_g, ln2_b)


def _ffn_kernel(hb_ref, h_ref, w1_ref, w3_ref, w2_ref, g_ref, b_ref, y_ref, acc_ref):
    f = pl.program_id(1)

    @pl.when(f == 0)
    def _():
        acc_ref[...] = jnp.zeros(acc_ref.shape, F32)

    hb = hb_ref[...]
    a = jnp.dot(hb, w1_ref[...], preferred_element_type=F32)
    b = jnp.dot(hb, w3_ref[...], preferred_element_type=F32)
    act = (_silu(a) * b).astype(BF16)
    acc_ref[...] += jnp.dot(act, w2_ref[...], preferred_element_type=F32)

    @pl.when(f == pl.num_programs(1) - 1)
    def _():
        y_ref[...] = _layer_norm(ALPHA * h_ref[...] + acc_ref[...], g_ref[...], b_ref[...])


def _ffn(h2b, h2, w1, w3, w2, ln_g, ln_b, *, tm=512, tf=512):
    M, D = h2.shape
    tm = min(tm, M)
    assert M % tm == 0 and D_FF % tf == 0
    return pl.pallas_call(
        _ffn_kernel,
        out_shape=jax.ShapeDtypeStruct((M, D), F32),
        grid=(M // tm, D_FF // tf),
        in_specs=[pl.BlockSpec((tm, D), lambda i, f: (i, 0)),
                  pl.BlockSpec((tm, D), lambda i, f: (i, 0)),
                  pl.BlockSpec((D, tf), lambda i, f: (0, f)),
                  pl.BlockSpec((D, tf), lambda i, f: (0, f)),
                  pl.BlockSpec((tf, D), lambda i, f: (f, 0)),
                  pl.BlockSpec((1, D), lambda i, f: (0, 0)),
                  pl.BlockSpec((1, D), lambda i, f: (0, 0))],
        out_specs=pl.BlockSpec((tm, D), lambda i, f: (i, 0)),
        scratch_shapes=[pltpu.VMEM((tm, D), F32)],
        compiler_params=_cparams(("parallel", "arbitrary")),
        name="ffn",
    )(h2b, h2, w1, w3, w2, ln_g, ln_b)


def _encoder_layer(x, mem_k, mem_v, past, W):
    B, T, D = x.shape
    M = B * T
    xb = x.reshape(M, D).astype(BF16)
    w_in = W["w_in"]

    (dq,) = _proj(xb, w_in, OFF_DQ, DIFF_WIDTH, (BF16,), scale=DIFF_HEAD_DIM ** -0.5, name="proj_dq")
    kv_dtypes = (F32, BF16) if past is None else (F32,)
    dk, *dkb = _proj(xb, w_in, OFF_DK, DIFF_WIDTH, kv_dtypes, rows_inner=True, name="proj_dk")
    dv, *dvb = _proj(xb, w_in, OFF_DV, DIFF_WIDTH, kv_dtypes, rows_inner=True, name="proj_dv")
    buf0 = jnp.zeros((B, CONV_W - 1, 3 * GDN_WIDTH), F32) if past is None else past[3]
    streams, tails = [], []
    for t, (norm, scale) in enumerate(((True, GDN_DK ** -0.5), (True, 1.0), (False, 1.0))):
        y, tail = _proj_conv(xb, w_in, OFF_GQKV + t * GDN_WIDTH, buf0, W["conv_w"], B=B, T=T, norm=norm,
                             scale=scale, name="proj_conv_" + "qkv"[t])
        streams.append(y.reshape(B, T, GDN_WIDTH))
        tails.append(tail[:, CONV_PAD - (CONV_W - 1):, :])
    (gz,) = _proj(xb, w_in, OFF_GZ, GDN_WIDTH, (BF16,), act="silu", name="proj_gz")
    (gab,) = _proj(xb, w_in, OFF_GAB, 128, (F32,), tn=128, name="proj_gab")
    (gates,) = _proj(xb, W["w_gates"], 0, 2 * D_MODEL, (BF16,), act="sigmoid", name="proj_gates")

    dq = dq.reshape(B, T, DIFF_WIDTH)
    dk = dk.reshape(B, T, N_DIFF_HEADS, DIFF_VDIM)
    dv = dv.reshape(B, T, N_DIFF_HEADS, DIFF_VDIM)
    lam_rows = W["lam_rows"]
    if past is None:
        o_a = _diff_attention_prompt(dq, dkb[0].reshape(B, T, DIFF_WIDTH), dvb[0].reshape(B, T, DIFF_WIDTH),
                                     lam_rows, W["diff_subln_g"])
        s0 = None
        c = CHUNK
    else:
        cache_k, cache_v, s0, _ = past
        o_a = _diff_attention_sample(dq, cache_k, cache_v, dk, dv, lam_rows, W["diff_subln_g"])
        c = T
    o_b, s_new = _gdn(*streams, gab.reshape(B, T, 128), W["gdn_a_log"], W["gdn_dt_bias"],
                      gz.reshape(B, T, GDN_WIDTH), W["gdn_norm_g"], s0, c=c)
    new_buf = jnp.concatenate(tails, axis=-1)

    mix = _mix(o_a.reshape(M, DIFF_WIDTH), o_b.reshape(M, GDN_WIDTH), W["w_pa"], W["w_pb"], gates)
    h2, h2b = _post(x, mix.reshape(B, T, D), W["w_o"], W["ln1_g"], W["ln1_b"], W["w_xq"], mem_k, mem_v,
                    W["w_xo"], W["ln2_g"], W["ln2_b"])
    y = _ffn(h2b.reshape(M, D), h2.reshape(M, D), W["w_ff1"], W["w_ff3"], W["w_ff2"], W["ln3_g"], W["ln3_b"])
    return y.reshape(B, T, D), dk, dv, s_new, new_buf


def kernel(x_prompt, x_sample, mem_prompt, cache_diff_k, cache_diff_v, state_gdn, state_gdn_conv, cache_mem_k, cache_mem_v, w_in, conv_w, lam_q1, lam_k1, lam_q2, lam_k2, diff_subln_g, gdn_a_log, gdn_dt_bias, gdn_norm_g, w_pa, w_pb, w_o, ln1_g, ln1_b, w_xq, w_xk, w_xv, w_xo, ln2_g, ln2_b, w_ff1, w_ff3, w_ff2, ln3_g, ln3_b):
    l = 0
    W = {
        "w_in": w_in[l],
        "w_gates": w_in[l][:, OFF_GATES:],
        "conv_w": conv_w[l],
        "lam_rows": tuple(v[l].reshape(1, DIFF_HEAD_DIM) for v in (lam_q1, lam_k1, lam_q2, lam_k2)),
        "diff_subln_g": diff_subln_g[l].reshape(1, DIFF_VDIM),
        "gdn_a_log": gdn_a_log[l].reshape(1, N_GDN_HEADS),
        "gdn_dt_bias": gdn_dt_bias[l].reshape(1, N_GDN_HEADS),
        "gdn_norm_g": gdn_norm_g[l].reshape(1, GDN_DV),
        "w_pa": w_pa[l].astype(BF16), "w_pb": w_pb[l].astype(BF16), "w_o": w_o[l].astype(BF16),
        "ln1_g": ln1_g[l].reshape(1, D_MODEL), "ln1_b": ln1_b[l].reshape(1, D_MODEL),
        "w_xq": w_xq[l].astype(BF16), "w_xo": w_xo[l].astype(BF16),
        "ln2_g": ln2_g[l].reshape(1, D_MODEL), "ln2_b": ln2_b[l].reshape(1, D_MODEL),
        "w_ff1": w_ff1[l].astype(BF16), "w_ff3": w_ff3[l].astype(BF16), "w_ff2": w_ff2[l].astype(BF16),
        "ln3_g": ln3_g[l].reshape(1, D_MODEL), "ln3_b": ln3_b[l].reshape(1, D_MODEL),
    }
    Bp = x_prompt.shape[0]
    memb = mem_prompt.reshape(Bp * N_MEM, D_MODEL).astype(BF16)
    mem_k, mem_kb = _proj(memb, w_xk[l], 0, XWIDTH, (F32, BF16), tn=XWIDTH, name="proj_mem_k")
    mem_v, mem_vb = _proj(memb, w_xv[l], 0, XWIDTH, (F32, BF16), tn=XWIDTH, name="proj_mem_v")

    yp, pk, pv, ps, pc = _encoder_layer(x_prompt, mem_kb.reshape(Bp, N_MEM, XWIDTH),
                                        mem_vb.reshape(Bp, N_MEM, XWIDTH), None, W)
    Bs = x_sample.shape[0]
    past = (cache_diff_k[l], cache_diff_v[l], state_gdn[l], state_gdn_conv[l])
    ys, sk, sv, ss, sc = _encoder_layer(x_sample, cache_mem_k[l].reshape(Bs, N_MEM, XWIDTH).astype(BF16),
                                        cache_mem_v[l].reshape(Bs, N_MEM, XWIDTH).astype(BF16), past, W)
    st = lambda a: a[None]
    return (yp, ys, st(pk), st(pv), st(ps), st(pc),
            st(mem_k.reshape(Bp, N_MEM, N_XHEADS, XHEAD_DIM)), st(mem_v.reshape(Bp, N_MEM, N_XHEADS, XHEAD_DIM)),
            st(sk), st(sv), st(ss), st(sc))
```

```python
import functools
import math

import jax
import jax.numpy as jnp
from jax import lax
from jax.experimental import pallas as pl
from jax.experimental.pallas import tpu as pltpu

D_MODEL = 2048
CHUNK = 64
N_DIFF_HEADS = 8
DIFF_HEAD_DIM = 128
DIFF_VDIM = 2 * DIFF_HEAD_DIM
DIFF_WIDTH = N_DIFF_HEADS * DIFF_VDIM
N_GDN_HEADS = 16
GDN_DK = 128
GDN_DV = 128
GDN_WIDTH = N_GDN_HEADS * GDN_DK
CONV_W = 4
N_MEM = 256
N_XHEADS = 4
XHEAD_DIM = 128
XWIDTH = N_XHEADS * XHEAD_DIM
D_FF = 5632
DEPTH = 1
ALPHA = (2.0 * DEPTH) ** 0.25
LN_EPS = 1e-5
NORM_EPS = 1e-6
LAM_INIT = 0.8 - 0.6 * math.exp(-0.3 * 0)

OFF_DQ = 0
OFF_DK = 2048
OFF_DV = 4096
OFF_GQKV = 6144
OFF_GZ = 12288
OFF_GAB = 14336
OFF_GATES = 14368

VMEM_LIMIT = 56 * 1024 * 1024
BF16 = jnp.bfloat16
F32 = jnp.float32
HI = lax.Precision.HIGHEST


def _cparams(sem):
    return pltpu.CompilerParams(dimension_semantics=sem, vmem_limit_bytes=VMEM_LIMIT)


def _sigmoid(x):
    return 1.0 / (1.0 + jnp.exp(-x))


def _silu(x):
    return x * _sigmoid(x)


def _layer_norm(x, g, b):
    mu = jnp.mean(x, axis=-1, keepdims=True)
    xc = x - mu
    var = jnp.mean(xc * xc, axis=-1, keepdims=True)
    return xc * lax.rsqrt(var + LN_EPS) * g + b


def _proj_kernel(x_ref, w_ref, *o_refs, act, scale, w_rows):
    w = w_ref[0].astype(BF16)
    acc = _nt_dot(x_ref[...], w) if w_rows else jnp.dot(x_ref[...], w, preferred_element_type=F32)
    if scale != 1.0:
        acc = acc * scale
    if act == "sigmoid":
        acc = _sigmoid(acc)
    elif act == "silu":
        acc = _silu(acc)
    for o in o_refs:
        o[...] = acc.astype(o.dtype)


def _weight_spec(w, K, tn, col_start, w_rows):
    assert w.shape[0] == DEPTH == 1
    if w_rows:
        return pl.BlockSpec((pl.Element(1), pl.Element(tn), pl.Element(K)),
                            lambda a, b: (0, pl.multiple_of(col_start(a, b), 8), 0))
    return pl.BlockSpec((1, K, tn), lambda a, b: (0, 0, col_start(a, b) // tn))


def _proj(x, w, col_off, n_cols, out_dtypes, *, act=None, scale=1.0, tm=1024, tn=1024, rows_inner=False,
          w_rows=False, name="proj"):
    M, K = x.shape
    tm = min(tm, M)
    tn = min(tn, n_cols)
    assert M % tm == 0 and n_cols % tn == 0 and col_off % (8 if w_rows else tn) == 0
    if rows_inner:
        grid = (n_cols // tn, M // tm)
        ij = lambda a, b: (b, a)
    else:
        grid = (M // tm, n_cols // tn)
        ij = lambda a, b: (a, b)
    outs = pl.pallas_call(
        functools.partial(_proj_kernel, act=act, scale=scale, w_rows=w_rows),
        out_shape=tuple(jax.ShapeDtypeStruct((M, n_cols), dt) for dt in out_dtypes),
        grid=grid,
        in_specs=[pl.BlockSpec((tm, K), lambda a, b: (ij(a, b)[0], 0)),
                  _weight_spec(w, K, tn, lambda a, b: col_off + ij(a, b)[1] * tn, w_rows)],
        out_specs=tuple(pl.BlockSpec((tm, tn), lambda a, b: ij(a, b)) for _ in out_dtypes),
        compiler_params=_cparams(("parallel", "parallel")),
        name=name,
    )(x, w)
    return outs


CONV_PAD = 8


def _proj_conv_kernel(x_ref, w_ref, hist_ref, cw_ref, y_ref, raw_ref, pad_ref, *, nb, T, sub, tr_max, norm, scale):
    tn = w_ref.shape[1]
    tr = min(T, tr_max) if nb == 1 else T
    units = [(s, r) for s in range(tn // sub) for r in range(T // tr)]
    w_slabs = [w_ref[0, s * sub:(s + 1) * sub, :].astype(BF16) for s in range(tn // sub)]

    def rows_of(r):
        return slice(r * tr, (r + 1) * tr) if nb == 1 else slice(None)

    def matmul(s, r, gate):
        rows = rows_of(r)
        if gate is None:
            lhs = x_ref[rows, :]
        else:
            n_rows = nb * tr
            first = x_ref[rows, 0:256] + jnp.concatenate([jnp.concatenate([gate] * 2, axis=1)] * (n_rows // 16), axis=0)
            lhs = jnp.concatenate([first, x_ref[rows, 256:]], axis=1)
        return _nt_dot(lhs, w_slabs[s])

    def epilogue(s, r, acc):
        cols = slice(s * sub, (s + 1) * sub)
        slot = s % 2
        r0 = r * tr
        a3 = acc.reshape(nb, tr, sub)
        pad_ref[slot, :, CONV_PAD + r0:CONV_PAD + r0 + tr, :] = a3
        if r == 0:
            pad_ref[slot, :, CONV_PAD - (CONV_W - 1):CONV_PAD, :] = hist_ref[:, :, cols]
        if r == T // tr - 1:
            raw_ref[:, :, cols] = a3[:, tr - CONV_PAD:, :]
        y = cw_ref[CONV_W - 1:CONV_W, cols] * a3
        for j in range(CONV_W - 1):
            lo = CONV_PAD - (CONV_W - 1) + j + r0
            y = y + cw_ref[j:j + 1, cols] * pad_ref[slot, :, lo:lo + tr, :]
        y = _silu(y)
        if norm:
            heads = []
            for hh in range(sub // GDN_DK):
                yh = y[:, :, hh * GDN_DK:(hh + 1) * GDN_DK]
                heads.append(yh * (lax.rsqrt(jnp.sum(yh * yh, axis=-1, keepdims=True) + NORM_EPS) * scale))
            y = jnp.concatenate(heads, axis=-1)
        y2 = y.reshape(nb * tr, sub)
        y_ref[rows_of(r), cols] = y2.astype(y_ref.dtype)
        bits = pltpu.bitcast(y2[nb * tr - 16:, 0:128], jnp.uint32)
        zero = lax.shift_right_logical(lax.shift_right_logical(bits, jnp.uint32(16)), jnp.uint32(16))
        return zero.astype(F32).astype(BF16)

    gates = [None, None]
    acc = matmul(*units[0], None)
    for i, u in enumerate(units):
        nxt = matmul(*units[i + 1], gates[i + 1]) if i + 1 < len(units) else None
        gates.append(epilogue(*u, acc))
        acc = nxt


def _proj_conv(x, w, col_off, hist, conv_w, *, B, T, norm, scale=1.0, tn=512, sub=512, tr=512, name="proj_conv"):
    M, K = x.shape
    n_cols = GDN_WIDTH
    nb = max(1, 256 // T) if T < 256 else 1
    nb = min(nb, B)
    tm = nb * T
    gq = OFF_GQKV
    assert B % nb == 0 and n_cols % tn == 0 and col_off % 8 == 0 and (col_off - gq) % tn == 0 and T >= CONV_PAD
    hb = (col_off - gq) // tn
    return pl.pallas_call(
        functools.partial(_proj_conv_kernel, nb=nb, T=T, sub=sub, tr_max=tr, norm=norm, scale=scale),
        out_shape=(jax.ShapeDtypeStruct((M, n_cols), BF16), jax.ShapeDtypeStruct((B, CONV_PAD, n_cols), F32)),
        grid=(B // nb, n_cols // tn),
        in_specs=[pl.BlockSpec((tm, K), lambda i, j: (i, 0)),
                  _weight_spec(w, K, tn, lambda i, j: col_off + j * tn, True),
                  pl.BlockSpec((nb, CONV_W - 1, tn), lambda i, j: (i, 0, j + hb)),
                  pl.BlockSpec((CONV_W, tn), lambda i, j: (0, j + hb))],
        out_specs=(pl.BlockSpec((tm, tn), lambda i, j: (i, j)),
                   pl.BlockSpec((nb, CONV_PAD, tn), lambda i, j: (i, 0, j))),
        scratch_shapes=[pltpu.VMEM((min(2, tn // sub), nb, T + CONV_PAD, sub), F32)],
        compiler_params=_cparams(("parallel", "parallel")),
        name=name,
    )(x, w, hist, conv_w)


def _lam_value(lq1, lk1, lq2, lk2):
    a = jnp.sum(lq1 * lk1, axis=-1, keepdims=True)
    b = jnp.sum(lq2 * lk2, axis=-1, keepdims=True)
    return jnp.exp(a) - jnp.exp(b) + LAM_INIT


def _subln(o, g):
    ms = jnp.mean(o * o, axis=-1, keepdims=True)
    return o * lax.rsqrt(ms + NORM_EPS) * g * (1.0 - LAM_INIT)


def _head_slope(h):
    e = (h + 1).astype(F32) * (-8.0 / N_DIFF_HEADS)
    return jnp.exp2(jnp.full((1, 1), e, F32))


def _nt_dot(a, b):
    return lax.dot_general(a, b, (((1,), (1,)), ((), ())), preferred_element_type=F32)


def _lane_tile(x, n):
    return x if n == 1 else jnp.concatenate([x] * n, axis=1)


def _diff_prompt_kernel(lq1_ref, lk1_ref, lq2_ref, lk2_ref, g_ref, base_ref, q_ref, k_ref, v_ref, o_ref,
                        m_ref, l_ref, acc_ref, *, tq):
    LANES = 128
    half = tq // 2
    h = pl.program_id(1)
    qi = pl.program_id(2)
    slope = _head_slope(h)
    lam = _lam_value(lq1_ref[...], lk1_ref[...], lq2_ref[...], lk2_ref[...])
    col_bias = slope * lax.broadcasted_iota(jnp.int32, (1, tq), 1).astype(F32)
    maps = range(2)

    m_ref[...] = jnp.full(m_ref.shape, -jnp.inf, F32)
    l_ref[...] = jnp.zeros(l_ref.shape, F32)
    acc_ref[...] = jnp.zeros(acc_ref.shape, F32)

    def update(*jobs):
        todo = [(job, m) for job in jobs for m in maps]
        t = [_nt_dot(q_ref[0, rows, m * DIFF_HEAD_DIM:(m + 1) * DIFF_HEAD_DIM],
                     kblk[:, m * DIFF_HEAD_DIM:(m + 1) * DIFF_HEAD_DIM]) + bias
             for (rows, kblk, _, bias, _), m in todo]
        m_old = [m_ref[m, rows] for (rows, *_), m in todo]
        m_new = [jnp.maximum(m_old[i], jnp.max(t[i], axis=-1, keepdims=True) + job[4])
                 for i, (job, m) in enumerate(todo)]
        a = [jnp.exp(m_old[i] - m_new[i]) for i in range(len(todo))]
        p = [jnp.exp(t[i] - _lane_tile(m_new[i] - job[4], job[1].shape[0] // LANES))
             for i, (job, m) in enumerate(todo)]
        pv = [jnp.dot(p[i].astype(BF16), job[2], preferred_element_type=F32) for i, (job, m) in enumerate(todo)]
        for i, ((rows, kblk, *_), m) in enumerate(todo):
            psum = p[i][:, :LANES]
            for c in range(1, kblk.shape[0] // LANES):
                psum = psum + p[i][:, c * LANES:(c + 1) * LANES]
            l_ref[m, rows] = a[i] * l_ref[m, rows] + psum
            acc_ref[m, rows] = _lane_tile(a[i], DIFF_VDIM // LANES) * acc_ref[m, rows] + pv[i]
            m_ref[m, rows] = m_new[i]

    every = slice(0, tq)

    def body(j, carry):
        start = pl.multiple_of(j * tq, tq)
        gap = ((qi - j) * tq).astype(F32)
        update((every, k_ref[0, pl.ds(start, tq), :], v_ref[0, pl.ds(start, tq), :], col_bias, -slope * gap))
        return carry

    lax.fori_loop(0, qi, body, 0)

    lo, hi = slice(0, half), slice(half, tq)
    start = pl.multiple_of(qi * tq, tq)
    zero = jnp.zeros((1, 1), F32)
    update((lo, k_ref[0, pl.ds(start, half), :], v_ref[0, pl.ds(start, half), :], slope * base_ref[:, half:],
            -slope * float(half)),
           (hi, k_ref[0, pl.ds(start, tq), :], v_ref[0, pl.ds(start, tq), :], slope * base_ref[...], zero))

    outs = []
    for m in maps:
        inv = 1.0 / jnp.sum(l_ref[m], axis=-1, keepdims=True)
        outs.append(acc_ref[m] * inv)
    o = outs[0] - lam * outs[1]
    o_ref[0] = _subln(o, g_ref[...]).astype(o_ref.dtype)


def _diff_attention_prompt(q, k, v, lam_rows, subln_g, *, tq=512):
    B, T, _ = q.shape
    tq = min(tq, T)
    half = tq // 2
    assert T % tq == 0 and half % CHUNK == 0
    ii = lax.broadcasted_iota(jnp.int32, (half, tq), 0)
    jj = lax.broadcasted_iota(jnp.int32, (half, tq), 1)
    jh = jj - half
    diag = jnp.where((jh // CHUNK) <= (ii // CHUNK), (half + ii - jnp.abs(ii - jh)).astype(F32), -jnp.inf)
    base = jnp.where(jj < half, jj.astype(F32), diag)
    vec = pl.BlockSpec((1, DIFF_HEAD_DIM), lambda b, h, i: (0, 0))
    return pl.pallas_call(
        functools.partial(_diff_prompt_kernel, tq=tq),
        out_shape=jax.ShapeDtypeStruct((B, T, DIFF_WIDTH), BF16),
        grid=(B, N_DIFF_HEADS, T // tq),
        in_specs=[vec, vec, vec, vec,
                  pl.BlockSpec((1, DIFF_VDIM), lambda b, h, i: (0, 0)),
                  pl.BlockSpec((half, tq), lambda b, h, i: (0, 0)),
                  pl.BlockSpec((1, tq, DIFF_VDIM), lambda b, h, i: (b, i, h)),
                  pl.BlockSpec((1, T, DIFF_VDIM), lambda b, h, i: (b, 0, h)),
                  pl.BlockSpec((1, T, DIFF_VDIM), lambda b, h, i: (b, 0, h))],
        out_specs=pl.BlockSpec((1, tq, DIFF_VDIM), lambda b, h, i: (b, i, h)),
        scratch_shapes=[pltpu.VMEM((2, tq, 128), F32), pltpu.VMEM((2, tq, 128), F32),
                        pltpu.VMEM((2, tq, DIFF_VDIM), F32)],
        compiler_params=_cparams(("parallel", "parallel", "parallel")),
        name="diff_attn_prompt",
    )(*lam_rows, subln_g, base, q, k, v)


def _diff_sample_kernel(lq1_ref, lk1_ref, lq2_ref, lk2_ref, g_ref, q_ref, kc_ref, vc_ref, kn_ref, vn_ref,
                        o_ref, *, past, pc):
    H = N_DIFF_HEADS
    T = q_ref.shape[1]
    L = H * T
    rows = pc * H
    n_chunks = past // pc
    lam = _lam_value(lq1_ref[...], lk1_ref[...], lq2_ref[...], lk2_ref[...])
    maps = range(2)

    def map_cols(m):
        return slice(m * DIFF_HEAD_DIM, (m + 1) * DIFF_HEAD_DIM)

    qm = [jnp.concatenate([q_ref[0, :, h * DIFF_VDIM + m * DIFF_HEAD_DIM:h * DIFF_VDIM + (m + 1) * DIFF_HEAD_DIM]
                           for h in range(H)], axis=0) for m in maps]
    lane = lax.broadcasted_iota(jnp.int32, (1, L), 1)
    h_lane = lane // T
    t_lane = lane % T
    slope = jnp.exp2((h_lane + 1).astype(F32) * (-8.0 / H))
    base = slope * (lax.broadcasted_iota(jnp.int32, (rows, L), 0) // H).astype(F32)
    valid = lax.broadcasted_iota(jnp.int32, (H, L), 0) == h_lane

    def chunk_shift(c):
        return slope * ((c * pc - past) - t_lane).astype(F32)

    def chunk_scores(c):
        r0 = pl.multiple_of(c * rows, rows)
        kblk = kc_ref[0, pl.ds(r0, rows), :].astype(BF16)
        return [(_nt_dot(kblk[:, map_cols(m)], qm[m]) + base).reshape(pc, H, L) for m in maps]

    def stats_step(c, carry):
        shift = chunk_shift(c)
        s = chunk_scores(c)
        out = []
        for m in maps:
            m_old, l_old = carry[2 * m], carry[2 * m + 1]
            m_new = jnp.maximum(m_old, jnp.max(s[m], axis=0) + shift)
            l_new = l_old * jnp.exp(m_old - m_new) + jnp.sum(jnp.exp(s[m] - (m_new - shift)[None]), axis=0)
            out += [m_new, l_new]
        return tuple(out)

    init = (jnp.full((H, L), -jnp.inf, F32), jnp.zeros((H, L), F32)) * 2
    stats = lax.fori_loop(0, n_chunks, stats_step, init)

    kn = kn_ref[0].astype(BF16)
    vn = vn_ref[0].astype(BF16)
    tk = lax.broadcasted_iota(jnp.int32, (T * H, L), 0) // H
    allowed = ((past + tk) // CHUNK) <= ((past + t_lane) // CHUNK)
    bias_n = jnp.where(allowed, -slope * jnp.abs(tk - t_lane).astype(F32), -jnp.inf)
    s_n = [(_nt_dot(kn[:, map_cols(m)], qm[m]) + bias_n).reshape(T, H, L) for m in maps]
    m_fin, coef = [], []
    for m in maps:
        m_run, l_run = stats[2 * m], stats[2 * m + 1]
        mf = jnp.maximum(m_run, jnp.max(s_n[m], axis=0))
        lf = l_run * jnp.exp(m_run - mf) + jnp.sum(jnp.exp(s_n[m] - mf[None]), axis=0)
        m_fin.append(mf)
        coef.append(jnp.where(valid, (1.0 if m == 0 else -lam) / lf, 0.0))

    def tn_dot(w, v):
        return lax.dot_general(w, v, (((0,), (0,)), ((), ())), preferred_element_type=F32)

    def out_step(c, acc):
        shift = chunk_shift(c)
        s = chunk_scores(c)
        r0 = pl.multiple_of(c * rows, rows)
        vblk = vc_ref[0, pl.ds(r0, rows), :].astype(BF16)
        w = sum(jnp.exp(s[m] - (m_fin[m] - shift)[None]) * coef[m][None] for m in maps)
        return acc + tn_dot(w.reshape(rows, L).astype(BF16), vblk)

    acc = lax.fori_loop(0, n_chunks, out_step, jnp.zeros((L, DIFF_VDIM), F32))
    w_n = sum(jnp.exp(s_n[m] - m_fin[m][None]) * coef[m][None] for m in maps)
    acc = acc + tn_dot(w_n.reshape(T * H, L).astype(BF16), vn)
    o = _subln(acc, g_ref[...]).astype(o_ref.dtype)
    for h in range(H):
        o_ref[0, :, h * DIFF_VDIM:(h + 1) * DIFF_VDIM] = o[h * T:(h + 1) * T]


def _diff_attention_sample(q, cache_k, cache_v, k_new, v_new, lam_rows, subln_g):
    B, T, _ = q.shape
    P = cache_k.shape[1]
    H = N_DIFF_HEADS
    pc = min(P, 128)
    assert H * T == 128 and P > 0 and P % pc == 0
    vec = pl.BlockSpec((1, DIFF_HEAD_DIM), lambda b: (0, 0))
    tok = pl.BlockSpec((1, T, DIFF_WIDTH), lambda b: (b, 0, 0))
    new = pl.BlockSpec((1, T * H, DIFF_VDIM), lambda b: (b, 0, 0))
    old = pl.BlockSpec((1, P * H, DIFF_VDIM), lambda b: (b, 0, 0))
    flat = lambda a: a.reshape(B, a.shape[1] * H, DIFF_VDIM)
    return pl.pallas_call(
        functools.partial(_diff_sample_kernel, past=P, pc=pc),
        out_shape=jax.ShapeDtypeStruct((B, T, DIFF_WIDTH), BF16),
        grid=(B,),
        in_specs=[vec, vec, vec, vec, pl.BlockSpec((1, DIFF_VDIM), lambda b: (0, 0)),
                  tok, old, old, new, new],
        out_specs=tok,
        compiler_params=_cparams(("parallel",)),
        name="diff_attn_sample",
    )(*lam_rows, subln_g, q, flat(cache_k), flat(cache_v), flat(k_new), flat(v_new))


def _gdn_kernel(*refs, c, G, has_state):
    if has_state:
        (xq_ref, xk_ref, xv_ref, ab_ref, alog_ref, dtb_ref, z_ref, ng_ref, s0_ref, o_ref, s_ref) = refs
    else:
        (xq_ref, xk_ref, xv_ref, ab_ref, alog_ref, dtb_ref, z_ref, ng_ref, o_ref, s_ref) = refs
        s0_ref = None
    hg = pl.program_id(1)
    n = pl.program_id(2)
    H = N_GDN_HEADS

    @pl.when(n == 0)
    def _init():
        if has_state:
            s_ref[0] = s0_ref[0]
        else:
            s_ref[...] = jnp.zeros(s_ref.shape, F32)

    conv = [xq_ref[0], xk_ref[0], xv_ref[0]]

    ab = ab_ref[0]
    a_in = ab[:, 0:H] + dtb_ref[...]
    softplus = jnp.maximum(a_in, 0.0) + jnp.log1p(jnp.exp(-jnp.abs(a_in)))
    g_col = -jnp.exp(alog_ref[...]) * softplus
    beta_col = _sigmoid(ab[:, H:2 * H])
    r = lax.broadcasted_iota(jnp.int32, (c, c), 0)
    s = lax.broadcasted_iota(jnp.int32, (c, c), 1)
    tri = r >= s
    strict = r > s
    tri_f = tri.astype(F32)
    eye_c = (r == s).astype(F32)
    gc_col = jnp.dot(tri_f, g_col, preferred_element_type=F32, precision=HI)
    eye_h = (lax.broadcasted_iota(jnp.int32, (H, H), 0) == lax.broadcasted_iota(jnp.int32, (H, H), 1)).astype(F32)
    gc_row = lax.dot_general(eye_h, gc_col, (((1,), (1,)), ((), ())), preferred_element_type=F32,
                             precision=HI)

    def bf(a):
        return a.astype(BF16)

    def split(a):
        hi = a.astype(BF16)
        return hi, (a - hi.astype(F32)).astype(BF16)

    def dot3(a, b):
        (ah, al), (bh, bl) = a, b
        return (jnp.dot(al, bh, preferred_element_type=F32) + jnp.dot(ah, bl, preferred_element_type=F32)
                + jnp.dot(ah, bh, preferred_element_type=F32))

    blocks = []
    b_ = 1
    while b_ < c:
        blocks.append((((r // (2 * b_)) == (s // (2 * b_))) & ((r // b_) != (s // b_)) & strict).astype(F32))
        b_ *= 2

    heads = range(G)
    lanes = [slice(hh * GDN_DK, (hh + 1) * GDN_DK) for hh in heads]
    if G == H:
        gcc = [gc_col[:, hh:hh + 1] for hh in heads]
        bet = [beta_col[:, hh:hh + 1] for hh in heads]
        gcr = [gc_row[hh:hh + 1, :] for hh in heads]
    else:
        gcc, bet, gcr = [], [], []
        for hh in heads:
            head = hg * G + hh
            sel_c = (lax.broadcasted_iota(jnp.int32, (c, H), 1) == head).astype(F32)
            sel_r = (lax.broadcasted_iota(jnp.int32, (H, c), 0) == head).astype(F32)
            gcc.append(jnp.sum(gc_col * sel_c, axis=1, keepdims=True))
            bet.append(jnp.sum(beta_col * sel_c, axis=1, keepdims=True))
            gcr.append(jnp.sum(gc_row * sel_r, axis=0, keepdims=True))
    g_last = [g[c - 1:c, :] for g in gcc]

    q = [conv[0][:, lanes[hh]].astype(F32) for hh in heads]
    k = [conv[1][:, lanes[hh]].astype(F32) for hh in heads]
    v = [conv[2][:, lanes[hh]].astype(F32) for hh in heads]

    decay = [jnp.exp(jnp.where(tri, gcc[hh] - gcr[hh], -jnp.inf)) for hh in heads]
    kb = [k[hh] * bet[hh] for hh in heads]
    k16 = [bf(k[hh]) for hh in heads]
    mmat = [jnp.where(strict, _nt_dot(bf(kb[hh]), k16[hh]) * decay[hh], 0.0) for hh in heads]
    qk = [jnp.where(tri, _nt_dot(bf(q[hh]), k16[hh]) * decay[hh], 0.0) for hh in heads]

    tinv = [eye_c - mmat[hh] * blocks[0] for hh in heads]
    for lvl in range(1, len(blocks)):
        d16 = [bf(t) for t in tinv]
        x = [jnp.dot(bf(mmat[hh] * blocks[lvl]), d16[hh], preferred_element_type=F32) for hh in heads]
        tinv = [tinv[hh] - jnp.dot(d16[hh], bf(x[hh]), preferred_element_type=F32) for hh in heads]
    m_s = [split(m) for m in mmat]
    t_s = [split(t) for t in tinv]
    res = [(eye_c - tinv[hh]) - dot3(m_s[hh], t_s[hh]) for hh in heads]
    tinv = [tinv[hh] + jnp.dot(t_s[hh][0], bf(res[hh]), preferred_element_type=F32) for hh in heads]

    rhs = [jnp.concatenate([v[hh] * bet[hh], kb[hh] * jnp.exp(gcc[hh])], axis=1) for hh in heads]
    sol = [dot3(split(tinv[hh]), split(rhs[hh])) for hh in heads]
    u = [x_[:, :GDN_DV] for x_ in sol]
    w = [x_[:, GDN_DV:] for x_ in sol]

    S = [s_ref[0, hh] for hh in heads]
    S16 = [bf(x_) for x_ in S]
    v_new = [u[hh] - jnp.dot(bf(w[hh]), S16[hh], preferred_element_type=F32) for hh in heads]
    v16 = [bf(x_) for x_ in v_new]
    o = [jnp.dot(bf(q[hh] * jnp.exp(gcc[hh])), S16[hh], preferred_element_type=F32)
         + jnp.dot(bf(qk[hh]), v16[hh], preferred_element_type=F32) for hh in heads]
    for hh in heads:
        kd = bf(k[hh] * jnp.exp(g_last[hh] - gcc[hh]))
        s_ref[0, hh] = S[hh] * jnp.exp(g_last[hh]) + lax.dot_general(kd, v16[hh], (((0,), (0,)), ((), ())),
                                                                     preferred_element_type=F32)
    for hh in heads:
        ms = jnp.mean(o[hh] * o[hh], axis=-1, keepdims=True)
        y = o[hh] * lax.rsqrt(ms + NORM_EPS) * ng_ref[...] * z_ref[0][:, lanes[hh]].astype(F32)
        o_ref[0, :, lanes[hh]] = y.astype(o_ref.dtype)


def _gdn(q, k, v, gab, a_log, dt_bias, z_silu, norm_g, s0, *, c, G=16):
    B, T, _ = q.shape
    H = N_GDN_HEADS
    assert T % c == 0 and c % 8 == 0 and (c & (c - 1)) == 0 and H % G == 0
    nG = H // G
    W = G * GDN_DK
    stream = pl.BlockSpec((1, c, W), lambda b, g, n: (b, n, g))
    small = pl.BlockSpec((1, H), lambda b, g, n: (0, 0))
    state = pl.BlockSpec((1, G, GDN_DK, GDN_DV), lambda b, g, n: (b, g, 0, 0))
    in_specs = [stream, stream, stream,
                pl.BlockSpec((1, c, 128), lambda b, g, n: (b, n, 0)), small, small,
                stream, pl.BlockSpec((1, GDN_DV), lambda b, g, n: (0, 0))]
    args = [q, k, v, gab, a_log, dt_bias, z_silu, norm_g]
    if s0 is not None:
        in_specs.append(state)
        args.append(s0)
    return pl.pallas_call(
        functools.partial(_gdn_kernel, c=c, G=G, has_state=s0 is not None),
        out_shape=(jax.ShapeDtypeStruct((B, T, GDN_WIDTH), BF16),
                   jax.ShapeDtypeStruct((B, H, GDN_DK, GDN_DV), F32)),
        grid=(B, nG, T // c),
        in_specs=in_specs,
        out_specs=(stream, state),
        compiler_params=_cparams(("parallel", "parallel", "arbitrary")),
        name="gdn",
    )(*args)


def _mix_kernel(oa_ref, ob_ref, wa_ref, wb_ref, sa_ref, sb_ref, o_ref):
    a = jnp.dot(oa_ref[...], wa_ref[...], preferred_element_type=F32)
    b = jnp.dot(ob_ref[...], wb_ref[...], preferred_element_type=F32)
    o_ref[...] = (sa_ref[...].astype(F32) * a + sb_ref[...].astype(F32) * b).astype(o_ref.dtype)


def _mix(o_a, o_b, w_pa, w_pb, gates, *, tm=512, tn=1024):
    M = o_a.shape[0]
    tm = min(tm, M)
    nb = D_MODEL // tn
    return pl.pallas_call(
        _mix_kernel,
        out_shape=jax.ShapeDtypeStruct((M, D_MODEL), BF16),
        grid=(M // tm, nb),
        in_specs=[pl.BlockSpec((tm, DIFF_WIDTH), lambda i, j: (i, 0)),
                  pl.BlockSpec((tm, GDN_WIDTH), lambda i, j: (i, 0)),
                  pl.BlockSpec((DIFF_WIDTH, tn), lambda i, j: (0, j)),
                  pl.BlockSpec((GDN_WIDTH, tn), lambda i, j: (0, j)),
                  pl.BlockSpec((tm, tn), lambda i, j: (i, j)),
                  pl.BlockSpec((tm, tn), lambda i, j: (i, j + nb))],
        out_specs=pl.BlockSpec((tm, tn), lambda i, j: (i, j)),
        compiler_params=_cparams(("parallel", "parallel")),
        name="mix",
    )(o_a, o_b, w_pa, w_pb, gates, gates)


def _post_kernel(x_ref, mix_ref, wo_ref, g1_ref, b1_ref, wxq_ref, mk_ref, mv_ref, wxo_ref, g2_ref, b2_ref,
                 h2_ref, h2b_ref):
    h1 = ALPHA * x_ref[0] + jnp.dot(mix_ref[0], wo_ref[...], preferred_element_type=F32)
    h1 = _layer_norm(h1, g1_ref[...], b1_ref[...])
    qx = jnp.dot(h1.astype(BF16), wxq_ref[...], preferred_element_type=F32) * (XHEAD_DIM ** -0.5)
    qx = qx.astype(BF16)
    mk = mk_ref[0]
    mv = mv_ref[0]
    heads = []
    for hh in range(N_XHEADS):
        sl = slice(hh * XHEAD_DIM, (hh + 1) * XHEAD_DIM)
        s = _nt_dot(qx[:, sl], mk[:, sl])
        p = jnp.exp(s - jnp.max(s, axis=-1, keepdims=True))
        p = p / jnp.sum(p, axis=-1, keepdims=True)
        heads.append(jnp.dot(p.astype(BF16), mv[:, sl], preferred_element_type=F32))
    ox = jnp.concatenate(heads, axis=1).astype(BF16)
    h2 = ALPHA * h1 + jnp.dot(ox, wxo_ref[...], preferred_element_type=F32)
    h2 = _layer_norm(h2, g2_ref[...], b2_ref[...])
    h2_ref[0] = h2
    h2b_ref[0] = h2.astype(BF16)


def _post(x, mix, w_o, ln1_g, ln1_b, w_xq, mem_k, mem_v, w_xo, ln2_g, ln2_b, *, tm=512):
    B, T, D = x.shape
    tm = min(tm, T)
    const = lambda shape: pl.BlockSpec(shape, lambda b, i: (0, 0), pipeline_mode=pl.Buffered(1))
    rows = lambda: pl.BlockSpec((1, tm, D), lambda b, i: (b, i, 0))
    mem = lambda: pl.BlockSpec((1, N_MEM, XWIDTH), lambda b, i: (b, 0, 0))
    return pl.pallas_call(
        _post_kernel,
        out_shape=(jax.ShapeDtypeStruct((B, T, D), F32), jax.ShapeDtypeStruct((B, T, D), BF16)),
        grid=(B, T // tm),
        in_specs=[rows(), rows(), const((D, D)), const((1, D)), const((1, D)), const((D, XWIDTH)),
                  mem(), mem(), const((XWIDTH, D)), const((1, D)), const((1, D))],
        out_specs=(rows(), rows()),
        compiler_params=_cparams(("parallel", "parallel")),
        name="post_attn",
    )(x, mix, w_o, ln1_g, ln1_b, w_xq, mem_k, mem_v, w_xo, ln2_g, ln2_b)


def _ffn_kernel(hb_ref, h_ref, w1_ref, w3_ref, w2_ref, g_ref, b_ref, y_ref, acc_ref):
    f = pl.program_id(1)

    @pl.when(f == 0)
    def _():
        acc_ref[...] = jnp.zeros(acc_ref.shape, F32)

    hb = hb_ref[...]
    a = jnp.dot(hb, w1_ref[...], preferred_element_type=F32)
    b = jnp.dot(hb, w3_ref[...], preferred_element_type=F32)
    act = (_silu(a) * b).astype(BF16)
    acc_ref[...] += jnp.dot(act, w2_ref[...], preferred_element_type=F32)

    @pl.when(f == pl.num_programs(1) - 1)
    def _():
        y_ref[...] = _layer_norm(ALPHA * h_ref[...] + acc_ref[...], g_ref[...], b_ref[...])


def _ffn(h2b, h2, w1, w3, w2, ln_g, ln_b, *, tm=512, tf=512):
    M, D = h2.shape
    tm = min(tm, M)
    assert M % tm == 0 and D_FF % tf == 0
    return pl.pallas_call(
        _ffn_kernel,
        out_shape=jax.ShapeDtypeStruct((M, D), F32),
        grid=(M // tm, D_FF // tf),
        in_specs=[pl.BlockSpec((tm, D), lambda i, f: (i, 0)),
                  pl.BlockSpec((tm, D), lambda i, f: (i, 0)),
                  pl.BlockSpec((D, tf), lambda i, f: (0, f)),
                  pl.BlockSpec((D, tf), lambda i, f: (0, f)),
                  pl.BlockSpec((tf, D), lambda i, f: (f, 0)),
                  pl.BlockSpec((1, D), lambda i, f: (0, 0)),
                  pl.BlockSpec((1, D), lambda i, f: (0, 0))],
        out_specs=pl.BlockSpec((tm, D), lambda i, f: (i, 0)),
        scratch_shapes=[pltpu.VMEM((tm, D), F32)],
        compiler_params=_cparams(("parallel", "arbitrary")),
        name="ffn",
    )(h2b, h2, w1, w3, w2, ln_g, ln_b)


def _encoder_layer(x, mem_k, mem_v, past, W):
    B, T, D = x.shape
    M = B * T
    xb = x.reshape(M, D).astype(BF16)
    w_in = W["w_in_t"]
    in_proj = functools.partial(_proj, xb, w_in, w_rows=True)

    (dq,) = in_proj(OFF_DQ, DIFF_WIDTH, (BF16,), scale=DIFF_HEAD_DIM ** -0.5, name="proj_dq")
    kv_dtypes = (F32, BF16) if past is None else (F32,)
    dk, *dkb = in_proj(OFF_DK, DIFF_WIDTH, kv_dtypes, rows_inner=True, name="proj_dk")
    dv, *dvb = in_proj(OFF_DV, DIFF_WIDTH, kv_dtypes, rows_inner=True, name="proj_dv")
    buf0 = jnp.zeros((B, CONV_W - 1, 3 * GDN_WIDTH), F32) if past is None else past[3]
    streams, tails = [], []
    for t, (norm, scale) in enumerate(((True, GDN_DK ** -0.5), (True, 1.0), (False, 1.0))):
        y, tail = _proj_conv(xb, w_in, OFF_GQKV + t * GDN_WIDTH, buf0, W["conv_w"], B=B, T=T, norm=norm,
                             scale=scale, name="proj_conv_" + "qkv"[t])
        streams.append(y.reshape(B, T, GDN_WIDTH))
        tails.append(tail[:, CONV_PAD - (CONV_W - 1):, :])
    (gz,) = in_proj(OFF_GZ, GDN_WIDTH, (BF16,), act="silu", name="proj_gz")
    (gab,) = in_proj(OFF_GAB, 128, (F32,), tn=128, name="proj_gab")
    (gates,) = in_proj(OFF_GATES, 2 * D_MODEL, (BF16,), act="sigmoid", name="proj_gates")

    dq = dq.reshape(B, T, DIFF_WIDTH)
    dk = dk.reshape(B, T, N_DIFF_HEADS, DIFF_VDIM)
    dv = dv.reshape(B, T, N_DIFF_HEADS, DIFF_VDIM)
    lam_rows = W["lam_rows"]
    if past is None:
        o_a = _diff_attention_prompt(dq, dkb[0].reshape(B, T, DIFF_WIDTH), dvb[0].reshape(B, T, DIFF_WIDTH),
                                     lam_rows, W["diff_subln_g"])
        s0 = None
        c = CHUNK
    else:
        cache_k, cache_v, s0, _ = past
        o_a = _diff_attention_sample(dq, cache_k, cache_v, dk, dv, lam_rows, W["diff_subln_g"])
        c = T
    o_b, s_new = _gdn(*streams, gab.reshape(B, T, 128), W["gdn_a_log"], W["gdn_dt_bias"],
                      gz.reshape(B, T, GDN_WIDTH), W["gdn_norm_g"], s0, c=c)
    new_buf = jnp.concatenate(tails, axis=-1)

    mix = _mix(o_a.reshape(M, DIFF_WIDTH), o_b.reshape(M, GDN_WIDTH), W["w_pa"], W["w_pb"], gates)
    h2, h2b = _post(x, mix.reshape(B, T, D), W["w_o"], W["ln1_g"], W["ln1_b"], W["w_xq"], mem_k, mem_v,
                    W["w_xo"], W["ln2_g"], W["ln2_b"])
    y = _ffn(h2b.reshape(M, D), h2.reshape(M, D), W["w_ff1"], W["w_ff3"], W["w_ff2"], W["ln3_g"], W["ln3_b"])
    return y.reshape(B, T, D), dk, dv, s_new, new_buf


def kernel(x_prompt, x_sample, mem_prompt, cache_diff_k, cache_diff_v, state_gdn, state_gdn_conv, cache_mem_k, cache_mem_v, w_in, conv_w, lam_q1, lam_k1, lam_q2, lam_k2, diff_subln_g, gdn_a_log, gdn_dt_bias, gdn_norm_g, w_pa, w_pb, w_o, ln1_g, ln1_b, w_xq, w_xk, w_xv, w_xo, ln2_g, ln2_b, w_ff1, w_ff3, w_ff2, ln3_g, ln3_b):
    l = 0
    W = {
        "w_in_t": jnp.swapaxes(w_in, 1, 2),
        "conv_w": conv_w[l],
        "lam_rows": tuple(v[l].reshape(1, DIFF_HEAD_DIM) for v in (lam_q1, lam_k1, lam_q2, lam_k2)),
        "diff_subln_g": diff_subln_g[l].reshape(1, DIFF_VDIM),
        "gdn_a_log": gdn_a_log[l].reshape(1, N_GDN_HEADS),
        "gdn_dt_bias": gdn_dt_bias[l].reshape(1, N_GDN_HEADS),
        "gdn_norm_g": gdn_norm_g[l].reshape(1, GDN_DV),
        "w_pa": w_pa[l].astype(BF16), "w_pb": w_pb[l].astype(BF16), "w_o": w_o[l].astype(BF16),
        "ln1_g": ln1_g[l].reshape(1, D_MODEL), "ln1_b": ln1_b[l].reshape(1, D_MODEL),
        "w_xq": w_xq[l].astype(BF16), "w_xo": w_xo[l].astype(BF16),
        "ln2_g": ln2_g[l].reshape(1, D_MODEL), "ln2_b": ln2_b[l].reshape(1, D_MODEL),
        "w_ff1": w_ff1[l].astype(BF16), "w_ff3": w_ff3[l].astype(BF16), "w_ff2": w_ff2[l].astype(BF16),
        "ln3_g": ln3_g[l].reshape(1, D_MODEL), "ln3_b": ln3_b[l].reshape(1, D_MODEL),
    }
    Bp = x_prompt.shape[0]
    memb = mem_prompt.reshape(Bp * N_MEM, D_MODEL).astype(BF16)
    mem_k, mem_kb = _proj(memb, w_xk, 0, XWIDTH, (F32, BF16), tn=XWIDTH, name="proj_mem_k")
    mem_v, mem_vb = _proj(memb, w_xv, 0, XWIDTH, (F32, BF16), tn=XWIDTH, name="proj_mem_v")

    yp, pk, pv, ps, pc = _encoder_layer(x_prompt, mem_kb.reshape(Bp, N_MEM, XWIDTH),
                                        mem_vb.reshape(Bp, N_MEM, XWIDTH), None, W)
    Bs = x_sample.shape[0]
    past = (cache_diff_k[l], cache_diff_v[l], state_gdn[l], state_gdn_conv[l])
    ys, sk, sv, ss, sc = _encoder_layer(x_sample, cache_mem_k[l].reshape(Bs, N_MEM, XWIDTH).astype(BF16),
                                        cache_mem_v[l].reshape(Bs, N_MEM, XWIDTH).astype(BF16), past, W)
    st = lambda a: a[None]
    return (yp, ys, st(pk), st(pv), st(ps), st(pc),
            st(mem_k.reshape(Bp, N_MEM, N_XHEADS, XHEAD_DIM)), st(mem_v.reshape(Bp, N_MEM, N_XHEADS, XHEAD_DIM)),
            st(sk), st(sv), st(ss), st(sc))
```

```python
import functools
import math

import jax
import jax.numpy as jnp
from jax import lax
from jax.experimental import pallas as pl
from jax.experimental.pallas import tpu as pltpu

D_MODEL = 2048
CHUNK = 64
N_DIFF_HEADS = 8
DIFF_HEAD_DIM = 128
DIFF_VDIM = 2 * DIFF_HEAD_DIM
DIFF_WIDTH = N_DIFF_HEADS * DIFF_VDIM
N_GDN_HEADS = 16
GDN_DK = 128
GDN_DV = 128
GDN_WIDTH = N_GDN_HEADS * GDN_DK
CONV_W = 4
N_MEM = 256
N_XHEADS = 4
XHEAD_DIM = 128
XWIDTH = N_XHEADS * XHEAD_DIM
D_FF = 5632
DEPTH = 1
ALPHA = (2.0 * DEPTH) ** 0.25
LN_EPS = 1e-5
NORM_EPS = 1e-6
LAM_INIT = 0.8 - 0.6 * math.exp(-0.3 * 0)

OFF_DQ = 0
OFF_DK = 2048
OFF_DV = 4096
OFF_GQKV = 6144
OFF_GZ = 12288
OFF_GAB = 14336
OFF_GATES = 14368

VMEM_LIMIT = 56 * 1024 * 1024
BF16 = jnp.bfloat16
F32 = jnp.float32
HI = lax.Precision.HIGHEST


def _cparams(sem):
    return pltpu.CompilerParams(dimension_semantics=sem, vmem_limit_bytes=VMEM_LIMIT)


def _sigmoid(x):
    return 1.0 / (1.0 + jnp.exp(-x))


def _silu(x):
    return x * _sigmoid(x)


def _layer_norm(x, g, b):
    mu = jnp.mean(x, axis=-1, keepdims=True)
    xc = x - mu
    var = jnp.mean(xc * xc, axis=-1, keepdims=True)
    return xc * lax.rsqrt(var + LN_EPS) * g + b


def _proj_kernel(x_ref, w_ref, *o_refs, act, scale, w_rows):
    w = w_ref[0].astype(BF16)
    acc = _nt_dot(x_ref[...], w) if w_rows else jnp.dot(x_ref[...], w, preferred_element_type=F32)
    if scale != 1.0:
        acc = acc * scale
    if act == "sigmoid":
        acc = _sigmoid(acc)
    elif act == "silu":
        acc = _silu(acc)
    for o in o_refs:
        o[...] = acc.astype(o.dtype)


def _weight_spec(w, K, tn, col_start, w_rows):
    assert w.shape[0] == DEPTH == 1
    if w_rows:
        return pl.BlockSpec((pl.Element(1), pl.Element(tn), pl.Element(K)),
                            lambda a, b: (0, pl.multiple_of(col_start(a, b), 8), 0))
    return pl.BlockSpec((1, K, tn), lambda a, b: (0, 0, col_start(a, b) // tn))


def _proj(x, w, col_off, n_cols, out_dtypes, *, act=None, scale=1.0, tm=1024, tn=1024, rows_inner=False,
          w_rows=False, name="proj"):
    M, K = x.shape
    tm = min(tm, M)
    tn = min(tn, n_cols)
    assert M % tm == 0 and n_cols % tn == 0 and col_off % (8 if w_rows else tn) == 0
    if rows_inner:
        grid = (n_cols // tn, M // tm)
        ij = lambda a, b: (b, a)
    else:
        grid = (M // tm, n_cols // tn)
        ij = lambda a, b: (a, b)
    outs = pl.pallas_call(
        functools.partial(_proj_kernel, act=act, scale=scale, w_rows=w_rows),
        out_shape=tuple(jax.ShapeDtypeStruct((M, n_cols), dt) for dt in out_dtypes),
        grid=grid,
        in_specs=[pl.BlockSpec((tm, K), lambda a, b: (ij(a, b)[0], 0)),
                  _weight_spec(w, K, tn, lambda a, b: col_off + ij(a, b)[1] * tn, w_rows)],
        out_specs=tuple(pl.BlockSpec((tm, tn), lambda a, b: ij(a, b)) for _ in out_dtypes),
        compiler_params=_cparams(("parallel", "parallel")),
        name=name,
    )(x, w)
    return outs


CONV_PAD = 8


def _proj_conv_kernel(x_ref, w_ref, hist_ref, cw_ref, y_ref, raw_ref, pad_ref, *, nb, T, sub, tr_max, norm, scale):
    tn = w_ref.shape[1]
    tr = min(T, tr_max) if nb == 1 else T
    units = [(s, r) for s in range(tn // sub) for r in range(T // tr)]
    w_slabs = [w_ref[0, s * sub:(s + 1) * sub, :].astype(BF16) for s in range(tn // sub)]

    def rows_of(r):
        return slice(r * tr, (r + 1) * tr) if nb == 1 else slice(None)

    def matmul(s, r, gate):
        rows = rows_of(r)
        if gate is None:
            lhs = x_ref[rows, :]
        else:
            n_rows = nb * tr
            first = x_ref[rows, 0:256] + jnp.concatenate([jnp.concatenate([gate] * 2, axis=1)] * (n_rows // 16), axis=0)
            lhs = jnp.concatenate([first, x_ref[rows, 256:]], axis=1)
        return _nt_dot(lhs, w_slabs[s])

    def epilogue(s, r, acc):
        cols = slice(s * sub, (s + 1) * sub)
        slot = s % 2
        r0 = r * tr
        a3 = acc.reshape(nb, tr, sub)
        pad_ref[slot, :, CONV_PAD + r0:CONV_PAD + r0 + tr, :] = a3
        if r == 0:
            pad_ref[slot, :, CONV_PAD - (CONV_W - 1):CONV_PAD, :] = hist_ref[:, :, cols]
        if r == T // tr - 1:
            raw_ref[:, :, cols] = a3[:, tr - CONV_PAD:, :]
        y = cw_ref[CONV_W - 1:CONV_W, cols] * a3
        for j in range(CONV_W - 1):
            lo = CONV_PAD - (CONV_W - 1) + j + r0
            y = y + cw_ref[j:j + 1, cols] * pad_ref[slot, :, lo:lo + tr, :]
        y = _silu(y)
        if norm:
            heads = []
            for hh in range(sub // GDN_DK):
                yh = y[:, :, hh * GDN_DK:(hh + 1) * GDN_DK]
                heads.append(yh * (lax.rsqrt(jnp.sum(yh * yh, axis=-1, keepdims=True) + NORM_EPS) * scale))
            y = jnp.concatenate(heads, axis=-1)
        y2 = y.reshape(nb * tr, sub)
        y_ref[rows_of(r), cols] = y2.astype(y_ref.dtype)
        bits = pltpu.bitcast(y2[nb * tr - 16:, 0:128], jnp.uint32)
        zero = lax.shift_right_logical(lax.shift_right_logical(bits, jnp.uint32(16)), jnp.uint32(16))
        return zero.astype(F32).astype(BF16)

    gates = [None, None]
    acc = matmul(*units[0], None)
    for i, u in enumerate(units):
        nxt = matmul(*units[i + 1], gates[i + 1]) if i + 1 < len(units) else None
        gates.append(epilogue(*u, acc))
        acc = nxt


def _proj_conv(x, w, col_off, hist, conv_w, *, B, T, norm, scale=1.0, tn=512, sub=512, tr=512, name="proj_conv"):
    M, K = x.shape
    n_cols = GDN_WIDTH
    nb = max(1, 256 // T) if T < 256 else 1
    nb = min(nb, B)
    tm = nb * T
    gq = OFF_GQKV
    assert B % nb == 0 and n_cols % tn == 0 and col_off % 8 == 0 and (col_off - gq) % tn == 0 and T >= CONV_PAD
    hb = (col_off - gq) // tn
    return pl.pallas_call(
        functools.partial(_proj_conv_kernel, nb=nb, T=T, sub=sub, tr_max=tr, norm=norm, scale=scale),
        out_shape=(jax.ShapeDtypeStruct((M, n_cols), BF16), jax.ShapeDtypeStruct((B, CONV_PAD, n_cols), F32)),
        grid=(B // nb, n_cols // tn),
        in_specs=[pl.BlockSpec((tm, K), lambda i, j: (i, 0)),
                  _weight_spec(w, K, tn, lambda i, j: col_off + j * tn, True),
                  pl.BlockSpec((nb, CONV_W - 1, tn), lambda i, j: (i, 0, j + hb)),
                  pl.BlockSpec((CONV_W, tn), lambda i, j: (0, j + hb))],
        out_specs=(pl.BlockSpec((tm, tn), lambda i, j: (i, j)),
                   pl.BlockSpec((nb, CONV_PAD, tn), lambda i, j: (i, 0, j))),
        scratch_shapes=[pltpu.VMEM((min(2, tn // sub), nb, T + CONV_PAD, sub), F32)],
        compiler_params=_cparams(("parallel", "parallel")),
        name=name,
    )(x, w, hist, conv_w)


def _lam_value(lq1, lk1, lq2, lk2):
    a = jnp.sum(lq1 * lk1, axis=-1, keepdims=True)
    b = jnp.sum(lq2 * lk2, axis=-1, keepdims=True)
    return jnp.exp(a) - jnp.exp(b) + LAM_INIT


def _subln(o, g):
    ms = jnp.mean(o * o, axis=-1, keepdims=True)
    return o * lax.rsqrt(ms + NORM_EPS) * g * (1.0 - LAM_INIT)


def _head_slope(h):
    e = (h + 1).astype(F32) * (-8.0 / N_DIFF_HEADS)
    return jnp.exp2(jnp.full((1, 1), e, F32))


def _nt_dot(a, b):
    return lax.dot_general(a, b, (((1,), (1,)), ((), ())), preferred_element_type=F32)


def _lane_tile(x, n):
    return x if n == 1 else jnp.concatenate([x] * n, axis=1)


def _diff_prompt_kernel(lq1_ref, lk1_ref, lq2_ref, lk2_ref, g_ref, base_ref, q_ref, k_ref, v_ref, o_ref,
                        m_ref, l_ref, acc_ref, *, tq):
    LANES = 128
    half = tq // 2
    h = pl.program_id(1)
    qi = pl.program_id(2)
    slope = _head_slope(h)
    lam = _lam_value(lq1_ref[...], lk1_ref[...], lq2_ref[...], lk2_ref[...])
    col_bias = slope * lax.broadcasted_iota(jnp.int32, (1, tq), 1).astype(F32)
    maps = range(2)

    m_ref[...] = jnp.full(m_ref.shape, -jnp.inf, F32)
    l_ref[...] = jnp.zeros(l_ref.shape, F32)
    acc_ref[...] = jnp.zeros(acc_ref.shape, F32)

    def update(*jobs):
        todo = [(job, m) for job in jobs for m in maps]
        t = [_nt_dot(q_ref[0, rows, m * DIFF_HEAD_DIM:(m + 1) * DIFF_HEAD_DIM],
                     kblk[:, m * DIFF_HEAD_DIM:(m + 1) * DIFF_HEAD_DIM]) + bias
             for (rows, kblk, _, bias, _), m in todo]
        m_old = [m_ref[m, rows] for (rows, *_), m in todo]
        m_new = [jnp.maximum(m_old[i], jnp.max(t[i], axis=-1, keepdims=True) + job[4])
                 for i, (job, m) in enumerate(todo)]
        a = [jnp.exp(m_old[i] - m_new[i]) for i in range(len(todo))]
        p = [jnp.exp(t[i] - _lane_tile(m_new[i] - job[4], job[1].shape[0] // LANES))
             for i, (job, m) in enumerate(todo)]
        pv = [jnp.dot(p[i].astype(BF16), job[2], preferred_element_type=F32) for i, (job, m) in enumerate(todo)]
        for i, ((rows, kblk, *_), m) in enumerate(todo):
            psum = p[i][:, :LANES]
            for c in range(1, kblk.shape[0] // LANES):
                psum = psum + p[i][:, c * LANES:(c + 1) * LANES]
            l_ref[m, rows] = a[i] * l_ref[m, rows] + psum
            acc_ref[m, rows] = _lane_tile(a[i], DIFF_VDIM // LANES) * acc_ref[m, rows] + pv[i]
            m_ref[m, rows] = m_new[i]

    every = slice(0, tq)

    def body(j, carry):
        start = pl.multiple_of(j * tq, tq)
        gap = ((qi - j) * tq).astype(F32)
        update((every, k_ref[0, pl.ds(start, tq), :], v_ref[0, pl.ds(start, tq), :], col_bias, -slope * gap))
        return carry

    lax.fori_loop(0, qi, body, 0)

    lo, hi = slice(0, half), slice(half, tq)
    start = pl.multiple_of(qi * tq, tq)
    zero = jnp.zeros((1, 1), F32)
    update((lo, k_ref[0, pl.ds(start, half), :], v_ref[0, pl.ds(start, half), :], slope * base_ref[:, half:],
            -slope * float(half)),
           (hi, k_ref[0, pl.ds(start, tq), :], v_ref[0, pl.ds(start, tq), :], slope * base_ref[...], zero))

    outs = []
    for m in maps:
        inv = 1.0 / jnp.sum(l_ref[m], axis=-1, keepdims=True)
        outs.append(acc_ref[m] * inv)
    o = outs[0] - lam * outs[1]
    o_ref[0] = _subln(o, g_ref[...]).astype(o_ref.dtype)


def _diff_attention_prompt(q, k, v, lam_rows, subln_g, *, tq=512):
    B, T, _ = q.shape
    tq = min(tq, T)
    half = tq // 2
    assert T % tq == 0 and half % CHUNK == 0
    ii = lax.broadcasted_iota(jnp.int32, (half, tq), 0)
    jj = lax.broadcasted_iota(jnp.int32, (half, tq), 1)
    jh = jj - half
    diag = jnp.where((jh // CHUNK) <= (ii // CHUNK), (half + ii - jnp.abs(ii - jh)).astype(F32), -jnp.inf)
    base = jnp.where(jj < half, jj.astype(F32), diag)
    vec = pl.BlockSpec((1, DIFF_HEAD_DIM), lambda b, h, i: (0, 0))
    return pl.pallas_call(
        functools.partial(_diff_prompt_kernel, tq=tq),
        out_shape=jax.ShapeDtypeStruct((B, T, DIFF_WIDTH), BF16),
        grid=(B, N_DIFF_HEADS, T // tq),
        in_specs=[vec, vec, vec, vec,
                  pl.BlockSpec((1, DIFF_VDIM), lambda b, h, i: (0, 0)),
                  pl.BlockSpec((half, tq), lambda b, h, i: (0, 0)),
                  pl.BlockSpec((1, tq, DIFF_VDIM), lambda b, h, i: (b, i, h)),
                  pl.BlockSpec((1, T, DIFF_VDIM), lambda b, h, i: (b, 0, h)),
                  pl.BlockSpec((1, T, DIFF_VDIM), lambda b, h, i: (b, 0, h))],
        out_specs=pl.BlockSpec((1, tq, DIFF_VDIM), lambda b, h, i: (b, i, h)),
        scratch_shapes=[pltpu.VMEM((2, tq, 128), F32), pltpu.VMEM((2, tq, 128), F32),
                        pltpu.VMEM((2, tq, DIFF_VDIM), F32)],
        compiler_params=_cparams(("parallel", "parallel", "parallel")),
        name="diff_attn_prompt",
    )(*lam_rows, subln_g, base, q, k, v)


def _diff_sample_kernel(lq1_ref, lk1_ref, lq2_ref, lk2_ref, g_ref, q_ref, kc_ref, vc_ref, kn_ref, vn_ref,
                        o_ref, *, past, pc):
    H = N_DIFF_HEADS
    T = q_ref.shape[1]
    L = H * T
    rows = pc * H
    n_chunks = past // pc
    lam = _lam_value(lq1_ref[...], lk1_ref[...], lq2_ref[...], lk2_ref[...])
    maps = range(2)

    def map_cols(m):
        return slice(m * DIFF_HEAD_DIM, (m + 1) * DIFF_HEAD_DIM)

    qm = [jnp.concatenate([q_ref[0, :, h * DIFF_VDIM + m * DIFF_HEAD_DIM:h * DIFF_VDIM + (m + 1) * DIFF_HEAD_DIM]
                           for h in range(H)], axis=0) for m in maps]
    lane = lax.broadcasted_iota(jnp.int32, (1, L), 1)
    h_lane = lane // T
    t_lane = lane % T
    slope = jnp.exp2((h_lane + 1).astype(F32) * (-8.0 / H))
    base = slope * (lax.broadcasted_iota(jnp.int32, (rows, L), 0) // H).astype(F32)
    valid = lax.broadcasted_iota(jnp.int32, (H, L), 0) == h_lane

    def chunk_shift(c):
        return slope * ((c * pc - past) - t_lane).astype(F32)

    def chunk_scores(c):
        r0 = pl.multiple_of(c * rows, rows)
        kblk = kc_ref[0, pl.ds(r0, rows), :].astype(BF16)
        return [(_nt_dot(kblk[:, map_cols(m)], qm[m]) + base).reshape(pc, H, L) for m in maps]

    def stats_step(c, carry):
        shift = chunk_shift(c)
        s = chunk_scores(c)
        out = []
        for m in maps:
            m_old, l_old = carry[2 * m], carry[2 * m + 1]
            m_new = jnp.maximum(m_old, jnp.max(s[m], axis=0) + shift)
            l_new = l_old * jnp.exp(m_old - m_new) + jnp.sum(jnp.exp(s[m] - (m_new - shift)[None]), axis=0)
            out += [m_new, l_new]
        return tuple(out)

    init = (jnp.full((H, L), -jnp.inf, F32), jnp.zeros((H, L), F32)) * 2
    stats = lax.fori_loop(0, n_chunks, stats_step, init)

    kn = kn_ref[0].astype(BF16)
    vn = vn_ref[0].astype(BF16)
    tk = lax.broadcasted_iota(jnp.int32, (T * H, L), 0) // H
    allowed = ((past + tk) // CHUNK) <= ((past + t_lane) // CHUNK)
    bias_n = jnp.where(allowed, -slope * jnp.abs(tk - t_lane).astype(F32), -jnp.inf)
    s_n = [(_nt_dot(kn[:, map_cols(m)], qm[m]) + bias_n).reshape(T, H, L) for m in maps]
    m_fin, coef = [], []
    for m in maps:
        m_run, l_run = stats[2 * m], stats[2 * m + 1]
        mf = jnp.maximum(m_run, jnp.max(s_n[m], axis=0))
        lf = l_run * jnp.exp(m_run - mf) + jnp.sum(jnp.exp(s_n[m] - mf[None]), axis=0)
        m_fin.append(mf)
        coef.append(jnp.where(valid, (1.0 if m == 0 else -lam) / lf, 0.0))

    def tn_dot(w, v):
        return lax.dot_general(w, v, (((0,), (0,)), ((), ())), preferred_element_type=F32)

    def out_step(c, acc):
        shift = chunk_shift(c)
        s = chunk_scores(c)
        r0 = pl.multiple_of(c * rows, rows)
        vblk = vc_ref[0, pl.ds(r0, rows), :].astype(BF16)
        w = sum(jnp.exp(s[m] - (m_fin[m] - shift)[None]) * coef[m][None] for m in maps)
        return acc + tn_dot(w.reshape(rows, L).astype(BF16), vblk)

    acc = lax.fori_loop(0, n_chunks, out_step, jnp.zeros((L, DIFF_VDIM), F32))
    w_n = sum(jnp.exp(s_n[m] - m_fin[m][None]) * coef[m][None] for m in maps)
    acc = acc + tn_dot(w_n.reshape(T * H, L).astype(BF16), vn)
    o = _subln(acc, g_ref[...]).astype(o_ref.dtype)
    for h in range(H):
        o_ref[0, :, h * DIFF_VDIM:(h + 1) * DIFF_VDIM] = o[h * T:(h + 1) * T]


def _diff_attention_sample(q, cache_k, cache_v, k_new, v_new, lam_rows, subln_g):
    B, T, _ = q.shape
    P = cache_k.shape[1]
    H = N_DIFF_HEADS
    pc = min(P, 128)
    assert H * T == 128 and P > 0 and P % pc == 0
    vec = pl.BlockSpec((1, DIFF_HEAD_DIM), lambda b: (0, 0))
    tok = pl.BlockSpec((1, T, DIFF_WIDTH), lambda b: (b, 0, 0))
    new = pl.BlockSpec((1, T * H, DIFF_VDIM), lambda b: (b, 0, 0))
    old = pl.BlockSpec((1, P * H, DIFF_VDIM), lambda b: (b, 0, 0))
    flat = lambda a: a.reshape(B, a.shape[1] * H, DIFF_VDIM)
    return pl.pallas_call(
        functools.partial(_diff_sample_kernel, past=P, pc=pc),
        out_shape=jax.ShapeDtypeStruct((B, T, DIFF_WIDTH), BF16),
        grid=(B,),
        in_specs=[vec, vec, vec, vec, pl.BlockSpec((1, DIFF_VDIM), lambda b: (0, 0)),
                  tok, old, old, new, new],
        out_specs=tok,
        compiler_params=_cparams(("parallel",)),
        name="diff_attn_sample",
    )(*lam_rows, subln_g, q, flat(cache_k), flat(cache_v), flat(k_new), flat(v_new))


def _gdn_kernel(*refs, c, cps, has_state):
    if has_state:
        (xq_ref, xk_ref, xv_ref, ab_ref, alog_ref, dtb_ref, z_ref, ng_ref, s0_ref, o_ref, s_ref) = refs
    else:
        (xq_ref, xk_ref, xv_ref, ab_ref, alog_ref, dtb_ref, z_ref, ng_ref, o_ref, s_ref) = refs
        s0_ref = None
    n = pl.program_id(1)
    H = N_GDN_HEADS

    @pl.when(n == 0)
    def _init():
        if has_state:
            s_ref[0] = s0_ref[0]
        else:
            s_ref[...] = jnp.zeros(s_ref.shape, F32)

    ab = ab_ref[0]
    a_in = ab[:, 0:H] + dtb_ref[...]
    softplus = jnp.maximum(a_in, 0.0) + jnp.log1p(jnp.exp(-jnp.abs(a_in)))
    g_col = -jnp.exp(alog_ref[...]) * softplus
    beta_col = _sigmoid(ab[:, H:2 * H])
    r = lax.broadcasted_iota(jnp.int32, (c, c), 0)
    s = lax.broadcasted_iota(jnp.int32, (c, c), 1)
    tri = r >= s
    strict = r > s
    tri_f = tri.astype(F32)
    eye_c = (r == s).astype(F32)
    eye_h = (lax.broadcasted_iota(jnp.int32, (H, H), 0) == lax.broadcasted_iota(jnp.int32, (H, H), 1)).astype(F32)

    def bf(a):
        return a.astype(BF16)

    def split(a):
        hi = a.astype(BF16)
        return hi, (a - hi.astype(F32)).astype(BF16)

    def dot3(a, b):
        (ah, al), (bh, bl) = a, b
        return (jnp.dot(al, bh, preferred_element_type=F32) + jnp.dot(ah, bl, preferred_element_type=F32)
                + jnp.dot(ah, bh, preferred_element_type=F32))

    blocks = []
    b_ = 1
    while b_ < c:
        blocks.append((((r // (2 * b_)) == (s // (2 * b_))) & ((r // b_) != (s // b_)) & strict).astype(F32))
        b_ *= 2

    rows = [slice(ci * c, (ci + 1) * c) for ci in range(cps)]
    lanes = [slice(hh * GDN_DK, (hh + 1) * GDN_DK) for hh in range(H)]
    gc_col = [jnp.dot(tri_f, g_col[rows[ci]], preferred_element_type=F32, precision=HI) for ci in range(cps)]
    gc_row = [lax.dot_general(eye_h, gc_col[ci], (((1,), (1,)), ((), ())), preferred_element_type=F32,
                              precision=HI) for ci in range(cps)]
    items = [(ci, hh) for ci in range(cps) for hh in range(H)]
    every = range(len(items))
    gcc = [gc_col[ci][:, hh:hh + 1] for ci, hh in items]
    bet = [beta_col[rows[ci], hh:hh + 1] for ci, hh in items]
    gcr = [gc_row[ci][hh:hh + 1, :] for ci, hh in items]
    g_last = [g[c - 1:c, :] for g in gcc]

    q = [xq_ref[0, rows[ci], lanes[hh]].astype(F32) for ci, hh in items]
    k = [xk_ref[0, rows[ci], lanes[hh]].astype(F32) for ci, hh in items]
    v = [xv_ref[0, rows[ci], lanes[hh]].astype(F32) for ci, hh in items]

    decay = [jnp.exp(jnp.where(tri, gcc[i] - gcr[i], -jnp.inf)) for i in every]
    kb = [k[i] * bet[i] for i in every]
    k16 = [bf(k[i]) for i in every]
    mmat = [jnp.where(strict, _nt_dot(bf(kb[i]), k16[i]) * decay[i], 0.0) for i in every]
    qk = [jnp.where(tri, _nt_dot(bf(q[i]), k16[i]) * decay[i], 0.0) for i in every]

    tinv = [eye_c - mmat[i] * blocks[0] for i in every]
    for lvl in range(1, len(blocks)):
        d16 = [bf(t) for t in tinv]
        x = [jnp.dot(bf(mmat[i] * blocks[lvl]), d16[i], preferred_element_type=F32) for i in every]
        tinv = [tinv[i] - jnp.dot(d16[i], bf(x[i]), preferred_element_type=F32) for i in every]
    m_s = [split(m) for m in mmat]
    t_s = [split(t) for t in tinv]
    res = [(eye_c - tinv[i]) - dot3(m_s[i], t_s[i]) for i in every]
    tinv = [tinv[i] + jnp.dot(t_s[i][0], bf(res[i]), preferred_element_type=F32) for i in every]

    rhs = [jnp.concatenate([v[i] * bet[i], kb[i] * jnp.exp(gcc[i])], axis=1) for i in every]
    sol = [dot3(split(tinv[i]), split(rhs[i])) for i in every]
    u = [x_[:, :GDN_DV] for x_ in sol]
    w16 = [bf(x_[:, GDN_DV:]) for x_ in sol]
    qg16 = [bf(q[i] * jnp.exp(gcc[i])) for i in every]
    qk16 = [bf(x_) for x_ in qk]
    kd16 = [bf(k[i] * jnp.exp(g_last[i] - gcc[i])) for i in every]
    e_last = [jnp.exp(g) for g in g_last]

    S = [s_ref[0, hh] for hh in range(H)]
    for ci in range(cps):
        of = ci * H
        S16 = [bf(x_) for x_ in S]
        v_new = [u[of + hh] - jnp.dot(w16[of + hh], S16[hh], preferred_element_type=F32) for hh in range(H)]
        v16 = [bf(x_) for x_ in v_new]
        o = [jnp.dot(qg16[of + hh], S16[hh], preferred_element_type=F32)
             + jnp.dot(qk16[of + hh], v16[hh], preferred_element_type=F32) for hh in range(H)]
        S = [S[hh] * e_last[of + hh] + lax.dot_general(kd16[of + hh], v16[hh], (((0,), (0,)), ((), ())),
                                                       preferred_element_type=F32) for hh in range(H)]
        for hh in range(H):
            ms = jnp.mean(o[hh] * o[hh], axis=-1, keepdims=True)
            y = o[hh] * lax.rsqrt(ms + NORM_EPS) * ng_ref[...] * z_ref[0, rows[ci], lanes[hh]].astype(F32)
            o_ref[0, rows[ci], lanes[hh]] = y.astype(o_ref.dtype)
    for hh in range(H):
        s_ref[0, hh] = S[hh]


def _gdn(q, k, v, gab, a_log, dt_bias, z_silu, norm_g, s0, *, c):
    B, T, _ = q.shape
    H = N_GDN_HEADS
    assert T % c == 0 and c % 8 == 0 and (c & (c - 1)) == 0
    cps = 2 if (T // c) % 2 == 0 else 1
    R = cps * c
    stream = pl.BlockSpec((1, R, GDN_WIDTH), lambda b, n: (b, n, 0))
    small = pl.BlockSpec((1, H), lambda b, n: (0, 0))
    state = pl.BlockSpec((1, H, GDN_DK, GDN_DV), lambda b, n: (b, 0, 0, 0))
    in_specs = [stream, stream, stream,
                pl.BlockSpec((1, R, 128), lambda b, n: (b, n, 0)), small, small,
                stream, pl.BlockSpec((1, GDN_DV), lambda b, n: (0, 0))]
    args = [q, k, v, gab, a_log, dt_bias, z_silu, norm_g]
    if s0 is not None:
        in_specs.append(state)
        args.append(s0)
    return pl.pallas_call(
        functools.partial(_gdn_kernel, c=c, cps=cps, has_state=s0 is not None),
        out_shape=(jax.ShapeDtypeStruct((B, T, GDN_WIDTH), BF16),
                   jax.ShapeDtypeStruct((B, H, GDN_DK, GDN_DV), F32)),
        grid=(B, T // R),
        in_specs=in_specs,
        out_specs=(stream, state),
        compiler_params=_cparams(("parallel", "arbitrary")),
        name="gdn",
    )(*args)


def _mix_kernel(oa_ref, ob_ref, wa_ref, wb_ref, sa_ref, sb_ref, o_ref):
    a = jnp.dot(oa_ref[...], wa_ref[...], preferred_element_type=F32)
    b = jnp.dot(ob_ref[...], wb_ref[...], preferred_element_type=F32)
    o_ref[...] = (sa_ref[...].astype(F32) * a + sb_ref[...].astype(F32) * b).astype(o_ref.dtype)


def _mix(o_a, o_b, w_pa, w_pb, gates, *, tm=512, tn=1024):
    M = o_a.shape[0]
    tm = min(tm, M)
    nb = D_MODEL // tn
    return pl.pallas_call(
        _mix_kernel,
        out_shape=jax.ShapeDtypeStruct((M, D_MODEL), BF16),
        grid=(M // tm, nb),
        in_specs=[pl.BlockSpec((tm, DIFF_WIDTH), lambda i, j: (i, 0)),
                  pl.BlockSpec((tm, GDN_WIDTH), lambda i, j: (i, 0)),
                  pl.BlockSpec((DIFF_WIDTH, tn), lambda i, j: (0, j)),
                  pl.BlockSpec((GDN_WIDTH, tn), lambda i, j: (0, j)),
                  pl.BlockSpec((tm, tn), lambda i, j: (i, j)),
                  pl.BlockSpec((tm, tn), lambda i, j: (i, j + nb))],
        out_specs=pl.BlockSpec((tm, tn), lambda i, j: (i, j)),
        compiler_params=_cparams(("parallel", "parallel")),
        name="mix",
    )(o_a, o_b, w_pa, w_pb, gates, gates)


def _post_kernel(x_ref, mix_ref, wo_ref, g1_ref, b1_ref, wxq_ref, mk_ref, mv_ref, wxo_ref, g2_ref, b2_ref,
                 h2_ref, h2b_ref):
    h1 = ALPHA * x_ref[0] + jnp.dot(mix_ref[0], wo_ref[...], preferred_element_type=F32)
    h1 = _layer_norm(h1, g1_ref[...], b1_ref[...])
    qx = jnp.dot(h1.astype(BF16), wxq_ref[...], preferred_element_type=F32) * (XHEAD_DIM ** -0.5)
    qx = qx.astype(BF16)
    mk = mk_ref[0]
    mv = mv_ref[0]
    heads = []
    for hh in range(N_XHEADS):
        sl = slice(hh * XHEAD_DIM, (hh + 1) * XHEAD_DIM)
        s = _nt_dot(qx[:, sl], mk[:, sl])
        p = jnp.exp(s - jnp.max(s, axis=-1, keepdims=True))
        p = p / jnp.sum(p, axis=-1, keepdims=True)
        heads.append(jnp.dot(p.astype(BF16), mv[:, sl], preferred_element_type=F32))
    ox = jnp.concatenate(heads, axis=1).astype(BF16)
    h2 = ALPHA * h1 + jnp.dot(ox, wxo_ref[...], preferred_element_type=F32)
    h2 = _layer_norm(h2, g2_ref[...], b2_ref[...])
    h2_ref[0] = h2
    h2b_ref[0] = h2.astype(BF16)


def _post(x, mix, w_o, ln1_g, ln1_b, w_xq, mem_k, mem_v, w_xo, ln2_g, ln2_b, *, tm=512):
    B, T, D = x.shape
    tm = min(tm, T)
    const = lambda shape: pl.BlockSpec(shape, lambda b, i: (0, 0), pipeline_mode=pl.Buffered(1))
    rows = lambda: pl.BlockSpec((1, tm, D), lambda b, i: (b, i, 0))
    mem = lambda: pl.BlockSpec((1, N_MEM, XWIDTH), lambda b, i: (b, 0, 0))
    return pl.pallas_call(
        _post_kernel,
        out_shape=(jax.ShapeDtypeStruct((B, T, D), F32), jax.ShapeDtypeStruct((B, T, D), BF16)),
        grid=(B, T // tm),
        in_specs=[rows(), rows(), const((D, D)), const((1, D)), const((1, D)), const((D, XWIDTH)),
                  mem(), mem(), const((XWIDTH, D)), const((1, D)), const((1, D))],
        out_specs=(rows(), rows()),
        compiler_params=_cparams(("parallel", "parallel")),
        name="post_attn",
    )(x, mix, w_o, ln1_g, ln1_b, w_xq, mem_k, mem_v, w_xo, ln2_g, ln2_b)


def _ffn_kernel(hb_ref, h_ref, w1_ref, w3_ref, w2_ref, g_ref, b_ref, y_ref, acc_ref):
    f = pl.program_id(1)

    @pl.when(f == 0)
    def _():
        acc_ref[...] = jnp.zeros(acc_ref.shape, F32)

    hb = hb_ref[...]
    a = jnp.dot(hb, w1_ref[...], preferred_element_type=F32)
    b = jnp.dot(hb, w3_ref[...], preferred_element_type=F32)
    act = (_silu(a) * b).astype(BF16)
    acc_ref[...] += jnp.dot(act, w2_ref[...], preferred_element_type=F32)

    @pl.when(f == pl.num_programs(1) - 1)
    def _():
        y_ref[...] = _layer_norm(ALPHA * h_ref[...] + acc_ref[...], g_ref[...], b_ref[...])


def _ffn(h2b, h2, w1, w3, w2, ln_g, ln_b, *, tm=512, tf=512):
    M, D = h2.shape
    tm = min(tm, M)
    assert M % tm == 0 and D_FF % tf == 0
    return pl.pallas_call(
        _ffn_kernel,
        out_shape=jax.ShapeDtypeStruct((M, D), F32),
        grid=(M // tm, D_FF // tf),
        in_specs=[pl.BlockSpec((tm, D), lambda i, f: (i, 0)),
                  pl.BlockSpec((tm, D), lambda i, f: (i, 0)),
                  pl.BlockSpec((D, tf), lambda i, f: (0, f)),
                  pl.BlockSpec((D, tf), lambda i, f: (0, f)),
                  pl.BlockSpec((tf, D), lambda i, f: (f, 0)),
                  pl.BlockSpec((1, D), lambda i, f: (0, 0)),
                  pl.BlockSpec((1, D), lambda i, f: (0, 0))],
        out_specs=pl.BlockSpec((tm, D), lambda i, f: (i, 0)),
        scratch_shapes=[pltpu.VMEM((tm, D), F32)],
        compiler_params=_cparams(("parallel", "arbitrary")),
        name="ffn",
    )(h2b, h2, w1, w3, w2, ln_g, ln_b)


def _encoder_layer(x, mem_k, mem_v, past, W):
    B, T, D = x.shape
    M = B * T
    xb = x.reshape(M, D).astype(BF16)
    w_in = W["w_in_t"]
    in_proj = functools.partial(_proj, xb, w_in, w_rows=True)

    (dq,) = in_proj(OFF_DQ, DIFF_WIDTH, (BF16,), scale=DIFF_HEAD_DIM ** -0.5, name="proj_dq")
    kv_dtypes = (F32, BF16) if past is None else (F32,)
    dk, *dkb = in_proj(OFF_DK, DIFF_WIDTH, kv_dtypes, rows_inner=True, name="proj_dk")
    dv, *dvb = in_proj(OFF_DV, DIFF_WIDTH, kv_dtypes, rows_inner=True, name="proj_dv")
    buf0 = jnp.zeros((B, CONV_W - 1, 3 * GDN_WIDTH), F32) if past is None else past[3]
    streams, tails = [], []
    for t, (norm, scale) in enumerate(((True, GDN_DK ** -0.5), (True, 1.0), (False, 1.0))):
        y, tail = _proj_conv(xb, w_in, OFF_GQKV + t * GDN_WIDTH, buf0, W["conv_w"], B=B, T=T, norm=norm,
                             scale=scale, name="proj_conv_" + "qkv"[t])
        streams.append(y.reshape(B, T, GDN_WIDTH))
        tails.append(tail[:, CONV_PAD - (CONV_W - 1):, :])
    (gz,) = in_proj(OFF_GZ, GDN_WIDTH, (BF16,), act="silu", name="proj_gz")
    (gab,) = in_proj(OFF_GAB, 128, (F32,), tn=128, name="proj_gab")
    (gates,) = in_proj(OFF_GATES, 2 * D_MODEL, (BF16,), act="sigmoid", name="proj_gates")

    dq = dq.reshape(B, T, DIFF_WIDTH)
    dk = dk.reshape(B, T, N_DIFF_HEADS, DIFF_VDIM)
    dv = dv.reshape(B, T, N_DIFF_HEADS, DIFF_VDIM)
    lam_rows = W["lam_rows"]
    if past is None:
        o_a = _diff_attention_prompt(dq, dkb[0].reshape(B, T, DIFF_WIDTH), dvb[0].reshape(B, T, DIFF_WIDTH),
                                     lam_rows, W["diff_subln_g"])
        s0 = None
        c = CHUNK
    else:
        cache_k, cache_v, s0, _ = past
        o_a = _diff_attention_sample(dq, cache_k, cache_v, dk, dv, lam_rows, W["diff_subln_g"])
        c = T
    o_b, s_new = _gdn(*streams, gab.reshape(B, T, 128), W["gdn_a_log"], W["gdn_dt_bias"],
                      gz.reshape(B, T, GDN_WIDTH), W["gdn_norm_g"], s0, c=c)
    new_buf = jnp.concatenate(tails, axis=-1)

    mix = _mix(o_a.reshape(M, DIFF_WIDTH), o_b.reshape(M, GDN_WIDTH), W["w_pa"], W["w_pb"], gates)
    h2, h2b = _post(x, mix.reshape(B, T, D), W["w_o"], W["ln1_g"], W["ln1_b"], W["w_xq"], mem_k, mem_v,
                    W["w_xo"], W["ln2_g"], W["ln2_b"])
    y = _ffn(h2b.reshape(M, D), h2.reshape(M, D), W["w_ff1"], W["w_ff3"], W["w_ff2"], W["ln3_g"], W["ln3_b"])
    return y.reshape(B, T, D), dk, dv, s_new, new_buf


def kernel(x_prompt, x_sample, mem_prompt, cache_diff_k, cache_diff_v, state_gdn, state_gdn_conv, cache_mem_k, cache_mem_v, w_in, conv_w, lam_q1, lam_k1, lam_q2, lam_k2, diff_subln_g, gdn_a_log, gdn_dt_bias, gdn_norm_g, w_pa, w_pb, w_o, ln1_g, ln1_b, w_xq, w_xk, w_xv, w_xo, ln2_g, ln2_b, w_ff1, w_ff3, w_ff2, ln3_g, ln3_b):
    l = 0
    W = {
        "w_in_t": jnp.swapaxes(w_in, 1, 2),
        "conv_w": conv_w[l],
        "lam_rows": tuple(v[l].reshape(1, DIFF_HEAD_DIM) for v in (lam_q1, lam_k1, lam_q2, lam_k2)),
        "diff_subln_g": diff_subln_g[l].reshape(1, DIFF_VDIM),
        "gdn_a_log": gdn_a_log[l].reshape(1, N_GDN_HEADS),
        "gdn_dt_bias": gdn_dt_bias[l].reshape(1, N_GDN_HEADS),
        "gdn_norm_g": gdn_norm_g[l].reshape(1, GDN_DV),
        "w_pa": w_pa[l].astype(BF16), "w_pb": w_pb[l].astype(BF16), "w_o": w_o[l].astype(BF16),
        "ln1_g": ln1_g[l].reshape(1, D_MODEL), "ln1_b": ln1_b[l].reshape(1, D_MODEL),
        "w_xq": w_xq[l].astype(BF16), "w_xo": w_xo[l].astype(BF16),
        "ln2_g": ln2_g[l].reshape(1, D_MODEL), "ln2_b": ln2_b[l].reshape(1, D_MODEL),
        "w_ff1": w_ff1[l].astype(BF16), "w_ff3": w_ff3[l].astype(BF16), "w_ff2": w_ff2[l].astype(BF16),
        "ln3_g": ln3_g[l].reshape(1, D_MODEL), "ln3_b": ln3_b[l].reshape(1, D_MODEL),
    }
    Bp = x_prompt.shape[0]
    memb = mem_prompt.reshape(Bp * N_MEM, D_MODEL).astype(BF16)
    mem_k, mem_kb = _proj(memb, w_xk, 0, XWIDTH, (F32, BF16), tn=XWIDTH, name="proj_mem_k")
    mem_v, mem_vb = _proj(memb, w_xv, 0, XWIDTH, (F32, BF16), tn=XWIDTH, name="proj_mem_v")

    yp, pk, pv, ps, pc = _encoder_layer(x_prompt, mem_kb.reshape(Bp, N_MEM, XWIDTH),
                                        mem_vb.reshape(Bp, N_MEM, XWIDTH), None, W)
    Bs = x_sample.shape[0]
    past = (cache_diff_k[l], cache_diff_v[l], state_gdn[l], state_gdn_conv[l])
    ys, sk, sv, ss, sc = _encoder_layer(x_sample, cache_mem_k[l].reshape(Bs, N_MEM, XWIDTH).astype(BF16),
                                        cache_mem_v[l].reshape(Bs, N_MEM, XWIDTH).astype(BF16), past, W)
    st = lambda a: a[None]
    return (yp, ys, st(pk), st(pv), st(ps), st(pc),
            st(mem_k.reshape(Bp, N_MEM, N_XHEADS, XHEAD_DIM)), st(mem_v.reshape(Bp, N_MEM, N_XHEADS, XHEAD_DIM)),
            st(sk), st(sv), st(ss), st(sc))
```

```python
import functools
import math

import jax
import jax.numpy as jnp
from jax import lax
from jax.experimental import pallas as pl
from jax.experimental.pallas import tpu as pltpu

D_MODEL = 2048
CHUNK = 64
N_DIFF_HEADS = 8
DIFF_HEAD_DIM = 128
DIFF_VDIM = 2 * DIFF_HEAD_DIM
DIFF_WIDTH = N_DIFF_HEADS * DIFF_VDIM
N_GDN_HEADS = 16
GDN_DK = 128
GDN_DV = 128
GDN_WIDTH = N_GDN_HEADS * GDN_DK
CONV_W = 4
N_MEM = 256
N_XHEADS = 4
XHEAD_DIM = 128
XWIDTH = N_XHEADS * XHEAD_DIM
D_FF = 5632
DEPTH = 1
ALPHA = (2.0 * DEPTH) ** 0.25
LN_EPS = 1e-5
NORM_EPS = 1e-6
LAM_INIT = 0.8 - 0.6 * math.exp(-0.3 * 0)

OFF_DQ = 0
OFF_DK = 2048
OFF_DV = 4096
OFF_GQKV = 6144
OFF_GZ = 12288
OFF_GAB = 14336
OFF_GATES = 14368

VMEM_LIMIT = 56 * 1024 * 1024
BF16 = jnp.bfloat16
F32 = jnp.float32
HI = lax.Precision.HIGHEST


def _cparams(sem):
    return pltpu.CompilerParams(dimension_semantics=sem, vmem_limit_bytes=VMEM_LIMIT)


def _sigmoid(x):
    return 1.0 / (1.0 + jnp.exp(-x))


def _silu(x):
    return x * _sigmoid(x)


def _layer_norm(x, g, b):
    mu = jnp.mean(x, axis=-1, keepdims=True)
    xc = x - mu
    var = jnp.mean(xc * xc, axis=-1, keepdims=True)
    return xc * lax.rsqrt(var + LN_EPS) * g + b


def _proj_kernel(x_ref, w_ref, *o_refs, act, scale, w_rows):
    w = w_ref[0].astype(BF16)
    acc = _nt_dot(x_ref[...], w) if w_rows else jnp.dot(x_ref[...], w, preferred_element_type=F32)
    if scale != 1.0:
        acc = acc * scale
    if act == "sigmoid":
        acc = _sigmoid(acc)
    elif act == "silu":
        acc = _silu(acc)
    for o in o_refs:
        o[...] = acc.astype(o.dtype)


def _weight_spec(w, K, tn, col_start, w_rows):
    assert w.shape[0] == DEPTH == 1
    if w_rows:
        return pl.BlockSpec((pl.Element(1), pl.Element(tn), pl.Element(K)),
                            lambda a, b: (0, pl.multiple_of(col_start(a, b), 8), 0))
    return pl.BlockSpec((1, K, tn), lambda a, b: (0, 0, col_start(a, b) // tn))


def _proj(x, w, col_off, n_cols, out_dtypes, *, act=None, scale=1.0, tm=1024, tn=1024, rows_inner=False,
          w_rows=False, name="proj"):
    M, K = x.shape
    tm = min(tm, M)
    tn = min(tn, n_cols)
    assert M % tm == 0 and n_cols % tn == 0 and col_off % (8 if w_rows else tn) == 0
    if rows_inner:
        grid = (n_cols // tn, M // tm)
        ij = lambda a, b: (b, a)
    else:
        grid = (M // tm, n_cols // tn)
        ij = lambda a, b: (a, b)
    outs = pl.pallas_call(
        functools.partial(_proj_kernel, act=act, scale=scale, w_rows=w_rows),
        out_shape=tuple(jax.ShapeDtypeStruct((M, n_cols), dt) for dt in out_dtypes),
        grid=grid,
        in_specs=[pl.BlockSpec((tm, K), lambda a, b: (ij(a, b)[0], 0)),
                  _weight_spec(w, K, tn, lambda a, b: col_off + ij(a, b)[1] * tn, w_rows)],
        out_specs=tuple(pl.BlockSpec((tm, tn), lambda a, b: ij(a, b)) for _ in out_dtypes),
        compiler_params=_cparams(("parallel", "parallel")),
        name=name,
    )(x, w)
    return outs


CONV_PAD = 8


def _proj_conv_kernel(x_ref, w_ref, hist_ref, cw_ref, y_ref, raw_ref, pad_ref, *, nb, T, sub, tr_max, norm, scale):
    tn = w_ref.shape[1]
    tr = min(T, tr_max) if nb == 1 else T
    units = [(s, r) for s in range(tn // sub) for r in range(T // tr)]
    w_slabs = [w_ref[0, s * sub:(s + 1) * sub, :].astype(BF16) for s in range(tn // sub)]

    def rows_of(r):
        return slice(r * tr, (r + 1) * tr) if nb == 1 else slice(None)

    def matmul(s, r, gate):
        rows = rows_of(r)
        if gate is None:
            lhs = x_ref[rows, :]
        else:
            n_rows = nb * tr
            first = x_ref[rows, 0:256] + jnp.concatenate([jnp.concatenate([gate] * 2, axis=1)] * (n_rows // 16), axis=0)
            lhs = jnp.concatenate([first, x_ref[rows, 256:]], axis=1)
        return _nt_dot(lhs, w_slabs[s])

    def epilogue(s, r, acc):
        cols = slice(s * sub, (s + 1) * sub)
        slot = s % 2
        r0 = r * tr
        a3 = acc.reshape(nb, tr, sub)
        pad_ref[slot, :, CONV_PAD + r0:CONV_PAD + r0 + tr, :] = a3
        if r == 0:
            pad_ref[slot, :, CONV_PAD - (CONV_W - 1):CONV_PAD, :] = hist_ref[:, :, cols]
        if r == T // tr - 1:
            raw_ref[:, :, cols] = a3[:, tr - CONV_PAD:, :]
        y = cw_ref[CONV_W - 1:CONV_W, cols] * a3
        for j in range(CONV_W - 1):
            lo = CONV_PAD - (CONV_W - 1) + j + r0
            y = y + cw_ref[j:j + 1, cols] * pad_ref[slot, :, lo:lo + tr, :]
        y = _silu(y)
        if norm:
            heads = []
            for hh in range(sub // GDN_DK):
                yh = y[:, :, hh * GDN_DK:(hh + 1) * GDN_DK]
                heads.append(yh * (lax.rsqrt(jnp.sum(yh * yh, axis=-1, keepdims=True) + NORM_EPS) * scale))
            y = jnp.concatenate(heads, axis=-1)
        y2 = y.reshape(nb * tr, sub)
        y_ref[rows_of(r), cols] = y2.astype(y_ref.dtype)
        bits = pltpu.bitcast(y2[nb * tr - 16:, 0:128], jnp.uint32)
        zero = lax.shift_right_logical(lax.shift_right_logical(bits, jnp.uint32(16)), jnp.uint32(16))
        return zero.astype(F32).astype(BF16)

    gates = [None, None]
    acc = matmul(*units[0], None)
    for i, u in enumerate(units):
        nxt = matmul(*units[i + 1], gates[i + 1]) if i + 1 < len(units) else None
        gates.append(epilogue(*u, acc))
        acc = nxt


def _proj_conv(x, w, col_off, hist, conv_w, *, B, T, norm, scale=1.0, tn=512, sub=512, tr=512, name="proj_conv"):
    M, K = x.shape
    n_cols = GDN_WIDTH
    nb = max(1, 256 // T) if T < 256 else 1
    nb = min(nb, B)
    tm = nb * T
    gq = OFF_GQKV
    assert B % nb == 0 and n_cols % tn == 0 and col_off % 8 == 0 and (col_off - gq) % tn == 0 and T >= CONV_PAD
    hb = (col_off - gq) // tn
    return pl.pallas_call(
        functools.partial(_proj_conv_kernel, nb=nb, T=T, sub=sub, tr_max=tr, norm=norm, scale=scale),
        out_shape=(jax.ShapeDtypeStruct((M, n_cols), BF16), jax.ShapeDtypeStruct((B, CONV_PAD, n_cols), F32)),
        grid=(B // nb, n_cols // tn),
        in_specs=[pl.BlockSpec((tm, K), lambda i, j: (i, 0)),
                  _weight_spec(w, K, tn, lambda i, j: col_off + j * tn, True),
                  pl.BlockSpec((nb, CONV_W - 1, tn), lambda i, j: (i, 0, j + hb)),
                  pl.BlockSpec((CONV_W, tn), lambda i, j: (0, j + hb))],
        out_specs=(pl.BlockSpec((tm, tn), lambda i, j: (i, j)),
                   pl.BlockSpec((nb, CONV_PAD, tn), lambda i, j: (i, 0, j))),
        scratch_shapes=[pltpu.VMEM((min(2, tn // sub), nb, T + CONV_PAD, sub), F32)],
        compiler_params=_cparams(("parallel", "parallel")),
        name=name,
    )(x, w, hist, conv_w)


def _lam_value(lq1, lk1, lq2, lk2):
    a = jnp.sum(lq1 * lk1, axis=-1, keepdims=True)
    b = jnp.sum(lq2 * lk2, axis=-1, keepdims=True)
    return jnp.exp(a) - jnp.exp(b) + LAM_INIT


def _subln(o, g):
    ms = jnp.mean(o * o, axis=-1, keepdims=True)
    return o * lax.rsqrt(ms + NORM_EPS) * g * (1.0 - LAM_INIT)


def _head_slope(h):
    e = (h + 1).astype(F32) * (-8.0 / N_DIFF_HEADS)
    return jnp.exp2(jnp.full((1, 1), e, F32))


def _nt_dot(a, b):
    return lax.dot_general(a, b, (((1,), (1,)), ((), ())), preferred_element_type=F32)


def _lane_tile(x, n):
    return x if n == 1 else jnp.concatenate([x] * n, axis=1)


def _diff_prompt_kernel(lq1_ref, lk1_ref, lq2_ref, lk2_ref, g_ref, base_ref, q_ref, k_ref, v_ref, o_ref,
                        m_ref, l_ref, acc_ref, *, tq, hps):
    LANES = 128
    half = tq // 2
    hg = pl.program_id(1)
    qi = pl.program_id(2)
    slope = [_head_slope(hg * hps + hh) for hh in range(hps)]
    lam = _lam_value(lq1_ref[...], lk1_ref[...], lq2_ref[...], lk2_ref[...])
    col_iota = lax.broadcasted_iota(jnp.int32, (1, tq), 1).astype(F32)
    col_bias = [sl * col_iota for sl in slope]
    maps = range(2)

    m_ref[...] = jnp.full(m_ref.shape, -jnp.inf, F32)
    l_ref[...] = jnp.zeros(l_ref.shape, F32)
    acc_ref[...] = jnp.zeros(acc_ref.shape, F32)

    def cols(hh, m):
        c0 = hh * DIFF_VDIM + m * DIFF_HEAD_DIM
        return slice(c0, c0 + DIFF_HEAD_DIM)

    def update(*jobs):
        todo = [(job, m) for job in jobs for m in maps]
        t = [_nt_dot(q_ref[0, rows, cols(hh, m)], k_ref[0, pl.ds(start, nk), cols(hh, m)]) + bias
             for (hh, rows, start, nk, bias, _), m in todo]
        st = [2 * job[0] + m for job, m in todo]
        m_old = [m_ref[st[i], job[1]] for i, (job, m) in enumerate(todo)]
        m_new = [jnp.maximum(m_old[i], jnp.max(t[i], axis=-1, keepdims=True) + job[5])
                 for i, (job, m) in enumerate(todo)]
        a = [jnp.exp(m_old[i] - m_new[i]) for i in range(len(todo))]
        p = [jnp.exp(t[i] - _lane_tile(m_new[i] - job[5], job[3] // LANES)) for i, (job, m) in enumerate(todo)]
        pv = [jnp.dot(p[i].astype(BF16), v_ref[0, pl.ds(start, nk), hh * DIFF_VDIM:(hh + 1) * DIFF_VDIM],
                      preferred_element_type=F32) for i, ((hh, _, start, nk, _, _), m) in enumerate(todo)]
        for i, ((hh, rows, start, nk, _, _), m) in enumerate(todo):
            psum = p[i][:, :LANES]
            for c in range(1, nk // LANES):
                psum = psum + p[i][:, c * LANES:(c + 1) * LANES]
            l_ref[st[i], rows] = a[i] * l_ref[st[i], rows] + psum
            acc_ref[st[i], rows] = _lane_tile(a[i], DIFF_VDIM // LANES) * acc_ref[st[i], rows] + pv[i]
            m_ref[st[i], rows] = m_new[i]

    every = slice(0, tq)

    def body(j, carry):
        start = pl.multiple_of(j * tq, tq)
        gap = ((qi - j) * tq).astype(F32)
        update(*[(hh, every, start, tq, col_bias[hh], -slope[hh] * gap) for hh in range(hps)])
        return carry

    lax.fori_loop(0, qi, body, 0)

    lo, hi = slice(0, half), slice(half, tq)
    start = pl.multiple_of(qi * tq, tq)
    zero = jnp.zeros((1, 1), F32)
    jobs = []
    for hh in range(hps):
        jobs.append((hh, lo, start, half, slope[hh] * base_ref[:, half:], -slope[hh] * float(half)))
        jobs.append((hh, hi, start, tq, slope[hh] * base_ref[...], zero))
    update(*jobs)

    for hh in range(hps):
        outs = []
        for m in maps:
            inv = 1.0 / jnp.sum(l_ref[2 * hh + m], axis=-1, keepdims=True)
            outs.append(acc_ref[2 * hh + m] * inv)
        o = outs[0] - lam * outs[1]
        o_ref[0, :, hh * DIFF_VDIM:(hh + 1) * DIFF_VDIM] = _subln(o, g_ref[...]).astype(o_ref.dtype)


def _diff_attention_prompt(q, k, v, lam_rows, subln_g, *, tq=512, hps=2):
    B, T, _ = q.shape
    tq = min(tq, T)
    half = tq // 2
    assert T % tq == 0 and half % CHUNK == 0 and N_DIFF_HEADS % hps == 0
    ii = lax.broadcasted_iota(jnp.int32, (half, tq), 0)
    jj = lax.broadcasted_iota(jnp.int32, (half, tq), 1)
    jh = jj - half
    diag = jnp.where((jh // CHUNK) <= (ii // CHUNK), (half + ii - jnp.abs(ii - jh)).astype(F32), -jnp.inf)
    base = jnp.where(jj < half, jj.astype(F32), diag)
    vec = pl.BlockSpec((1, DIFF_HEAD_DIM), lambda b, h, i: (0, 0))
    wide = hps * DIFF_VDIM
    return pl.pallas_call(
        functools.partial(_diff_prompt_kernel, tq=tq, hps=hps),
        out_shape=jax.ShapeDtypeStruct((B, T, DIFF_WIDTH), BF16),
        grid=(B, N_DIFF_HEADS // hps, T // tq),
        in_specs=[vec, vec, vec, vec,
                  pl.BlockSpec((1, DIFF_VDIM), lambda b, h, i: (0, 0)),
                  pl.BlockSpec((half, tq), lambda b, h, i: (0, 0)),
                  pl.BlockSpec((1, tq, wide), lambda b, h, i: (b, i, h)),
                  pl.BlockSpec((1, T, wide), lambda b, h, i: (b, 0, h)),
                  pl.BlockSpec((1, T, wide), lambda b, h, i: (b, 0, h))],
        out_specs=pl.BlockSpec((1, tq, wide), lambda b, h, i: (b, i, h)),
        scratch_shapes=[pltpu.VMEM((2 * hps, tq, 128), F32), pltpu.VMEM((2 * hps, tq, 128), F32),
                        pltpu.VMEM((2 * hps, tq, DIFF_VDIM), F32)],
        compiler_params=_cparams(("parallel", "parallel", "parallel")),
        name="diff_attn_prompt",
    )(*lam_rows, subln_g, base, q, k, v)


def _diff_sample_kernel(lq1_ref, lk1_ref, lq2_ref, lk2_ref, g_ref, q_ref, kc_ref, vc_ref, kn_ref, vn_ref,
                        o_ref, *, past, pc):
    H = N_DIFF_HEADS
    T = q_ref.shape[1]
    L = H * T
    rows = pc * H
    n_chunks = past // pc
    lam = _lam_value(lq1_ref[...], lk1_ref[...], lq2_ref[...], lk2_ref[...])
    maps = range(2)

    def map_cols(m):
        return slice(m * DIFF_HEAD_DIM, (m + 1) * DIFF_HEAD_DIM)

    qm = [jnp.concatenate([q_ref[0, :, h * DIFF_VDIM + m * DIFF_HEAD_DIM:h * DIFF_VDIM + (m + 1) * DIFF_HEAD_DIM]
                           for h in range(H)], axis=0) for m in maps]
    lane = lax.broadcasted_iota(jnp.int32, (1, L), 1)
    h_lane = lane // T
    t_lane = lane % T
    slope = jnp.exp2((h_lane + 1).astype(F32) * (-8.0 / H))
    base = slope * (lax.broadcasted_iota(jnp.int32, (rows, L), 0) // H).astype(F32)
    valid = lax.broadcasted_iota(jnp.int32, (H, L), 0) == h_lane

    def chunk_shift(c):
        return slope * ((c * pc - past) - t_lane).astype(F32)

    def chunk_scores(c):
        r0 = pl.multiple_of(c * rows, rows)
        kblk = kc_ref[0, pl.ds(r0, rows), :].astype(BF16)
        return [(_nt_dot(kblk[:, map_cols(m)], qm[m]) + base).reshape(pc, H, L) for m in maps]

    def stats_step(c, carry):
        shift = chunk_shift(c)
        s = chunk_scores(c)
        out = []
        for m in maps:
            m_old, l_old = carry[2 * m], carry[2 * m + 1]
            m_new = jnp.maximum(m_old, jnp.max(s[m], axis=0) + shift)
            l_new = l_old * jnp.exp(m_old - m_new) + jnp.sum(jnp.exp(s[m] - (m_new - shift)[None]), axis=0)
            out += [m_new, l_new]
        return tuple(out)

    init = (jnp.full((H, L), -jnp.inf, F32), jnp.zeros((H, L), F32)) * 2
    stats = lax.fori_loop(0, n_chunks, stats_step, init)

    kn = kn_ref[0].astype(BF16)
    vn = vn_ref[0].astype(BF16)
    tk = lax.broadcasted_iota(jnp.int32, (T * H, L), 0) // H
    allowed = ((past + tk) // CHUNK) <= ((past + t_lane) // CHUNK)
    bias_n = jnp.where(allowed, -slope * jnp.abs(tk - t_lane).astype(F32), -jnp.inf)
    s_n = [(_nt_dot(kn[:, map_cols(m)], qm[m]) + bias_n).reshape(T, H, L) for m in maps]
    m_fin, coef = [], []
    for m in maps:
        m_run, l_run = stats[2 * m], stats[2 * m + 1]
        mf = jnp.maximum(m_run, jnp.max(s_n[m], axis=0))
        lf = l_run * jnp.exp(m_run - mf) + jnp.sum(jnp.exp(s_n[m] - mf[None]), axis=0)
        m_fin.append(mf)
        coef.append(jnp.where(valid, (1.0 if m == 0 else -lam) / lf, 0.0))

    def tn_dot(w, v):
        return lax.dot_general(w, v, (((0,), (0,)), ((), ())), preferred_element_type=F32)

    def out_step(c, acc):
        shift = chunk_shift(c)
        s = chunk_scores(c)
        r0 = pl.multiple_of(c * rows, rows)
        vblk = vc_ref[0, pl.ds(r0, rows), :].astype(BF16)
        w = sum(jnp.exp(s[m] - (m_fin[m] - shift)[None]) * coef[m][None] for m in maps)
        return acc + tn_dot(w.reshape(rows, L).astype(BF16), vblk)

    acc = lax.fori_loop(0, n_chunks, out_step, jnp.zeros((L, DIFF_VDIM), F32))
    w_n = sum(jnp.exp(s_n[m] - m_fin[m][None]) * coef[m][None] for m in maps)
    acc = acc + tn_dot(w_n.reshape(T * H, L).astype(BF16), vn)
    o = _subln(acc, g_ref[...]).astype(o_ref.dtype)
    for h in range(H):
        o_ref[0, :, h * DIFF_VDIM:(h + 1) * DIFF_VDIM] = o[h * T:(h + 1) * T]


def _diff_attention_sample(q, cache_k, cache_v, k_new, v_new, lam_rows, subln_g):
    B, T, _ = q.shape
    P = cache_k.shape[1]
    H = N_DIFF_HEADS
    pc = min(P, 128)
    assert H * T == 128 and P > 0 and P % pc == 0
    vec = pl.BlockSpec((1, DIFF_HEAD_DIM), lambda b: (0, 0))
    tok = pl.BlockSpec((1, T, DIFF_WIDTH), lambda b: (b, 0, 0))
    new = pl.BlockSpec((1, T * H, DIFF_VDIM), lambda b: (b, 0, 0))
    old = pl.BlockSpec((1, P * H, DIFF_VDIM), lambda b: (b, 0, 0))
    flat = lambda a: a.reshape(B, a.shape[1] * H, DIFF_VDIM)
    return pl.pallas_call(
        functools.partial(_diff_sample_kernel, past=P, pc=pc),
        out_shape=jax.ShapeDtypeStruct((B, T, DIFF_WIDTH), BF16),
        grid=(B,),
        in_specs=[vec, vec, vec, vec, pl.BlockSpec((1, DIFF_VDIM), lambda b: (0, 0)),
                  tok, old, old, new, new],
        out_specs=tok,
        compiler_params=_cparams(("parallel",)),
        name="diff_attn_sample",
    )(*lam_rows, subln_g, q, flat(cache_k), flat(cache_v), flat(k_new), flat(v_new))


def _gdn_kernel(*refs, c, cps, has_state):
    if has_state:
        (xq_ref, xk_ref, xv_ref, ab_ref, alog_ref, dtb_ref, z_ref, ng_ref, s0_ref, o_ref, s_ref) = refs
    else:
        (xq_ref, xk_ref, xv_ref, ab_ref, alog_ref, dtb_ref, z_ref, ng_ref, o_ref, s_ref) = refs
        s0_ref = None
    n = pl.program_id(1)
    H = N_GDN_HEADS

    @pl.when(n == 0)
    def _init():
        if has_state:
            s_ref[0] = s0_ref[0]
        else:
            s_ref[...] = jnp.zeros(s_ref.shape, F32)

    ab = ab_ref[0]
    a_in = ab[:, 0:H] + dtb_ref[...]
    softplus = jnp.maximum(a_in, 0.0) + jnp.log1p(jnp.exp(-jnp.abs(a_in)))
    g_col = -jnp.exp(alog_ref[...]) * softplus
    beta_col = _sigmoid(ab[:, H:2 * H])
    r = lax.broadcasted_iota(jnp.int32, (c, c), 0)
    s = lax.broadcasted_iota(jnp.int32, (c, c), 1)
    tri = r >= s
    strict = r > s
    tri_f = tri.astype(F32)
    eye_c = (r == s).astype(F32)
    eye_h = (lax.broadcasted_iota(jnp.int32, (H, H), 0) == lax.broadcasted_iota(jnp.int32, (H, H), 1)).astype(F32)

    def bf(a):
        return a.astype(BF16)

    def split(a):
        hi = a.astype(BF16)
        return hi, (a - hi.astype(F32)).astype(BF16)

    def dot3(a, b):
        (ah, al), (bh, bl) = a, b
        return (jnp.dot(al, bh, preferred_element_type=F32) + jnp.dot(ah, bl, preferred_element_type=F32)
                + jnp.dot(ah, bh, preferred_element_type=F32))

    blocks = []
    b_ = 1
    while b_ < c:
        blocks.append((((r // (2 * b_)) == (s // (2 * b_))) & ((r // b_) != (s // b_)) & strict).astype(F32))
        b_ *= 2

    rows = [slice(ci * c, (ci + 1) * c) for ci in range(cps)]
    lanes = [slice(hh * GDN_DK, (hh + 1) * GDN_DK) for hh in range(H)]
    gc_col = [jnp.dot(tri_f, g_col[rows[ci]], preferred_element_type=F32, precision=HI) for ci in range(cps)]
    gc_row = [lax.dot_general(eye_h, gc_col[ci], (((1,), (1,)), ((), ())), preferred_element_type=F32,
                              precision=HI) for ci in range(cps)]
    items = [(ci, hh) for ci in range(cps) for hh in range(H)]
    every = range(len(items))
    gcc = [gc_col[ci][:, hh:hh + 1] for ci, hh in items]
    bet = [beta_col[rows[ci], hh:hh + 1] for ci, hh in items]
    gcr = [gc_row[ci][hh:hh + 1, :] for ci, hh in items]
    g_last = [g[c - 1:c, :] for g in gcc]

    q = [xq_ref[0, rows[ci], lanes[hh]].astype(F32) for ci, hh in items]
    k = [xk_ref[0, rows[ci], lanes[hh]].astype(F32) for ci, hh in items]
    v = [xv_ref[0, rows[ci], lanes[hh]].astype(F32) for ci, hh in items]

    decay = [jnp.exp(jnp.where(tri, gcc[i] - gcr[i], -jnp.inf)) for i in every]
    kb = [k[i] * bet[i] for i in every]
    k16 = [bf(k[i]) for i in every]
    mmat = [jnp.where(strict, _nt_dot(bf(kb[i]), k16[i]) * decay[i], 0.0) for i in every]
    qk = [jnp.where(tri, _nt_dot(bf(q[i]), k16[i]) * decay[i], 0.0) for i in every]

    tinv = [eye_c - mmat[i] * blocks[0] for i in every]
    for lvl in range(1, len(blocks)):
        d16 = [bf(t) for t in tinv]
        x = [jnp.dot(bf(mmat[i] * blocks[lvl]), d16[i], preferred_element_type=F32) for i in every]
        tinv = [tinv[i] - jnp.dot(d16[i], bf(x[i]), preferred_element_type=F32) for i in every]
    m_s = [split(m) for m in mmat]
    t_s = [split(t) for t in tinv]
    res = [(eye_c - tinv[i]) - dot3(m_s[i], t_s[i]) for i in every]
    tinv = [tinv[i] + jnp.dot(t_s[i][0], bf(res[i]), preferred_element_type=F32) for i in every]

    rhs = [jnp.concatenate([v[i] * bet[i], kb[i] * jnp.exp(gcc[i])], axis=1) for i in every]
    sol = [dot3(split(tinv[i]), split(rhs[i])) for i in every]
    u = [x_[:, :GDN_DV] for x_ in sol]
    w16 = [bf(x_[:, GDN_DV:]) for x_ in sol]
    qg16 = [bf(q[i] * jnp.exp(gcc[i])) for i in every]
    qk16 = [bf(x_) for x_ in qk]
    kd16 = [bf(k[i] * jnp.exp(g_last[i] - gcc[i])) for i in every]
    e_last = [jnp.exp(g) for g in g_last]

    S = [s_ref[0, hh] for hh in range(H)]
    for ci in range(cps):
        of = ci * H
        S16 = [bf(x_) for x_ in S]
        v_new = [u[of + hh] - jnp.dot(w16[of + hh], S16[hh], preferred_element_type=F32) for hh in range(H)]
        v16 = [bf(x_) for x_ in v_new]
        o = [jnp.dot(qg16[of + hh], S16[hh], preferred_element_type=F32)
             + jnp.dot(qk16[of + hh], v16[hh], preferred_element_type=F32) for hh in range(H)]
        S = [S[hh] * e_last[of + hh] + lax.dot_general(kd16[of + hh], v16[hh], (((0,), (0,)), ((), ())),
                                                       preferred_element_type=F32) for hh in range(H)]
        for hh in range(H):
            ms = jnp.mean(o[hh] * o[hh], axis=-1, keepdims=True)
            y = o[hh] * lax.rsqrt(ms + NORM_EPS) * ng_ref[...] * z_ref[0, rows[ci], lanes[hh]].astype(F32)
            o_ref[0, rows[ci], lanes[hh]] = y.astype(o_ref.dtype)
    for hh in range(H):
        s_ref[0, hh] = S[hh]


def _gdn(q, k, v, gab, a_log, dt_bias, z_silu, norm_g, s0, *, c):
    B, T, _ = q.shape
    H = N_GDN_HEADS
    assert T % c == 0 and c % 8 == 0 and (c & (c - 1)) == 0
    cps = 2 if (T // c) % 2 == 0 else 1
    R = cps * c
    stream = pl.BlockSpec((1, R, GDN_WIDTH), lambda b, n: (b, n, 0))
    small = pl.BlockSpec((1, H), lambda b, n: (0, 0))
    state = pl.BlockSpec((1, H, GDN_DK, GDN_DV), lambda b, n: (b, 0, 0, 0))
    in_specs = [stream, stream, stream,
                pl.BlockSpec((1, R, 128), lambda b, n: (b, n, 0)), small, small,
                stream, pl.BlockSpec((1, GDN_DV), lambda b, n: (0, 0))]
    args = [q, k, v, gab, a_log, dt_bias, z_silu, norm_g]
    if s0 is not None:
        in_specs.append(state)
        args.append(s0)
    return pl.pallas_call(
        functools.partial(_gdn_kernel, c=c, cps=cps, has_state=s0 is not None),
        out_shape=(jax.ShapeDtypeStruct((B, T, GDN_WIDTH), BF16),
                   jax.ShapeDtypeStruct((B, H, GDN_DK, GDN_DV), F32)),
        grid=(B, T // R),
        in_specs=in_specs,
        out_specs=(stream, state),
        compiler_params=_cparams(("parallel", "arbitrary")),
        name="gdn",
    )(*args)


def _mix_kernel(oa_ref, ob_ref, wa_ref, wb_ref, sa_ref, sb_ref, o_ref):
    a = jnp.dot(oa_ref[...], wa_ref[...], preferred_element_type=F32)
    b = jnp.dot(ob_ref[...], wb_ref[...], preferred_element_type=F32)
    o_ref[...] = (sa_ref[...].astype(F32) * a + sb_ref[...].astype(F32) * b).astype(o_ref.dtype)


def _mix(o_a, o_b, w_pa, w_pb, gates, *, tm=512, tn=1024):
    M = o_a.shape[0]
    tm = min(tm, M)
    nb = D_MODEL // tn
    return pl.pallas_call(
        _mix_kernel,
        out_shape=jax.ShapeDtypeStruct((M, D_MODEL), BF16),
        grid=(M // tm, nb),
        in_specs=[pl.BlockSpec((tm, DIFF_WIDTH), lambda i, j: (i, 0)),
                  pl.BlockSpec((tm, GDN_WIDTH), lambda i, j: (i, 0)),
                  pl.BlockSpec((DIFF_WIDTH, tn), lambda i, j: (0, j)),
                  pl.BlockSpec((GDN_WIDTH, tn), lambda i, j: (0, j)),
                  pl.BlockSpec((tm, tn), lambda i, j: (i, j)),
                  pl.BlockSpec((tm, tn), lambda i, j: (i, j + nb))],
        out_specs=pl.BlockSpec((tm, tn), lambda i, j: (i, j)),
        compiler_params=_cparams(("parallel", "parallel")),
        name="mix",
    )(o_a, o_b, w_pa, w_pb, gates, gates)


def _post_kernel(x_ref, mix_ref, wo_ref, g1_ref, b1_ref, wxq_ref, mk_ref, mv_ref, wxo_ref, g2_ref, b2_ref,
                 h2_ref, h2b_ref):
    nb, tm, D = x_ref.shape
    x = x_ref[...].reshape(nb * tm, D)
    mix = mix_ref[...].reshape(nb * tm, D)
    h1 = ALPHA * x + jnp.dot(mix, wo_ref[...], preferred_element_type=F32)
    h1 = _layer_norm(h1, g1_ref[...], b1_ref[...])
    qx = jnp.dot(h1.astype(BF16), wxq_ref[...], preferred_element_type=F32) * (XHEAD_DIM ** -0.5)
    qx = qx.astype(BF16)
    seqs = []
    for b in range(nb):
        rows = slice(b * tm, (b + 1) * tm)
        heads = []
        for hh in range(N_XHEADS):
            sl = slice(hh * XHEAD_DIM, (hh + 1) * XHEAD_DIM)
            s = _nt_dot(qx[rows, sl], mk_ref[b, :, sl])
            p = jnp.exp(s - jnp.max(s, axis=-1, keepdims=True))
            p = p / jnp.sum(p, axis=-1, keepdims=True)
            heads.append(jnp.dot(p.astype(BF16), mv_ref[b, :, sl], preferred_element_type=F32))
        seqs.append(jnp.concatenate(heads, axis=1))
    ox = jnp.concatenate(seqs, axis=0).astype(BF16)
    h2 = ALPHA * h1 + jnp.dot(ox, wxo_ref[...], preferred_element_type=F32)
    h2 = _layer_norm(h2, g2_ref[...], b2_ref[...])
    h2_ref[...] = h2.reshape(nb, tm, D)
    h2b_ref[...] = h2.astype(BF16).reshape(nb, tm, D)


def _post(x, mix, w_o, ln1_g, ln1_b, w_xq, mem_k, mem_v, w_xo, ln2_g, ln2_b, *, tm=512):
    B, T, D = x.shape
    tm = min(tm, T)
    nb = min(B, max(1, 256 // tm))
    assert B % nb == 0 and T % tm == 0 and tm % 16 == 0
    const = lambda shape: pl.BlockSpec(shape, lambda b, i: (0, 0), pipeline_mode=pl.Buffered(1))
    rows = lambda: pl.BlockSpec((nb, tm, D), lambda b, i: (b, i, 0))
    mem = lambda: pl.BlockSpec((nb, N_MEM, XWIDTH), lambda b, i: (b, 0, 0))
    return pl.pallas_call(
        _post_kernel,
        out_shape=(jax.ShapeDtypeStruct((B, T, D), F32), jax.ShapeDtypeStruct((B, T, D), BF16)),
        grid=(B // nb, T // tm),
        in_specs=[rows(), rows(), const((D, D)), const((1, D)), const((1, D)), const((D, XWIDTH)),
                  mem(), mem(), const((XWIDTH, D)), const((1, D)), const((1, D))],
        out_specs=(rows(), rows()),
        compiler_params=_cparams(("parallel", "parallel")),
        name="post_attn",
    )(x, mix, w_o, ln1_g, ln1_b, w_xq, mem_k, mem_v, w_xo, ln2_g, ln2_b)


def _ffn_kernel(hb_ref, h_ref, w1_ref, w3_ref, w2_ref, g_ref, b_ref, y_ref, acc_ref):
    f = pl.program_id(1)

    @pl.when(f == 0)
    def _():
        acc_ref[...] = jnp.zeros(acc_ref.shape, F32)

    hb = hb_ref[...]
    a = jnp.dot(hb, w1_ref[...], preferred_element_type=F32)
    b = jnp.dot(hb, w3_ref[...], preferred_element_type=F32)
    act = (_silu(a) * b).astype(BF16)
    acc_ref[...] += jnp.dot(act, w2_ref[...], preferred_element_type=F32)

    @pl.when(f == pl.num_programs(1) - 1)
    def _():
        y_ref[...] = _layer_norm(ALPHA * h_ref[...] + acc_ref[...], g_ref[...], b_ref[...])


def _ffn(h2b, h2, w1, w3, w2, ln_g, ln_b, *, tm=512, tf=512):
    M, D = h2.shape
    tm = min(tm, M)
    assert M % tm == 0 and D_FF % tf == 0
    return pl.pallas_call(
        _ffn_kernel,
        out_shape=jax.ShapeDtypeStruct((M, D), F32),
        grid=(M // tm, D_FF // tf),
        in_specs=[pl.BlockSpec((tm, D), lambda i, f: (i, 0)),
                  pl.BlockSpec((tm, D), lambda i, f: (i, 0)),
                  pl.BlockSpec((D, tf), lambda i, f: (0, f)),
                  pl.BlockSpec((D, tf), lambda i, f: (0, f)),
                  pl.BlockSpec((tf, D), lambda i, f: (f, 0)),
                  pl.BlockSpec((1, D), lambda i, f: (0, 0)),
                  pl.BlockSpec((1, D), lambda i, f: (0, 0))],
        out_specs=pl.BlockSpec((tm, D), lambda i, f: (i, 0)),
        scratch_shapes=[pltpu.VMEM((tm, D), F32)],
        compiler_params=_cparams(("parallel", "arbitrary")),
        name="ffn",
    )(h2b, h2, w1, w3, w2, ln_g, ln_b)


def _encoder_layer(x, mem_k, mem_v, past, W):
    B, T, D = x.shape
    M = B * T
    xb = x.reshape(M, D).astype(BF16)
    w_in = W["w_in_t"]
    in_proj = functools.partial(_proj, xb, w_in, w_rows=True)

    (dq,) = in_proj(OFF_DQ, DIFF_WIDTH, (BF16,), scale=DIFF_HEAD_DIM ** -0.5, name="proj_dq")
    kv_dtypes = (F32, BF16) if past is None else (F32,)
    dk, *dkb = in_proj(OFF_DK, DIFF_WIDTH, kv_dtypes, rows_inner=True, name="proj_dk")
    dv, *dvb = in_proj(OFF_DV, DIFF_WIDTH, kv_dtypes, rows_inner=True, name="proj_dv")
    buf0 = jnp.zeros((B, CONV_W - 1, 3 * GDN_WIDTH), F32) if past is None else past[3]
    streams, tails = [], []
    for t, (norm, scale) in enumerate(((True, GDN_DK ** -0.5), (True, 1.0), (False, 1.0))):
        y, tail = _proj_conv(xb, w_in, OFF_GQKV + t * GDN_WIDTH, buf0, W["conv_w"], B=B, T=T, norm=norm,
                             scale=scale, name="proj_conv_" + "qkv"[t])
        streams.append(y.reshape(B, T, GDN_WIDTH))
        tails.append(tail[:, CONV_PAD - (CONV_W - 1):, :])
    (gz,) = in_proj(OFF_GZ, GDN_WIDTH, (BF16,), act="silu", name="proj_gz")
    (gab,) = in_proj(OFF_GAB, 128, (F32,), tn=128, name="proj_gab")
    (gates,) = in_proj(OFF_GATES, 2 * D_MODEL, (BF16,), act="sigmoid", name="proj_gates")

    dq = dq.reshape(B, T, DIFF_WIDTH)
    dk = dk.reshape(B, T, N_DIFF_HEADS, DIFF_VDIM)
    dv = dv.reshape(B, T, N_DIFF_HEADS, DIFF_VDIM)
    lam_rows = W["lam_rows"]
    if past is None:
        o_a = _diff_attention_prompt(dq, dkb[0].reshape(B, T, DIFF_WIDTH), dvb[0].reshape(B, T, DIFF_WIDTH),
                                     lam_rows, W["diff_subln_g"])
        s0 = None
        c = CHUNK
    else:
        cache_k, cache_v, s0, _ = past
        o_a = _diff_attention_sample(dq, cache_k, cache_v, dk, dv, lam_rows, W["diff_subln_g"])
        c = T
    o_b, s_new = _gdn(*streams, gab.reshape(B, T, 128), W["gdn_a_log"], W["gdn_dt_bias"],
                      gz.reshape(B, T, GDN_WIDTH), W["gdn_norm_g"], s0, c=c)
    new_buf = jnp.concatenate(tails, axis=-1)

    mix = _mix(o_a.reshape(M, DIFF_WIDTH), o_b.reshape(M, GDN_WIDTH), W["w_pa"], W["w_pb"], gates)
    h2, h2b = _post(x, mix.reshape(B, T, D), W["w_o"], W["ln1_g"], W["ln1_b"], W["w_xq"], mem_k, mem_v,
                    W["w_xo"], W["ln2_g"], W["ln2_b"])
    y = _ffn(h2b.reshape(M, D), h2.reshape(M, D), W["w_ff1"], W["w_ff3"], W["w_ff2"], W["ln3_g"], W["ln3_b"])
    return y.reshape(B, T, D), dk, dv, s_new, new_buf


def kernel(x_prompt, x_sample, mem_prompt, cache_diff_k, cache_diff_v, state_gdn, state_gdn_conv, cache_mem_k, cache_mem_v, w_in, conv_w, lam_q1, lam_k1, lam_q2, lam_k2, diff_subln_g, gdn_a_log, gdn_dt_bias, gdn_norm_g, w_pa, w_pb, w_o, ln1_g, ln1_b, w_xq, w_xk, w_xv, w_xo, ln2_g, ln2_b, w_ff1, w_ff3, w_ff2, ln3_g, ln3_b):
    l = 0
    W = {
        "w_in_t": jnp.swapaxes(w_in, 1, 2),
        "conv_w": conv_w[l],
        "lam_rows": tuple(v[l].reshape(1, DIFF_HEAD_DIM) for v in (lam_q1, lam_k1, lam_q2, lam_k2)),
        "diff_subln_g": diff_subln_g[l].reshape(1, DIFF_VDIM),
        "gdn_a_log": gdn_a_log[l].reshape(1, N_GDN_HEADS),
        "gdn_dt_bias": gdn_dt_bias[l].reshape(1, N_GDN_HEADS),
        "gdn_norm_g": gdn_norm_g[l].reshape(1, GDN_DV),
        "w_pa": w_pa[l].astype(BF16), "w_pb": w_pb[l].astype(BF16), "w_o": w_o[l].astype(BF16),
        "ln1_g": ln1_g[l].reshape(1, D_MODEL), "ln1_b": ln1_b[l].reshape(1, D_MODEL),
        "w_xq": w_xq[l].astype(BF16), "w_xo": w_xo[l].astype(BF16),
        "ln2_g": ln2_g[l].reshape(1, D_MODEL), "ln2_b": ln2_b[l].reshape(1, D_MODEL),
        "w_ff1": w_ff1[l].astype(BF16), "w_ff3": w_ff3[l].astype(BF16), "w_ff2": w_ff2[l].astype(BF16),
        "ln3_g": ln3_g[l].reshape(1, D_MODEL), "ln3_b": ln3_b[l].reshape(1, D_MODEL),
    }
    Bp = x_prompt.shape[0]
    memb = mem_prompt.reshape(Bp * N_MEM, D_MODEL).astype(BF16)
    mem_k, mem_kb = _proj(memb, w_xk, 0, XWIDTH, (F32, BF16), tn=XWIDTH, name="proj_mem_k")
    mem_v, mem_vb = _proj(memb, w_xv, 0, XWIDTH, (F32, BF16), tn=XWIDTH, name="proj_mem_v")

    yp, pk, pv, ps, pc = _encoder_layer(x_prompt, mem_kb.reshape(Bp, N_MEM, XWIDTH),
                                        mem_vb.reshape(Bp, N_MEM, XWIDTH), None, W)
    Bs = x_sample.shape[0]
    past = (cache_diff_k[l], cache_diff_v[l], state_gdn[l], state_gdn_conv[l])
    ys, sk, sv, ss, sc = _encoder_layer(x_sample, cache_mem_k[l].reshape(Bs, N_MEM, XWIDTH).astype(BF16),
                                        cache_mem_v[l].reshape(Bs, N_MEM, XWIDTH).astype(BF16), past, W)
    st = lambda a: a[None]
    return (yp, ys, st(pk), st(pv), st(ps), st(pc),
            st(mem_k.reshape(Bp, N_MEM, N_XHEADS, XHEAD_DIM)), st(mem_v.reshape(Bp, N_MEM, N_XHEADS, XHEAD_DIM)),
            st(sk), st(sv), st(ss), st(sc))
```

```python
import functools
import math

import jax
import jax.numpy as jnp
from jax import lax
from jax.experimental import pallas as pl
from jax.experimental.pallas import tpu as pltpu

D_MODEL = 2048
CHUNK = 64
N_DIFF_HEADS = 8
DIFF_HEAD_DIM = 128
DIFF_VDIM = 2 * DIFF_HEAD_DIM
DIFF_WIDTH = N_DIFF_HEADS * DIFF_VDIM
N_GDN_HEADS = 16
GDN_DK = 128
GDN_DV = 128
GDN_WIDTH = N_GDN_HEADS * GDN_DK
CONV_W = 4
N_MEM = 256
N_XHEADS = 4
XHEAD_DIM = 128
XWIDTH = N_XHEADS * XHEAD_DIM
D_FF = 5632
DEPTH = 1
ALPHA = (2.0 * DEPTH) ** 0.25
LN_EPS = 1e-5
NORM_EPS = 1e-6
LAM_INIT = 0.8 - 0.6 * math.exp(-0.3 * 0)

OFF_DQ = 0
OFF_DK = 2048
OFF_DV = 4096
OFF_GQKV = 6144
OFF_GZ = 12288
OFF_GAB = 14336
OFF_GATES = 14368

VMEM_LIMIT = 56 * 1024 * 1024
BF16 = jnp.bfloat16
F32 = jnp.float32
HI = lax.Precision.HIGHEST


def _cparams(sem):
    return pltpu.CompilerParams(dimension_semantics=sem, vmem_limit_bytes=VMEM_LIMIT)


def _sigmoid(x):
    return 1.0 / (1.0 + jnp.exp(-x))


def _silu(x):
    return x * _sigmoid(x)


def _layer_norm(x, g, b):
    mu = jnp.mean(x, axis=-1, keepdims=True)
    xc = x - mu
    var = jnp.mean(xc * xc, axis=-1, keepdims=True)
    return xc * lax.rsqrt(var + LN_EPS) * g + b


def _proj_kernel(x_ref, w_ref, *o_refs, act, scale, w_rows):
    w = w_ref[0].astype(BF16)
    acc = _nt_dot(x_ref[...], w) if w_rows else jnp.dot(x_ref[...], w, preferred_element_type=F32)
    if scale != 1.0:
        acc = acc * scale
    if act == "sigmoid":
        acc = _sigmoid(acc)
    elif act == "silu":
        acc = _silu(acc)
    for o in o_refs:
        o[...] = acc.astype(o.dtype)


def _weight_spec(w, K, tn, col_start, w_rows):
    assert w.shape[0] == DEPTH == 1
    if w_rows:
        return pl.BlockSpec((pl.Element(1), pl.Element(tn), pl.Element(K)),
                            lambda a, b: (0, pl.multiple_of(col_start(a, b), 8), 0))
    return pl.BlockSpec((1, K, tn), lambda a, b: (0, 0, col_start(a, b) // tn))


def _proj(x, w, col_off, n_cols, out_dtypes, *, act=None, scale=1.0, tm=1024, tn=1024, rows_inner=False,
          w_rows=False, name="proj"):
    M, K = x.shape
    tm = min(tm, M)
    tn = min(tn, n_cols)
    assert M % tm == 0 and n_cols % tn == 0 and col_off % (8 if w_rows else tn) == 0
    if rows_inner:
        grid = (n_cols // tn, M // tm)
        ij = lambda a, b: (b, a)
    else:
        grid = (M // tm, n_cols // tn)
        ij = lambda a, b: (a, b)
    outs = pl.pallas_call(
        functools.partial(_proj_kernel, act=act, scale=scale, w_rows=w_rows),
        out_shape=tuple(jax.ShapeDtypeStruct((M, n_cols), dt) for dt in out_dtypes),
        grid=grid,
        in_specs=[pl.BlockSpec((tm, K), lambda a, b: (ij(a, b)[0], 0)),
                  _weight_spec(w, K, tn, lambda a, b: col_off + ij(a, b)[1] * tn, w_rows)],
        out_specs=tuple(pl.BlockSpec((tm, tn), lambda a, b: ij(a, b)) for _ in out_dtypes),
        compiler_params=_cparams(("parallel", "parallel")),
        name=name,
    )(x, w)
    return outs


CONV_PAD = 8


def _proj_conv_kernel(x_ref, w_ref, hist_ref, cw_ref, y_ref, raw_ref, pad_ref, *, nb, T, sub, tr_max, norm, scale):
    tn = w_ref.shape[1]
    tr = min(T, tr_max) if nb == 1 else T
    units = [(s, r) for s in range(tn // sub) for r in range(T // tr)]
    w_slabs = [w_ref[0, s * sub:(s + 1) * sub, :].astype(BF16) for s in range(tn // sub)]

    def rows_of(r):
        return slice(r * tr, (r + 1) * tr) if nb == 1 else slice(None)

    def matmul(s, r, gate):
        rows = rows_of(r)
        if gate is None:
            lhs = x_ref[rows, :]
        else:
            n_rows = nb * tr
            first = x_ref[rows, 0:256] + jnp.concatenate([jnp.concatenate([gate] * 2, axis=1)] * (n_rows // 16), axis=0)
            lhs = jnp.concatenate([first, x_ref[rows, 256:]], axis=1)
        return _nt_dot(lhs, w_slabs[s])

    def epilogue(s, r, acc):
        cols = slice(s * sub, (s + 1) * sub)
        slot = s % 2
        r0 = r * tr
        a3 = acc.reshape(nb, tr, sub)
        pad_ref[slot, :, CONV_PAD + r0:CONV_PAD + r0 + tr, :] = a3
        if r == 0:
            pad_ref[slot, :, CONV_PAD - (CONV_W - 1):CONV_PAD, :] = hist_ref[:, :, cols]
        if r == T // tr - 1:
            raw_ref[:, :, cols] = a3[:, tr - CONV_PAD:, :]
        y = cw_ref[CONV_W - 1:CONV_W, cols] * a3
        for j in range(CONV_W - 1):
            lo = CONV_PAD - (CONV_W - 1) + j + r0
            y = y + cw_ref[j:j + 1, cols] * pad_ref[slot, :, lo:lo + tr, :]
        y = _silu(y)
        if norm:
            heads = []
            for hh in range(sub // GDN_DK):
                yh = y[:, :, hh * GDN_DK:(hh + 1) * GDN_DK]
                heads.append(yh * (lax.rsqrt(jnp.sum(yh * yh, axis=-1, keepdims=True) + NORM_EPS) * scale))
            y = jnp.concatenate(heads, axis=-1)
        y2 = y.reshape(nb * tr, sub)
        y_ref[rows_of(r), cols] = y2.astype(y_ref.dtype)
        bits = pltpu.bitcast(y2[nb * tr - 16:, 0:128], jnp.uint32)
        zero = lax.shift_right_logical(lax.shift_right_logical(bits, jnp.uint32(16)), jnp.uint32(16))
        return zero.astype(F32).astype(BF16)

    gates = [None, None]
    acc = matmul(*units[0], None)
    for i, u in enumerate(units):
        nxt = matmul(*units[i + 1], gates[i + 1]) if i + 1 < len(units) else None
        gates.append(epilogue(*u, acc))
        acc = nxt


def _proj_conv(x, w, col_off, hist, conv_w, *, B, T, norm, scale=1.0, tn=512, sub=512, tr=512, name="proj_conv"):
    M, K = x.shape
    n_cols = GDN_WIDTH
    nb = max(1, 256 // T) if T < 256 else 1
    nb = min(nb, B)
    tm = nb * T
    gq = OFF_GQKV
    assert B % nb == 0 and n_cols % tn == 0 and col_off % 8 == 0 and (col_off - gq) % tn == 0 and T >= CONV_PAD
    hb = (col_off - gq) // tn
    return pl.pallas_call(
        functools.partial(_proj_conv_kernel, nb=nb, T=T, sub=sub, tr_max=tr, norm=norm, scale=scale),
        out_shape=(jax.ShapeDtypeStruct((M, n_cols), BF16), jax.ShapeDtypeStruct((B, CONV_PAD, n_cols), F32)),
        grid=(B // nb, n_cols // tn),
        in_specs=[pl.BlockSpec((tm, K), lambda i, j: (i, 0)),
                  _weight_spec(w, K, tn, lambda i, j: col_off + j * tn, True),
                  pl.BlockSpec((nb, CONV_W - 1, tn), lambda i, j: (i, 0, j + hb)),
                  pl.BlockSpec((CONV_W, tn), lambda i, j: (0, j + hb))],
        out_specs=(pl.BlockSpec((tm, tn), lambda i, j: (i, j)),
                   pl.BlockSpec((nb, CONV_PAD, tn), lambda i, j: (i, 0, j))),
        scratch_shapes=[pltpu.VMEM((min(2, tn // sub), nb, T + CONV_PAD, sub), F32)],
        compiler_params=_cparams(("parallel", "parallel")),
        name=name,
    )(x, w, hist, conv_w)


def _lam_value(lq1, lk1, lq2, lk2):
    a = jnp.sum(lq1 * lk1, axis=-1, keepdims=True)
    b = jnp.sum(lq2 * lk2, axis=-1, keepdims=True)
    return jnp.exp(a) - jnp.exp(b) + LAM_INIT


def _subln(o, g):
    ms = jnp.mean(o * o, axis=-1, keepdims=True)
    return o * lax.rsqrt(ms + NORM_EPS) * g * (1.0 - LAM_INIT)


def _head_slope(h):
    e = (h + 1).astype(F32) * (-8.0 / N_DIFF_HEADS)
    return jnp.exp2(jnp.full((1, 1), e, F32))


def _nt_dot(a, b):
    return lax.dot_general(a, b, (((1,), (1,)), ((), ())), preferred_element_type=F32)


def _lane_tile(x, n):
    return x if n == 1 else jnp.concatenate([x] * n, axis=1)


def _diff_prompt_kernel(lq1_ref, lk1_ref, lq2_ref, lk2_ref, g_ref, base_ref, q_ref, k_ref, v_ref, o_ref,
                        m_ref, l_ref, acc_ref, *, tq, hps):
    LANES = 128
    half = tq // 2
    hg = pl.program_id(1)
    qi = pl.program_id(2)
    slope = [_head_slope(hg * hps + hh) for hh in range(hps)]
    lam = _lam_value(lq1_ref[...], lk1_ref[...], lq2_ref[...], lk2_ref[...])
    col_iota = lax.broadcasted_iota(jnp.int32, (1, tq), 1).astype(F32)
    col_bias = [sl * col_iota for sl in slope]
    maps = range(2)

    m_ref[...] = jnp.full(m_ref.shape, -jnp.inf, F32)
    l_ref[...] = jnp.zeros(l_ref.shape, F32)
    acc_ref[...] = jnp.zeros(acc_ref.shape, F32)

    def cols(hh, m):
        c0 = hh * DIFF_VDIM + m * DIFF_HEAD_DIM
        return slice(c0, c0 + DIFF_HEAD_DIM)

    def update(*jobs):
        todo = [(job, m) for job in jobs for m in maps]
        t = [_nt_dot(q_ref[0, rows, cols(hh, m)], k_ref[0, pl.ds(start, nk), cols(hh, m)]) + bias
             for (hh, rows, start, nk, bias, _), m in todo]
        st = [2 * job[0] + m for job, m in todo]
        m_old = [m_ref[st[i], job[1]] for i, (job, m) in enumerate(todo)]
        m_new = [jnp.maximum(m_old[i], jnp.max(t[i], axis=-1, keepdims=True) + job[5])
                 for i, (job, m) in enumerate(todo)]
        a = [jnp.exp(m_old[i] - m_new[i]) for i in range(len(todo))]
        p = [jnp.exp(t[i] - _lane_tile(m_new[i] - job[5], job[3] // LANES)) for i, (job, m) in enumerate(todo)]
        pv = [jnp.dot(p[i].astype(BF16), v_ref[0, pl.ds(start, nk), hh * DIFF_VDIM:(hh + 1) * DIFF_VDIM],
                      preferred_element_type=F32) for i, ((hh, _, start, nk, _, _), m) in enumerate(todo)]
        for i, ((hh, rows, start, nk, _, _), m) in enumerate(todo):
            psum = p[i][:, :LANES]
            for c in range(1, nk // LANES):
                psum = psum + p[i][:, c * LANES:(c + 1) * LANES]
            l_ref[st[i], rows] = a[i] * l_ref[st[i], rows] + psum
            acc_ref[st[i], rows] = _lane_tile(a[i], DIFF_VDIM // LANES) * acc_ref[st[i], rows] + pv[i]
            m_ref[st[i], rows] = m_new[i]

    every = slice(0, tq)

    def body(j, carry):
        start = pl.multiple_of(j * tq, tq)
        gap = ((qi - j) * tq).astype(F32)
        update(*[(hh, every, start, tq, col_bias[hh], -slope[hh] * gap) for hh in range(hps)])
        return carry

    lax.fori_loop(0, qi, body, 0)

    lo, hi = slice(0, half), slice(half, tq)
    start = pl.multiple_of(qi * tq, tq)
    zero = jnp.zeros((1, 1), F32)
    jobs = []
    for hh in range(hps):
        jobs.append((hh, lo, start, half, slope[hh] * base_ref[:, half:], -slope[hh] * float(half)))
        jobs.append((hh, hi, start, tq, slope[hh] * base_ref[...], zero))
    update(*jobs)

    for hh in range(hps):
        outs = []
        for m in maps:
            inv = 1.0 / jnp.sum(l_ref[2 * hh + m], axis=-1, keepdims=True)
            outs.append(acc_ref[2 * hh + m] * inv)
        o = outs[0] - lam * outs[1]
        o_ref[0, :, hh * DIFF_VDIM:(hh + 1) * DIFF_VDIM] = _subln(o, g_ref[...]).astype(o_ref.dtype)


def _diff_attention_prompt(q, k, v, lam_rows, subln_g, *, tq=512, hps=2):
    B, T, _ = q.shape
    tq = min(tq, T)
    half = tq // 2
    assert T % tq == 0 and half % CHUNK == 0 and N_DIFF_HEADS % hps == 0
    ii = lax.broadcasted_iota(jnp.int32, (half, tq), 0)
    jj = lax.broadcasted_iota(jnp.int32, (half, tq), 1)
    jh = jj - half
    diag = jnp.where((jh // CHUNK) <= (ii // CHUNK), (half + ii - jnp.abs(ii - jh)).astype(F32), -jnp.inf)
    base = jnp.where(jj < half, jj.astype(F32), diag)
    vec = pl.BlockSpec((1, DIFF_HEAD_DIM), lambda b, h, i: (0, 0))
    wide = hps * DIFF_VDIM
    return pl.pallas_call(
        functools.partial(_diff_prompt_kernel, tq=tq, hps=hps),
        out_shape=jax.ShapeDtypeStruct((B, T, DIFF_WIDTH), BF16),
        grid=(B, N_DIFF_HEADS // hps, T // tq),
        in_specs=[vec, vec, vec, vec,
                  pl.BlockSpec((1, DIFF_VDIM), lambda b, h, i: (0, 0)),
                  pl.BlockSpec((half, tq), lambda b, h, i: (0, 0)),
                  pl.BlockSpec((1, tq, wide), lambda b, h, i: (b, i, h)),
                  pl.BlockSpec((1, T, wide), lambda b, h, i: (b, 0, h)),
                  pl.BlockSpec((1, T, wide), lambda b, h, i: (b, 0, h))],
        out_specs=pl.BlockSpec((1, tq, wide), lambda b, h, i: (b, i, h)),
        scratch_shapes=[pltpu.VMEM((2 * hps, tq, 128), F32), pltpu.VMEM((2 * hps, tq, 128), F32),
                        pltpu.VMEM((2 * hps, tq, DIFF_VDIM), F32)],
        compiler_params=_cparams(("parallel", "parallel", "parallel")),
        name="diff_attn_prompt",
    )(*lam_rows, subln_g, base, q, k, v)


def _diff_sample_kernel(lq1_ref, lk1_ref, lq2_ref, lk2_ref, g_ref, q_ref, kc_ref, vc_ref, kn_ref, vn_ref,
                        o_ref, *, past, pc):
    H = N_DIFF_HEADS
    T = q_ref.shape[1]
    L = H * T
    rows = pc * H
    n_chunks = past // pc
    lam = _lam_value(lq1_ref[...], lk1_ref[...], lq2_ref[...], lk2_ref[...])
    maps = range(2)

    def map_cols(m):
        return slice(m * DIFF_HEAD_DIM, (m + 1) * DIFF_HEAD_DIM)

    qm = [jnp.concatenate([q_ref[0, :, h * DIFF_VDIM + m * DIFF_HEAD_DIM:h * DIFF_VDIM + (m + 1) * DIFF_HEAD_DIM]
                           for h in range(H)], axis=0) for m in maps]
    lane = lax.broadcasted_iota(jnp.int32, (1, L), 1)
    h_lane = lane // T
    t_lane = lane % T
    slope = jnp.exp2((h_lane + 1).astype(F32) * (-8.0 / H))
    base = slope * (lax.broadcasted_iota(jnp.int32, (rows, L), 0) // H).astype(F32)
    valid = lax.broadcasted_iota(jnp.int32, (H, L), 0) == h_lane

    def chunk_shift(c):
        return slope * ((c * pc - past) - t_lane).astype(F32)

    zq = jnp.zeros((L, DIFF_HEAD_DIM), BF16)
    q_both = jnp.concatenate([jnp.concatenate([qm[0], zq], axis=1), jnp.concatenate([zq, qm[1]], axis=1)], axis=0)

    def both_maps(keys):
        s2 = _nt_dot(keys, q_both)
        return [s2[:, m * L:(m + 1) * L] for m in maps]

    def chunk_scores(c):
        r0 = c * rows if isinstance(c, int) else pl.multiple_of(c * rows, rows)
        s2 = both_maps(kc_ref[0, pl.ds(r0, rows), :].astype(BF16))
        return [(s2[m] + base).reshape(pc, H, L) for m in maps]

    shared = chunk_scores(0) if n_chunks == 1 else None

    def stats_step(c, carry):
        shift = chunk_shift(c)
        s = chunk_scores(c) if shared is None else shared
        out = []
        for m in maps:
            m_old, l_old = carry[2 * m], carry[2 * m + 1]
            m_new = jnp.maximum(m_old, jnp.max(s[m], axis=0) + shift)
            l_new = l_old * jnp.exp(m_old - m_new) + jnp.sum(jnp.exp(s[m] - (m_new - shift)[None]), axis=0)
            out += [m_new, l_new]
        return tuple(out)

    init = (jnp.full((H, L), -jnp.inf, F32), jnp.zeros((H, L), F32)) * 2
    stats = stats_step(0, init) if n_chunks == 1 else lax.fori_loop(0, n_chunks, stats_step, init)

    kn = kn_ref[0].astype(BF16)
    vn = vn_ref[0].astype(BF16)
    tk = lax.broadcasted_iota(jnp.int32, (T * H, L), 0) // H
    allowed = ((past + tk) // CHUNK) <= ((past + t_lane) // CHUNK)
    bias_n = jnp.where(allowed, -slope * jnp.abs(tk - t_lane).astype(F32), -jnp.inf)
    s_n = [(s + bias_n).reshape(T, H, L) for s in both_maps(kn)]
    m_fin, coef = [], []
    for m in maps:
        m_run, l_run = stats[2 * m], stats[2 * m + 1]
        mf = jnp.maximum(m_run, jnp.max(s_n[m], axis=0))
        lf = l_run * jnp.exp(m_run - mf) + jnp.sum(jnp.exp(s_n[m] - mf[None]), axis=0)
        m_fin.append(mf)
        coef.append(jnp.where(valid, (1.0 if m == 0 else -lam) / lf, 0.0))

    def tn_dot(w, v):
        return lax.dot_general(w, v, (((0,), (0,)), ((), ())), preferred_element_type=F32)

    def out_step(c, acc):
        shift = chunk_shift(c)
        s = chunk_scores(c) if shared is None else shared
        r0 = c * rows if isinstance(c, int) else pl.multiple_of(c * rows, rows)
        vblk = vc_ref[0, pl.ds(r0, rows), :].astype(BF16)
        w = sum(jnp.exp(s[m] - (m_fin[m] - shift)[None]) * coef[m][None] for m in maps)
        return acc + tn_dot(w.reshape(rows, L).astype(BF16), vblk)

    acc0 = jnp.zeros((L, DIFF_VDIM), F32)
    acc = out_step(0, acc0) if n_chunks == 1 else lax.fori_loop(0, n_chunks, out_step, acc0)
    w_n = sum(jnp.exp(s_n[m] - m_fin[m][None]) * coef[m][None] for m in maps)
    acc = acc + tn_dot(w_n.reshape(T * H, L).astype(BF16), vn)
    o = _subln(acc, g_ref[...]).astype(o_ref.dtype)
    for h in range(H):
        o_ref[0, :, h * DIFF_VDIM:(h + 1) * DIFF_VDIM] = o[h * T:(h + 1) * T]


def _diff_attention_sample(q, cache_k, cache_v, k_new, v_new, lam_rows, subln_g):
    B, T, _ = q.shape
    P = cache_k.shape[1]
    H = N_DIFF_HEADS
    pc = min(P, 1024)
    assert H * T == 128 and P > 0 and P % pc == 0
    vec = pl.BlockSpec((1, DIFF_HEAD_DIM), lambda b: (0, 0))
    tok = pl.BlockSpec((1, T, DIFF_WIDTH), lambda b: (b, 0, 0))
    new = pl.BlockSpec((1, T * H, DIFF_VDIM), lambda b: (b, 0, 0))
    old = pl.BlockSpec((1, P * H, DIFF_VDIM), lambda b: (b, 0, 0))
    flat = lambda a: a.reshape(B, a.shape[1] * H, DIFF_VDIM)
    return pl.pallas_call(
        functools.partial(_diff_sample_kernel, past=P, pc=pc),
        out_shape=jax.ShapeDtypeStruct((B, T, DIFF_WIDTH), BF16),
        grid=(B,),
        in_specs=[vec, vec, vec, vec, pl.BlockSpec((1, DIFF_VDIM), lambda b: (0, 0)),
                  tok, old, old, new, new],
        out_specs=tok,
        compiler_params=_cparams(("parallel",)),
        name="diff_attn_sample",
    )(*lam_rows, subln_g, q, flat(cache_k), flat(cache_v), flat(k_new), flat(v_new))


def _gdn_kernel(*refs, c, cps, has_state):
    if has_state:
        (xq_ref, xk_ref, xv_ref, ab_ref, alog_ref, dtb_ref, z_ref, ng_ref, s0_ref, o_ref, s_ref) = refs
    else:
        (xq_ref, xk_ref, xv_ref, ab_ref, alog_ref, dtb_ref, z_ref, ng_ref, o_ref, s_ref) = refs
        s0_ref = None
    n = pl.program_id(1)
    H = N_GDN_HEADS

    @pl.when(n == 0)
    def _init():
        if has_state:
            s_ref[0] = s0_ref[0]
        else:
            s_ref[...] = jnp.zeros(s_ref.shape, F32)

    ab = ab_ref[0]
    a_in = ab[:, 0:H] + dtb_ref[...]
    softplus = jnp.maximum(a_in, 0.0) + jnp.log1p(jnp.exp(-jnp.abs(a_in)))
    g_col = -jnp.exp(alog_ref[...]) * softplus
    beta_col = _sigmoid(ab[:, H:2 * H])
    r = lax.broadcasted_iota(jnp.int32, (c, c), 0)
    s = lax.broadcasted_iota(jnp.int32, (c, c), 1)
    tri = r >= s
    strict = r > s
    tri_f = tri.astype(F32)
    eye_c = (r == s).astype(F32)
    eye_h = (lax.broadcasted_iota(jnp.int32, (H, H), 0) == lax.broadcasted_iota(jnp.int32, (H, H), 1)).astype(F32)

    def bf(a):
        return a.astype(BF16)

    def split(a):
        hi = a.astype(BF16)
        return hi, (a - hi.astype(F32)).astype(BF16)

    def dot3(a, b):
        (ah, al), (bh, bl) = a, b
        return (jnp.dot(al, bh, preferred_element_type=F32) + jnp.dot(ah, bl, preferred_element_type=F32)
                + jnp.dot(ah, bh, preferred_element_type=F32))

    blocks = []
    b_ = 1
    while b_ < c:
        blocks.append((((r // (2 * b_)) == (s // (2 * b_))) & ((r // b_) != (s // b_)) & strict).astype(F32))
        b_ *= 2

    rows = [slice(ci * c, (ci + 1) * c) for ci in range(cps)]
    lanes = [slice(hh * GDN_DK, (hh + 1) * GDN_DK) for hh in range(H)]
    gc_col = [jnp.dot(tri_f, g_col[rows[ci]], preferred_element_type=F32, precision=HI) for ci in range(cps)]
    gc_row = [lax.dot_general(eye_h, gc_col[ci], (((1,), (1,)), ((), ())), preferred_element_type=F32,
                              precision=HI) for ci in range(cps)]
    items = [(ci, hh) for ci in range(cps) for hh in range(H)]
    every = range(len(items))
    gcc = [gc_col[ci][:, hh:hh + 1] for ci, hh in items]
    bet = [beta_col[rows[ci], hh:hh + 1] for ci, hh in items]
    gcr = [gc_row[ci][hh:hh + 1, :] for ci, hh in items]
    g_last = [g[c - 1:c, :] for g in gcc]

    q = [xq_ref[0, rows[ci], lanes[hh]].astype(F32) for ci, hh in items]
    k = [xk_ref[0, rows[ci], lanes[hh]].astype(F32) for ci, hh in items]
    v = [xv_ref[0, rows[ci], lanes[hh]].astype(F32) for ci, hh in items]

    decay = [jnp.exp(jnp.where(tri, gcc[i] - gcr[i], -jnp.inf)) for i in every]
    kb = [k[i] * bet[i] for i in every]
    k16 = [bf(k[i]) for i in every]
    mmat = [jnp.where(strict, _nt_dot(bf(kb[i]), k16[i]) * decay[i], 0.0) for i in every]
    qk = [jnp.where(tri, _nt_dot(bf(q[i]), k16[i]) * decay[i], 0.0) for i in every]

    tinv = [eye_c - mmat[i] * blocks[0] for i in every]
    for lvl in range(1, len(blocks)):
        d16 = [bf(t) for t in tinv]
        x = [jnp.dot(bf(mmat[i] * blocks[lvl]), d16[i], preferred_element_type=F32) for i in every]
        tinv = [tinv[i] - jnp.dot(d16[i], bf(x[i]), preferred_element_type=F32) for i in every]
    m_s = [split(m) for m in mmat]
    t_s = [split(t) for t in tinv]
    res = [(eye_c - tinv[i]) - dot3(m_s[i], t_s[i]) for i in every]
    tinv = [tinv[i] + jnp.dot(t_s[i][0], bf(res[i]), preferred_element_type=F32) for i in every]

    rhs = [jnp.concatenate([v[i] * bet[i], kb[i] * jnp.exp(gcc[i])], axis=1) for i in every]
    sol = [dot3(split(tinv[i]), split(rhs[i])) for i in every]
    u = [x_[:, :GDN_DV] for x_ in sol]
    w16 = [bf(x_[:, GDN_DV:]) for x_ in sol]
    qg16 = [bf(q[i] * jnp.exp(gcc[i])) for i in every]
    qk16 = [bf(x_) for x_ in qk]
    kd16 = [bf(k[i] * jnp.exp(g_last[i] - gcc[i])) for i in every]
    e_last = [jnp.exp(g) for g in g_last]

    S = [s_ref[0, hh] for hh in range(H)]
    for ci in range(cps):
        of = ci * H
        S16 = [bf(x_) for x_ in S]
        v_new = [u[of + hh] - jnp.dot(w16[of + hh], S16[hh], preferred_element_type=F32) for hh in range(H)]
        v16 = [bf(x_) for x_ in v_new]
        o = [jnp.dot(qg16[of + hh], S16[hh], preferred_element_type=F32)
             + jnp.dot(qk16[of + hh], v16[hh], preferred_element_type=F32) for hh in range(H)]
        S = [S[hh] * e_last[of + hh] + lax.dot_general(kd16[of + hh], v16[hh], (((0,), (0,)), ((), ())),
                                                       preferred_element_type=F32) for hh in range(H)]
        for hh in range(H):
            ms = jnp.mean(o[hh] * o[hh], axis=-1, keepdims=True)
            y = o[hh] * lax.rsqrt(ms + NORM_EPS) * ng_ref[...] * z_ref[0, rows[ci], lanes[hh]].astype(F32)
            o_ref[0, rows[ci], lanes[hh]] = y.astype(o_ref.dtype)
    for hh in range(H):
        s_ref[0, hh] = S[hh]


def _gdn(q, k, v, gab, a_log, dt_bias, z_silu, norm_g, s0, *, c):
    B, T, _ = q.shape
    H = N_GDN_HEADS
    assert T % c == 0 and c % 8 == 0 and (c & (c - 1)) == 0
    cps = 2 if (T // c) % 2 == 0 else 1
    R = cps * c
    stream = pl.BlockSpec((1, R, GDN_WIDTH), lambda b, n: (b, n, 0))
    small = pl.BlockSpec((1, H), lambda b, n: (0, 0))
    state = pl.BlockSpec((1, H, GDN_DK, GDN_DV), lambda b, n: (b, 0, 0, 0))
    in_specs = [stream, stream, stream,
                pl.BlockSpec((1, R, 128), lambda b, n: (b, n, 0)), small, small,
                stream, pl.BlockSpec((1, GDN_DV), lambda b, n: (0, 0))]
    args = [q, k, v, gab, a_log, dt_bias, z_silu, norm_g]
    if s0 is not None:
        in_specs.append(state)
        args.append(s0)
    return pl.pallas_call(
        functools.partial(_gdn_kernel, c=c, cps=cps, has_state=s0 is not None),
        out_shape=(jax.ShapeDtypeStruct((B, T, GDN_WIDTH), BF16),
                   jax.ShapeDtypeStruct((B, H, GDN_DK, GDN_DV), F32)),
        grid=(B, T // R),
        in_specs=in_specs,
        out_specs=(stream, state),
        compiler_params=_cparams(("parallel", "arbitrary")),
        name="gdn",
    )(*args)


def _mix_kernel(oa_ref, ob_ref, wa_ref, wb_ref, sa_ref, sb_ref, o_ref):
    a = jnp.dot(oa_ref[...], wa_ref[...], preferred_element_type=F32)
    b = jnp.dot(ob_ref[...], wb_ref[...], preferred_element_type=F32)
    o_ref[...] = (sa_ref[...].astype(F32) * a + sb_ref[...].astype(F32) * b).astype(o_ref.dtype)


def _mix(o_a, o_b, w_pa, w_pb, gates, *, tm=512, tn=1024):
    M = o_a.shape[0]
    tm = min(tm, M)
    nb = D_MODEL // tn
    return pl.pallas_call(
        _mix_kernel,
        out_shape=jax.ShapeDtypeStruct((M, D_MODEL), BF16),
        grid=(M // tm, nb),
        in_specs=[pl.BlockSpec((tm, DIFF_WIDTH), lambda i, j: (i, 0)),
                  pl.BlockSpec((tm, GDN_WIDTH), lambda i, j: (i, 0)),
                  pl.BlockSpec((DIFF_WIDTH, tn), lambda i, j: (0, j)),
                  pl.BlockSpec((GDN_WIDTH, tn), lambda i, j: (0, j)),
                  pl.BlockSpec((tm, tn), lambda i, j: (i, j)),
                  pl.BlockSpec((tm, tn), lambda i, j: (i, j + nb))],
        out_specs=pl.BlockSpec((tm, tn), lambda i, j: (i, j)),
        compiler_params=_cparams(("parallel", "parallel")),
        name="mix",
    )(o_a, o_b, w_pa, w_pb, gates, gates)


def _post_kernel(x_ref, mix_ref, wo_ref, g1_ref, b1_ref, wxq_ref, mk_ref, mv_ref, wxo_ref, g2_ref, b2_ref,
                 h2_ref, h2b_ref):
    nb, tm, D = x_ref.shape
    x = x_ref[...].reshape(nb * tm, D)
    mix = mix_ref[...].reshape(nb * tm, D)
    h1 = ALPHA * x + jnp.dot(mix, wo_ref[...], preferred_element_type=F32)
    h1 = _layer_norm(h1, g1_ref[...], b1_ref[...])
    qx = jnp.dot(h1.astype(BF16), wxq_ref[...], preferred_element_type=F32) * (XHEAD_DIM ** -0.5)
    qx = qx.astype(BF16)
    seqs = []
    for b in range(nb):
        rows = slice(b * tm, (b + 1) * tm)
        heads = []
        for hh in range(N_XHEADS):
            sl = slice(hh * XHEAD_DIM, (hh + 1) * XHEAD_DIM)
            s = _nt_dot(qx[rows, sl], mk_ref[b, :, sl])
            p = jnp.exp(s - jnp.max(s, axis=-1, keepdims=True))
            p = p / jnp.sum(p, axis=-1, keepdims=True)
            heads.append(jnp.dot(p.astype(BF16), mv_ref[b, :, sl], preferred_element_type=F32))
        seqs.append(jnp.concatenate(heads, axis=1))
    ox = jnp.concatenate(seqs, axis=0).astype(BF16)
    h2 = ALPHA * h1 + jnp.dot(ox, wxo_ref[...], preferred_element_type=F32)
    h2 = _layer_norm(h2, g2_ref[...], b2_ref[...])
    h2_ref[...] = h2.reshape(nb, tm, D)
    h2b_ref[...] = h2.astype(BF16).reshape(nb, tm, D)


def _post(x, mix, w_o, ln1_g, ln1_b, w_xq, mem_k, mem_v, w_xo, ln2_g, ln2_b, *, tm=512):
    B, T, D = x.shape
    tm = min(tm, T)
    nb = min(B, max(1, 256 // tm))
    assert B % nb == 0 and T % tm == 0 and tm % 16 == 0
    const = lambda shape: pl.BlockSpec(shape, lambda b, i: (0, 0), pipeline_mode=pl.Buffered(1))
    rows = lambda: pl.BlockSpec((nb, tm, D), lambda b, i: (b, i, 0))
    mem = lambda: pl.BlockSpec((nb, N_MEM, XWIDTH), lambda b, i: (b, 0, 0))
    return pl.pallas_call(
        _post_kernel,
        out_shape=(jax.ShapeDtypeStruct((B, T, D), F32), jax.ShapeDtypeStruct((B, T, D), BF16)),
        grid=(B // nb, T // tm),
        in_specs=[rows(), rows(), const((D, D)), const((1, D)), const((1, D)), const((D, XWIDTH)),
                  mem(), mem(), const((XWIDTH, D)), const((1, D)), const((1, D))],
        out_specs=(rows(), rows()),
        compiler_params=_cparams(("parallel", "parallel")),
        name="post_attn",
    )(x, mix, w_o, ln1_g, ln1_b, w_xq, mem_k, mem_v, w_xo, ln2_g, ln2_b)


def _ffn_kernel(hb_ref, h_ref, w1_ref, w3_ref, w2_ref, g_ref, b_ref, y_ref, acc_ref):
    f = pl.program_id(1)

    @pl.when(f == 0)
    def _():
        acc_ref[...] = jnp.zeros(acc_ref.shape, F32)

    hb = hb_ref[...]
    a = jnp.dot(hb, w1_ref[...], preferred_element_type=F32)
    b = jnp.dot(hb, w3_ref[...], preferred_element_type=F32)
    act = (_silu(a) * b).astype(BF16)
    acc_ref[...] += jnp.dot(act, w2_ref[...], preferred_element_type=F32)

    @pl.when(f == pl.num_programs(1) - 1)
    def _():
        y_ref[...] = _layer_norm(ALPHA * h_ref[...] + acc_ref[...], g_ref[...], b_ref[...])


def _ffn(h2b, h2, w1, w3, w2, ln_g, ln_b, *, tm=512, tf=512):
    M, D = h2.shape
    tm = min(tm, M)
    assert M % tm == 0 and D_FF % tf == 0
    return pl.pallas_call(
        _ffn_kernel,
        out_shape=jax.ShapeDtypeStruct((M, D), F32),
        grid=(M // tm, D_FF // tf),
        in_specs=[pl.BlockSpec((tm, D), lambda i, f: (i, 0)),
                  pl.BlockSpec((tm, D), lambda i, f: (i, 0)),
                  pl.BlockSpec((D, tf), lambda i, f: (0, f)),
                  pl.BlockSpec((D, tf), lambda i, f: (0, f)),
                  pl.BlockSpec((tf, D), lambda i, f: (f, 0)),
                  pl.BlockSpec((1, D), lambda i, f: (0, 0)),
                  pl.BlockSpec((1, D), lambda i, f: (0, 0))],
        out_specs=pl.BlockSpec((tm, D), lambda i, f: (i, 0)),
        scratch_shapes=[pltpu.VMEM((tm, D), F32)],
        compiler_params=_cparams(("parallel", "arbitrary")),
        name="ffn",
    )(h2b, h2, w1, w3, w2, ln_g, ln_b)


def _encoder_layer(x, mem_k, mem_v, past, W):
    B, T, D = x.shape
    M = B * T
    xb = x.reshape(M, D).astype(BF16)
    w_in = W["w_in_t"]
    in_proj = functools.partial(_proj, xb, w_in, w_rows=True)

    (dq,) = in_proj(OFF_DQ, DIFF_WIDTH, (BF16,), scale=DIFF_HEAD_DIM ** -0.5, name="proj_dq")
    kv_dtypes = (F32, BF16) if past is None else (F32,)
    dk, *dkb = in_proj(OFF_DK, DIFF_WIDTH, kv_dtypes, rows_inner=True, name="proj_dk")
    dv, *dvb = in_proj(OFF_DV, DIFF_WIDTH, kv_dtypes, rows_inner=True, name="proj_dv")
    buf0 = jnp.zeros((B, CONV_W - 1, 3 * GDN_WIDTH), F32) if past is None else past[3]
    streams, tails = [], []
    for t, (norm, scale) in enumerate(((True, GDN_DK ** -0.5), (True, 1.0), (False, 1.0))):
        y, tail = _proj_conv(xb, w_in, OFF_GQKV + t * GDN_WIDTH, buf0, W["conv_w"], B=B, T=T, norm=norm,
                             scale=scale, name="proj_conv_" + "qkv"[t])
        streams.append(y.reshape(B, T, GDN_WIDTH))
        tails.append(tail[:, CONV_PAD - (CONV_W - 1):, :])
    (gz,) = in_proj(OFF_GZ, GDN_WIDTH, (BF16,), act="silu", name="proj_gz")
    (gab,) = in_proj(OFF_GAB, 128, (F32,), tn=128, name="proj_gab")
    (gates,) = in_proj(OFF_GATES, 2 * D_MODEL, (BF16,), act="sigmoid", name="proj_gates")

    dq = dq.reshape(B, T, DIFF_WIDTH)
    dk = dk.reshape(B, T, N_DIFF_HEADS, DIFF_VDIM)
    dv = dv.reshape(B, T, N_DIFF_HEADS, DIFF_VDIM)
    lam_rows = W["lam_rows"]
    if past is None:
        o_a = _diff_attention_prompt(dq, dkb[0].reshape(B, T, DIFF_WIDTH), dvb[0].reshape(B, T, DIFF_WIDTH),
                                     lam_rows, W["diff_subln_g"])
        s0 = None
        c = CHUNK
    else:
        cache_k, cache_v, s0, _ = past
        o_a = _diff_attention_sample(dq, cache_k, cache_v, dk, dv, lam_rows, W["diff_subln_g"])
        c = T
    o_b, s_new = _gdn(*streams, gab.reshape(B, T, 128), W["gdn_a_log"], W["gdn_dt_bias"],
                      gz.reshape(B, T, GDN_WIDTH), W["gdn_norm_g"], s0, c=c)
    new_buf = jnp.concatenate(tails, axis=-1)

    mix = _mix(o_a.reshape(M, DIFF_WIDTH), o_b.reshape(M, GDN_WIDTH), W["w_pa"], W["w_pb"], gates)
    h2, h2b = _post(x, mix.reshape(B, T, D), W["w_o"], W["ln1_g"], W["ln1_b"], W["w_xq"], mem_k, mem_v,
                    W["w_xo"], W["ln2_g"], W["ln2_b"])
    y = _ffn(h2b.reshape(M, D), h2.reshape(M, D), W["w_ff1"], W["w_ff3"], W["w_ff2"], W["ln3_g"], W["ln3_b"])
    return y.reshape(B, T, D), dk, dv, s_new, new_buf


def kernel(x_prompt, x_sample, mem_prompt, cache_diff_k, cache_diff_v, state_gdn, state_gdn_conv, cache_mem_k, cache_mem_v, w_in, conv_w, lam_q1, lam_k1, lam_q2, lam_k2, diff_subln_g, gdn_a_log, gdn_dt_bias, gdn_norm_g, w_pa, w_pb, w_o, ln1_g, ln1_b, w_xq, w_xk, w_xv, w_xo, ln2_g, ln2_b, w_ff1, w_ff3, w_ff2, ln3_g, ln3_b):
    l = 0
    W = {
        "w_in_t": jnp.swapaxes(w_in, 1, 2),
        "conv_w": conv_w[l],
        "lam_rows": tuple(v[l].reshape(1, DIFF_HEAD_DIM) for v in (lam_q1, lam_k1, lam_q2, lam_k2)),
        "diff_subln_g": diff_subln_g[l].reshape(1, DIFF_VDIM),
        "gdn_a_log": gdn_a_log[l].reshape(1, N_GDN_HEADS),
        "gdn_dt_bias": gdn_dt_bias[l].reshape(1, N_GDN_HEADS),
        "gdn_norm_g": gdn_norm_g[l].reshape(1, GDN_DV),
        "w_pa": w_pa[l].astype(BF16), "w_pb": w_pb[l].astype(BF16), "w_o": w_o[l].astype(BF16),
        "ln1_g": ln1_g[l].reshape(1, D_MODEL), "ln1_b": ln1_b[l].reshape(1, D_MODEL),
        "w_xq": w_xq[l].astype(BF16), "w_xo": w_xo[l].astype(BF16),
        "ln2_g": ln2_g[l].reshape(1, D_MODEL), "ln2_b": ln2_b[l].reshape(1, D_MODEL),
        "w_ff1": w_ff1[l].astype(BF16), "w_ff3": w_ff3[l].astype(BF16), "w_ff2": w_ff2[l].astype(BF16),
        "ln3_g": ln3_g[l].reshape(1, D_MODEL), "ln3_b": ln3_b[l].reshape(1, D_MODEL),
    }
    Bp = x_prompt.shape[0]
    memb = mem_prompt.reshape(Bp * N_MEM, D_MODEL).astype(BF16)
    mem_k, mem_kb = _proj(memb, w_xk, 0, XWIDTH, (F32, BF16), tn=XWIDTH, name="proj_mem_k")
    mem_v, mem_vb = _proj(memb, w_xv, 0, XWIDTH, (F32, BF16), tn=XWIDTH, name="proj_mem_v")

    yp, pk, pv, ps, pc = _encoder_layer(x_prompt, mem_kb.reshape(Bp, N_MEM, XWIDTH),
                                        mem_vb.reshape(Bp, N_MEM, XWIDTH), None, W)
    Bs = x_sample.shape[0]
    past = (cache_diff_k[l], cache_diff_v[l], state_gdn[l], state_gdn_conv[l])
    ys, sk, sv, ss, sc = _encoder_layer(x_sample, cache_mem_k[l].reshape(Bs, N_MEM, XWIDTH).astype(BF16),
                                        cache_mem_v[l].reshape(Bs, N_MEM, XWIDTH).astype(BF16), past, W)
    st = lambda a: a[None]
    return (yp, ys, st(pk), st(pv), st(ps), st(pc),
            st(mem_k.reshape(Bp, N_MEM, N_XHEADS, XHEAD_DIM)), st(mem_v.reshape(Bp, N_MEM, N_XHEADS, XHEAD_DIM)),
            st(sk), st(sv), st(ss), st(sc))
```

```python
import functools
import math

import jax
import jax.numpy as jnp
from jax import lax
from jax.experimental import pallas as pl
from jax.experimental.pallas import tpu as pltpu

D_MODEL = 2048
CHUNK = 64
N_DIFF_HEADS = 8
DIFF_HEAD_DIM = 128
DIFF_VDIM = 2 * DIFF_HEAD_DIM
DIFF_WIDTH = N_DIFF_HEADS * DIFF_VDIM
N_GDN_HEADS = 16
GDN_DK = 128
GDN_DV = 128
GDN_WIDTH = N_GDN_HEADS * GDN_DK
CONV_W = 4
N_MEM = 256
N_XHEADS = 4
XHEAD_DIM = 128
XWIDTH = N_XHEADS * XHEAD_DIM
D_FF = 5632
DEPTH = 1
ALPHA = (2.0 * DEPTH) ** 0.25
LN_EPS = 1e-5
NORM_EPS = 1e-6
LAM_INIT = 0.8 - 0.6 * math.exp(-0.3 * 0)

OFF_DQ = 0
OFF_DK = 2048
OFF_DV = 4096
OFF_GQKV = 6144
OFF_GZ = 12288
OFF_GAB = 14336
OFF_GATES = 14368

VMEM_LIMIT = 56 * 1024 * 1024
BF16 = jnp.bfloat16
F32 = jnp.float32
HI = lax.Precision.HIGHEST


def _cparams(sem):
    return pltpu.CompilerParams(dimension_semantics=sem, vmem_limit_bytes=VMEM_LIMIT)


def _sigmoid(x):
    return 1.0 / (1.0 + jnp.exp(-x))


def _silu(x):
    return x * _sigmoid(x)


def _layer_norm(x, g, b):
    mu = jnp.mean(x, axis=-1, keepdims=True)
    xc = x - mu
    var = jnp.mean(xc * xc, axis=-1, keepdims=True)
    return xc * lax.rsqrt(var + LN_EPS) * g + b


def _proj_kernel(x_ref, w_ref, *o_refs, act, scale, w_rows, emit_x):
    w = w_ref[0].astype(BF16)
    x = x_ref[...].astype(BF16)
    if emit_x:
        o_refs[0][...] = x
        o_refs = o_refs[1:]
    acc = _nt_dot(x, w) if w_rows else jnp.dot(x, w, preferred_element_type=F32)
    if scale != 1.0:
        acc = acc * scale
    if act == "sigmoid":
        acc = _sigmoid(acc)
    elif act == "silu":
        acc = _silu(acc)
    for o in o_refs:
        o[...] = acc.astype(o.dtype)


def _weight_spec(w, K, tn, col_start, w_rows):
    assert w.shape[0] == DEPTH == 1
    if w_rows:
        return pl.BlockSpec((pl.Element(1), pl.Element(tn), pl.Element(K)),
                            lambda a, b: (0, pl.multiple_of(col_start(a, b), 8), 0))
    return pl.BlockSpec((1, K, tn), lambda a, b: (0, 0, col_start(a, b) // tn))


def _proj(x, w, col_off, n_cols, out_dtypes, *, act=None, scale=1.0, tm=1024, tn=1024, rows_inner=False,
          w_rows=False, emit_x=False, name="proj"):
    M, K = x.shape
    tm = min(tm, M)
    tn = min(tn, n_cols)
    assert M % tm == 0 and n_cols % tn == 0 and col_off % (8 if w_rows else tn) == 0
    if rows_inner:
        grid = (n_cols // tn, M // tm)
        ij = lambda a, b: (b, a)
    else:
        grid = (M // tm, n_cols // tn)
        ij = lambda a, b: (a, b)
    x_spec = pl.BlockSpec((tm, K), lambda a, b: (ij(a, b)[0], 0))
    out_shape = [jax.ShapeDtypeStruct((M, n_cols), dt) for dt in out_dtypes]
    out_specs = [pl.BlockSpec((tm, tn), lambda a, b: ij(a, b)) for _ in out_dtypes]
    if emit_x:
        assert not rows_inner
        out_shape.insert(0, jax.ShapeDtypeStruct((M, K), BF16))
        out_specs.insert(0, x_spec)
    outs = pl.pallas_call(
        functools.partial(_proj_kernel, act=act, scale=scale, w_rows=w_rows, emit_x=emit_x),
        out_shape=tuple(out_shape),
        grid=grid,
        in_specs=[x_spec, _weight_spec(w, K, tn, lambda a, b: col_off + ij(a, b)[1] * tn, w_rows)],
        out_specs=tuple(out_specs),
        compiler_params=_cparams(("parallel", "arbitrary") if emit_x else ("parallel", "parallel")),
        name=name,
    )(x, w)
    return outs


CONV_PAD = 8


def _proj_conv_kernel(x_ref, w_ref, hist_ref, cw_ref, y_ref, raw_ref, pad_ref, *, nb, T, sub, tr_max, norm, scale):
    tn = w_ref.shape[1]
    tr = min(T, tr_max) if nb == 1 else T
    units = [(s, r) for s in range(tn // sub) for r in range(T // tr)]
    w_slabs = [w_ref[0, s * sub:(s + 1) * sub, :].astype(BF16) for s in range(tn // sub)]

    def rows_of(r):
        return slice(r * tr, (r + 1) * tr) if nb == 1 else slice(None)

    def matmul(s, r, gate):
        rows = rows_of(r)
        if gate is None:
            lhs = x_ref[rows, :]
        else:
            n_rows = nb * tr
            first = x_ref[rows, 0:256] + jnp.concatenate([jnp.concatenate([gate] * 2, axis=1)] * (n_rows // 16), axis=0)
            lhs = jnp.concatenate([first, x_ref[rows, 256:]], axis=1)
        return _nt_dot(lhs, w_slabs[s])

    def epilogue(s, r, acc):
        cols = slice(s * sub, (s + 1) * sub)
        slot = s % 2
        r0 = r * tr
        a3 = acc.reshape(nb, tr, sub)
        pad_ref[slot, :, CONV_PAD + r0:CONV_PAD + r0 + tr, :] = a3
        if r == 0:
            pad_ref[slot, :, CONV_PAD - (CONV_W - 1):CONV_PAD, :] = hist_ref[:, :, cols]
        if r == T // tr - 1:
            raw_ref[:, :, cols] = a3[:, tr - CONV_PAD:, :]
        y = cw_ref[CONV_W - 1:CONV_W, cols] * a3
        for j in range(CONV_W - 1):
            lo = CONV_PAD - (CONV_W - 1) + j + r0
            y = y + cw_ref[j:j + 1, cols] * pad_ref[slot, :, lo:lo + tr, :]
        y = _silu(y)
        if norm:
            heads = []
            for hh in range(sub // GDN_DK):
                yh = y[:, :, hh * GDN_DK:(hh + 1) * GDN_DK]
                heads.append(yh * (lax.rsqrt(jnp.sum(yh * yh, axis=-1, keepdims=True) + NORM_EPS) * scale))
            y = jnp.concatenate(heads, axis=-1)
        y2 = y.reshape(nb * tr, sub)
        y_ref[rows_of(r), cols] = y2.astype(y_ref.dtype)
        bits = pltpu.bitcast(y2[nb * tr - 16:, 0:128], jnp.uint32)
        zero = lax.shift_right_logical(lax.shift_right_logical(bits, jnp.uint32(16)), jnp.uint32(16))
        return zero.astype(F32).astype(BF16)

    gates = [None, None]
    acc = matmul(*units[0], None)
    for i, u in enumerate(units):
        nxt = matmul(*units[i + 1], gates[i + 1]) if i + 1 < len(units) else None
        gates.append(epilogue(*u, acc))
        acc = nxt


def _proj_conv(x, w, col_off, hist, conv_w, *, B, T, norm, scale=1.0, tn=512, sub=512, tr=512, name="proj_conv"):
    M, K = x.shape
    n_cols = GDN_WIDTH
    nb = max(1, 256 // T) if T < 256 else 1
    nb = min(nb, B)
    tm = nb * T
    gq = OFF_GQKV
    assert B % nb == 0 and n_cols % tn == 0 and col_off % 8 == 0 and (col_off - gq) % tn == 0 and T >= CONV_PAD
    hb = (col_off - gq) // tn
    return pl.pallas_call(
        functools.partial(_proj_conv_kernel, nb=nb, T=T, sub=sub, tr_max=tr, norm=norm, scale=scale),
        out_shape=(jax.ShapeDtypeStruct((M, n_cols), BF16), jax.ShapeDtypeStruct((B, CONV_PAD, n_cols), F32)),
        grid=(B // nb, n_cols // tn),
        in_specs=[pl.BlockSpec((tm, K), lambda i, j: (i, 0)),
                  _weight_spec(w, K, tn, lambda i, j: col_off + j * tn, True),
                  pl.BlockSpec((nb, CONV_W - 1, tn), lambda i, j: (i, 0, j + hb)),
                  pl.BlockSpec((CONV_W, tn), lambda i, j: (0, j + hb))],
        out_specs=(pl.BlockSpec((tm, tn), lambda i, j: (i, j)),
                   pl.BlockSpec((nb, CONV_PAD, tn), lambda i, j: (i, 0, j))),
        scratch_shapes=[pltpu.VMEM((min(2, tn // sub), nb, T + CONV_PAD, sub), F32)],
        compiler_params=_cparams(("parallel", "parallel")),
        name=name,
    )(x, w, hist, conv_w)


def _lam_value(lq1, lk1, lq2, lk2):
    a = jnp.sum(lq1 * lk1, axis=-1, keepdims=True)
    b = jnp.sum(lq2 * lk2, axis=-1, keepdims=True)
    return jnp.exp(a) - jnp.exp(b) + LAM_INIT


def _subln(o, g):
    ms = jnp.mean(o * o, axis=-1, keepdims=True)
    return o * lax.rsqrt(ms + NORM_EPS) * g * (1.0 - LAM_INIT)


def _head_slope(h):
    e = (h + 1).astype(F32) * (-8.0 / N_DIFF_HEADS)
    return jnp.exp2(jnp.full((1, 1), e, F32))


def _nt_dot(a, b):
    return lax.dot_general(a, b, (((1,), (1,)), ((), ())), preferred_element_type=F32)


def _lane_tile(x, n):
    return x if n == 1 else jnp.concatenate([x] * n, axis=1)


def _diff_prompt_kernel(lq1_ref, lk1_ref, lq2_ref, lk2_ref, g_ref, base_ref, q_ref, k_ref, v_ref, o_ref,
                        m_ref, l_ref, acc_ref, *, tq, hps):
    LANES = 128
    half = tq // 2
    hg = pl.program_id(1)
    qi = pl.program_id(2)
    slope = [_head_slope(hg * hps + hh) for hh in range(hps)]
    lam = _lam_value(lq1_ref[...], lk1_ref[...], lq2_ref[...], lk2_ref[...])
    col_iota = lax.broadcasted_iota(jnp.int32, (1, tq), 1).astype(F32)
    col_bias = [sl * col_iota for sl in slope]
    maps = range(2)

    m_ref[...] = jnp.full(m_ref.shape, -jnp.inf, F32)
    l_ref[...] = jnp.zeros(l_ref.shape, F32)
    acc_ref[...] = jnp.zeros(acc_ref.shape, F32)

    def cols(hh, m):
        c0 = hh * DIFF_VDIM + m * DIFF_HEAD_DIM
        return slice(c0, c0 + DIFF_HEAD_DIM)

    def update(*jobs):
        todo = [(job, m) for job in jobs for m in maps]
        t = [_nt_dot(q_ref[0, rows, cols(hh, m)], k_ref[0, pl.ds(start, nk), cols(hh, m)]) + bias
             for (hh, rows, start, nk, bias, _), m in todo]
        st = [2 * job[0] + m for job, m in todo]
        m_old = [m_ref[st[i], job[1]] for i, (job, m) in enumerate(todo)]
        m_new = [jnp.maximum(m_old[i], jnp.max(t[i], axis=-1, keepdims=True) + job[5])
                 for i, (job, m) in enumerate(todo)]
        a = [jnp.exp(m_old[i] - m_new[i]) for i in range(len(todo))]
        p = [jnp.exp(t[i] - _lane_tile(m_new[i] - job[5], job[3] // LANES)) for i, (job, m) in enumerate(todo)]
        pv = [jnp.dot(p[i].astype(BF16), v_ref[0, pl.ds(start, nk), hh * DIFF_VDIM:(hh + 1) * DIFF_VDIM],
                      preferred_element_type=F32) for i, ((hh, _, start, nk, _, _), m) in enumerate(todo)]
        for i, ((hh, rows, start, nk, _, _), m) in enumerate(todo):
            psum = p[i][:, :LANES]
            for c in range(1, nk // LANES):
                psum = psum + p[i][:, c * LANES:(c + 1) * LANES]
            l_ref[st[i], rows] = a[i] * l_ref[st[i], rows] + psum
            acc_ref[st[i], rows] = _lane_tile(a[i], DIFF_VDIM // LANES) * acc_ref[st[i], rows] + pv[i]
            m_ref[st[i], rows] = m_new[i]

    every = slice(0, tq)

    def body(j, carry):
        start = pl.multiple_of(j * tq, tq)
        gap = ((qi - j) * tq).astype(F32)
        update(*[(hh, every, start, tq, col_bias[hh], -slope[hh] * gap) for hh in range(hps)])
        return carry

    lax.fori_loop(0, qi, body, 0)

    lo, hi = slice(0, half), slice(half, tq)
    start = pl.multiple_of(qi * tq, tq)
    zero = jnp.zeros((1, 1), F32)
    jobs = []
    for hh in range(hps):
        jobs.append((hh, lo, start, half, slope[hh] * base_ref[:, half:], -slope[hh] * float(half)))
        jobs.append((hh, hi, start, tq, slope[hh] * base_ref[...], zero))
    update(*jobs)

    for hh in range(hps):
        outs = []
        for m in maps:
            inv = 1.0 / jnp.sum(l_ref[2 * hh + m], axis=-1, keepdims=True)
            outs.append(acc_ref[2 * hh + m] * inv)
        o = outs[0] - lam * outs[1]
        o_ref[0, :, hh * DIFF_VDIM:(hh + 1) * DIFF_VDIM] = _subln(o, g_ref[...]).astype(o_ref.dtype)


def _diff_attention_prompt(q, k, v, lam_rows, subln_g, *, tq=512, hps=2):
    B, T, _ = q.shape
    tq = min(tq, T)
    half = tq // 2
    assert T % tq == 0 and half % CHUNK == 0 and N_DIFF_HEADS % hps == 0
    ii = lax.broadcasted_iota(jnp.int32, (half, tq), 0)
    jj = lax.broadcasted_iota(jnp.int32, (half, tq), 1)
    jh = jj - half
    diag = jnp.where((jh // CHUNK) <= (ii // CHUNK), (half + ii - jnp.abs(ii - jh)).astype(F32), -jnp.inf)
    base = jnp.where(jj < half, jj.astype(F32), diag)
    vec = pl.BlockSpec((1, DIFF_HEAD_DIM), lambda b, h, i: (0, 0))
    wide = hps * DIFF_VDIM
    return pl.pallas_call(
        functools.partial(_diff_prompt_kernel, tq=tq, hps=hps),
        out_shape=jax.ShapeDtypeStruct((B, T, DIFF_WIDTH), BF16),
        grid=(B, N_DIFF_HEADS // hps, T // tq),
        in_specs=[vec, vec, vec, vec,
                  pl.BlockSpec((1, DIFF_VDIM), lambda b, h, i: (0, 0)),
                  pl.BlockSpec((half, tq), lambda b, h, i: (0, 0)),
                  pl.BlockSpec((1, tq, wide), lambda b, h, i: (b, i, h)),
                  pl.BlockSpec((1, T, wide), lambda b, h, i: (b, 0, h)),
                  pl.BlockSpec((1, T, wide), lambda b, h, i: (b, 0, h))],
        out_specs=pl.BlockSpec((1, tq, wide), lambda b, h, i: (b, i, h)),
        scratch_shapes=[pltpu.VMEM((2 * hps, tq, 128), F32), pltpu.VMEM((2 * hps, tq, 128), F32),
                        pltpu.VMEM((2 * hps, tq, DIFF_VDIM), F32)],
        compiler_params=_cparams(("parallel", "parallel", "parallel")),
        name="diff_attn_prompt",
    )(*lam_rows, subln_g, base, q, k, v)


def _diff_sample_kernel(lq1_ref, lk1_ref, lq2_ref, lk2_ref, g_ref, q_ref, kc_ref, vc_ref, kn_ref, vn_ref,
                        o_ref, *, past, pc):
    H = N_DIFF_HEADS
    T = q_ref.shape[1]
    L = H * T
    rows = pc * H
    n_chunks = past // pc
    lam = _lam_value(lq1_ref[...], lk1_ref[...], lq2_ref[...], lk2_ref[...])
    maps = range(2)

    def map_cols(m):
        return slice(m * DIFF_HEAD_DIM, (m + 1) * DIFF_HEAD_DIM)

    qm = [jnp.concatenate([q_ref[0, :, h * DIFF_VDIM + m * DIFF_HEAD_DIM:h * DIFF_VDIM + (m + 1) * DIFF_HEAD_DIM]
                           for h in range(H)], axis=0) for m in maps]
    lane = lax.broadcasted_iota(jnp.int32, (1, L), 1)
    h_lane = lane // T
    t_lane = lane % T
    slope = jnp.exp2((h_lane + 1).astype(F32) * (-8.0 / H))
    base = slope * (lax.broadcasted_iota(jnp.int32, (rows, L), 0) // H).astype(F32)
    valid = lax.broadcasted_iota(jnp.int32, (H, L), 0) == h_lane

    def chunk_shift(c):
        return slope * ((c * pc - past) - t_lane).astype(F32)

    zq = jnp.zeros((L, DIFF_HEAD_DIM), BF16)
    q_both = jnp.concatenate([jnp.concatenate([qm[0], zq], axis=1), jnp.concatenate([zq, qm[1]], axis=1)], axis=0)

    def both_maps(keys):
        s2 = _nt_dot(keys, q_both)
        return [s2[:, m * L:(m + 1) * L] for m in maps]

    def chunk_scores(c):
        r0 = c * rows if isinstance(c, int) else pl.multiple_of(c * rows, rows)
        s2 = both_maps(kc_ref[0, pl.ds(r0, rows), :].astype(BF16))
        return [(s2[m] + base).reshape(pc, H, L) for m in maps]

    shared = chunk_scores(0) if n_chunks == 1 else None

    def stats_step(c, carry):
        shift = chunk_shift(c)
        s = chunk_scores(c) if shared is None else shared
        out = []
        for m in maps:
            m_old, l_old = carry[2 * m], carry[2 * m + 1]
            m_new = jnp.maximum(m_old, jnp.max(s[m], axis=0) + shift)
            l_new = l_old * jnp.exp(m_old - m_new) + jnp.sum(jnp.exp(s[m] - (m_new - shift)[None]), axis=0)
            out += [m_new, l_new]
        return tuple(out)

    init = (jnp.full((H, L), -jnp.inf, F32), jnp.zeros((H, L), F32)) * 2
    stats = stats_step(0, init) if n_chunks == 1 else lax.fori_loop(0, n_chunks, stats_step, init)

    kn = kn_ref[0].astype(BF16)
    vn = vn_ref[0].astype(BF16)
    tk = lax.broadcasted_iota(jnp.int32, (T * H, L), 0) // H
    allowed = ((past + tk) // CHUNK) <= ((past + t_lane) // CHUNK)
    bias_n = jnp.where(allowed, -slope * jnp.abs(tk - t_lane).astype(F32), -jnp.inf)
    s_n = [(s + bias_n).reshape(T, H, L) for s in both_maps(kn)]
    m_fin, coef = [], []
    for m in maps:
        m_run, l_run = stats[2 * m], stats[2 * m + 1]
        mf = jnp.maximum(m_run, jnp.max(s_n[m], axis=0))
        lf = l_run * jnp.exp(m_run - mf) + jnp.sum(jnp.exp(s_n[m] - mf[None]), axis=0)
        m_fin.append(mf)
        coef.append(jnp.where(valid, (1.0 if m == 0 else -lam) / lf, 0.0))

    def tn_dot(w, v):
        return lax.dot_general(w, v, (((0,), (0,)), ((), ())), preferred_element_type=F32)

    def out_step(c, acc):
        shift = chunk_shift(c)
        s = chunk_scores(c) if shared is None else shared
        r0 = c * rows if isinstance(c, int) else pl.multiple_of(c * rows, rows)
        vblk = vc_ref[0, pl.ds(r0, rows), :].astype(BF16)
        w = sum(jnp.exp(s[m] - (m_fin[m] - shift)[None]) * coef[m][None] for m in maps)
        return acc + tn_dot(w.reshape(rows, L).astype(BF16), vblk)

    acc0 = jnp.zeros((L, DIFF_VDIM), F32)
    acc = out_step(0, acc0) if n_chunks == 1 else lax.fori_loop(0, n_chunks, out_step, acc0)
    w_n = sum(jnp.exp(s_n[m] - m_fin[m][None]) * coef[m][None] for m in maps)
    acc = acc + tn_dot(w_n.reshape(T * H, L).astype(BF16), vn)
    o = _subln(acc, g_ref[...]).astype(o_ref.dtype)
    for h in range(H):
        o_ref[0, :, h * DIFF_VDIM:(h + 1) * DIFF_VDIM] = o[h * T:(h + 1) * T]


def _diff_attention_sample(q, cache_k, cache_v, k_new, v_new, lam_rows, subln_g):
    B, T, _ = q.shape
    P = cache_k.shape[1]
    H = N_DIFF_HEADS
    pc = min(P, 1024)
    assert H * T == 128 and P > 0 and P % pc == 0
    vec = pl.BlockSpec((1, DIFF_HEAD_DIM), lambda b: (0, 0))
    tok = pl.BlockSpec((1, T, DIFF_WIDTH), lambda b: (b, 0, 0))
    new = pl.BlockSpec((1, T * H, DIFF_VDIM), lambda b: (b, 0, 0))
    old = pl.BlockSpec((1, P * H, DIFF_VDIM), lambda b: (b, 0, 0))
    flat = lambda a: a.reshape(B, a.shape[1] * H, DIFF_VDIM)
    return pl.pallas_call(
        functools.partial(_diff_sample_kernel, past=P, pc=pc),
        out_shape=jax.ShapeDtypeStruct((B, T, DIFF_WIDTH), BF16),
        grid=(B,),
        in_specs=[vec, vec, vec, vec, pl.BlockSpec((1, DIFF_VDIM), lambda b: (0, 0)),
                  tok, old, old, new, new],
        out_specs=tok,
        compiler_params=_cparams(("parallel",)),
        name="diff_attn_sample",
    )(*lam_rows, subln_g, q, flat(cache_k), flat(cache_v), flat(k_new), flat(v_new))


def _gdn_kernel(*refs, c, cps, has_state):
    if has_state:
        (xq_ref, xk_ref, xv_ref, ab_ref, alog_ref, dtb_ref, z_ref, ng_ref, s0_ref, o_ref, s_ref) = refs
    else:
        (xq_ref, xk_ref, xv_ref, ab_ref, alog_ref, dtb_ref, z_ref, ng_ref, o_ref, s_ref) = refs
        s0_ref = None
    n = pl.program_id(1)
    H = N_GDN_HEADS

    @pl.when(n == 0)
    def _init():
        if has_state:
            s_ref[0] = s0_ref[0]
        else:
            s_ref[...] = jnp.zeros(s_ref.shape, F32)

    ab = ab_ref[0]
    a_in = ab[:, 0:H] + dtb_ref[...]
    softplus = jnp.maximum(a_in, 0.0) + jnp.log1p(jnp.exp(-jnp.abs(a_in)))
    g_col = -jnp.exp(alog_ref[...]) * softplus
    beta_col = _sigmoid(ab[:, H:2 * H])
    r = lax.broadcasted_iota(jnp.int32, (c, c), 0)
    s = lax.broadcasted_iota(jnp.int32, (c, c), 1)
    tri = r >= s
    strict = r > s
    tri_f = tri.astype(F32)
    eye_c = (r == s).astype(F32)
    eye_h = (lax.broadcasted_iota(jnp.int32, (H, H), 0) == lax.broadcasted_iota(jnp.int32, (H, H), 1)).astype(F32)

    def bf(a):
        return a.astype(BF16)

    def split(a):
        hi = a.astype(BF16)
        return hi, (a - hi.astype(F32)).astype(BF16)

    def dot3(a, b):
        (ah, al), (bh, bl) = a, b
        return (jnp.dot(al, bh, preferred_element_type=F32) + jnp.dot(ah, bl, preferred_element_type=F32)
                + jnp.dot(ah, bh, preferred_element_type=F32))

    blocks = []
    b_ = 1
    while b_ < c:
        blocks.append((((r // (2 * b_)) == (s // (2 * b_))) & ((r // b_) != (s // b_)) & strict).astype(F32))
        b_ *= 2

    rows = [slice(ci * c, (ci + 1) * c) for ci in range(cps)]
    lanes = [slice(hh * GDN_DK, (hh + 1) * GDN_DK) for hh in range(H)]
    gc_col = [jnp.dot(tri_f, g_col[rows[ci]], preferred_element_type=F32, precision=HI) for ci in range(cps)]
    gc_row = [lax.dot_general(eye_h, gc_col[ci], (((1,), (1,)), ((), ())), preferred_element_type=F32,
                              precision=HI) for ci in range(cps)]
    items = [(ci, hh) for ci in range(cps) for hh in range(H)]
    every = range(len(items))
    gcc = [gc_col[ci][:, hh:hh + 1] for ci, hh in items]
    bet = [beta_col[rows[ci], hh:hh + 1] for ci, hh in items]
    gcr = [gc_row[ci][hh:hh + 1, :] for ci, hh in items]
    g_last = [g[c - 1:c, :] for g in gcc]

    q = [xq_ref[0, rows[ci], lanes[hh]].astype(F32) for ci, hh in items]
    k = [xk_ref[0, rows[ci], lanes[hh]].astype(F32) for ci, hh in items]
    v = [xv_ref[0, rows[ci], lanes[hh]].astype(F32) for ci, hh in items]

    decay = [jnp.exp(jnp.where(tri, gcc[i] - gcr[i], -jnp.inf)) for i in every]
    kb = [k[i] * bet[i] for i in every]
    k16 = [bf(k[i]) for i in every]
    mmat = [jnp.where(strict, _nt_dot(bf(kb[i]), k16[i]) * decay[i], 0.0) for i in every]
    qk = [jnp.where(tri, _nt_dot(bf(q[i]), k16[i]) * decay[i], 0.0) for i in every]

    tinv = [eye_c - mmat[i] * blocks[0] for i in every]
    for lvl in range(1, len(blocks)):
        d16 = [bf(t) for t in tinv]
        x = [jnp.dot(bf(mmat[i] * blocks[lvl]), d16[i], preferred_element_type=F32) for i in every]
        tinv = [tinv[i] - jnp.dot(d16[i], bf(x[i]), preferred_element_type=F32) for i in every]
    m_s = [split(m) for m in mmat]
    t_s = [split(t) for t in tinv]
    res = [(eye_c - tinv[i]) - dot3(m_s[i], t_s[i]) for i in every]
    tinv = [tinv[i] + jnp.dot(t_s[i][0], bf(res[i]), preferred_element_type=F32) for i in every]

    rhs = [jnp.concatenate([v[i] * bet[i], kb[i] * jnp.exp(gcc[i])], axis=1) for i in every]
    sol = [dot3(split(tinv[i]), split(rhs[i])) for i in every]
    u = [x_[:, :GDN_DV] for x_ in sol]
    w16 = [bf(x_[:, GDN_DV:]) for x_ in sol]
    qg16 = [bf(q[i] * jnp.exp(gcc[i])) for i in every]
    qk16 = [bf(x_) for x_ in qk]
    kd16 = [bf(k[i] * jnp.exp(g_last[i] - gcc[i])) for i in every]
    e_last = [jnp.exp(g) for g in g_last]

    S = [s_ref[0, hh] for hh in range(H)]
    for ci in range(cps):
        of = ci * H
        S16 = [bf(x_) for x_ in S]
        v_new = [u[of + hh] - jnp.dot(w16[of + hh], S16[hh], preferred_element_type=F32) for hh in range(H)]
        v16 = [bf(x_) for x_ in v_new]
        o = [jnp.dot(qg16[of + hh], S16[hh], preferred_element_type=F32)
             + jnp.dot(qk16[of + hh], v16[hh], preferred_element_type=F32) for hh in range(H)]
        S = [S[hh] * e_last[of + hh] + lax.dot_general(kd16[of + hh], v16[hh], (((0,), (0,)), ((), ())),
                                                       preferred_element_type=F32) for hh in range(H)]
        for hh in range(H):
            ms = jnp.mean(o[hh] * o[hh], axis=-1, keepdims=True)
            y = o[hh] * lax.rsqrt(ms + NORM_EPS) * ng_ref[...] * z_ref[0, rows[ci], lanes[hh]].astype(F32)
            o_ref[0, rows[ci], lanes[hh]] = y.astype(o_ref.dtype)
    for hh in range(H):
        s_ref[0, hh] = S[hh]


def _gdn(q, k, v, gab, a_log, dt_bias, z_silu, norm_g, s0, *, c):
    B, T, _ = q.shape
    H = N_GDN_HEADS
    assert T % c == 0 and c % 8 == 0 and (c & (c - 1)) == 0
    cps = 2 if (T // c) % 2 == 0 else 1
    R = cps * c
    stream = pl.BlockSpec((1, R, GDN_WIDTH), lambda b, n: (b, n, 0))
    small = pl.BlockSpec((1, H), lambda b, n: (0, 0))
    state = pl.BlockSpec((1, H, GDN_DK, GDN_DV), lambda b, n: (b, 0, 0, 0))
    in_specs = [stream, stream, stream,
                pl.BlockSpec((1, R, 128), lambda b, n: (b, n, 0)), small, small,
                stream, pl.BlockSpec((1, GDN_DV), lambda b, n: (0, 0))]
    args = [q, k, v, gab, a_log, dt_bias, z_silu, norm_g]
    if s0 is not None:
        in_specs.append(state)
        args.append(s0)
    return pl.pallas_call(
        functools.partial(_gdn_kernel, c=c, cps=cps, has_state=s0 is not None),
        out_shape=(jax.ShapeDtypeStruct((B, T, GDN_WIDTH), BF16),
                   jax.ShapeDtypeStruct((B, H, GDN_DK, GDN_DV), F32)),
        grid=(B, T // R),
        in_specs=in_specs,
        out_specs=(stream, state),
        compiler_params=_cparams(("parallel", "arbitrary")),
        name="gdn",
    )(*args)


def _mix_kernel(oa_ref, ob_ref, wa_ref, wb_ref, sa_ref, sb_ref, o_ref):
    a = jnp.dot(oa_ref[...], wa_ref[0].astype(BF16), preferred_element_type=F32)
    b = jnp.dot(ob_ref[...], wb_ref[0].astype(BF16), preferred_element_type=F32)
    o_ref[...] = (sa_ref[...].astype(F32) * a + sb_ref[...].astype(F32) * b).astype(o_ref.dtype)


def _mix(o_a, o_b, w_pa, w_pb, gates, *, tm=1024, tn=512):
    M = o_a.shape[0]
    tm = min(tm, M)
    nb = D_MODEL // tn
    assert w_pa.shape[0] == w_pb.shape[0] == DEPTH == 1 and M % tm == 0
    return pl.pallas_call(
        _mix_kernel,
        out_shape=jax.ShapeDtypeStruct((M, D_MODEL), BF16),
        grid=(nb, M // tm),
        in_specs=[pl.BlockSpec((tm, DIFF_WIDTH), lambda j, i: (i, 0)),
                  pl.BlockSpec((tm, GDN_WIDTH), lambda j, i: (i, 0)),
                  pl.BlockSpec((1, DIFF_WIDTH, tn), lambda j, i: (0, 0, j)),
                  pl.BlockSpec((1, GDN_WIDTH, tn), lambda j, i: (0, 0, j)),
                  pl.BlockSpec((tm, tn), lambda j, i: (i, j)),
                  pl.BlockSpec((tm, tn), lambda j, i: (i, j + nb))],
        out_specs=pl.BlockSpec((tm, tn), lambda j, i: (i, j)),
        compiler_params=_cparams(("parallel", "parallel")),
        name="mix",
    )(o_a, o_b, w_pa, w_pb, gates, gates)


def _post_kernel(x_ref, mix_ref, wo_ref, g1_ref, b1_ref, wxq_ref, mk_ref, mv_ref, wxo_ref, g2_ref, b2_ref,
                 h2_ref, h2b_ref):
    nb, tm, D = x_ref.shape
    x = x_ref[...].reshape(nb * tm, D)
    mix = mix_ref[...].reshape(nb * tm, D)
    h1 = ALPHA * x + jnp.dot(mix, wo_ref[...], preferred_element_type=F32)
    h1 = _layer_norm(h1, g1_ref[...], b1_ref[...])
    qx = jnp.dot(h1.astype(BF16), wxq_ref[...], preferred_element_type=F32) * (XHEAD_DIM ** -0.5)
    qx = qx.astype(BF16)
    seqs = []
    for b in range(nb):
        rows = slice(b * tm, (b + 1) * tm)
        heads = []
        for hh in range(N_XHEADS):
            sl = slice(hh * XHEAD_DIM, (hh + 1) * XHEAD_DIM)
            s = _nt_dot(qx[rows, sl], mk_ref[b, :, sl])
            p = jnp.exp(s - jnp.max(s, axis=-1, keepdims=True))
            p = p / jnp.sum(p, axis=-1, keepdims=True)
            heads.append(jnp.dot(p.astype(BF16), mv_ref[b, :, sl], preferred_element_type=F32))
        seqs.append(jnp.concatenate(heads, axis=1))
    ox = jnp.concatenate(seqs, axis=0).astype(BF16)
    h2 = ALPHA * h1 + jnp.dot(ox, wxo_ref[...], preferred_element_type=F32)
    h2 = _layer_norm(h2, g2_ref[...], b2_ref[...])
    h2_ref[...] = h2.reshape(nb, tm, D)
    h2b_ref[...] = h2.astype(BF16).reshape(nb, tm, D)


def _post(x, mix, w_o, ln1_g, ln1_b, w_xq, mem_k, mem_v, w_xo, ln2_g, ln2_b, *, tm=512):
    B, T, D = x.shape
    tm = min(tm, T)
    nb = min(B, max(1, 256 // tm))
    assert B % nb == 0 and T % tm == 0 and tm % 16 == 0
    const = lambda shape: pl.BlockSpec(shape, lambda b, i: (0, 0), pipeline_mode=pl.Buffered(1))
    rows = lambda: pl.BlockSpec((nb, tm, D), lambda b, i: (b, i, 0))
    mem = lambda: pl.BlockSpec((nb, N_MEM, XWIDTH), lambda b, i: (b, 0, 0))
    return pl.pallas_call(
        _post_kernel,
        out_shape=(jax.ShapeDtypeStruct((B, T, D), F32), jax.ShapeDtypeStruct((B, T, D), BF16)),
        grid=(B // nb, T // tm),
        in_specs=[rows(), rows(), const((D, D)), const((1, D)), const((1, D)), const((D, XWIDTH)),
                  mem(), mem(), const((XWIDTH, D)), const((1, D)), const((1, D))],
        out_specs=(rows(), rows()),
        compiler_params=_cparams(("parallel", "parallel")),
        name="post_attn",
    )(x, mix, w_o, ln1_g, ln1_b, w_xq, mem_k, mem_v, w_xo, ln2_g, ln2_b)


def _ffn_kernel(hb_ref, h_ref, w1_ref, w3_ref, w2_ref, g_ref, b_ref, y_ref, acc_ref):
    f = pl.program_id(1)

    @pl.when(f == 0)
    def _():
        acc_ref[...] = jnp.zeros(acc_ref.shape, F32)

    hb = hb_ref[...]
    a = jnp.dot(hb, w1_ref[...], preferred_element_type=F32)
    b = jnp.dot(hb, w3_ref[...], preferred_element_type=F32)
    act = (_silu(a) * b).astype(BF16)
    acc_ref[...] += jnp.dot(act, w2_ref[...], preferred_element_type=F32)

    @pl.when(f == pl.num_programs(1) - 1)
    def _():
        y_ref[...] = _layer_norm(ALPHA * h_ref[...] + acc_ref[...], g_ref[...], b_ref[...])


def _ffn(h2b, h2, w1, w3, w2, ln_g, ln_b, *, tm=512, tf=512):
    M, D = h2.shape
    tm = min(tm, M)
    assert M % tm == 0 and D_FF % tf == 0
    return pl.pallas_call(
        _ffn_kernel,
        out_shape=jax.ShapeDtypeStruct((M, D), F32),
        grid=(M // tm, D_FF // tf),
        in_specs=[pl.BlockSpec((tm, D), lambda i, f: (i, 0)),
                  pl.BlockSpec((tm, D), lambda i, f: (i, 0)),
                  pl.BlockSpec((D, tf), lambda i, f: (0, f)),
                  pl.BlockSpec((D, tf), lambda i, f: (0, f)),
                  pl.BlockSpec((tf, D), lambda i, f: (f, 0)),
                  pl.BlockSpec((1, D), lambda i, f: (0, 0)),
                  pl.BlockSpec((1, D), lambda i, f: (0, 0))],
        out_specs=pl.BlockSpec((tm, D), lambda i, f: (i, 0)),
        scratch_shapes=[pltpu.VMEM((tm, D), F32)],
        compiler_params=_cparams(("parallel", "arbitrary")),
        name="ffn",
    )(h2b, h2, w1, w3, w2, ln_g, ln_b)


def _encoder_layer(x, mem_k, mem_v, past, W):
    B, T, D = x.shape
    M = B * T
    w_in = W["w_in_t"]
    xb, dq = _proj(x.reshape(M, D), w_in, OFF_DQ, DIFF_WIDTH, (BF16,), scale=DIFF_HEAD_DIM ** -0.5, w_rows=True,
                   emit_x=True, name="proj_dq")
    in_proj = functools.partial(_proj, xb, w_in, w_rows=True)
    kv_dtypes = (F32, BF16) if past is None else (F32,)
    dk, *dkb = in_proj(OFF_DK, DIFF_WIDTH, kv_dtypes, rows_inner=True, name="proj_dk")
    dv, *dvb = in_proj(OFF_DV, DIFF_WIDTH, kv_dtypes, rows_inner=True, name="proj_dv")
    buf0 = jnp.zeros((B, CONV_W - 1, 3 * GDN_WIDTH), F32) if past is None else past[3]
    streams, tails = [], []
    for t, (norm, scale) in enumerate(((True, GDN_DK ** -0.5), (True, 1.0), (False, 1.0))):
        y, tail = _proj_conv(xb, w_in, OFF_GQKV + t * GDN_WIDTH, buf0, W["conv_w"], B=B, T=T, norm=norm,
                             scale=scale, name="proj_conv_" + "qkv"[t])
        streams.append(y.reshape(B, T, GDN_WIDTH))
        tails.append(tail[:, CONV_PAD - (CONV_W - 1):, :])
    (gz,) = in_proj(OFF_GZ, GDN_WIDTH, (BF16,), act="silu", name="proj_gz")
    (gab,) = in_proj(OFF_GAB, 128, (F32,), tn=128, name="proj_gab")
    (gates,) = in_proj(OFF_GATES, 2 * D_MODEL, (BF16,), act="sigmoid", name="proj_gates")

    dq = dq.reshape(B, T, DIFF_WIDTH)
    dk = dk.reshape(B, T, N_DIFF_HEADS, DIFF_VDIM)
    dv = dv.reshape(B, T, N_DIFF_HEADS, DIFF_VDIM)
    lam_rows = W["lam_rows"]
    if past is None:
        o_a = _diff_attention_prompt(dq, dkb[0].reshape(B, T, DIFF_WIDTH), dvb[0].reshape(B, T, DIFF_WIDTH),
                                     lam_rows, W["diff_subln_g"])
        s0 = None
        c = CHUNK
    else:
        cache_k, cache_v, s0, _ = past
        o_a = _diff_attention_sample(dq, cache_k, cache_v, dk, dv, lam_rows, W["diff_subln_g"])
        c = T
    o_b, s_new = _gdn(*streams, gab.reshape(B, T, 128), W["gdn_a_log"], W["gdn_dt_bias"],
                      gz.reshape(B, T, GDN_WIDTH), W["gdn_norm_g"], s0, c=c)
    new_buf = jnp.concatenate(tails, axis=-1)

    mix = _mix(o_a.reshape(M, DIFF_WIDTH), o_b.reshape(M, GDN_WIDTH), W["w_pa"], W["w_pb"], gates)
    h2, h2b = _post(x, mix.reshape(B, T, D), W["w_o"], W["ln1_g"], W["ln1_b"], W["w_xq"], mem_k, mem_v,
                    W["w_xo"], W["ln2_g"], W["ln2_b"])
    y = _ffn(h2b.reshape(M, D), h2.reshape(M, D), W["w_ff1"], W["w_ff3"], W["w_ff2"], W["ln3_g"], W["ln3_b"])
    return y.reshape(B, T, D), dk, dv, s_new, new_buf


def kernel(x_prompt, x_sample, mem_prompt, cache_diff_k, cache_diff_v, state_gdn, state_gdn_conv, cache_mem_k, cache_mem_v, w_in, conv_w, lam_q1, lam_k1, lam_q2, lam_k2, diff_subln_g, gdn_a_log, gdn_dt_bias, gdn_norm_g, w_pa, w_pb, w_o, ln1_g, ln1_b, w_xq, w_xk, w_xv, w_xo, ln2_g, ln2_b, w_ff1, w_ff3, w_ff2, ln3_g, ln3_b):
    l = 0
    W = {
        "w_in_t": jnp.swapaxes(w_in, 1, 2),
        "conv_w": conv_w[l],
        "lam_rows": tuple(v[l].reshape(1, DIFF_HEAD_DIM) for v in (lam_q1, lam_k1, lam_q2, lam_k2)),
        "diff_subln_g": diff_subln_g[l].reshape(1, DIFF_VDIM),
        "gdn_a_log": gdn_a_log[l].reshape(1, N_GDN_HEADS),
        "gdn_dt_bias": gdn_dt_bias[l].reshape(1, N_GDN_HEADS),
        "gdn_norm_g": gdn_norm_g[l].reshape(1, GDN_DV),
        "w_pa": w_pa, "w_pb": w_pb, "w_o": w_o[l].astype(BF16),
        "ln1_g": ln1_g[l].reshape(1, D_MODEL), "ln1_b": ln1_b[l].reshape(1, D_MODEL),
        "w_xq": w_xq[l].astype(BF16), "w_xo": w_xo[l].astype(BF16),
        "ln2_g": ln2_g[l].reshape(1, D_MODEL), "ln2_b": ln2_b[l].reshape(1, D_MODEL),
        "w_ff1": w_ff1[l].astype(BF16), "w_ff3": w_ff3[l].astype(BF16), "w_ff2": w_ff2[l].astype(BF16),
        "ln3_g": ln3_g[l].reshape(1, D_MODEL), "ln3_b": ln3_b[l].reshape(1, D_MODEL),
    }
    Bp = x_prompt.shape[0]
    memb = mem_prompt.reshape(Bp * N_MEM, D_MODEL).astype(BF16)
    mem_k, mem_kb = _proj(memb, w_xk, 0, XWIDTH, (F32, BF16), tn=XWIDTH, name="proj_mem_k")
    mem_v, mem_vb = _proj(memb, w_xv, 0, XWIDTH, (F32, BF16), tn=XWIDTH, name="proj_mem_v")

    yp, pk, pv, ps, pc = _encoder_layer(x_prompt, mem_kb.reshape(Bp, N_MEM, XWIDTH),
                                        mem_vb.reshape(Bp, N_MEM, XWIDTH), None, W)
    Bs = x_sample.shape[0]
    past = (cache_diff_k[l], cache_diff_v[l], state_gdn[l], state_gdn_conv[l])
    ys, sk, sv, ss, sc = _encoder_layer(x_sample, cache_mem_k[l].reshape(Bs, N_MEM, XWIDTH).astype(BF16),
                                        cache_mem_v[l].reshape(Bs, N_MEM, XWIDTH).astype(BF16), past, W)
    st = lambda a: a[None]
    return (yp, ys, st(pk), st(pv), st(ps), st(pc),
            st(mem_k.reshape(Bp, N_MEM, N_XHEADS, XHEAD_DIM)), st(mem_v.reshape(Bp, N_MEM, N_XHEADS, XHEAD_DIM)),
            st(sk), st(sv), st(ss), st(sc))
```

```python
import functools
import math

import jax
import jax.numpy as jnp
from jax import lax
from jax.experimental import pallas as pl
from jax.experimental.pallas import tpu as pltpu

D_MODEL = 2048
CHUNK = 64
N_DIFF_HEADS = 8
DIFF_HEAD_DIM = 128
DIFF_VDIM = 2 * DIFF_HEAD_DIM
DIFF_WIDTH = N_DIFF_HEADS * DIFF_VDIM
N_GDN_HEADS = 16
GDN_DK = 128
GDN_DV = 128
GDN_WIDTH = N_GDN_HEADS * GDN_DK
CONV_W = 4
N_MEM = 256
N_XHEADS = 4
XHEAD_DIM = 128
XWIDTH = N_XHEADS * XHEAD_DIM
D_FF = 5632
DEPTH = 1
ALPHA = (2.0 * DEPTH) ** 0.25
LN_EPS = 1e-5
NORM_EPS = 1e-6
LAM_INIT = 0.8 - 0.6 * math.exp(-0.3 * 0)

OFF_DQ = 0
OFF_DK = 2048
OFF_DV = 4096
OFF_GQKV = 6144
OFF_GZ = 12288
OFF_GAB = 14336
OFF_GATES = 14368

VMEM_LIMIT = 56 * 1024 * 1024
BF16 = jnp.bfloat16
F32 = jnp.float32
HI = lax.Precision.HIGHEST


def _cparams(sem):
    return pltpu.CompilerParams(dimension_semantics=sem, vmem_limit_bytes=VMEM_LIMIT)


def _sigmoid(x):
    return 1.0 / (1.0 + jnp.exp(-x))


def _silu(x):
    return x * _sigmoid(x)


def _layer_norm(x, g, b):
    mu = jnp.mean(x, axis=-1, keepdims=True)
    xc = x - mu
    var = jnp.mean(xc * xc, axis=-1, keepdims=True)
    return xc * lax.rsqrt(var + LN_EPS) * g + b


def _proj_kernel(x_ref, w_ref, *o_refs, act, scale, w_rows, emit_x):
    w = w_ref[0].astype(BF16)
    x = x_ref[...].astype(BF16)
    if emit_x:
        o_refs[0][...] = x
        o_refs = o_refs[1:]
    acc = _nt_dot(x, w) if w_rows else jnp.dot(x, w, preferred_element_type=F32)
    if scale != 1.0:
        acc = acc * scale
    if act == "sigmoid":
        acc = _sigmoid(acc)
    elif act == "silu":
        acc = _silu(acc)
    for o in o_refs:
        o[...] = acc.astype(o.dtype)


def _weight_spec(w, K, tn, col_start, w_rows):
    assert w.shape[0] == DEPTH == 1
    if w_rows:
        return pl.BlockSpec((pl.Element(1), pl.Element(tn), pl.Element(K)),
                            lambda a, b: (0, pl.multiple_of(col_start(a, b), 8), 0))
    return pl.BlockSpec((1, K, tn), lambda a, b: (0, 0, col_start(a, b) // tn))


def _proj(x, w, col_off, n_cols, out_dtypes, *, act=None, scale=1.0, tm=1024, tn=1024, rows_inner=False,
          w_rows=False, emit_x=False, name="proj"):
    M, K = x.shape
    tm = min(tm, M)
    tn = min(tn, n_cols)
    assert M % tm == 0 and n_cols % tn == 0 and col_off % (8 if w_rows else tn) == 0
    if rows_inner:
        grid = (n_cols // tn, M // tm)
        ij = lambda a, b: (b, a)
    else:
        grid = (M // tm, n_cols // tn)
        ij = lambda a, b: (a, b)
    x_spec = pl.BlockSpec((tm, K), lambda a, b: (ij(a, b)[0], 0))
    out_shape = [jax.ShapeDtypeStruct((M, n_cols), dt) for dt in out_dtypes]
    out_specs = [pl.BlockSpec((tm, tn), lambda a, b: ij(a, b)) for _ in out_dtypes]
    if emit_x:
        assert not rows_inner
        out_shape.insert(0, jax.ShapeDtypeStruct((M, K), BF16))
        out_specs.insert(0, x_spec)
    outs = pl.pallas_call(
        functools.partial(_proj_kernel, act=act, scale=scale, w_rows=w_rows, emit_x=emit_x),
        out_shape=tuple(out_shape),
        grid=grid,
        in_specs=[x_spec, _weight_spec(w, K, tn, lambda a, b: col_off + ij(a, b)[1] * tn, w_rows)],
        out_specs=tuple(out_specs),
        compiler_params=_cparams(("parallel", "arbitrary") if emit_x else ("parallel", "parallel")),
        name=name,
    )(x, w)
    return outs


CONV_PAD = 8


def _proj_conv_kernel(x_ref, w_ref, hist_ref, cw_ref, y_ref, raw_ref, pad_ref, *, nb, T, sub, tr_max, norm, scale):
    tn = w_ref.shape[1]
    tr = min(T, tr_max) if nb == 1 else T
    units = [(s, r) for s in range(tn // sub) for r in range(T // tr)]
    w_slabs = [w_ref[0, s * sub:(s + 1) * sub, :].astype(BF16) for s in range(tn // sub)]

    def rows_of(r):
        return slice(r * tr, (r + 1) * tr) if nb == 1 else slice(None)

    def matmul(s, r, gate):
        rows = rows_of(r)
        if gate is None:
            lhs = x_ref[rows, :]
        else:
            n_rows = nb * tr
            first = x_ref[rows, 0:256] + jnp.concatenate([jnp.concatenate([gate] * 2, axis=1)] * (n_rows // 16), axis=0)
            lhs = jnp.concatenate([first, x_ref[rows, 256:]], axis=1)
        return _nt_dot(lhs, w_slabs[s])

    def epilogue(s, r, acc):
        cols = slice(s * sub, (s + 1) * sub)
        slot = s % 2
        r0 = r * tr
        a3 = acc.reshape(nb, tr, sub)
        pad_ref[slot, :, CONV_PAD + r0:CONV_PAD + r0 + tr, :] = a3
        if r == 0:
            pad_ref[slot, :, CONV_PAD - (CONV_W - 1):CONV_PAD, :] = hist_ref[:, :, cols]
        if r == T // tr - 1:
            raw_ref[:, :, cols] = a3[:, tr - CONV_PAD:, :]
        y = cw_ref[CONV_W - 1:CONV_W, cols] * a3
        for j in range(CONV_W - 1):
            lo = CONV_PAD - (CONV_W - 1) + j + r0
            y = y + cw_ref[j:j + 1, cols] * pad_ref[slot, :, lo:lo + tr, :]
        y = _silu(y)
        if norm:
            heads = []
            for hh in range(sub // GDN_DK):
                yh = y[:, :, hh * GDN_DK:(hh + 1) * GDN_DK]
                heads.append(yh * (lax.rsqrt(jnp.sum(yh * yh, axis=-1, keepdims=True) + NORM_EPS) * scale))
            y = jnp.concatenate(heads, axis=-1)
        y2 = y.reshape(nb * tr, sub)
        y_ref[rows_of(r), cols] = y2.astype(y_ref.dtype)
        bits = pltpu.bitcast(y2[nb * tr - 16:, 0:128], jnp.uint32)
        zero = lax.shift_right_logical(lax.shift_right_logical(bits, jnp.uint32(16)), jnp.uint32(16))
        return zero.astype(F32).astype(BF16)

    gates = [None, None]
    acc = matmul(*units[0], None)
    for i, u in enumerate(units):
        nxt = matmul(*units[i + 1], gates[i + 1]) if i + 1 < len(units) else None
        gates.append(epilogue(*u, acc))
        acc = nxt


def _proj_conv(x, w, col_off, hist, conv_w, *, B, T, norm, scale=1.0, tn=512, sub=512, tr=512, name="proj_conv"):
    M, K = x.shape
    n_cols = GDN_WIDTH
    nb = max(1, 256 // T) if T < 256 else 1
    nb = min(nb, B)
    tm = nb * T
    gq = OFF_GQKV
    assert B % nb == 0 and n_cols % tn == 0 and col_off % 8 == 0 and (col_off - gq) % tn == 0 and T >= CONV_PAD
    hb = (col_off - gq) // tn
    return pl.pallas_call(
        functools.partial(_proj_conv_kernel, nb=nb, T=T, sub=sub, tr_max=tr, norm=norm, scale=scale),
        out_shape=(jax.ShapeDtypeStruct((M, n_cols), BF16), jax.ShapeDtypeStruct((B, CONV_PAD, n_cols), F32)),
        grid=(B // nb, n_cols // tn),
        in_specs=[pl.BlockSpec((tm, K), lambda i, j: (i, 0)),
                  _weight_spec(w, K, tn, lambda i, j: col_off + j * tn, True),
                  pl.BlockSpec((nb, CONV_W - 1, tn), lambda i, j: (i, 0, j + hb)),
                  pl.BlockSpec((CONV_W, tn), lambda i, j: (0, j + hb))],
        out_specs=(pl.BlockSpec((tm, tn), lambda i, j: (i, j)),
                   pl.BlockSpec((nb, CONV_PAD, tn), lambda i, j: (i, 0, j))),
        scratch_shapes=[pltpu.VMEM((min(2, tn // sub), nb, T + CONV_PAD, sub), F32)],
        compiler_params=_cparams(("parallel", "parallel")),
        name=name,
    )(x, w, hist, conv_w)


def _lam_value(lq1, lk1, lq2, lk2):
    a = jnp.sum(lq1 * lk1, axis=-1, keepdims=True)
    b = jnp.sum(lq2 * lk2, axis=-1, keepdims=True)
    return jnp.exp(a) - jnp.exp(b) + LAM_INIT


def _subln(o, g):
    ms = jnp.mean(o * o, axis=-1, keepdims=True)
    return o * lax.rsqrt(ms + NORM_EPS) * g * (1.0 - LAM_INIT)


def _head_slope(h):
    e = (h + 1).astype(F32) * (-8.0 / N_DIFF_HEADS)
    return jnp.exp2(jnp.full((1, 1), e, F32))


def _nt_dot(a, b):
    return lax.dot_general(a, b, (((1,), (1,)), ((), ())), preferred_element_type=F32)


def _lane_tile(x, n):
    return x if n == 1 else jnp.concatenate([x] * n, axis=1)


def _diff_prompt_kernel(lq1_ref, lk1_ref, lq2_ref, lk2_ref, g_ref, base_ref, q_ref, k_ref, v_ref, o_ref,
                        m_ref, l_ref, acc_ref, *, tq, hps):
    LANES = 128
    half = tq // 2
    hg = pl.program_id(1)
    qi = pl.program_id(2)
    slope = [_head_slope(hg * hps + hh) for hh in range(hps)]
    lam = _lam_value(lq1_ref[...], lk1_ref[...], lq2_ref[...], lk2_ref[...])
    col_iota = lax.broadcasted_iota(jnp.int32, (1, tq), 1).astype(F32)
    col_bias = [sl * col_iota for sl in slope]
    maps = range(2)

    m_ref[...] = jnp.full(m_ref.shape, -jnp.inf, F32)
    l_ref[...] = jnp.zeros(l_ref.shape, F32)
    acc_ref[...] = jnp.zeros(acc_ref.shape, F32)

    def cols(hh, m):
        c0 = hh * DIFF_VDIM + m * DIFF_HEAD_DIM
        return slice(c0, c0 + DIFF_HEAD_DIM)

    def update(*jobs):
        todo = [(job, m) for job in jobs for m in maps]
        t = [_nt_dot(q_ref[0, rows, cols(hh, m)], k_ref[0, pl.ds(start, nk), cols(hh, m)]) + bias
             for (hh, rows, start, nk, bias, _), m in todo]
        st = [2 * job[0] + m for job, m in todo]
        m_old = [m_ref[st[i], job[1]] for i, (job, m) in enumerate(todo)]
        m_new = [jnp.maximum(m_old[i], jnp.max(t[i], axis=-1, keepdims=True) + job[5])
                 for i, (job, m) in enumerate(todo)]
        a = [jnp.exp(m_old[i] - m_new[i]) for i in range(len(todo))]
        p = [jnp.exp(t[i] - _lane_tile(m_new[i] - job[5], job[3] // LANES)) for i, (job, m) in enumerate(todo)]
        pv = [jnp.dot(p[i].astype(BF16), v_ref[0, pl.ds(start, nk), hh * DIFF_VDIM:(hh + 1) * DIFF_VDIM],
                      preferred_element_type=F32) for i, ((hh, _, start, nk, _, _), m) in enumerate(todo)]
        for i, ((hh, rows, start, nk, _, _), m) in enumerate(todo):
            psum = p[i][:, :LANES]
            for c in range(1, nk // LANES):
                psum = psum + p[i][:, c * LANES:(c + 1) * LANES]
            l_ref[st[i], rows] = a[i] * l_ref[st[i], rows] + psum
            acc_ref[st[i], rows] = _lane_tile(a[i], DIFF_VDIM // LANES) * acc_ref[st[i], rows] + pv[i]
            m_ref[st[i], rows] = m_new[i]

    every = slice(0, tq)

    def body(j, carry):
        start = pl.multiple_of(j * tq, tq)
        gap = ((qi - j) * tq).astype(F32)
        update(*[(hh, every, start, tq, col_bias[hh], -slope[hh] * gap) for hh in range(hps)])
        return carry

    lax.fori_loop(0, qi, body, 0)

    lo, hi = slice(0, half), slice(half, tq)
    start = pl.multiple_of(qi * tq, tq)
    zero = jnp.zeros((1, 1), F32)
    jobs = []
    for hh in range(hps):
        jobs.append((hh, lo, start, half, slope[hh] * base_ref[:, half:], -slope[hh] * float(half)))
        jobs.append((hh, hi, start, tq, slope[hh] * base_ref[...], zero))
    update(*jobs)

    for hh in range(hps):
        outs = []
        for m in maps:
            inv = 1.0 / jnp.sum(l_ref[2 * hh + m], axis=-1, keepdims=True)
            outs.append(acc_ref[2 * hh + m] * inv)
        o = outs[0] - lam * outs[1]
        o_ref[0, :, hh * DIFF_VDIM:(hh + 1) * DIFF_VDIM] = _subln(o, g_ref[...]).astype(o_ref.dtype)


def _diff_attention_prompt(q, k, v, lam_rows, subln_g, *, tq=512, hps=2):
    B, T, _ = q.shape
    tq = min(tq, T)
    half = tq // 2
    assert T % tq == 0 and half % CHUNK == 0 and N_DIFF_HEADS % hps == 0
    ii = lax.broadcasted_iota(jnp.int32, (half, tq), 0)
    jj = lax.broadcasted_iota(jnp.int32, (half, tq), 1)
    jh = jj - half
    diag = jnp.where((jh // CHUNK) <= (ii // CHUNK), (half + ii - jnp.abs(ii - jh)).astype(F32), -jnp.inf)
    base = jnp.where(jj < half, jj.astype(F32), diag)
    vec = pl.BlockSpec((1, DIFF_HEAD_DIM), lambda b, h, i: (0, 0))
    wide = hps * DIFF_VDIM
    return pl.pallas_call(
        functools.partial(_diff_prompt_kernel, tq=tq, hps=hps),
        out_shape=jax.ShapeDtypeStruct((B, T, DIFF_WIDTH), BF16),
        grid=(B, N_DIFF_HEADS // hps, T // tq),
        in_specs=[vec, vec, vec, vec,
                  pl.BlockSpec((1, DIFF_VDIM), lambda b, h, i: (0, 0)),
                  pl.BlockSpec((half, tq), lambda b, h, i: (0, 0)),
                  pl.BlockSpec((1, tq, wide), lambda b, h, i: (b, i, h)),
                  pl.BlockSpec((1, T, wide), lambda b, h, i: (b, 0, h)),
                  pl.BlockSpec((1, T, wide), lambda b, h, i: (b, 0, h))],
        out_specs=pl.BlockSpec((1, tq, wide), lambda b, h, i: (b, i, h)),
        scratch_shapes=[pltpu.VMEM((2 * hps, tq, 128), F32), pltpu.VMEM((2 * hps, tq, 128), F32),
                        pltpu.VMEM((2 * hps, tq, DIFF_VDIM), F32)],
        compiler_params=_cparams(("parallel", "parallel", "parallel")),
        name="diff_attn_prompt",
    )(*lam_rows, subln_g, base, q, k, v)


def _diff_sample_kernel(lq1_ref, lk1_ref, lq2_ref, lk2_ref, g_ref, q_ref, kc_ref, vc_ref, kn_ref, vn_ref,
                        o_ref, *, past, pc):
    H = N_DIFF_HEADS
    T = q_ref.shape[1]
    L = H * T
    rows = pc * H
    n_chunks = past // pc
    lam = _lam_value(lq1_ref[...], lk1_ref[...], lq2_ref[...], lk2_ref[...])
    maps = range(2)

    def map_cols(m):
        return slice(m * DIFF_HEAD_DIM, (m + 1) * DIFF_HEAD_DIM)

    qm = [jnp.concatenate([q_ref[0, :, h * DIFF_VDIM + m * DIFF_HEAD_DIM:h * DIFF_VDIM + (m + 1) * DIFF_HEAD_DIM]
                           for h in range(H)], axis=0) for m in maps]
    lane = lax.broadcasted_iota(jnp.int32, (1, L), 1)
    h_lane = lane // T
    t_lane = lane % T
    slope = jnp.exp2((h_lane + 1).astype(F32) * (-8.0 / H))
    base = slope * (lax.broadcasted_iota(jnp.int32, (rows, L), 0) // H).astype(F32)
    valid = lax.broadcasted_iota(jnp.int32, (H, L), 0) == h_lane

    def chunk_shift(c):
        return slope * ((c * pc - past) - t_lane).astype(F32)

    zq = jnp.zeros((L, DIFF_HEAD_DIM), BF16)
    q_both = jnp.concatenate([jnp.concatenate([qm[0], zq], axis=1), jnp.concatenate([zq, qm[1]], axis=1)], axis=0)

    def both_maps(keys):
        s2 = _nt_dot(keys, q_both)
        return [s2[:, m * L:(m + 1) * L] for m in maps]

    def chunk_scores(c):
        r0 = c * rows if isinstance(c, int) else pl.multiple_of(c * rows, rows)
        s2 = both_maps(kc_ref[0, pl.ds(r0, rows), :].astype(BF16))
        return [(s2[m] + base).reshape(pc, H, L) for m in maps]

    shared = chunk_scores(0) if n_chunks == 1 else None

    def stats_step(c, carry):
        shift = chunk_shift(c)
        s = chunk_scores(c) if shared is None else shared
        out = []
        for m in maps:
            m_old, l_old = carry[2 * m], carry[2 * m + 1]
            m_new = jnp.maximum(m_old, jnp.max(s[m], axis=0) + shift)
            l_new = l_old * jnp.exp(m_old - m_new) + jnp.sum(jnp.exp(s[m] - (m_new - shift)[None]), axis=0)
            out += [m_new, l_new]
        return tuple(out)

    init = (jnp.full((H, L), -jnp.inf, F32), jnp.zeros((H, L), F32)) * 2
    stats = stats_step(0, init) if n_chunks == 1 else lax.fori_loop(0, n_chunks, stats_step, init)

    kn = kn_ref[0].astype(BF16)
    vn = vn_ref[0].astype(BF16)
    tk = lax.broadcasted_iota(jnp.int32, (T * H, L), 0) // H
    allowed = ((past + tk) // CHUNK) <= ((past + t_lane) // CHUNK)
    bias_n = jnp.where(allowed, -slope * jnp.abs(tk - t_lane).astype(F32), -jnp.inf)
    s_n = [(s + bias_n).reshape(T, H, L) for s in both_maps(kn)]
    m_fin, coef = [], []
    for m in maps:
        m_run, l_run = stats[2 * m], stats[2 * m + 1]
        mf = jnp.maximum(m_run, jnp.max(s_n[m], axis=0))
        lf = l_run * jnp.exp(m_run - mf) + jnp.sum(jnp.exp(s_n[m] - mf[None]), axis=0)
        m_fin.append(mf)
        coef.append(jnp.where(valid, (1.0 if m == 0 else -lam) / lf, 0.0))

    def tn_dot(w, v):
        return lax.dot_general(w, v, (((0,), (0,)), ((), ())), preferred_element_type=F32)

    def out_step(c, acc):
        shift = chunk_shift(c)
        s = chunk_scores(c) if shared is None else shared
        r0 = c * rows if isinstance(c, int) else pl.multiple_of(c * rows, rows)
        vblk = vc_ref[0, pl.ds(r0, rows), :].astype(BF16)
        w = sum(jnp.exp(s[m] - (m_fin[m] - shift)[None]) * coef[m][None] for m in maps)
        return acc + tn_dot(w.reshape(rows, L).astype(BF16), vblk)

    acc0 = jnp.zeros((L, DIFF_VDIM), F32)
    acc = out_step(0, acc0) if n_chunks == 1 else lax.fori_loop(0, n_chunks, out_step, acc0)
    w_n = sum(jnp.exp(s_n[m] - m_fin[m][None]) * coef[m][None] for m in maps)
    acc = acc + tn_dot(w_n.reshape(T * H, L).astype(BF16), vn)
    o = _subln(acc, g_ref[...]).astype(o_ref.dtype)
    for h in range(H):
        o_ref[0, :, h * DIFF_VDIM:(h + 1) * DIFF_VDIM] = o[h * T:(h + 1) * T]


def _diff_attention_sample(q, cache_k, cache_v, k_new, v_new, lam_rows, subln_g):
    B, T, _ = q.shape
    P = cache_k.shape[1]
    H = N_DIFF_HEADS
    pc = min(P, 1024)
    assert H * T == 128 and P > 0 and P % pc == 0
    vec = pl.BlockSpec((1, DIFF_HEAD_DIM), lambda b: (0, 0))
    tok = pl.BlockSpec((1, T, DIFF_WIDTH), lambda b: (b, 0, 0))
    new = pl.BlockSpec((1, T * H, DIFF_VDIM), lambda b: (b, 0, 0))
    old = pl.BlockSpec((1, P * H, DIFF_VDIM), lambda b: (b, 0, 0))
    flat = lambda a: a.reshape(B, a.shape[1] * H, DIFF_VDIM)
    return pl.pallas_call(
        functools.partial(_diff_sample_kernel, past=P, pc=pc),
        out_shape=jax.ShapeDtypeStruct((B, T, DIFF_WIDTH), BF16),
        grid=(B,),
        in_specs=[vec, vec, vec, vec, pl.BlockSpec((1, DIFF_VDIM), lambda b: (0, 0)),
                  tok, old, old, new, new],
        out_specs=tok,
        compiler_params=_cparams(("parallel",)),
        name="diff_attn_sample",
    )(*lam_rows, subln_g, q, flat(cache_k), flat(cache_v), flat(k_new), flat(v_new))


def _gdn_kernel(*refs, c, cps, has_state):
    if has_state:
        (xq_ref, xk_ref, xv_ref, ab_ref, alog_ref, dtb_ref, z_ref, ng_ref, s0_ref, o_ref, s_ref) = refs
    else:
        (xq_ref, xk_ref, xv_ref, ab_ref, alog_ref, dtb_ref, z_ref, ng_ref, o_ref, s_ref) = refs
        s0_ref = None
    n = pl.program_id(1)
    H = N_GDN_HEADS

    @pl.when(n == 0)
    def _init():
        if has_state:
            s_ref[0] = s0_ref[0]
        else:
            s_ref[...] = jnp.zeros(s_ref.shape, F32)

    ab = ab_ref[0]
    a_in = ab[:, 0:H] + dtb_ref[...]
    softplus = jnp.maximum(a_in, 0.0) + jnp.log1p(jnp.exp(-jnp.abs(a_in)))
    g_col = -jnp.exp(alog_ref[...]) * softplus
    beta_col = _sigmoid(ab[:, H:2 * H])
    r = lax.broadcasted_iota(jnp.int32, (c, c), 0)
    s = lax.broadcasted_iota(jnp.int32, (c, c), 1)
    tri = r >= s
    strict = r > s
    tri_f = tri.astype(F32)
    eye_c = (r == s).astype(F32)
    eye_h = (lax.broadcasted_iota(jnp.int32, (H, H), 0) == lax.broadcasted_iota(jnp.int32, (H, H), 1)).astype(F32)

    def bf(a):
        return a.astype(BF16)

    def split(a):
        hi = a.astype(BF16)
        return hi, (a - hi.astype(F32)).astype(BF16)

    def dot3(a, b):
        (ah, al), (bh, bl) = a, b
        return (jnp.dot(al, bh, preferred_element_type=F32) + jnp.dot(ah, bl, preferred_element_type=F32)
                + jnp.dot(ah, bh, preferred_element_type=F32))

    blocks = []
    b_ = 1
    while b_ < c:
        blocks.append((((r // (2 * b_)) == (s // (2 * b_))) & ((r // b_) != (s // b_)) & strict).astype(F32))
        b_ *= 2

    rows = [slice(ci * c, (ci + 1) * c) for ci in range(cps)]
    lanes = [slice(hh * GDN_DK, (hh + 1) * GDN_DK) for hh in range(H)]
    gc_col = [jnp.dot(tri_f, g_col[rows[ci]], preferred_element_type=F32, precision=HI) for ci in range(cps)]
    gc_row = [lax.dot_general(eye_h, gc_col[ci], (((1,), (1,)), ((), ())), preferred_element_type=F32,
                              precision=HI) for ci in range(cps)]
    items = [(ci, hh) for ci in range(cps) for hh in range(H)]
    every = range(len(items))
    gcc = [gc_col[ci][:, hh:hh + 1] for ci, hh in items]
    bet = [beta_col[rows[ci], hh:hh + 1] for ci, hh in items]
    gcr = [gc_row[ci][hh:hh + 1, :] for ci, hh in items]
    g_last = [g[c - 1:c, :] for g in gcc]

    q = [xq_ref[0, rows[ci], lanes[hh]].astype(F32) for ci, hh in items]
    k = [xk_ref[0, rows[ci], lanes[hh]].astype(F32) for ci, hh in items]
    v = [xv_ref[0, rows[ci], lanes[hh]].astype(F32) for ci, hh in items]

    decay = [jnp.exp(jnp.where(tri, gcc[i] - gcr[i], -jnp.inf)) for i in every]
    kb = [k[i] * bet[i] for i in every]
    k16 = [bf(k[i]) for i in every]
    mmat = [jnp.where(strict, _nt_dot(bf(kb[i]), k16[i]) * decay[i], 0.0) for i in every]
    qk = [jnp.where(tri, _nt_dot(bf(q[i]), k16[i]) * decay[i], 0.0) for i in every]

    tinv = [eye_c - mmat[i] * blocks[0] for i in every]
    for lvl in range(1, len(blocks)):
        d16 = [bf(t) for t in tinv]
        x = [jnp.dot(bf(mmat[i] * blocks[lvl]), d16[i], preferred_element_type=F32) for i in every]
        tinv = [tinv[i] - jnp.dot(d16[i], bf(x[i]), preferred_element_type=F32) for i in every]
    m_s = [split(m) for m in mmat]
    t_s = [split(t) for t in tinv]
    res = [(eye_c - tinv[i]) - dot3(m_s[i], t_s[i]) for i in every]
    tinv = [tinv[i] + jnp.dot(t_s[i][0], bf(res[i]), preferred_element_type=F32) for i in every]

    rhs = [jnp.concatenate([v[i] * bet[i], kb[i] * jnp.exp(gcc[i])], axis=1) for i in every]
    sol = [dot3(split(tinv[i]), split(rhs[i])) for i in every]
    u = [x_[:, :GDN_DV] for x_ in sol]
    w16 = [bf(x_[:, GDN_DV:]) for x_ in sol]
    qg16 = [bf(q[i] * jnp.exp(gcc[i])) for i in every]
    qk16 = [bf(x_) for x_ in qk]
    kd16 = [bf(k[i] * jnp.exp(g_last[i] - gcc[i])) for i in every]
    e_last = [jnp.exp(g) for g in g_last]

    S = [s_ref[0, hh] for hh in range(H)]
    for ci in range(cps):
        of = ci * H
        S16 = [bf(x_) for x_ in S]
        v_new = [u[of + hh] - jnp.dot(w16[of + hh], S16[hh], preferred_element_type=F32) for hh in range(H)]
        v16 = [bf(x_) for x_ in v_new]
        o = [jnp.dot(qg16[of + hh], S16[hh], preferred_element_type=F32)
             + jnp.dot(qk16[of + hh], v16[hh], preferred_element_type=F32) for hh in range(H)]
        S = [S[hh] * e_last[of + hh] + lax.dot_general(kd16[of + hh], v16[hh], (((0,), (0,)), ((), ())),
                                                       preferred_element_type=F32) for hh in range(H)]
        for hh in range(H):
            ms = jnp.mean(o[hh] * o[hh], axis=-1, keepdims=True)
            y = o[hh] * lax.rsqrt(ms + NORM_EPS) * ng_ref[...] * z_ref[0, rows[ci], lanes[hh]].astype(F32)
            o_ref[0, rows[ci], lanes[hh]] = y.astype(o_ref.dtype)
    for hh in range(H):
        s_ref[0, hh] = S[hh]


def _gdn(q, k, v, gab, a_log, dt_bias, z_silu, norm_g, s0, *, c):
    B, T, _ = q.shape
    H = N_GDN_HEADS
    assert T % c == 0 and c % 8 == 0 and (c & (c - 1)) == 0
    cps = 2 if (T // c) % 2 == 0 else 1
    R = cps * c
    stream = pl.BlockSpec((1, R, GDN_WIDTH), lambda b, n: (b, n, 0))
    small = pl.BlockSpec((1, H), lambda b, n: (0, 0))
    state = pl.BlockSpec((1, H, GDN_DK, GDN_DV), lambda b, n: (b, 0, 0, 0))
    in_specs = [stream, stream, stream,
                pl.BlockSpec((1, R, 128), lambda b, n: (b, n, 0)), small, small,
                stream, pl.BlockSpec((1, GDN_DV), lambda b, n: (0, 0))]
    args = [q, k, v, gab, a_log, dt_bias, z_silu, norm_g]
    if s0 is not None:
        in_specs.append(state)
        args.append(s0)
    return pl.pallas_call(
        functools.partial(_gdn_kernel, c=c, cps=cps, has_state=s0 is not None),
        out_shape=(jax.ShapeDtypeStruct((B, T, GDN_WIDTH), BF16),
                   jax.ShapeDtypeStruct((B, H, GDN_DK, GDN_DV), F32)),
        grid=(B, T // R),
        in_specs=in_specs,
        out_specs=(stream, state),
        compiler_params=_cparams(("parallel", "arbitrary")),
        name="gdn",
    )(*args)


def _mix_kernel(oa_ref, ob_ref, wa_ref, wb_ref, sa_ref, sb_ref, o_ref):
    a = jnp.dot(oa_ref[...], wa_ref[0].astype(BF16), preferred_element_type=F32)
    b = jnp.dot(ob_ref[...], wb_ref[0].astype(BF16), preferred_element_type=F32)
    o_ref[...] = (sa_ref[...].astype(F32) * a + sb_ref[...].astype(F32) * b).astype(o_ref.dtype)


def _mix(o_a, o_b, w_pa, w_pb, gates, *, tm=1024, tn=512):
    M = o_a.shape[0]
    tm = min(tm, M)
    nb = D_MODEL // tn
    assert w_pa.shape[0] == w_pb.shape[0] == DEPTH == 1 and M % tm == 0
    return pl.pallas_call(
        _mix_kernel,
        out_shape=jax.ShapeDtypeStruct((M, D_MODEL), BF16),
        grid=(nb, M // tm),
        in_specs=[pl.BlockSpec((tm, DIFF_WIDTH), lambda j, i: (i, 0)),
                  pl.BlockSpec((tm, GDN_WIDTH), lambda j, i: (i, 0)),
                  pl.BlockSpec((1, DIFF_WIDTH, tn), lambda j, i: (0, 0, j)),
                  pl.BlockSpec((1, GDN_WIDTH, tn), lambda j, i: (0, 0, j)),
                  pl.BlockSpec((tm, tn), lambda j, i: (i, j)),
                  pl.BlockSpec((tm, tn), lambda j, i: (i, j + nb))],
        out_specs=pl.BlockSpec((tm, tn), lambda j, i: (i, j)),
        compiler_params=_cparams(("parallel", "parallel")),
        name="mix",
    )(o_a, o_b, w_pa, w_pb, gates, gates)


def _post_kernel(x_ref, mix_ref, wo_ref, g1_ref, b1_ref, wxq_ref, mk_ref, mv_ref, wxo_ref, g2_ref, b2_ref,
                 h2_ref, h2b_ref):
    nb, tm, D = x_ref.shape
    x = x_ref[...].reshape(nb * tm, D)
    mix = mix_ref[...].reshape(nb * tm, D)
    h1 = ALPHA * x + jnp.dot(mix, wo_ref[...], preferred_element_type=F32)
    h1 = _layer_norm(h1, g1_ref[...], b1_ref[...])
    qx = jnp.dot(h1.astype(BF16), wxq_ref[...], preferred_element_type=F32) * (XHEAD_DIM ** -0.5)
    qx = qx.astype(BF16)
    seqs = []
    for b in range(nb):
        rows = slice(b * tm, (b + 1) * tm)
        heads = []
        for hh in range(N_XHEADS):
            sl = slice(hh * XHEAD_DIM, (hh + 1) * XHEAD_DIM)
            s = _nt_dot(qx[rows, sl], mk_ref[b, :, sl])
            p = jnp.exp(s - jnp.max(s, axis=-1, keepdims=True))
            p = p / jnp.sum(p, axis=-1, keepdims=True)
            heads.append(jnp.dot(p.astype(BF16), mv_ref[b, :, sl], preferred_element_type=F32))
        seqs.append(jnp.concatenate(heads, axis=1))
    ox = jnp.concatenate(seqs, axis=0).astype(BF16)
    h2 = ALPHA * h1 + jnp.dot(ox, wxo_ref[...], preferred_element_type=F32)
    h2 = _layer_norm(h2, g2_ref[...], b2_ref[...])
    h2_ref[...] = h2.reshape(nb, tm, D)
    h2b_ref[...] = h2.astype(BF16).reshape(nb, tm, D)


def _post(x, mix, w_o, ln1_g, ln1_b, w_xq, mem_k, mem_v, w_xo, ln2_g, ln2_b, *, tm=512):
    B, T, D = x.shape
    tm = min(tm, T)
    nb = min(B, max(1, 256 // tm))
    assert B % nb == 0 and T % tm == 0 and tm % 16 == 0
    const = lambda shape: pl.BlockSpec(shape, lambda b, i: (0, 0), pipeline_mode=pl.Buffered(1))
    rows = lambda: pl.BlockSpec((nb, tm, D), lambda b, i: (b, i, 0))
    mem = lambda: pl.BlockSpec((nb, N_MEM, XWIDTH), lambda b, i: (b, 0, 0))
    return pl.pallas_call(
        _post_kernel,
        out_shape=(jax.ShapeDtypeStruct((B, T, D), F32), jax.ShapeDtypeStruct((B, T, D), BF16)),
        grid=(B // nb, T // tm),
        in_specs=[rows(), rows(), const((D, D)), const((1, D)), const((1, D)), const((D, XWIDTH)),
                  mem(), mem(), const((XWIDTH, D)), const((1, D)), const((1, D))],
        out_specs=(rows(), rows()),
        compiler_params=_cparams(("parallel", "parallel")),
        name="post_attn",
    )(x, mix, w_o, ln1_g, ln1_b, w_xq, mem_k, mem_v, w_xo, ln2_g, ln2_b)


def _ffn_kernel(hb_ref, h_ref, w1_ref, w3_ref, w2_ref, g_ref, b_ref, y_ref, acc_ref):
    f = pl.program_id(1)

    @pl.when(f == 0)
    def _():
        acc_ref[...] = jnp.zeros(acc_ref.shape, F32)

    hb = hb_ref[...]
    a = jnp.dot(hb, w1_ref[...], preferred_element_type=F32)
    b = jnp.dot(hb, w3_ref[...], preferred_element_type=F32)
    act = (_silu(a) * b).astype(BF16)
    acc_ref[...] += jnp.dot(act, w2_ref[...], preferred_element_type=F32)

    @pl.when(f == pl.num_programs(1) - 1)
    def _():
        y_ref[...] = _layer_norm(ALPHA * h_ref[...] + acc_ref[...], g_ref[...], b_ref[...])


def _ffn(h2b, h2, w1, w3, w2, ln_g, ln_b, *, tm=512, tf=512):
    M, D = h2.shape
    tm = min(tm, M)
    assert M % tm == 0 and D_FF % tf == 0
    return pl.pallas_call(
        _ffn_kernel,
        out_shape=jax.ShapeDtypeStruct((M, D), F32),
        grid=(M // tm, D_FF // tf),
        in_specs=[pl.BlockSpec((tm, D), lambda i, f: (i, 0)),
                  pl.BlockSpec((tm, D), lambda i, f: (i, 0)),
                  pl.BlockSpec((D, tf), lambda i, f: (0, f)),
                  pl.BlockSpec((D, tf), lambda i, f: (0, f)),
                  pl.BlockSpec((tf, D), lambda i, f: (f, 0)),
                  pl.BlockSpec((1, D), lambda i, f: (0, 0)),
                  pl.BlockSpec((1, D), lambda i, f: (0, 0))],
        out_specs=pl.BlockSpec((tm, D), lambda i, f: (i, 0)),
        scratch_shapes=[pltpu.VMEM((tm, D), F32)],
        compiler_params=_cparams(("parallel", "arbitrary")),
        name="ffn",
    )(h2b, h2, w1, w3, w2, ln_g, ln_b)


def _encoder_layer(x, mem_k, mem_v, past, W):
    B, T, D = x.shape
    M = B * T
    w_in = W["w_in_t"]
    xb, dq = _proj(x.reshape(M, D), w_in, OFF_DQ, DIFF_WIDTH, (BF16,), scale=DIFF_HEAD_DIM ** -0.5, w_rows=True,
                   emit_x=True, name="proj_dq")
    in_proj = functools.partial(_proj, xb, w_in, w_rows=True)
    buf0 = jnp.zeros((B, CONV_W - 1, 3 * GDN_WIDTH), F32) if past is None else past[3]
    streams, tails = [], []
    for t, (norm, scale) in enumerate(((True, GDN_DK ** -0.5), (True, 1.0), (False, 1.0))):
        y, tail = _proj_conv(xb, w_in, OFF_GQKV + t * GDN_WIDTH, buf0, W["conv_w"], B=B, T=T, norm=norm,
                             scale=scale, name="proj_conv_" + "qkv"[t])
        streams.append(y.reshape(B, T, GDN_WIDTH))
        tails.append(tail[:, CONV_PAD - (CONV_W - 1):, :])
    (gz,) = in_proj(OFF_GZ, GDN_WIDTH, (BF16,), act="silu", name="proj_gz")
    (gab,) = in_proj(OFF_GAB, 128, (F32,), tn=128, name="proj_gab")
    (gates,) = in_proj(OFF_GATES, 2 * D_MODEL, (BF16,), act="sigmoid", name="proj_gates")
    kv_dtypes = (F32, BF16) if past is None else (F32,)
    dk, *dkb = in_proj(OFF_DK, DIFF_WIDTH, kv_dtypes, rows_inner=True, name="proj_dk")
    dv, *dvb = in_proj(OFF_DV, DIFF_WIDTH, kv_dtypes, rows_inner=True, name="proj_dv")

    dq = dq.reshape(B, T, DIFF_WIDTH)
    dk = dk.reshape(B, T, N_DIFF_HEADS, DIFF_VDIM)
    dv = dv.reshape(B, T, N_DIFF_HEADS, DIFF_VDIM)
    lam_rows = W["lam_rows"]
    if past is None:
        o_a = _diff_attention_prompt(dq, dkb[0].reshape(B, T, DIFF_WIDTH), dvb[0].reshape(B, T, DIFF_WIDTH),
                                     lam_rows, W["diff_subln_g"])
        s0 = None
        c = CHUNK
    else:
        cache_k, cache_v, s0, _ = past
        o_a = _diff_attention_sample(dq, cache_k, cache_v, dk, dv, lam_rows, W["diff_subln_g"])
        c = T
    o_b, s_new = _gdn(*streams, gab.reshape(B, T, 128), W["gdn_a_log"], W["gdn_dt_bias"],
                      gz.reshape(B, T, GDN_WIDTH), W["gdn_norm_g"], s0, c=c)
    new_buf = jnp.concatenate(tails, axis=-1)

    mix = _mix(o_a.reshape(M, DIFF_WIDTH), o_b.reshape(M, GDN_WIDTH), W["w_pa"], W["w_pb"], gates)
    h2, h2b = _post(x, mix.reshape(B, T, D), W["w_o"], W["ln1_g"], W["ln1_b"], W["w_xq"], mem_k, mem_v,
                    W["w_xo"], W["ln2_g"], W["ln2_b"])
    y = _ffn(h2b.reshape(M, D), h2.reshape(M, D), W["w_ff1"], W["w_ff3"], W["w_ff2"], W["ln3_g"], W["ln3_b"])
    return y.reshape(B, T, D), dk, dv, s_new, new_buf


def kernel(x_prompt, x_sample, mem_prompt, cache_diff_k, cache_diff_v, state_gdn, state_gdn_conv, cache_mem_k, cache_mem_v, w_in, conv_w, lam_q1, lam_k1, lam_q2, lam_k2, diff_subln_g, gdn_a_log, gdn_dt_bias, gdn_norm_g, w_pa, w_pb, w_o, ln1_g, ln1_b, w_xq, w_xk, w_xv, w_xo, ln2_g, ln2_b, w_ff1, w_ff3, w_ff2, ln3_g, ln3_b):
    l = 0
    W = {
        "w_in_t": jnp.swapaxes(w_in, 1, 2),
        "conv_w": conv_w[l],
        "lam_rows": tuple(v[l].reshape(1, DIFF_HEAD_DIM) for v in (lam_q1, lam_k1, lam_q2, lam_k2)),
        "diff_subln_g": diff_subln_g[l].reshape(1, DIFF_VDIM),
        "gdn_a_log": gdn_a_log[l].reshape(1, N_GDN_HEADS),
        "gdn_dt_bias": gdn_dt_bias[l].reshape(1, N_GDN_HEADS),
        "gdn_norm_g": gdn_norm_g[l].reshape(1, GDN_DV),
        "w_pa": w_pa, "w_pb": w_pb, "w_o": w_o[l].astype(BF16),
        "ln1_g": ln1_g[l].reshape(1, D_MODEL), "ln1_b": ln1_b[l].reshape(1, D_MODEL),
        "w_xq": w_xq[l].astype(BF16), "w_xo": w_xo[l].astype(BF16),
        "ln2_g": ln2_g[l].reshape(1, D_MODEL), "ln2_b": ln2_b[l].reshape(1, D_MODEL),
        "w_ff1": w_ff1[l].astype(BF16), "w_ff3": w_ff3[l].astype(BF16), "w_ff2": w_ff2[l].astype(BF16),
        "ln3_g": ln3_g[l].reshape(1, D_MODEL), "ln3_b": ln3_b[l].reshape(1, D_MODEL),
    }
    Bp = x_prompt.shape[0]
    memb = mem_prompt.reshape(Bp * N_MEM, D_MODEL).astype(BF16)
    mem_k, mem_kb = _proj(memb, w_xk, 0, XWIDTH, (F32, BF16), tn=XWIDTH, name="proj_mem_k")
    mem_v, mem_vb = _proj(memb, w_xv, 0, XWIDTH, (F32, BF16), tn=XWIDTH, name="proj_mem_v")

    yp, pk, pv, ps, pc = _encoder_layer(x_prompt, mem_kb.reshape(Bp, N_MEM, XWIDTH),
                                        mem_vb.reshape(Bp, N_MEM, XWIDTH), None, W)
    Bs = x_sample.shape[0]
    past = (cache_diff_k[l], cache_diff_v[l], state_gdn[l], state_gdn_conv[l])
    ys, sk, sv, ss, sc = _encoder_layer(x_sample, cache_mem_k[l].reshape(Bs, N_MEM, XWIDTH).astype(BF16),
                                        cache_mem_v[l].reshape(Bs, N_MEM, XWIDTH).astype(BF16), past, W)
    st = lambda a: a[None]
    return (yp, ys, st(pk), st(pv), st(ps), st(pc),
            st(mem_k.reshape(Bp, N_MEM, N_XHEADS, XHEAD_DIM)), st(mem_v.reshape(Bp, N_MEM, N_XHEADS, XHEAD_DIM)),
            st(sk), st(sv), st(ss), st(sc))
```

```python
import functools
import math

import jax
import jax.numpy as jnp
from jax import lax
from jax.experimental import pallas as pl
from jax.experimental.pallas import tpu as pltpu

D_MODEL = 2048
CHUNK = 64
N_DIFF_HEADS = 8
DIFF_HEAD_DIM = 128
DIFF_VDIM = 2 * DIFF_HEAD_DIM
DIFF_WIDTH = N_DIFF_HEADS * DIFF_VDIM
N_GDN_HEADS = 16
GDN_DK = 128
GDN_DV = 128
GDN_WIDTH = N_GDN_HEADS * GDN_DK
CONV_W = 4
N_MEM = 256
N_XHEADS = 4
XHEAD_DIM = 128
XWIDTH = N_XHEADS * XHEAD_DIM
D_FF = 5632
DEPTH = 1
ALPHA = (2.0 * DEPTH) ** 0.25
LN_EPS = 1e-5
NORM_EPS = 1e-6
LAM_INIT = 0.8 - 0.6 * math.exp(-0.3 * 0)

OFF_DQ = 0
OFF_DK = 2048
OFF_DV = 4096
OFF_GQKV = 6144
OFF_GZ = 12288
OFF_GAB = 14336
OFF_GATES = 14368

VMEM_LIMIT = 56 * 1024 * 1024
BF16 = jnp.bfloat16
F32 = jnp.float32
HI = lax.Precision.HIGHEST


def _cparams(sem):
    return pltpu.CompilerParams(dimension_semantics=sem, vmem_limit_bytes=VMEM_LIMIT)


def _sigmoid(x):
    return 1.0 / (1.0 + jnp.exp(-x))


def _silu(x):
    return x * _sigmoid(x)


def _layer_norm(x, g, b):
    mu = jnp.mean(x, axis=-1, keepdims=True)
    xc = x - mu
    var = jnp.mean(xc * xc, axis=-1, keepdims=True)
    return xc * lax.rsqrt(var + LN_EPS) * g + b


def _proj_kernel(x_ref, w_ref, *o_refs, act, scale, w_rows, emit_x):
    w = w_ref[0].astype(BF16)
    x = x_ref[...].astype(BF16)
    if emit_x:
        o_refs[0][...] = x
        o_refs = o_refs[1:]
    acc = _nt_dot(x, w) if w_rows else jnp.dot(x, w, preferred_element_type=F32)
    if scale != 1.0:
        acc = acc * scale
    if act == "sigmoid":
        acc = _sigmoid(acc)
    elif act == "silu":
        acc = _silu(acc)
    for o in o_refs:
        o[...] = acc.astype(o.dtype)


def _weight_spec(w, K, tn, col_start, w_rows):
    assert w.shape[0] == DEPTH == 1
    if w_rows:
        return pl.BlockSpec((pl.Element(1), pl.Element(tn), pl.Element(K)),
                            lambda a, b: (0, pl.multiple_of(col_start(a, b), 8), 0))
    return pl.BlockSpec((1, K, tn), lambda a, b: (0, 0, col_start(a, b) // tn))


def _proj(x, w, col_off, n_cols, out_dtypes, *, act=None, scale=1.0, tm=1024, tn=1024, rows_inner=False,
          w_rows=False, emit_x=False, name="proj"):
    M, K = x.shape
    tm = min(tm, M)
    tn = min(tn, n_cols)
    assert M % tm == 0 and n_cols % tn == 0 and col_off % (8 if w_rows else tn) == 0
    if rows_inner:
        grid = (n_cols // tn, M // tm)
        ij = lambda a, b: (b, a)
    else:
        grid = (M // tm, n_cols // tn)
        ij = lambda a, b: (a, b)
    x_spec = pl.BlockSpec((tm, K), lambda a, b: (ij(a, b)[0], 0))
    out_shape = [jax.ShapeDtypeStruct((M, n_cols), dt) for dt in out_dtypes]
    out_specs = [pl.BlockSpec((tm, tn), lambda a, b: ij(a, b)) for _ in out_dtypes]
    if emit_x:
        assert not rows_inner
        out_shape.insert(0, jax.ShapeDtypeStruct((M, K), BF16))
        out_specs.insert(0, x_spec)
    outs = pl.pallas_call(
        functools.partial(_proj_kernel, act=act, scale=scale, w_rows=w_rows, emit_x=emit_x),
        out_shape=tuple(out_shape),
        grid=grid,
        in_specs=[x_spec, _weight_spec(w, K, tn, lambda a, b: col_off + ij(a, b)[1] * tn, w_rows)],
        out_specs=tuple(out_specs),
        compiler_params=_cparams(("parallel", "arbitrary") if emit_x else ("parallel", "parallel")),
        name=name,
    )(x, w)
    return outs


CONV_PAD = 8


def _proj_conv_kernel(x_ref, w_ref, hist_ref, cw_ref, y_ref, raw_ref, pad_ref, *, nb, T, sub, tr_max, norm, scale):
    tn = w_ref.shape[1]
    tr = min(T, tr_max) if nb == 1 else T
    units = [(s, r) for s in range(tn // sub) for r in range(T // tr)]
    w_slabs = [w_ref[0, s * sub:(s + 1) * sub, :].astype(BF16) for s in range(tn // sub)]

    def rows_of(r):
        return slice(r * tr, (r + 1) * tr) if nb == 1 else slice(None)

    def matmul(s, r, gate):
        rows = rows_of(r)
        if gate is None:
            lhs = x_ref[rows, :]
        else:
            n_rows = nb * tr
            first = x_ref[rows, 0:256] + jnp.concatenate([jnp.concatenate([gate] * 2, axis=1)] * (n_rows // 16), axis=0)
            lhs = jnp.concatenate([first, x_ref[rows, 256:]], axis=1)
        return _nt_dot(lhs, w_slabs[s])

    def epilogue(s, r, acc):
        cols = slice(s * sub, (s + 1) * sub)
        slot = s % 2
        r0 = r * tr
        a3 = acc.reshape(nb, tr, sub)
        pad_ref[slot, :, CONV_PAD + r0:CONV_PAD + r0 + tr, :] = a3
        if r == 0:
            pad_ref[slot, :, CONV_PAD - (CONV_W - 1):CONV_PAD, :] = hist_ref[:, :, cols]
        if r == T // tr - 1:
            raw_ref[:, :, cols] = a3[:, tr - CONV_PAD:, :]
        y = cw_ref[CONV_W - 1:CONV_W, cols] * a3
        for j in range(CONV_W - 1):
            lo = CONV_PAD - (CONV_W - 1) + j + r0
            y = y + cw_ref[j:j + 1, cols] * pad_ref[slot, :, lo:lo + tr, :]
        y = _silu(y)
        if norm:
            heads = []
            for hh in range(sub // GDN_DK):
                yh = y[:, :, hh * GDN_DK:(hh + 1) * GDN_DK]
                heads.append(yh * (lax.rsqrt(jnp.sum(yh * yh, axis=-1, keepdims=True) + NORM_EPS) * scale))
            y = jnp.concatenate(heads, axis=-1)
        y2 = y.reshape(nb * tr, sub)
        y_ref[rows_of(r), cols] = y2.astype(y_ref.dtype)
        bits = pltpu.bitcast(y2[nb * tr - 16:, 0:128], jnp.uint32)
        zero = lax.shift_right_logical(lax.shift_right_logical(bits, jnp.uint32(16)), jnp.uint32(16))
        return zero.astype(F32).astype(BF16)

    gates = [None, None]
    acc = matmul(*units[0], None)
    for i, u in enumerate(units):
        nxt = matmul(*units[i + 1], gates[i + 1]) if i + 1 < len(units) else None
        gates.append(epilogue(*u, acc))
        acc = nxt


def _proj_conv(x, w, col_off, hist, conv_w, *, B, T, norm, scale=1.0, tn=512, sub=512, tr=512, name="proj_conv"):
    M, K = x.shape
    n_cols = GDN_WIDTH
    nb = max(1, 256 // T) if T < 256 else 1
    nb = min(nb, B)
    tm = nb * T
    gq = OFF_GQKV
    assert B % nb == 0 and n_cols % tn == 0 and col_off % 8 == 0 and (col_off - gq) % tn == 0 and T >= CONV_PAD
    hb = (col_off - gq) // tn
    return pl.pallas_call(
        functools.partial(_proj_conv_kernel, nb=nb, T=T, sub=sub, tr_max=tr, norm=norm, scale=scale),
        out_shape=(jax.ShapeDtypeStruct((M, n_cols), BF16), jax.ShapeDtypeStruct((B, CONV_PAD, n_cols), F32)),
        grid=(B // nb, n_cols // tn),
        in_specs=[pl.BlockSpec((tm, K), lambda i, j: (i, 0)),
                  _weight_spec(w, K, tn, lambda i, j: col_off + j * tn, True),
                  pl.BlockSpec((nb, CONV_W - 1, tn), lambda i, j: (i, 0, j + hb)),
                  pl.BlockSpec((CONV_W, tn), lambda i, j: (0, j + hb))],
        out_specs=(pl.BlockSpec((tm, tn), lambda i, j: (i, j)),
                   pl.BlockSpec((nb, CONV_PAD, tn), lambda i, j: (i, 0, j))),
        scratch_shapes=[pltpu.VMEM((min(2, tn // sub), nb, T + CONV_PAD, sub), F32)],
        compiler_params=_cparams(("parallel", "parallel")),
        name=name,
    )(x, w, hist, conv_w)


def _lam_value(lq1, lk1, lq2, lk2):
    a = jnp.sum(lq1 * lk1, axis=-1, keepdims=True)
    b = jnp.sum(lq2 * lk2, axis=-1, keepdims=True)
    return jnp.exp(a) - jnp.exp(b) + LAM_INIT


def _subln(o, g):
    ms = jnp.mean(o * o, axis=-1, keepdims=True)
    return o * lax.rsqrt(ms + NORM_EPS) * g * (1.0 - LAM_INIT)


def _head_slope(h):
    e = (h + 1).astype(F32) * (-8.0 / N_DIFF_HEADS)
    return jnp.exp2(jnp.full((1, 1), e, F32))


def _nt_dot(a, b):
    return lax.dot_general(a, b, (((1,), (1,)), ((), ())), preferred_element_type=F32)


def _lane_tile(x, n):
    return x if n == 1 else jnp.concatenate([x] * n, axis=1)


def _diff_prompt_kernel(lq1_ref, lk1_ref, lq2_ref, lk2_ref, g_ref, base_ref, q_ref, k_ref, v_ref, *rest, tq, hps,
                        n_cast):
    cast_in, o_ref, cast_out = rest[:n_cast], rest[n_cast], rest[n_cast + 1:2 * n_cast + 1]
    m_ref, l_ref, acc_ref = rest[2 * n_cast + 1:]
    for src, dst in zip(cast_in, cast_out):
        dst[...] = src[0].astype(dst.dtype)
    LANES = 128
    half = tq // 2
    hg = pl.program_id(1)
    qi = pl.program_id(2)
    slope = [_head_slope(hg * hps + hh) for hh in range(hps)]
    lam = _lam_value(lq1_ref[...], lk1_ref[...], lq2_ref[...], lk2_ref[...])
    col_iota = lax.broadcasted_iota(jnp.int32, (1, tq), 1).astype(F32)
    col_bias = [sl * col_iota for sl in slope]
    maps = range(2)

    m_ref[...] = jnp.full(m_ref.shape, -jnp.inf, F32)
    l_ref[...] = jnp.zeros(l_ref.shape, F32)
    acc_ref[...] = jnp.zeros(acc_ref.shape, F32)

    def cols(hh, m):
        c0 = hh * DIFF_VDIM + m * DIFF_HEAD_DIM
        return slice(c0, c0 + DIFF_HEAD_DIM)

    def update(*jobs):
        todo = [(job, m) for job in jobs for m in maps]
        t = [_nt_dot(q_ref[0, rows, cols(hh, m)], k_ref[0, pl.ds(start, nk), cols(hh, m)]) + bias
             for (hh, rows, start, nk, bias, _), m in todo]
        st = [2 * job[0] + m for job, m in todo]
        m_old = [m_ref[st[i], job[1]] for i, (job, m) in enumerate(todo)]
        m_new = [jnp.maximum(m_old[i], jnp.max(t[i], axis=-1, keepdims=True) + job[5])
                 for i, (job, m) in enumerate(todo)]
        a = [jnp.exp(m_old[i] - m_new[i]) for i in range(len(todo))]
        p = [jnp.exp(t[i] - _lane_tile(m_new[i] - job[5], job[3] // LANES)) for i, (job, m) in enumerate(todo)]
        pv = [jnp.dot(p[i].astype(BF16), v_ref[0, pl.ds(start, nk), hh * DIFF_VDIM:(hh + 1) * DIFF_VDIM],
                      preferred_element_type=F32) for i, ((hh, _, start, nk, _, _), m) in enumerate(todo)]
        for i, ((hh, rows, start, nk, _, _), m) in enumerate(todo):
            psum = p[i][:, :LANES]
            for c in range(1, nk // LANES):
                psum = psum + p[i][:, c * LANES:(c + 1) * LANES]
            l_ref[st[i], rows] = a[i] * l_ref[st[i], rows] + psum
            acc_ref[st[i], rows] = _lane_tile(a[i], DIFF_VDIM // LANES) * acc_ref[st[i], rows] + pv[i]
            m_ref[st[i], rows] = m_new[i]

    every = slice(0, tq)

    def body(j, carry):
        start = pl.multiple_of(j * tq, tq)
        gap = ((qi - j) * tq).astype(F32)
        update(*[(hh, every, start, tq, col_bias[hh], -slope[hh] * gap) for hh in range(hps)])
        return carry

    lax.fori_loop(0, qi, body, 0)

    lo, hi = slice(0, half), slice(half, tq)
    start = pl.multiple_of(qi * tq, tq)
    zero = jnp.zeros((1, 1), F32)
    jobs = []
    for hh in range(hps):
        jobs.append((hh, lo, start, half, slope[hh] * base_ref[:, half:], -slope[hh] * float(half)))
        jobs.append((hh, hi, start, tq, slope[hh] * base_ref[...], zero))
    update(*jobs)

    for hh in range(hps):
        outs = []
        for m in maps:
            inv = 1.0 / jnp.sum(l_ref[2 * hh + m], axis=-1, keepdims=True)
            outs.append(acc_ref[2 * hh + m] * inv)
        o = outs[0] - lam * outs[1]
        o_ref[0, :, hh * DIFF_VDIM:(hh + 1) * DIFF_VDIM] = _subln(o, g_ref[...]).astype(o_ref.dtype)


def _diff_attention_prompt(q, k, v, lam_rows, subln_g, cast=(), *, tq=512, hps=2):
    B, T, _ = q.shape
    tq = min(tq, T)
    half = tq // 2
    assert T % tq == 0 and half % CHUNK == 0 and N_DIFF_HEADS % hps == 0
    ii = lax.broadcasted_iota(jnp.int32, (half, tq), 0)
    jj = lax.broadcasted_iota(jnp.int32, (half, tq), 1)
    jh = jj - half
    diag = jnp.where((jh // CHUNK) <= (ii // CHUNK), (half + ii - jnp.abs(ii - jh)).astype(F32), -jnp.inf)
    base = jnp.where(jj < half, jj.astype(F32), diag)
    vec = pl.BlockSpec((1, DIFF_HEAD_DIM), lambda b, h, i: (0, 0))
    wide = hps * DIFF_VDIM
    nh, nq = N_DIFF_HEADS // hps, T // tq
    n_steps = B * nh * nq
    step = lambda b, h, i: (b * nh + h) * nq + i
    cast_in, cast_out, cast_shapes = [], [], []
    for w in cast:
        _, R, C = w.shape
        per = next(p for p in (1, 2, 4, 8) if R % (n_steps // p) == 0 and (R // (n_steps // p)) % 16 == 0)
        rows = R // (n_steps // per)
        cast_in.append(pl.BlockSpec((1, rows, C), lambda b, h, i, per=per: (0, step(b, h, i) // per, 0)))
        cast_out.append(pl.BlockSpec((rows, C), lambda b, h, i, per=per: (step(b, h, i) // per, 0)))
        cast_shapes.append(jax.ShapeDtypeStruct((R, C), BF16))
    outs = pl.pallas_call(
        functools.partial(_diff_prompt_kernel, tq=tq, hps=hps, n_cast=len(cast)),
        out_shape=(jax.ShapeDtypeStruct((B, T, DIFF_WIDTH), BF16), *cast_shapes),
        grid=(B, nh, nq),
        in_specs=[vec, vec, vec, vec,
                  pl.BlockSpec((1, DIFF_VDIM), lambda b, h, i: (0, 0)),
                  pl.BlockSpec((half, tq), lambda b, h, i: (0, 0)),
                  pl.BlockSpec((1, tq, wide), lambda b, h, i: (b, i, h)),
                  pl.BlockSpec((1, T, wide), lambda b, h, i: (b, 0, h)),
                  pl.BlockSpec((1, T, wide), lambda b, h, i: (b, 0, h)), *cast_in],
        out_specs=(pl.BlockSpec((1, tq, wide), lambda b, h, i: (b, i, h)), *cast_out),
        scratch_shapes=[pltpu.VMEM((2 * hps, tq, 128), F32), pltpu.VMEM((2 * hps, tq, 128), F32),
                        pltpu.VMEM((2 * hps, tq, DIFF_VDIM), F32)],
        compiler_params=_cparams(("arbitrary", "arbitrary", "arbitrary")),
        name="diff_attn_prompt",
    )(*lam_rows, subln_g, base, q, k, v, *cast)
    return outs[0], tuple(outs[1:])


def _diff_sample_kernel(lq1_ref, lk1_ref, lq2_ref, lk2_ref, g_ref, q_ref, kc_ref, vc_ref, kn_ref, vn_ref,
                        o_ref, *, past, pc):
    H = N_DIFF_HEADS
    T = q_ref.shape[1]
    L = H * T
    rows = pc * H
    n_chunks = past // pc
    lam = _lam_value(lq1_ref[...], lk1_ref[...], lq2_ref[...], lk2_ref[...])
    maps = range(2)

    def map_cols(m):
        return slice(m * DIFF_HEAD_DIM, (m + 1) * DIFF_HEAD_DIM)

    qm = [jnp.concatenate([q_ref[0, :, h * DIFF_VDIM + m * DIFF_HEAD_DIM:h * DIFF_VDIM + (m + 1) * DIFF_HEAD_DIM]
                           for h in range(H)], axis=0) for m in maps]
    lane = lax.broadcasted_iota(jnp.int32, (1, L), 1)
    h_lane = lane // T
    t_lane = lane % T
    slope = jnp.exp2((h_lane + 1).astype(F32) * (-8.0 / H))
    base = slope * (lax.broadcasted_iota(jnp.int32, (rows, L), 0) // H).astype(F32)
    valid = lax.broadcasted_iota(jnp.int32, (H, L), 0) == h_lane

    def chunk_shift(c):
        return slope * ((c * pc - past) - t_lane).astype(F32)

    zq = jnp.zeros((L, DIFF_HEAD_DIM), BF16)
    q_both = jnp.concatenate([jnp.concatenate([qm[0], zq], axis=1), jnp.concatenate([zq, qm[1]], axis=1)], axis=0)

    def both_maps(keys):
        s2 = _nt_dot(keys, q_both)
        return [s2[:, m * L:(m + 1) * L] for m in maps]

    def chunk_scores(c):
        r0 = c * rows if isinstance(c, int) else pl.multiple_of(c * rows, rows)
        s2 = both_maps(kc_ref[0, pl.ds(r0, rows), :].astype(BF16))
        return [(s2[m] + base).reshape(pc, H, L) for m in maps]

    shared = chunk_scores(0) if n_chunks == 1 else None

    def stats_step(c, carry):
        shift = chunk_shift(c)
        s = chunk_scores(c) if shared is None else shared
        out = []
        for m in maps:
            m_old, l_old = carry[2 * m], carry[2 * m + 1]
            m_new = jnp.maximum(m_old, jnp.max(s[m], axis=0) + shift)
            l_new = l_old * jnp.exp(m_old - m_new) + jnp.sum(jnp.exp(s[m] - (m_new - shift)[None]), axis=0)
            out += [m_new, l_new]
        return tuple(out)

    init = (jnp.full((H, L), -jnp.inf, F32), jnp.zeros((H, L), F32)) * 2
    stats = stats_step(0, init) if n_chunks == 1 else lax.fori_loop(0, n_chunks, stats_step, init)

    kn = kn_ref[0].astype(BF16)
    vn = vn_ref[0].astype(BF16)
    tk = lax.broadcasted_iota(jnp.int32, (T * H, L), 0) // H
    allowed = ((past + tk) // CHUNK) <= ((past + t_lane) // CHUNK)
    bias_n = jnp.where(allowed, -slope * jnp.abs(tk - t_lane).astype(F32), -jnp.inf)
    s_n = [(s + bias_n).reshape(T, H, L) for s in both_maps(kn)]
    m_fin, coef = [], []
    for m in maps:
        m_run, l_run = stats[2 * m], stats[2 * m + 1]
        mf = jnp.maximum(m_run, jnp.max(s_n[m], axis=0))
        lf = l_run * jnp.exp(m_run - mf) + jnp.sum(jnp.exp(s_n[m] - mf[None]), axis=0)
        m_fin.append(mf)
        coef.append(jnp.where(valid, (1.0 if m == 0 else -lam) / lf, 0.0))

    def tn_dot(w, v):
        return lax.dot_general(w, v, (((0,), (0,)), ((), ())), preferred_element_type=F32)

    def out_step(c, acc):
        shift = chunk_shift(c)
        s = chunk_scores(c) if shared is None else shared
        r0 = c * rows if isinstance(c, int) else pl.multiple_of(c * rows, rows)
        vblk = vc_ref[0, pl.ds(r0, rows), :].astype(BF16)
        w = sum(jnp.exp(s[m] - (m_fin[m] - shift)[None]) * coef[m][None] for m in maps)
        return acc + tn_dot(w.reshape(rows, L).astype(BF16), vblk)

    acc0 = jnp.zeros((L, DIFF_VDIM), F32)
    acc = out_step(0, acc0) if n_chunks == 1 else lax.fori_loop(0, n_chunks, out_step, acc0)
    w_n = sum(jnp.exp(s_n[m] - m_fin[m][None]) * coef[m][None] for m in maps)
    acc = acc + tn_dot(w_n.reshape(T * H, L).astype(BF16), vn)
    o = _subln(acc, g_ref[...]).astype(o_ref.dtype)
    for h in range(H):
        o_ref[0, :, h * DIFF_VDIM:(h + 1) * DIFF_VDIM] = o[h * T:(h + 1) * T]


def _diff_attention_sample(q, cache_k, cache_v, k_new, v_new, lam_rows, subln_g):
    B, T, _ = q.shape
    P = cache_k.shape[1]
    H = N_DIFF_HEADS
    pc = min(P, 1024)
    assert H * T == 128 and P > 0 and P % pc == 0
    vec = pl.BlockSpec((1, DIFF_HEAD_DIM), lambda b: (0, 0))
    tok = pl.BlockSpec((1, T, DIFF_WIDTH), lambda b: (b, 0, 0))
    new = pl.BlockSpec((1, T * H, DIFF_VDIM), lambda b: (b, 0, 0))
    old = pl.BlockSpec((1, P * H, DIFF_VDIM), lambda b: (b, 0, 0))
    flat = lambda a: a.reshape(B, a.shape[1] * H, DIFF_VDIM)
    return pl.pallas_call(
        functools.partial(_diff_sample_kernel, past=P, pc=pc),
        out_shape=jax.ShapeDtypeStruct((B, T, DIFF_WIDTH), BF16),
        grid=(B,),
        in_specs=[vec, vec, vec, vec, pl.BlockSpec((1, DIFF_VDIM), lambda b: (0, 0)),
                  tok, old, old, new, new],
        out_specs=tok,
        compiler_params=_cparams(("parallel",)),
        name="diff_attn_sample",
    )(*lam_rows, subln_g, q, flat(cache_k), flat(cache_v), flat(k_new), flat(v_new))


def _gdn_kernel(*refs, c, cps, has_state):
    if has_state:
        (xq_ref, xk_ref, xv_ref, ab_ref, alog_ref, dtb_ref, z_ref, ng_ref, s0_ref, o_ref, s_ref) = refs
    else:
        (xq_ref, xk_ref, xv_ref, ab_ref, alog_ref, dtb_ref, z_ref, ng_ref, o_ref, s_ref) = refs
        s0_ref = None
    n = pl.program_id(1)
    H = N_GDN_HEADS

    @pl.when(n == 0)
    def _init():
        if has_state:
            s_ref[0] = s0_ref[0]
        else:
            s_ref[...] = jnp.zeros(s_ref.shape, F32)

    ab = ab_ref[0]
    a_in = ab[:, 0:H] + dtb_ref[...]
    softplus = jnp.maximum(a_in, 0.0) + jnp.log1p(jnp.exp(-jnp.abs(a_in)))
    g_col = -jnp.exp(alog_ref[...]) * softplus
    beta_col = _sigmoid(ab[:, H:2 * H])
    r = lax.broadcasted_iota(jnp.int32, (c, c), 0)
    s = lax.broadcasted_iota(jnp.int32, (c, c), 1)
    tri = r >= s
    strict = r > s
    tri_f = tri.astype(F32)
    eye_c = (r == s).astype(F32)
    eye_h = (lax.broadcasted_iota(jnp.int32, (H, H), 0) == lax.broadcasted_iota(jnp.int32, (H, H), 1)).astype(F32)

    def bf(a):
        return a.astype(BF16)

    def split(a):
        hi = a.astype(BF16)
        return hi, (a - hi.astype(F32)).astype(BF16)

    def dot3(a, b):
        (ah, al), (bh, bl) = a, b
        return (jnp.dot(al, bh, preferred_element_type=F32) + jnp.dot(ah, bl, preferred_element_type=F32)
                + jnp.dot(ah, bh, preferred_element_type=F32))

    blocks = []
    b_ = 1
    while b_ < c:
        blocks.append((((r // (2 * b_)) == (s // (2 * b_))) & ((r // b_) != (s // b_)) & strict).astype(F32))
        b_ *= 2

    rows = [slice(ci * c, (ci + 1) * c) for ci in range(cps)]
    lanes = [slice(hh * GDN_DK, (hh + 1) * GDN_DK) for hh in range(H)]
    gc_col = [jnp.dot(tri_f, g_col[rows[ci]], preferred_element_type=F32, precision=HI) for ci in range(cps)]
    gc_row = [lax.dot_general(eye_h, gc_col[ci], (((1,), (1,)), ((), ())), preferred_element_type=F32,
                              precision=HI) for ci in range(cps)]
    items = [(ci, hh) for ci in range(cps) for hh in range(H)]
    every = range(len(items))
    gcc = [gc_col[ci][:, hh:hh + 1] for ci, hh in items]
    bet = [beta_col[rows[ci], hh:hh + 1] for ci, hh in items]
    gcr = [gc_row[ci][hh:hh + 1, :] for ci, hh in items]
    g_last = [g[c - 1:c, :] for g in gcc]

    q = [xq_ref[0, rows[ci], lanes[hh]].astype(F32) for ci, hh in items]
    k = [xk_ref[0, rows[ci], lanes[hh]].astype(F32) for ci, hh in items]
    v = [xv_ref[0, rows[ci], lanes[hh]].astype(F32) for ci, hh in items]

    decay = [jnp.exp(jnp.where(tri, gcc[i] - gcr[i], -jnp.inf)) for i in every]
    kb = [k[i] * bet[i] for i in every]
    k16 = [bf(k[i]) for i in every]
    mmat = [jnp.where(strict, _nt_dot(bf(kb[i]), k16[i]) * decay[i], 0.0) for i in every]
    qk = [jnp.where(tri, _nt_dot(bf(q[i]), k16[i]) * decay[i], 0.0) for i in every]

    tinv = [eye_c - mmat[i] * blocks[0] for i in every]
    for lvl in range(1, len(blocks)):
        d16 = [bf(t) for t in tinv]
        x = [jnp.dot(bf(mmat[i] * blocks[lvl]), d16[i], preferred_element_type=F32) for i in every]
        tinv = [tinv[i] - jnp.dot(d16[i], bf(x[i]), preferred_element_type=F32) for i in every]
    m_s = [split(m) for m in mmat]
    t_s = [split(t) for t in tinv]
    res = [(eye_c - tinv[i]) - dot3(m_s[i], t_s[i]) for i in every]
    tinv = [tinv[i] + jnp.dot(t_s[i][0], bf(res[i]), preferred_element_type=F32) for i in every]

    rhs = [jnp.concatenate([v[i] * bet[i], kb[i] * jnp.exp(gcc[i])], axis=1) for i in every]
    sol = [dot3(split(tinv[i]), split(rhs[i])) for i in every]
    u = [x_[:, :GDN_DV] for x_ in sol]
    w16 = [bf(x_[:, GDN_DV:]) for x_ in sol]
    qg16 = [bf(q[i] * jnp.exp(gcc[i])) for i in every]
    qk16 = [bf(x_) for x_ in qk]
    kd16 = [bf(k[i] * jnp.exp(g_last[i] - gcc[i])) for i in every]
    e_last = [jnp.exp(g) for g in g_last]

    S = [s_ref[0, hh] for hh in range(H)]
    for ci in range(cps):
        of = ci * H
        S16 = [bf(x_) for x_ in S]
        v_new = [u[of + hh] - jnp.dot(w16[of + hh], S16[hh], preferred_element_type=F32) for hh in range(H)]
        v16 = [bf(x_) for x_ in v_new]
        o = [jnp.dot(qg16[of + hh], S16[hh], preferred_element_type=F32)
             + jnp.dot(qk16[of + hh], v16[hh], preferred_element_type=F32) for hh in range(H)]
        S = [S[hh] * e_last[of + hh] + lax.dot_general(kd16[of + hh], v16[hh], (((0,), (0,)), ((), ())),
                                                       preferred_element_type=F32) for hh in range(H)]
        for hh in range(H):
            ms = jnp.mean(o[hh] * o[hh], axis=-1, keepdims=True)
            y = o[hh] * lax.rsqrt(ms + NORM_EPS) * ng_ref[...] * z_ref[0, rows[ci], lanes[hh]].astype(F32)
            o_ref[0, rows[ci], lanes[hh]] = y.astype(o_ref.dtype)
    for hh in range(H):
        s_ref[0, hh] = S[hh]


def _gdn(q, k, v, gab, a_log, dt_bias, z_silu, norm_g, s0, *, c):
    B, T, _ = q.shape
    H = N_GDN_HEADS
    assert T % c == 0 and c % 8 == 0 and (c & (c - 1)) == 0
    cps = 2 if (T // c) % 2 == 0 else 1
    R = cps * c
    stream = pl.BlockSpec((1, R, GDN_WIDTH), lambda b, n: (b, n, 0))
    small = pl.BlockSpec((1, H), lambda b, n: (0, 0))
    state = pl.BlockSpec((1, H, GDN_DK, GDN_DV), lambda b, n: (b, 0, 0, 0))
    in_specs = [stream, stream, stream,
                pl.BlockSpec((1, R, 128), lambda b, n: (b, n, 0)), small, small,
                stream, pl.BlockSpec((1, GDN_DV), lambda b, n: (0, 0))]
    args = [q, k, v, gab, a_log, dt_bias, z_silu, norm_g]
    if s0 is not None:
        in_specs.append(state)
        args.append(s0)
    return pl.pallas_call(
        functools.partial(_gdn_kernel, c=c, cps=cps, has_state=s0 is not None),
        out_shape=(jax.ShapeDtypeStruct((B, T, GDN_WIDTH), BF16),
                   jax.ShapeDtypeStruct((B, H, GDN_DK, GDN_DV), F32)),
        grid=(B, T // R),
        in_specs=in_specs,
        out_specs=(stream, state),
        compiler_params=_cparams(("parallel", "arbitrary")),
        name="gdn",
    )(*args)


def _mix_kernel(oa_ref, ob_ref, wa_ref, wb_ref, sa_ref, sb_ref, o_ref):
    a = jnp.dot(oa_ref[...], wa_ref[0].astype(BF16), preferred_element_type=F32)
    b = jnp.dot(ob_ref[...], wb_ref[0].astype(BF16), preferred_element_type=F32)
    o_ref[...] = (sa_ref[...].astype(F32) * a + sb_ref[...].astype(F32) * b).astype(o_ref.dtype)


def _mix(o_a, o_b, w_pa, w_pb, gates, *, tm=1024, tn=512):
    M = o_a.shape[0]
    tm = min(tm, M)
    nb = D_MODEL // tn
    assert w_pa.shape[0] == w_pb.shape[0] == DEPTH == 1 and M % tm == 0
    return pl.pallas_call(
        _mix_kernel,
        out_shape=jax.ShapeDtypeStruct((M, D_MODEL), BF16),
        grid=(nb, M // tm),
        in_specs=[pl.BlockSpec((tm, DIFF_WIDTH), lambda j, i: (i, 0)),
                  pl.BlockSpec((tm, GDN_WIDTH), lambda j, i: (i, 0)),
                  pl.BlockSpec((1, DIFF_WIDTH, tn), lambda j, i: (0, 0, j)),
                  pl.BlockSpec((1, GDN_WIDTH, tn), lambda j, i: (0, 0, j)),
                  pl.BlockSpec((tm, tn), lambda j, i: (i, j)),
                  pl.BlockSpec((tm, tn), lambda j, i: (i, j + nb))],
        out_specs=pl.BlockSpec((tm, tn), lambda j, i: (i, j)),
        compiler_params=_cparams(("parallel", "parallel")),
        name="mix",
    )(o_a, o_b, w_pa, w_pb, gates, gates)


def _post_kernel(x_ref, mix_ref, wo_ref, g1_ref, b1_ref, wxq_ref, mk_ref, mv_ref, wxo_ref, g2_ref, b2_ref,
                 h2_ref, h2b_ref):
    nb, tm, D = x_ref.shape
    x = x_ref[...].reshape(nb * tm, D)
    mix = mix_ref[...].reshape(nb * tm, D)
    h1 = ALPHA * x + jnp.dot(mix, wo_ref[...], preferred_element_type=F32)
    h1 = _layer_norm(h1, g1_ref[...], b1_ref[...])
    qx = jnp.dot(h1.astype(BF16), wxq_ref[...], preferred_element_type=F32) * (XHEAD_DIM ** -0.5)
    qx = qx.astype(BF16)
    seqs = []
    for b in range(nb):
        rows = slice(b * tm, (b + 1) * tm)
        heads = []
        for hh in range(N_XHEADS):
            sl = slice(hh * XHEAD_DIM, (hh + 1) * XHEAD_DIM)
            s = _nt_dot(qx[rows, sl], mk_ref[b, :, sl])
            p = jnp.exp(s - jnp.max(s, axis=-1, keepdims=True))
            p = p / jnp.sum(p, axis=-1, keepdims=True)
            heads.append(jnp.dot(p.astype(BF16), mv_ref[b, :, sl], preferred_element_type=F32))
        seqs.append(jnp.concatenate(heads, axis=1))
    ox = jnp.concatenate(seqs, axis=0).astype(BF16)
    h2 = ALPHA * h1 + jnp.dot(ox, wxo_ref[...], preferred_element_type=F32)
    h2 = _layer_norm(h2, g2_ref[...], b2_ref[...])
    h2_ref[...] = h2.reshape(nb, tm, D)
    h2b_ref[...] = h2.astype(BF16).reshape(nb, tm, D)


def _post(x, mix, w_o, ln1_g, ln1_b, w_xq, mem_k, mem_v, w_xo, ln2_g, ln2_b, *, tm=512):
    B, T, D = x.shape
    tm = min(tm, T)
    nb = min(B, max(1, 256 // tm))
    assert B % nb == 0 and T % tm == 0 and tm % 16 == 0
    const = lambda shape: pl.BlockSpec(shape, lambda b, i: (0, 0), pipeline_mode=pl.Buffered(1))
    rows = lambda: pl.BlockSpec((nb, tm, D), lambda b, i: (b, i, 0))
    mem = lambda: pl.BlockSpec((nb, N_MEM, XWIDTH), lambda b, i: (b, 0, 0))
    return pl.pallas_call(
        _post_kernel,
        out_shape=(jax.ShapeDtypeStruct((B, T, D), F32), jax.ShapeDtypeStruct((B, T, D), BF16)),
        grid=(B // nb, T // tm),
        in_specs=[rows(), rows(), const((D, D)), const((1, D)), const((1, D)), const((D, XWIDTH)),
                  mem(), mem(), const((XWIDTH, D)), const((1, D)), const((1, D))],
        out_specs=(rows(), rows()),
        compiler_params=_cparams(("parallel", "parallel")),
        name="post_attn",
    )(x, mix, w_o, ln1_g, ln1_b, w_xq, mem_k, mem_v, w_xo, ln2_g, ln2_b)


def _ffn_kernel(hb_ref, h_ref, w1_ref, w3_ref, w2_ref, g_ref, b_ref, y_ref, acc_ref):
    f = pl.program_id(1)

    @pl.when(f == 0)
    def _():
        acc_ref[...] = jnp.zeros(acc_ref.shape, F32)

    hb = hb_ref[...]
    a = jnp.dot(hb, w1_ref[...], preferred_element_type=F32)
    b = jnp.dot(hb, w3_ref[...], preferred_element_type=F32)
    act = (_silu(a) * b).astype(BF16)
    acc_ref[...] += jnp.dot(act, w2_ref[...], preferred_element_type=F32)

    @pl.when(f == pl.num_programs(1) - 1)
    def _():
        y_ref[...] = _layer_norm(ALPHA * h_ref[...] + acc_ref[...], g_ref[...], b_ref[...])


def _ffn(h2b, h2, w1, w3, w2, ln_g, ln_b, *, tm=512, tf=512):
    M, D = h2.shape
    tm = min(tm, M)
    assert M % tm == 0 and D_FF % tf == 0
    return pl.pallas_call(
        _ffn_kernel,
        out_shape=jax.ShapeDtypeStruct((M, D), F32),
        grid=(M // tm, D_FF // tf),
        in_specs=[pl.BlockSpec((tm, D), lambda i, f: (i, 0)),
                  pl.BlockSpec((tm, D), lambda i, f: (i, 0)),
                  pl.BlockSpec((D, tf), lambda i, f: (0, f)),
                  pl.BlockSpec((D, tf), lambda i, f: (0, f)),
                  pl.BlockSpec((tf, D), lambda i, f: (f, 0)),
                  pl.BlockSpec((1, D), lambda i, f: (0, 0)),
                  pl.BlockSpec((1, D), lambda i, f: (0, 0))],
        out_specs=pl.BlockSpec((tm, D), lambda i, f: (i, 0)),
        scratch_shapes=[pltpu.VMEM((tm, D), F32)],
        compiler_params=_cparams(("parallel", "arbitrary")),
        name="ffn",
    )(h2b, h2, w1, w3, w2, ln_g, ln_b)


def _encoder_layer(x, mem_k, mem_v, past, W):
    B, T, D = x.shape
    M = B * T
    w_in = W["w_in_t"]
    xb, dq = _proj(x.reshape(M, D), w_in, OFF_DQ, DIFF_WIDTH, (BF16,), scale=DIFF_HEAD_DIM ** -0.5, w_rows=True,
                   emit_x=True, name="proj_dq")
    in_proj = functools.partial(_proj, xb, w_in, w_rows=True)
    kv_dtypes = (F32, BF16) if past is None else (F32,)
    dk, *dkb = in_proj(OFF_DK, DIFF_WIDTH, kv_dtypes, rows_inner=True, name="proj_dk")
    dv, *dvb = in_proj(OFF_DV, DIFF_WIDTH, kv_dtypes, rows_inner=True, name="proj_dv")
    buf0 = jnp.zeros((B, CONV_W - 1, 3 * GDN_WIDTH), F32) if past is None else past[3]
    streams, tails = [], []
    for t, (norm, scale) in enumerate(((True, GDN_DK ** -0.5), (True, 1.0), (False, 1.0))):
        y, tail = _proj_conv(xb, w_in, OFF_GQKV + t * GDN_WIDTH, buf0, W["conv_w"], B=B, T=T, norm=norm,
                             scale=scale, name="proj_conv_" + "qkv"[t])
        streams.append(y.reshape(B, T, GDN_WIDTH))
        tails.append(tail[:, CONV_PAD - (CONV_W - 1):, :])
    (gz,) = in_proj(OFF_GZ, GDN_WIDTH, (BF16,), act="silu", name="proj_gz")
    (gab,) = in_proj(OFF_GAB, 128, (F32,), tn=128, name="proj_gab")
    (gates,) = in_proj(OFF_GATES, 2 * D_MODEL, (BF16,), act="sigmoid", name="proj_gates")

    dq = dq.reshape(B, T, DIFF_WIDTH)
    dk = dk.reshape(B, T, N_DIFF_HEADS, DIFF_VDIM)
    dv = dv.reshape(B, T, N_DIFF_HEADS, DIFF_VDIM)
    lam_rows = W["lam_rows"]
    if past is None:
        o_a, W["w_ff_bf16"] = _diff_attention_prompt(dq, dkb[0].reshape(B, T, DIFF_WIDTH),
                                                     dvb[0].reshape(B, T, DIFF_WIDTH), lam_rows, W["diff_subln_g"],
                                                     cast=W["w_ff"])
        s0 = None
        c = CHUNK
    else:
        cache_k, cache_v, s0, _ = past
        o_a = _diff_attention_sample(dq, cache_k, cache_v, dk, dv, lam_rows, W["diff_subln_g"])
        c = T
    o_b, s_new = _gdn(*streams, gab.reshape(B, T, 128), W["gdn_a_log"], W["gdn_dt_bias"],
                      gz.reshape(B, T, GDN_WIDTH), W["gdn_norm_g"], s0, c=c)
    new_buf = jnp.concatenate(tails, axis=-1)

    mix = _mix(o_a.reshape(M, DIFF_WIDTH), o_b.reshape(M, GDN_WIDTH), W["w_pa"], W["w_pb"], gates)
    h2, h2b = _post(x, mix.reshape(B, T, D), W["w_o"], W["ln1_g"], W["ln1_b"], W["w_xq"], mem_k, mem_v,
                    W["w_xo"], W["ln2_g"], W["ln2_b"])
    y = _ffn(h2b.reshape(M, D), h2.reshape(M, D), *W["w_ff_bf16"], W["ln3_g"], W["ln3_b"])
    return y.reshape(B, T, D), dk, dv, s_new, new_buf


def kernel(x_prompt, x_sample, mem_prompt, cache_diff_k, cache_diff_v, state_gdn, state_gdn_conv, cache_mem_k, cache_mem_v, w_in, conv_w, lam_q1, lam_k1, lam_q2, lam_k2, diff_subln_g, gdn_a_log, gdn_dt_bias, gdn_norm_g, w_pa, w_pb, w_o, ln1_g, ln1_b, w_xq, w_xk, w_xv, w_xo, ln2_g, ln2_b, w_ff1, w_ff3, w_ff2, ln3_g, ln3_b):
    l = 0
    W = {
        "w_in_t": jnp.swapaxes(w_in, 1, 2),
        "conv_w": conv_w[l],
        "lam_rows": tuple(v[l].reshape(1, DIFF_HEAD_DIM) for v in (lam_q1, lam_k1, lam_q2, lam_k2)),
        "diff_subln_g": diff_subln_g[l].reshape(1, DIFF_VDIM),
        "gdn_a_log": gdn_a_log[l].reshape(1, N_GDN_HEADS),
        "gdn_dt_bias": gdn_dt_bias[l].reshape(1, N_GDN_HEADS),
        "gdn_norm_g": gdn_norm_g[l].reshape(1, GDN_DV),
        "w_pa": w_pa, "w_pb": w_pb, "w_o": w_o[l].astype(BF16),
        "ln1_g": ln1_g[l].reshape(1, D_MODEL), "ln1_b": ln1_b[l].reshape(1, D_MODEL),
        "w_xq": w_xq[l].astype(BF16), "w_xo": w_xo[l].astype(BF16),
        "ln2_g": ln2_g[l].reshape(1, D_MODEL), "ln2_b": ln2_b[l].reshape(1, D_MODEL),
        "w_ff": (w_ff1, w_ff3, w_ff2),
        "ln3_g": ln3_g[l].reshape(1, D_MODEL), "ln3_b": ln3_b[l].reshape(1, D_MODEL),
    }
    Bp = x_prompt.shape[0]
    memb = mem_prompt.reshape(Bp * N_MEM, D_MODEL).astype(BF16)
    mem_k, mem_kb = _proj(memb, w_xk, 0, XWIDTH, (F32, BF16), tn=XWIDTH, name="proj_mem_k")
    mem_v, mem_vb = _proj(memb, w_xv, 0, XWIDTH, (F32, BF16), tn=XWIDTH, name="proj_mem_v")

    yp, pk, pv, ps, pc = _encoder_layer(x_prompt, mem_kb.reshape(Bp, N_MEM, XWIDTH),
                                        mem_vb.reshape(Bp, N_MEM, XWIDTH), None, W)
    Bs = x_sample.shape[0]
    past = (cache_diff_k[l], cache_diff_v[l], state_gdn[l], state_gdn_conv[l])
    ys, sk, sv, ss, sc = _encoder_layer(x_sample, cache_mem_k[l].reshape(Bs, N_MEM, XWIDTH).astype(BF16),
                                        cache_mem_v[l].reshape(Bs, N_MEM, XWIDTH).astype(BF16), past, W)
    st = lambda a: a[None]
    return (yp, ys, st(pk), st(pv), st(ps), st(pc),
            st(mem_k.reshape(Bp, N_MEM, N_XHEADS, XHEAD_DIM)), st(mem_v.reshape(Bp, N_MEM, N_XHEADS, XHEAD_DIM)),
            st(sk), st(sv), st(ss), st(sc))
```

```python
import functools
import math

import jax
import jax.numpy as jnp
from jax import lax
from jax.experimental import pallas as pl
from jax.experimental.pallas import tpu as pltpu

D_MODEL = 2048
CHUNK = 64
N_DIFF_HEADS = 8
DIFF_HEAD_DIM = 128
DIFF_VDIM = 2 * DIFF_HEAD_DIM
DIFF_WIDTH = N_DIFF_HEADS * DIFF_VDIM
N_GDN_HEADS = 16
GDN_DK = 128
GDN_DV = 128
GDN_WIDTH = N_GDN_HEADS * GDN_DK
CONV_W = 4
N_MEM = 256
N_XHEADS = 4
XHEAD_DIM = 128
XWIDTH = N_XHEADS * XHEAD_DIM
D_FF = 5632
DEPTH = 1
ALPHA = (2.0 * DEPTH) ** 0.25
LN_EPS = 1e-5
NORM_EPS = 1e-6
LAM_INIT = 0.8 - 0.6 * math.exp(-0.3 * 0)

OFF_DQ = 0
OFF_DK = 2048
OFF_DV = 4096
OFF_GQKV = 6144
OFF_GZ = 12288
OFF_GAB = 14336
OFF_GATES = 14368

VMEM_LIMIT = 56 * 1024 * 1024
BF16 = jnp.bfloat16
F32 = jnp.float32
HI = lax.Precision.HIGHEST


def _cparams(sem):
    return pltpu.CompilerParams(dimension_semantics=sem, vmem_limit_bytes=VMEM_LIMIT)


def _sigmoid(x):
    return 1.0 / (1.0 + jnp.exp(-x))


def _silu(x):
    return x * _sigmoid(x)


def _layer_norm(x, g, b):
    mu = jnp.mean(x, axis=-1, keepdims=True)
    xc = x - mu
    var = jnp.mean(xc * xc, axis=-1, keepdims=True)
    return xc * lax.rsqrt(var + LN_EPS) * g + b


def _proj_kernel(x_ref, w_ref, *o_refs, act, scale, w_rows, emit_x):
    w = w_ref[0].astype(BF16)
    x = x_ref[...].astype(BF16)
    if emit_x:
        o_refs[0][...] = x
        o_refs = o_refs[1:]
    acc = _nt_dot(x, w) if w_rows else jnp.dot(x, w, preferred_element_type=F32)
    if scale != 1.0:
        acc = acc * scale
    if act == "sigmoid":
        acc = _sigmoid(acc)
    elif act == "silu":
        acc = _silu(acc)
    for o in o_refs:
        o[...] = acc.astype(o.dtype)


def _weight_spec(w, K, tn, col_start, w_rows):
    assert w.shape[0] == DEPTH == 1
    if w_rows:
        return pl.BlockSpec((pl.Element(1), pl.Element(tn), pl.Element(K)),
                            lambda a, b: (0, pl.multiple_of(col_start(a, b), 8), 0))
    return pl.BlockSpec((1, K, tn), lambda a, b: (0, 0, col_start(a, b) // tn))


def _proj(x, w, col_off, n_cols, out_dtypes, *, act=None, scale=1.0, tm=1024, tn=1024, rows_inner=False,
          w_rows=False, emit_x=False, name="proj"):
    M, K = x.shape
    tm = min(tm, M)
    tn = min(tn, n_cols)
    assert M % tm == 0 and n_cols % tn == 0 and col_off % (8 if w_rows else tn) == 0
    if rows_inner:
        grid = (n_cols // tn, M // tm)
        ij = lambda a, b: (b, a)
    else:
        grid = (M // tm, n_cols // tn)
        ij = lambda a, b: (a, b)
    x_spec = pl.BlockSpec((tm, K), lambda a, b: (ij(a, b)[0], 0))
    out_shape = [jax.ShapeDtypeStruct((M, n_cols), dt) for dt in out_dtypes]
    out_specs = [pl.BlockSpec((tm, tn), lambda a, b: ij(a, b)) for _ in out_dtypes]
    if emit_x:
        assert not rows_inner
        out_shape.insert(0, jax.ShapeDtypeStruct((M, K), BF16))
        out_specs.insert(0, x_spec)
    outs = pl.pallas_call(
        functools.partial(_proj_kernel, act=act, scale=scale, w_rows=w_rows, emit_x=emit_x),
        out_shape=tuple(out_shape),
        grid=grid,
        in_specs=[x_spec, _weight_spec(w, K, tn, lambda a, b: col_off + ij(a, b)[1] * tn, w_rows)],
        out_specs=tuple(out_specs),
        compiler_params=_cparams(("parallel", "arbitrary") if emit_x else ("parallel", "parallel")),
        name=name,
    )(x, w)
    return outs


CONV_PAD = 8


def _proj_conv_kernel(x_ref, w_ref, hist_ref, cw_ref, y_ref, raw_ref, pad_ref, *, nb, T, sub, tr_max, norm, scale):
    tn = w_ref.shape[1]
    tr = min(T, tr_max) if nb == 1 else T
    units = [(s, r) for s in range(tn // sub) for r in range(T // tr)]
    w_slabs = [w_ref[0, s * sub:(s + 1) * sub, :].astype(BF16) for s in range(tn // sub)]

    def rows_of(r):
        return slice(r * tr, (r + 1) * tr) if nb == 1 else slice(None)

    def matmul(s, r, gate):
        rows = rows_of(r)
        if gate is None:
            lhs = x_ref[rows, :]
        else:
            n_rows = nb * tr
            first = x_ref[rows, 0:256] + jnp.concatenate([jnp.concatenate([gate] * 2, axis=1)] * (n_rows // 16), axis=0)
            lhs = jnp.concatenate([first, x_ref[rows, 256:]], axis=1)
        return _nt_dot(lhs, w_slabs[s])

    def epilogue(s, r, acc):
        cols = slice(s * sub, (s + 1) * sub)
        slot = s % 2
        r0 = r * tr
        a3 = acc.reshape(nb, tr, sub)
        pad_ref[slot, :, CONV_PAD + r0:CONV_PAD + r0 + tr, :] = a3
        if r == 0:
            pad_ref[slot, :, CONV_PAD - (CONV_W - 1):CONV_PAD, :] = hist_ref[:, :, cols]
        if r == T // tr - 1:
            raw_ref[:, :, cols] = a3[:, tr - CONV_PAD:, :]
        y = cw_ref[CONV_W - 1:CONV_W, cols] * a3
        for j in range(CONV_W - 1):
            lo = CONV_PAD - (CONV_W - 1) + j + r0
            y = y + cw_ref[j:j + 1, cols] * pad_ref[slot, :, lo:lo + tr, :]
        y = _silu(y)
        if norm:
            heads = []
            for hh in range(sub // GDN_DK):
                yh = y[:, :, hh * GDN_DK:(hh + 1) * GDN_DK]
                heads.append(yh * (lax.rsqrt(jnp.sum(yh * yh, axis=-1, keepdims=True) + NORM_EPS) * scale))
            y = jnp.concatenate(heads, axis=-1)
        y2 = y.reshape(nb * tr, sub)
        y_ref[rows_of(r), cols] = y2.astype(y_ref.dtype)
        bits = pltpu.bitcast(y2[nb * tr - 16:, 0:128], jnp.uint32)
        zero = lax.shift_right_logical(lax.shift_right_logical(bits, jnp.uint32(16)), jnp.uint32(16))
        return zero.astype(F32).astype(BF16)

    gates = [None, None]
    acc = matmul(*units[0], None)
    for i, u in enumerate(units):
        nxt = matmul(*units[i + 1], gates[i + 1]) if i + 1 < len(units) else None
        gates.append(epilogue(*u, acc))
        acc = nxt


def _proj_conv(x, w, col_off, hist, conv_w, *, B, T, norm, scale=1.0, tn=512, sub=512, tr=512, name="proj_conv"):
    M, K = x.shape
    n_cols = GDN_WIDTH
    nb = max(1, 256 // T) if T < 256 else 1
    nb = min(nb, B)
    tm = nb * T
    gq = OFF_GQKV
    assert B % nb == 0 and n_cols % tn == 0 and col_off % 8 == 0 and (col_off - gq) % tn == 0 and T >= CONV_PAD
    hb = (col_off - gq) // tn
    return pl.pallas_call(
        functools.partial(_proj_conv_kernel, nb=nb, T=T, sub=sub, tr_max=tr, norm=norm, scale=scale),
        out_shape=(jax.ShapeDtypeStruct((M, n_cols), BF16), jax.ShapeDtypeStruct((B, CONV_PAD, n_cols), F32)),
        grid=(B // nb, n_cols // tn),
        in_specs=[pl.BlockSpec((tm, K), lambda i, j: (i, 0)),
                  _weight_spec(w, K, tn, lambda i, j: col_off + j * tn, True),
                  pl.BlockSpec((nb, CONV_W - 1, tn), lambda i, j: (i, 0, j + hb)),
                  pl.BlockSpec((CONV_W, tn), lambda i, j: (0, j + hb))],
        out_specs=(pl.BlockSpec((tm, tn), lambda i, j: (i, j)),
                   pl.BlockSpec((nb, CONV_PAD, tn), lambda i, j: (i, 0, j))),
        scratch_shapes=[pltpu.VMEM((min(2, tn // sub), nb, T + CONV_PAD, sub), F32)],
        compiler_params=_cparams(("parallel", "parallel")),
        name=name,
    )(x, w, hist, conv_w)


def _lam_value(lq1, lk1, lq2, lk2):
    a = jnp.sum(lq1 * lk1, axis=-1, keepdims=True)
    b = jnp.sum(lq2 * lk2, axis=-1, keepdims=True)
    return jnp.exp(a) - jnp.exp(b) + LAM_INIT


def _subln(o, g):
    ms = jnp.mean(o * o, axis=-1, keepdims=True)
    return o * lax.rsqrt(ms + NORM_EPS) * g * (1.0 - LAM_INIT)


def _head_slope(h):
    e = (h + 1).astype(F32) * (-8.0 / N_DIFF_HEADS)
    return jnp.exp2(jnp.full((1, 1), e, F32))


def _nt_dot(a, b):
    return lax.dot_general(a, b, (((1,), (1,)), ((), ())), preferred_element_type=F32)


def _lane_tile(x, n):
    return x if n == 1 else jnp.concatenate([x] * n, axis=1)


def _diff_prompt_kernel(lq1_ref, lk1_ref, lq2_ref, lk2_ref, g_ref, base_ref, q_ref, k_ref, v_ref, *rest, tq, hps,
                        n_cast):
    cast_in, o_ref, cast_out = rest[:n_cast], rest[n_cast], rest[n_cast + 1:2 * n_cast + 1]
    m_ref, l_ref, acc_ref = rest[2 * n_cast + 1:]
    for src, dst in zip(cast_in, cast_out):
        dst[...] = src[0].astype(dst.dtype)
    LANES = 128
    half = tq // 2
    hg = pl.program_id(1)
    qi = pl.program_id(2)
    slope = [_head_slope(hg * hps + hh) for hh in range(hps)]
    lam = _lam_value(lq1_ref[...], lk1_ref[...], lq2_ref[...], lk2_ref[...])
    col_iota = lax.broadcasted_iota(jnp.int32, (1, tq), 1).astype(F32)
    col_bias = [sl * col_iota for sl in slope]
    maps = range(2)

    m_ref[...] = jnp.full(m_ref.shape, -jnp.inf, F32)
    l_ref[...] = jnp.zeros(l_ref.shape, F32)
    acc_ref[...] = jnp.zeros(acc_ref.shape, F32)

    def cols(hh, m):
        c0 = hh * DIFF_VDIM + m * DIFF_HEAD_DIM
        return slice(c0, c0 + DIFF_HEAD_DIM)

    def update(*jobs):
        todo = [(job, m) for job in jobs for m in maps]
        t = [_nt_dot(q_ref[0, rows, cols(hh, m)], k_ref[0, pl.ds(start, nk), cols(hh, m)]) + bias
             for (hh, rows, start, nk, bias, _), m in todo]
        st = [2 * job[0] + m for job, m in todo]
        m_old = [m_ref[st[i], job[1]] for i, (job, m) in enumerate(todo)]
        m_new = [jnp.maximum(m_old[i], jnp.max(t[i], axis=-1, keepdims=True) + job[5])
                 for i, (job, m) in enumerate(todo)]
        a = [jnp.exp(m_old[i] - m_new[i]) for i in range(len(todo))]
        p = [jnp.exp(t[i] - _lane_tile(m_new[i] - job[5], job[3] // LANES)) for i, (job, m) in enumerate(todo)]
        pv = [jnp.dot(p[i].astype(BF16), v_ref[0, pl.ds(start, nk), hh * DIFF_VDIM:(hh + 1) * DIFF_VDIM],
                      preferred_element_type=F32) for i, ((hh, _, start, nk, _, _), m) in enumerate(todo)]
        for i, ((hh, rows, start, nk, _, _), m) in enumerate(todo):
            psum = p[i][:, :LANES]
            for c in range(1, nk // LANES):
                psum = psum + p[i][:, c * LANES:(c + 1) * LANES]
            l_ref[st[i], rows] = a[i] * l_ref[st[i], rows] + psum
            acc_ref[st[i], rows] = _lane_tile(a[i], DIFF_VDIM // LANES) * acc_ref[st[i], rows] + pv[i]
            m_ref[st[i], rows] = m_new[i]

    every = slice(0, tq)

    def body(j, carry):
        start = pl.multiple_of(j * tq, tq)
        gap = ((qi - j) * tq).astype(F32)
        update(*[(hh, every, start, tq, col_bias[hh], -slope[hh] * gap) for hh in range(hps)])
        return carry

    lax.fori_loop(0, qi, body, 0)

    lo, hi = slice(0, half), slice(half, tq)
    start = pl.multiple_of(qi * tq, tq)
    zero = jnp.zeros((1, 1), F32)
    jobs = []
    for hh in range(hps):
        jobs.append((hh, lo, start, half, slope[hh] * base_ref[:, half:], -slope[hh] * float(half)))
        jobs.append((hh, hi, start, tq, slope[hh] * base_ref[...], zero))
    update(*jobs)

    for hh in range(hps):
        outs = []
        for m in maps:
            inv = 1.0 / jnp.sum(l_ref[2 * hh + m], axis=-1, keepdims=True)
            outs.append(acc_ref[2 * hh + m] * inv)
        o = outs[0] - lam * outs[1]
        o_ref[0, :, hh * DIFF_VDIM:(hh + 1) * DIFF_VDIM] = _subln(o, g_ref[...]).astype(o_ref.dtype)


def _diff_attention_prompt(q, k, v, lam_rows, subln_g, cast=(), *, tq=512, hps=2):
    B, T, _ = q.shape
    tq = min(tq, T)
    half = tq // 2
    assert T % tq == 0 and half % CHUNK == 0 and N_DIFF_HEADS % hps == 0
    ii = lax.broadcasted_iota(jnp.int32, (half, tq), 0)
    jj = lax.broadcasted_iota(jnp.int32, (half, tq), 1)
    jh = jj - half
    diag = jnp.where((jh // CHUNK) <= (ii // CHUNK), (half + ii - jnp.abs(ii - jh)).astype(F32), -jnp.inf)
    base = jnp.where(jj < half, jj.astype(F32), diag)
    vec = pl.BlockSpec((1, DIFF_HEAD_DIM), lambda b, h, i: (0, 0))
    wide = hps * DIFF_VDIM
    nh, nq = N_DIFF_HEADS // hps, T // tq
    n_steps = B * nh * nq
    step = lambda b, h, i: (b * nh + h) * nq + i
    cast_in, cast_out, cast_shapes = [], [], []
    for w in cast:
        _, R, C = w.shape
        per = next(p for p in (1, 2, 4, 8) if R % (n_steps // p) == 0 and (R // (n_steps // p)) % 16 == 0)
        rows = R // (n_steps // per)
        cast_in.append(pl.BlockSpec((1, rows, C), lambda b, h, i, per=per: (0, step(b, h, i) // per, 0)))
        cast_out.append(pl.BlockSpec((rows, C), lambda b, h, i, per=per: (step(b, h, i) // per, 0)))
        cast_shapes.append(jax.ShapeDtypeStruct((R, C), BF16))
    outs = pl.pallas_call(
        functools.partial(_diff_prompt_kernel, tq=tq, hps=hps, n_cast=len(cast)),
        out_shape=(jax.ShapeDtypeStruct((B, T, DIFF_WIDTH), BF16), *cast_shapes),
        grid=(B, nh, nq),
        in_specs=[vec, vec, vec, vec,
                  pl.BlockSpec((1, DIFF_VDIM), lambda b, h, i: (0, 0)),
                  pl.BlockSpec((half, tq), lambda b, h, i: (0, 0)),
                  pl.BlockSpec((1, tq, wide), lambda b, h, i: (b, i, h)),
                  pl.BlockSpec((1, T, wide), lambda b, h, i: (b, 0, h)),
                  pl.BlockSpec((1, T, wide), lambda b, h, i: (b, 0, h)), *cast_in],
        out_specs=(pl.BlockSpec((1, tq, wide), lambda b, h, i: (b, i, h)), *cast_out),
        scratch_shapes=[pltpu.VMEM((2 * hps, tq, 128), F32), pltpu.VMEM((2 * hps, tq, 128), F32),
                        pltpu.VMEM((2 * hps, tq, DIFF_VDIM), F32)],
        compiler_params=_cparams(("arbitrary", "arbitrary", "arbitrary")),
        name="diff_attn_prompt",
    )(*lam_rows, subln_g, base, q, k, v, *cast)
    return outs[0], tuple(outs[1:])


def _diff_sample_kernel(lq1_ref, lk1_ref, lq2_ref, lk2_ref, g_ref, q_ref, kc_ref, vc_ref, kn_ref, vn_ref,
                        o_ref, *, past, pc):
    H = N_DIFF_HEADS
    T = q_ref.shape[1]
    L = H * T
    rows = pc * H
    n_chunks = past // pc
    lam = _lam_value(lq1_ref[...], lk1_ref[...], lq2_ref[...], lk2_ref[...])
    maps = range(2)

    def map_cols(m):
        return slice(m * DIFF_HEAD_DIM, (m + 1) * DIFF_HEAD_DIM)

    qm = [jnp.concatenate([q_ref[0, :, h * DIFF_VDIM + m * DIFF_HEAD_DIM:h * DIFF_VDIM + (m + 1) * DIFF_HEAD_DIM]
                           for h in range(H)], axis=0) for m in maps]
    lane = lax.broadcasted_iota(jnp.int32, (1, L), 1)
    h_lane = lane // T
    t_lane = lane % T
    slope = jnp.exp2((h_lane + 1).astype(F32) * (-8.0 / H))
    base = slope * (lax.broadcasted_iota(jnp.int32, (rows, L), 0) // H).astype(F32)
    valid = lax.broadcasted_iota(jnp.int32, (H, L), 0) == h_lane

    def chunk_shift(c):
        return slope * ((c * pc - past) - t_lane).astype(F32)

    zq = jnp.zeros((L, DIFF_HEAD_DIM), BF16)
    q_both = jnp.concatenate([jnp.concatenate([qm[0], zq], axis=1), jnp.concatenate([zq, qm[1]], axis=1)], axis=0)

    def both_maps(keys):
        s2 = _nt_dot(keys, q_both)
        return [s2[:, m * L:(m + 1) * L] for m in maps]

    def chunk_scores(c):
        r0 = c * rows if isinstance(c, int) else pl.multiple_of(c * rows, rows)
        s2 = both_maps(kc_ref[0, pl.ds(r0, rows), :].astype(BF16))
        return [(s2[m] + base).reshape(pc, H, L) for m in maps]

    shared = chunk_scores(0) if n_chunks == 1 else None

    def stats_step(c, carry):
        shift = chunk_shift(c)
        s = chunk_scores(c) if shared is None else shared
        out = []
        for m in maps:
            m_old, l_old = carry[2 * m], carry[2 * m + 1]
            m_new = jnp.maximum(m_old, jnp.max(s[m], axis=0) + shift)
            l_new = l_old * jnp.exp(m_old - m_new) + jnp.sum(jnp.exp(s[m] - (m_new - shift)[None]), axis=0)
            out += [m_new, l_new]
        return tuple(out)

    init = (jnp.full((H, L), -jnp.inf, F32), jnp.zeros((H, L), F32)) * 2
    stats = stats_step(0, init) if n_chunks == 1 else lax.fori_loop(0, n_chunks, stats_step, init)

    kn = kn_ref[0].astype(BF16)
    vn = vn_ref[0].astype(BF16)
    tk = lax.broadcasted_iota(jnp.int32, (T * H, L), 0) // H
    allowed = ((past + tk) // CHUNK) <= ((past + t_lane) // CHUNK)
    bias_n = jnp.where(allowed, -slope * jnp.abs(tk - t_lane).astype(F32), -jnp.inf)
    s_n = [(s + bias_n).reshape(T, H, L) for s in both_maps(kn)]
    m_fin, coef = [], []
    for m in maps:
        m_run, l_run = stats[2 * m], stats[2 * m + 1]
        mf = jnp.maximum(m_run, jnp.max(s_n[m], axis=0))
        lf = l_run * jnp.exp(m_run - mf) + jnp.sum(jnp.exp(s_n[m] - mf[None]), axis=0)
        m_fin.append(mf)
        coef.append(jnp.where(valid, (1.0 if m == 0 else -lam) / lf, 0.0))

    def tn_dot(w, v):
        return lax.dot_general(w, v, (((0,), (0,)), ((), ())), preferred_element_type=F32)

    def out_step(c, acc):
        shift = chunk_shift(c)
        s = chunk_scores(c) if shared is None else shared
        r0 = c * rows if isinstance(c, int) else pl.multiple_of(c * rows, rows)
        vblk = vc_ref[0, pl.ds(r0, rows), :].astype(BF16)
        w = sum(jnp.exp(s[m] - (m_fin[m] - shift)[None]) * coef[m][None] for m in maps)
        return acc + tn_dot(w.reshape(rows, L).astype(BF16), vblk)

    acc0 = jnp.zeros((L, DIFF_VDIM), F32)
    acc = out_step(0, acc0) if n_chunks == 1 else lax.fori_loop(0, n_chunks, out_step, acc0)
    w_n = sum(jnp.exp(s_n[m] - m_fin[m][None]) * coef[m][None] for m in maps)
    acc = acc + tn_dot(w_n.reshape(T * H, L).astype(BF16), vn)
    o = _subln(acc, g_ref[...]).astype(o_ref.dtype)
    for h in range(H):
        o_ref[0, :, h * DIFF_VDIM:(h + 1) * DIFF_VDIM] = o[h * T:(h + 1) * T]


def _diff_attention_sample(q, cache_k, cache_v, k_new, v_new, lam_rows, subln_g):
    B, T, _ = q.shape
    P = cache_k.shape[1]
    H = N_DIFF_HEADS
    pc = min(P, 1024)
    assert H * T == 128 and P > 0 and P % pc == 0
    vec = pl.BlockSpec((1, DIFF_HEAD_DIM), lambda b: (0, 0))
    tok = pl.BlockSpec((1, T, DIFF_WIDTH), lambda b: (b, 0, 0))
    new = pl.BlockSpec((1, T * H, DIFF_VDIM), lambda b: (b, 0, 0))
    old = pl.BlockSpec((1, P * H, DIFF_VDIM), lambda b: (b, 0, 0))
    flat = lambda a: a.reshape(B, a.shape[1] * H, DIFF_VDIM)
    return pl.pallas_call(
        functools.partial(_diff_sample_kernel, past=P, pc=pc),
        out_shape=jax.ShapeDtypeStruct((B, T, DIFF_WIDTH), BF16),
        grid=(B,),
        in_specs=[vec, vec, vec, vec, pl.BlockSpec((1, DIFF_VDIM), lambda b: (0, 0)),
                  tok, old, old, new, new],
        out_specs=tok,
        compiler_params=_cparams(("parallel",)),
        name="diff_attn_sample",
    )(*lam_rows, subln_g, q, flat(cache_k), flat(cache_v), flat(k_new), flat(v_new))


def _gdn_kernel(*refs, c, cps, has_state):
    if has_state:
        (xq_ref, xk_ref, xv_ref, ab_ref, alog_ref, dtb_ref, z_ref, ng_ref, s0_ref, o_ref, s_ref) = refs
    else:
        (xq_ref, xk_ref, xv_ref, ab_ref, alog_ref, dtb_ref, z_ref, ng_ref, o_ref, s_ref) = refs
        s0_ref = None
    n = pl.program_id(1)
    H = N_GDN_HEADS

    @pl.when(n == 0)
    def _init():
        if has_state:
            s_ref[0] = s0_ref[0]
        else:
            s_ref[...] = jnp.zeros(s_ref.shape, F32)

    ab = ab_ref[0]
    a_in = ab[:, 0:H] + dtb_ref[...]
    softplus = jnp.maximum(a_in, 0.0) + jnp.log1p(jnp.exp(-jnp.abs(a_in)))
    g_col = -jnp.exp(alog_ref[...]) * softplus
    beta_col = _sigmoid(ab[:, H:2 * H])
    r = lax.broadcasted_iota(jnp.int32, (c, c), 0)
    s = lax.broadcasted_iota(jnp.int32, (c, c), 1)
    tri = r >= s
    strict = r > s
    tri_f = tri.astype(F32)
    eye_c = (r == s).astype(F32)
    eye_h = (lax.broadcasted_iota(jnp.int32, (H, H), 0) == lax.broadcasted_iota(jnp.int32, (H, H), 1)).astype(F32)

    def bf(a):
        return a.astype(BF16)

    def split(a):
        hi = a.astype(BF16)
        return hi, (a - hi.astype(F32)).astype(BF16)

    def dot3(a, b):
        (ah, al), (bh, bl) = a, b
        return (jnp.dot(al, bh, preferred_element_type=F32) + jnp.dot(ah, bl, preferred_element_type=F32)
                + jnp.dot(ah, bh, preferred_element_type=F32))

    blocks = []
    b_ = 1
    while b_ < c:
        blocks.append((((r // (2 * b_)) == (s // (2 * b_))) & ((r // b_) != (s // b_)) & strict).astype(F32))
        b_ *= 2

    rows = [slice(ci * c, (ci + 1) * c) for ci in range(cps)]
    lanes = [slice(hh * GDN_DK, (hh + 1) * GDN_DK) for hh in range(H)]
    gc_col = [jnp.dot(tri_f, g_col[rows[ci]], preferred_element_type=F32, precision=HI) for ci in range(cps)]
    gc_row = [lax.dot_general(eye_h, gc_col[ci], (((1,), (1,)), ((), ())), preferred_element_type=F32,
                              precision=HI) for ci in range(cps)]
    items = [(ci, hh) for ci in range(cps) for hh in range(H)]
    every = range(len(items))
    gcc = [gc_col[ci][:, hh:hh + 1] for ci, hh in items]
    bet = [beta_col[rows[ci], hh:hh + 1] for ci, hh in items]
    gcr = [gc_row[ci][hh:hh + 1, :] for ci, hh in items]
    g_last = [g[c - 1:c, :] for g in gcc]

    q = [xq_ref[0, rows[ci], lanes[hh]].astype(F32) for ci, hh in items]
    k = [xk_ref[0, rows[ci], lanes[hh]].astype(F32) for ci, hh in items]
    v = [xv_ref[0, rows[ci], lanes[hh]].astype(F32) for ci, hh in items]

    decay = [jnp.exp(jnp.where(tri, gcc[i] - gcr[i], -jnp.inf)) for i in every]
    kb = [k[i] * bet[i] for i in every]
    k16 = [bf(k[i]) for i in every]
    mmat = [jnp.where(strict, _nt_dot(bf(kb[i]), k16[i]) * decay[i], 0.0) for i in every]
    qk = [jnp.where(tri, _nt_dot(bf(q[i]), k16[i]) * decay[i], 0.0) for i in every]

    tinv = [eye_c - mmat[i] * blocks[0] for i in every]
    for lvl in range(1, len(blocks)):
        d16 = [bf(t) for t in tinv]
        x = [jnp.dot(bf(mmat[i] * blocks[lvl]), d16[i], preferred_element_type=F32) for i in every]
        tinv = [tinv[i] - jnp.dot(d16[i], bf(x[i]), preferred_element_type=F32) for i in every]
    m_s = [split(m) for m in mmat]
    t_s = [split(t) for t in tinv]
    res = [(eye_c - tinv[i]) - dot3(m_s[i], t_s[i]) for i in every]
    tinv = [tinv[i] + jnp.dot(t_s[i][0], bf(res[i]), preferred_element_type=F32) for i in every]

    rhs = [jnp.concatenate([v[i] * bet[i], kb[i] * jnp.exp(gcc[i])], axis=1) for i in every]
    sol = [dot3(split(tinv[i]), split(rhs[i])) for i in every]
    u = [x_[:, :GDN_DV] for x_ in sol]
    w16 = [bf(x_[:, GDN_DV:]) for x_ in sol]
    qg16 = [bf(q[i] * jnp.exp(gcc[i])) for i in every]
    qk16 = [bf(x_) for x_ in qk]
    kd16 = [bf(k[i] * jnp.exp(g_last[i] - gcc[i])) for i in every]
    e_last = [jnp.exp(g) for g in g_last]

    S = [s_ref[0, hh] for hh in range(H)]
    for ci in range(cps):
        of = ci * H
        S16 = [bf(x_) for x_ in S]
        v_new = [u[of + hh] - jnp.dot(w16[of + hh], S16[hh], preferred_element_type=F32) for hh in range(H)]
        v16 = [bf(x_) for x_ in v_new]
        o = [jnp.dot(qg16[of + hh], S16[hh], preferred_element_type=F32)
             + jnp.dot(qk16[of + hh], v16[hh], preferred_element_type=F32) for hh in range(H)]
        S = [S[hh] * e_last[of + hh] + lax.dot_general(kd16[of + hh], v16[hh], (((0,), (0,)), ((), ())),
                                                       preferred_element_type=F32) for hh in range(H)]
        for hh in range(H):
            ms = jnp.mean(o[hh] * o[hh], axis=-1, keepdims=True)
            y = o[hh] * lax.rsqrt(ms + NORM_EPS) * ng_ref[...] * z_ref[0, rows[ci], lanes[hh]].astype(F32)
            o_ref[0, rows[ci], lanes[hh]] = y.astype(o_ref.dtype)
    for hh in range(H):
        s_ref[0, hh] = S[hh]


def _gdn(q, k, v, gab, a_log, dt_bias, z_silu, norm_g, s0, *, c):
    B, T, _ = q.shape
    H = N_GDN_HEADS
    assert T % c == 0 and c % 8 == 0 and (c & (c - 1)) == 0
    cps = 2 if (T // c) % 2 == 0 else 1
    R = cps * c
    stream = pl.BlockSpec((1, R, GDN_WIDTH), lambda b, n: (b, n, 0))
    small = pl.BlockSpec((1, H), lambda b, n: (0, 0))
    state = pl.BlockSpec((1, H, GDN_DK, GDN_DV), lambda b, n: (b, 0, 0, 0))
    in_specs = [stream, stream, stream,
                pl.BlockSpec((1, R, 128), lambda b, n: (b, n, 0)), small, small,
                stream, pl.BlockSpec((1, GDN_DV), lambda b, n: (0, 0))]
    args = [q, k, v, gab, a_log, dt_bias, z_silu, norm_g]
    if s0 is not None:
        in_specs.append(state)
        args.append(s0)
    return pl.pallas_call(
        functools.partial(_gdn_kernel, c=c, cps=cps, has_state=s0 is not None),
        out_shape=(jax.ShapeDtypeStruct((B, T, GDN_WIDTH), BF16),
                   jax.ShapeDtypeStruct((B, H, GDN_DK, GDN_DV), F32)),
        grid=(B, T // R),
        in_specs=in_specs,
        out_specs=(stream, state),
        compiler_params=_cparams(("parallel", "arbitrary")),
        name="gdn",
    )(*args)


def _mix_kernel(oa_ref, ob_ref, wa_ref, wb_ref, sa_ref, sb_ref, o_ref):
    a = jnp.dot(oa_ref[...], wa_ref[0].astype(BF16), preferred_element_type=F32)
    b = jnp.dot(ob_ref[...], wb_ref[0].astype(BF16), preferred_element_type=F32)
    o_ref[...] = (sa_ref[...].astype(F32) * a + sb_ref[...].astype(F32) * b).astype(o_ref.dtype)


def _mix(o_a, o_b, w_pa, w_pb, gates, *, tm=1024, tn=512):
    M = o_a.shape[0]
    tm = min(tm, M)
    nb = D_MODEL // tn
    assert w_pa.shape[0] == w_pb.shape[0] == DEPTH == 1 and M % tm == 0
    return pl.pallas_call(
        _mix_kernel,
        out_shape=jax.ShapeDtypeStruct((M, D_MODEL), BF16),
        grid=(nb, M // tm),
        in_specs=[pl.BlockSpec((tm, DIFF_WIDTH), lambda j, i: (i, 0)),
                  pl.BlockSpec((tm, GDN_WIDTH), lambda j, i: (i, 0)),
                  pl.BlockSpec((1, DIFF_WIDTH, tn), lambda j, i: (0, 0, j)),
                  pl.BlockSpec((1, GDN_WIDTH, tn), lambda j, i: (0, 0, j)),
                  pl.BlockSpec((tm, tn), lambda j, i: (i, j)),
                  pl.BlockSpec((tm, tn), lambda j, i: (i, j + nb))],
        out_specs=pl.BlockSpec((tm, tn), lambda j, i: (i, j)),
        compiler_params=_cparams(("parallel", "parallel")),
        name="mix",
    )(o_a, o_b, w_pa, w_pb, gates, gates)


def _post_kernel(x_ref, mix_ref, wo_ref, g1_ref, b1_ref, wxq_ref, mk_ref, mv_ref, wxo_ref, g2_ref, b2_ref,
                 h2_ref, h2b_ref):
    nb, tm, D = x_ref.shape
    x = x_ref[...].reshape(nb * tm, D)
    mix = mix_ref[...].reshape(nb * tm, D)
    h1 = ALPHA * x + jnp.dot(mix, wo_ref[...], preferred_element_type=F32)
    h1 = _layer_norm(h1, g1_ref[...], b1_ref[...])
    qx = jnp.dot(h1.astype(BF16), wxq_ref[...], preferred_element_type=F32) * (XHEAD_DIM ** -0.5)
    qx = qx.astype(BF16)
    units = [(b, slice(b * tm, (b + 1) * tm), slice(hh * XHEAD_DIM, (hh + 1) * XHEAD_DIM))
             for b in range(nb) for hh in range(N_XHEADS)]
    s = [_nt_dot(qx[rows, sl], mk_ref[b, :, sl]) for b, rows, sl in units]
    p = [jnp.exp(x_ - jnp.max(x_, axis=-1, keepdims=True)) for x_ in s]
    p = [x_ / jnp.sum(x_, axis=-1, keepdims=True) for x_ in p]
    o = [jnp.dot(p[i].astype(BF16), mv_ref[b, :, sl], preferred_element_type=F32)
         for i, (b, rows, sl) in enumerate(units)]
    seqs = [jnp.concatenate(o[b * N_XHEADS:(b + 1) * N_XHEADS], axis=1) for b in range(nb)]
    ox = jnp.concatenate(seqs, axis=0).astype(BF16)
    h2 = ALPHA * h1 + jnp.dot(ox, wxo_ref[...], preferred_element_type=F32)
    h2 = _layer_norm(h2, g2_ref[...], b2_ref[...])
    h2_ref[...] = h2.reshape(nb, tm, D)
    h2b_ref[...] = h2.astype(BF16).reshape(nb, tm, D)


def _post(x, mix, w_o, ln1_g, ln1_b, w_xq, mem_k, mem_v, w_xo, ln2_g, ln2_b, *, tm=512):
    B, T, D = x.shape
    tm = min(tm, T)
    nb = min(B, max(1, 256 // tm))
    assert B % nb == 0 and T % tm == 0 and tm % 16 == 0
    const = lambda shape: pl.BlockSpec(shape, lambda b, i: (0, 0), pipeline_mode=pl.Buffered(1))
    rows = lambda: pl.BlockSpec((nb, tm, D), lambda b, i: (b, i, 0))
    mem = lambda: pl.BlockSpec((nb, N_MEM, XWIDTH), lambda b, i: (b, 0, 0))
    return pl.pallas_call(
        _post_kernel,
        out_shape=(jax.ShapeDtypeStruct((B, T, D), F32), jax.ShapeDtypeStruct((B, T, D), BF16)),
        grid=(B // nb, T // tm),
        in_specs=[rows(), rows(), const((D, D)), const((1, D)), const((1, D)), const((D, XWIDTH)),
                  mem(), mem(), const((XWIDTH, D)), const((1, D)), const((1, D))],
        out_specs=(rows(), rows()),
        compiler_params=_cparams(("parallel", "parallel")),
        name="post_attn",
    )(x, mix, w_o, ln1_g, ln1_b, w_xq, mem_k, mem_v, w_xo, ln2_g, ln2_b)


def _ffn_kernel(hb_ref, h_ref, w1_ref, w3_ref, w2_ref, g_ref, b_ref, y_ref, acc_ref):
    f = pl.program_id(1)

    @pl.when(f == 0)
    def _():
        acc_ref[...] = jnp.zeros(acc_ref.shape, F32)

    hb = hb_ref[...]
    a = jnp.dot(hb, w1_ref[...], preferred_element_type=F32)
    b = jnp.dot(hb, w3_ref[...], preferred_element_type=F32)
    act = (_silu(a) * b).astype(BF16)
    acc_ref[...] += jnp.dot(act, w2_ref[...], preferred_element_type=F32)

    @pl.when(f == pl.num_programs(1) - 1)
    def _():
        y_ref[...] = _layer_norm(ALPHA * h_ref[...] + acc_ref[...], g_ref[...], b_ref[...])


def _ffn(h2b, h2, w1, w3, w2, ln_g, ln_b, *, tm=512, tf=512):
    M, D = h2.shape
    tm = min(tm, M)
    assert M % tm == 0 and D_FF % tf == 0
    return pl.pallas_call(
        _ffn_kernel,
        out_shape=jax.ShapeDtypeStruct((M, D), F32),
        grid=(M // tm, D_FF // tf),
        in_specs=[pl.BlockSpec((tm, D), lambda i, f: (i, 0)),
                  pl.BlockSpec((tm, D), lambda i, f: (i, 0)),
                  pl.BlockSpec((D, tf), lambda i, f: (0, f)),
                  pl.BlockSpec((D, tf), lambda i, f: (0, f)),
                  pl.BlockSpec((tf, D), lambda i, f: (f, 0)),
                  pl.BlockSpec((1, D), lambda i, f: (0, 0)),
                  pl.BlockSpec((1, D), lambda i, f: (0, 0))],
        out_specs=pl.BlockSpec((tm, D), lambda i, f: (i, 0)),
        scratch_shapes=[pltpu.VMEM((tm, D), F32)],
        compiler_params=_cparams(("parallel", "arbitrary")),
        name="ffn",
    )(h2b, h2, w1, w3, w2, ln_g, ln_b)


def _encoder_layer(x, mem_k, mem_v, past, W):
    B, T, D = x.shape
    M = B * T
    w_in = W["w_in_t"]
    xb, dq = _proj(x.reshape(M, D), w_in, OFF_DQ, DIFF_WIDTH, (BF16,), scale=DIFF_HEAD_DIM ** -0.5, w_rows=True,
                   emit_x=True, name="proj_dq")
    in_proj = functools.partial(_proj, xb, w_in, w_rows=True)
    kv_dtypes = (F32, BF16) if past is None else (F32,)
    dk, *dkb = in_proj(OFF_DK, DIFF_WIDTH, kv_dtypes, rows_inner=True, name="proj_dk")
    dv, *dvb = in_proj(OFF_DV, DIFF_WIDTH, kv_dtypes, rows_inner=True, name="proj_dv")
    buf0 = jnp.zeros((B, CONV_W - 1, 3 * GDN_WIDTH), F32) if past is None else past[3]
    streams, tails = [], []
    for t, (norm, scale) in enumerate(((True, GDN_DK ** -0.5), (True, 1.0), (False, 1.0))):
        y, tail = _proj_conv(xb, w_in, OFF_GQKV + t * GDN_WIDTH, buf0, W["conv_w"], B=B, T=T, norm=norm,
                             scale=scale, name="proj_conv_" + "qkv"[t])
        streams.append(y.reshape(B, T, GDN_WIDTH))
        tails.append(tail[:, CONV_PAD - (CONV_W - 1):, :])
    (gz,) = in_proj(OFF_GZ, GDN_WIDTH, (BF16,), act="silu", name="proj_gz")
    (gab,) = in_proj(OFF_GAB, 128, (F32,), tn=128, name="proj_gab")
    (gates,) = in_proj(OFF_GATES, 2 * D_MODEL, (BF16,), act="sigmoid", name="proj_gates")

    dq = dq.reshape(B, T, DIFF_WIDTH)
    dk = dk.reshape(B, T, N_DIFF_HEADS, DIFF_VDIM)
    dv = dv.reshape(B, T, N_DIFF_HEADS, DIFF_VDIM)
    lam_rows = W["lam_rows"]
    if past is None:
        names = ("w_ff1", "w_ff3", "w_ff2", "w_o", "w_xq", "w_xo")
        o_a, copies = _diff_attention_prompt(dq, dkb[0].reshape(B, T, DIFF_WIDTH), dvb[0].reshape(B, T, DIFF_WIDTH),
                                             lam_rows, W["diff_subln_g"], cast=tuple(W["f32"][n] for n in names))
        W.update(zip(names, copies))
        s0 = None
        c = CHUNK
    else:
        cache_k, cache_v, s0, _ = past
        o_a = _diff_attention_sample(dq, cache_k, cache_v, dk, dv, lam_rows, W["diff_subln_g"])
        c = T
    o_b, s_new = _gdn(*streams, gab.reshape(B, T, 128), W["gdn_a_log"], W["gdn_dt_bias"],
                      gz.reshape(B, T, GDN_WIDTH), W["gdn_norm_g"], s0, c=c)
    new_buf = jnp.concatenate(tails, axis=-1)

    mix = _mix(o_a.reshape(M, DIFF_WIDTH), o_b.reshape(M, GDN_WIDTH), W["w_pa"], W["w_pb"], gates)
    h2, h2b = _post(x, mix.reshape(B, T, D), W["w_o"], W["ln1_g"], W["ln1_b"], W["w_xq"], mem_k, mem_v,
                    W["w_xo"], W["ln2_g"], W["ln2_b"])
    y = _ffn(h2b.reshape(M, D), h2.reshape(M, D), W["w_ff1"], W["w_ff3"], W["w_ff2"], W["ln3_g"], W["ln3_b"])
    return y.reshape(B, T, D), dk, dv, s_new, new_buf


def kernel(x_prompt, x_sample, mem_prompt, cache_diff_k, cache_diff_v, state_gdn, state_gdn_conv, cache_mem_k, cache_mem_v, w_in, conv_w, lam_q1, lam_k1, lam_q2, lam_k2, diff_subln_g, gdn_a_log, gdn_dt_bias, gdn_norm_g, w_pa, w_pb, w_o, ln1_g, ln1_b, w_xq, w_xk, w_xv, w_xo, ln2_g, ln2_b, w_ff1, w_ff3, w_ff2, ln3_g, ln3_b):
    l = 0
    W = {
        "w_in_t": jnp.swapaxes(w_in, 1, 2),
        "conv_w": conv_w[l],
        "lam_rows": tuple(v[l].reshape(1, DIFF_HEAD_DIM) for v in (lam_q1, lam_k1, lam_q2, lam_k2)),
        "diff_subln_g": diff_subln_g[l].reshape(1, DIFF_VDIM),
        "gdn_a_log": gdn_a_log[l].reshape(1, N_GDN_HEADS),
        "gdn_dt_bias": gdn_dt_bias[l].reshape(1, N_GDN_HEADS),
        "gdn_norm_g": gdn_norm_g[l].reshape(1, GDN_DV),
        "w_pa": w_pa, "w_pb": w_pb,
        "ln1_g": ln1_g[l].reshape(1, D_MODEL), "ln1_b": ln1_b[l].reshape(1, D_MODEL),
        "ln2_g": ln2_g[l].reshape(1, D_MODEL), "ln2_b": ln2_b[l].reshape(1, D_MODEL),
        "f32": {"w_ff1": w_ff1, "w_ff3": w_ff3, "w_ff2": w_ff2, "w_o": w_o, "w_xq": w_xq, "w_xo": w_xo},
        "ln3_g": ln3_g[l].reshape(1, D_MODEL), "ln3_b": ln3_b[l].reshape(1, D_MODEL),
    }
    Bp = x_prompt.shape[0]
    memb = mem_prompt.reshape(Bp * N_MEM, D_MODEL).astype(BF16)
    mem_k, mem_kb = _proj(memb, w_xk, 0, XWIDTH, (F32, BF16), tn=XWIDTH, name="proj_mem_k")
    mem_v, mem_vb = _proj(memb, w_xv, 0, XWIDTH, (F32, BF16), tn=XWIDTH, name="proj_mem_v")

    yp, pk, pv, ps, pc = _encoder_layer(x_prompt, mem_kb.reshape(Bp, N_MEM, XWIDTH),
                                        mem_vb.reshape(Bp, N_MEM, XWIDTH), None, W)
    Bs = x_sample.shape[0]
    past = (cache_diff_k[l], cache_diff_v[l], state_gdn[l], state_gdn_conv[l])
    ys, sk, sv, ss, sc = _encoder_layer(x_sample, cache_mem_k[l].reshape(Bs, N_MEM, XWIDTH).astype(BF16),
                                        cache_mem_v[l].reshape(Bs, N_MEM, XWIDTH).astype(BF16), past, W)
    st = lambda a: a[None]
    return (yp, ys, st(pk), st(pv), st(ps), st(pc),
            st(mem_k.reshape(Bp, N_MEM, N_XHEADS, XHEAD_DIM)), st(mem_v.reshape(Bp, N_MEM, N_XHEADS, XHEAD_DIM)),
            st(sk), st(sv), st(ss), st(sc))
```

```python
import functools
import math

import jax
import jax.numpy as jnp
from jax import lax
from jax.experimental import pallas as pl
from jax.experimental.pallas import tpu as pltpu

D_MODEL = 2048
CHUNK = 64
N_DIFF_HEADS = 8
DIFF_HEAD_DIM = 128
DIFF_VDIM = 2 * DIFF_HEAD_DIM
DIFF_WIDTH = N_DIFF_HEADS * DIFF_VDIM
N_GDN_HEADS = 16
GDN_DK = 128
GDN_DV = 128
GDN_WIDTH = N_GDN_HEADS * GDN_DK
CONV_W = 4
N_MEM = 256
N_XHEADS = 4
XHEAD_DIM = 128
XWIDTH = N_XHEADS * XHEAD_DIM
D_FF = 5632
DEPTH = 1
ALPHA = (2.0 * DEPTH) ** 0.25
LN_EPS = 1e-5
NORM_EPS = 1e-6
LAM_INIT = 0.8 - 0.6 * math.exp(-0.3 * 0)

OFF_DQ = 0
OFF_DK = 2048
OFF_DV = 4096
OFF_GQKV = 6144
OFF_GZ = 12288
OFF_GAB = 14336
OFF_GATES = 14368

VMEM_LIMIT = 56 * 1024 * 1024
BF16 = jnp.bfloat16
F32 = jnp.float32
HI = lax.Precision.HIGHEST


def _cparams(sem):
    return pltpu.CompilerParams(dimension_semantics=sem, vmem_limit_bytes=VMEM_LIMIT)


def _sigmoid(x):
    return 1.0 / (1.0 + jnp.exp(-x))


def _silu(x):
    return x * _sigmoid(x)


def _layer_norm(x, g, b):
    mu = jnp.mean(x, axis=-1, keepdims=True)
    xc = x - mu
    var = jnp.mean(xc * xc, axis=-1, keepdims=True)
    return xc * lax.rsqrt(var + LN_EPS) * g + b


def _proj_kernel(x_ref, w_ref, *o_refs, act, scale, w_rows, emit_x):
    w = w_ref[0].astype(BF16)
    x = x_ref[...].astype(BF16)
    if emit_x:
        o_refs[0][...] = x
        o_refs = o_refs[1:]
    acc = _nt_dot(x, w) if w_rows else jnp.dot(x, w, preferred_element_type=F32)
    if scale != 1.0:
        acc = acc * scale
    if act == "sigmoid":
        acc = _sigmoid(acc)
    elif act == "silu":
        acc = _silu(acc)
    for o in o_refs:
        o[...] = acc.astype(o.dtype)


def _weight_spec(w, K, tn, col_start, w_rows):
    assert w.shape[0] == DEPTH == 1
    if w_rows:
        return pl.BlockSpec((pl.Element(1), pl.Element(tn), pl.Element(K)),
                            lambda a, b: (0, pl.multiple_of(col_start(a, b), 8), 0))
    return pl.BlockSpec((1, K, tn), lambda a, b: (0, 0, col_start(a, b) // tn))


def _proj(x, w, col_off, n_cols, out_dtypes, *, act=None, scale=1.0, tm=1024, tn=1024, rows_inner=False,
          w_rows=False, emit_x=False, name="proj"):
    M, K = x.shape
    tm = min(tm, M)
    tn = min(tn, n_cols)
    assert M % tm == 0 and n_cols % tn == 0 and col_off % (8 if w_rows else tn) == 0
    if rows_inner:
        grid = (n_cols // tn, M // tm)
        ij = lambda a, b: (b, a)
    else:
        grid = (M // tm, n_cols // tn)
        ij = lambda a, b: (a, b)
    x_spec = pl.BlockSpec((tm, K), lambda a, b: (ij(a, b)[0], 0))
    out_shape = [jax.ShapeDtypeStruct((M, n_cols), dt) for dt in out_dtypes]
    out_specs = [pl.BlockSpec((tm, tn), lambda a, b: ij(a, b)) for _ in out_dtypes]
    if emit_x:
        assert not rows_inner
        out_shape.insert(0, jax.ShapeDtypeStruct((M, K), BF16))
        out_specs.insert(0, x_spec)
    outs = pl.pallas_call(
        functools.partial(_proj_kernel, act=act, scale=scale, w_rows=w_rows, emit_x=emit_x),
        out_shape=tuple(out_shape),
        grid=grid,
        in_specs=[x_spec, _weight_spec(w, K, tn, lambda a, b: col_off + ij(a, b)[1] * tn, w_rows)],
        out_specs=tuple(out_specs),
        compiler_params=_cparams(("parallel", "arbitrary") if emit_x else ("parallel", "parallel")),
        name=name,
    )(x, w)
    return outs


CONV_PAD = 8


def _proj_conv_kernel(x_ref, w_ref, hist_ref, cw_ref, y_ref, raw_ref, pad_ref, *, nb, T, sub, tr_max, norm, scale):
    tn = w_ref.shape[1]
    tr = min(T, tr_max) if nb == 1 else T
    units = [(s, r) for s in range(tn // sub) for r in range(T // tr)]
    w_slabs = [w_ref[0, s * sub:(s + 1) * sub, :].astype(BF16) for s in range(tn // sub)]

    def rows_of(r):
        return slice(r * tr, (r + 1) * tr) if nb == 1 else slice(None)

    def matmul(s, r, gate):
        rows = rows_of(r)
        if gate is None:
            lhs = x_ref[rows, :]
        else:
            n_rows = nb * tr
            first = x_ref[rows, 0:256] + jnp.concatenate([jnp.concatenate([gate] * 2, axis=1)] * (n_rows // 16), axis=0)
            lhs = jnp.concatenate([first, x_ref[rows, 256:]], axis=1)
        return _nt_dot(lhs, w_slabs[s])

    def epilogue(s, r, acc):
        cols = slice(s * sub, (s + 1) * sub)
        slot = s % 2
        r0 = r * tr
        a3 = acc.reshape(nb, tr, sub)
        pad_ref[slot, :, CONV_PAD + r0:CONV_PAD + r0 + tr, :] = a3
        if r == 0:
            pad_ref[slot, :, CONV_PAD - (CONV_W - 1):CONV_PAD, :] = hist_ref[:, :, cols]
        if r == T // tr - 1:
            raw_ref[:, :, cols] = a3[:, tr - CONV_PAD:, :]
        y = cw_ref[CONV_W - 1:CONV_W, cols] * a3
        for j in range(CONV_W - 1):
            lo = CONV_PAD - (CONV_W - 1) + j + r0
            y = y + cw_ref[j:j + 1, cols] * pad_ref[slot, :, lo:lo + tr, :]
        y = _silu(y)
        if norm:
            heads = []
            for hh in range(sub // GDN_DK):
                yh = y[:, :, hh * GDN_DK:(hh + 1) * GDN_DK]
                heads.append(yh * (lax.rsqrt(jnp.sum(yh * yh, axis=-1, keepdims=True) + NORM_EPS) * scale))
            y = jnp.concatenate(heads, axis=-1)
        y2 = y.reshape(nb * tr, sub)
        y_ref[rows_of(r), cols] = y2.astype(y_ref.dtype)
        bits = pltpu.bitcast(y2[nb * tr - 16:, 0:128], jnp.uint32)
        zero = lax.shift_right_logical(lax.shift_right_logical(bits, jnp.uint32(16)), jnp.uint32(16))
        return zero.astype(F32).astype(BF16)

    gates = [None, None]
    acc = matmul(*units[0], None)
    for i, u in enumerate(units):
        nxt = matmul(*units[i + 1], gates[i + 1]) if i + 1 < len(units) else None
        gates.append(epilogue(*u, acc))
        acc = nxt


def _proj_conv(x, w, col_off, hist, conv_w, *, B, T, norm, scale=1.0, tn=512, sub=512, tr=512, name="proj_conv"):
    M, K = x.shape
    n_cols = GDN_WIDTH
    nb = max(1, 256 // T) if T < 256 else 1
    nb = min(nb, B)
    tm = nb * T
    gq = OFF_GQKV
    assert B % nb == 0 and n_cols % tn == 0 and col_off % 8 == 0 and (col_off - gq) % tn == 0 and T >= CONV_PAD
    hb = (col_off - gq) // tn
    return pl.pallas_call(
        functools.partial(_proj_conv_kernel, nb=nb, T=T, sub=sub, tr_max=tr, norm=norm, scale=scale),
        out_shape=(jax.ShapeDtypeStruct((M, n_cols), BF16), jax.ShapeDtypeStruct((B, CONV_PAD, n_cols), F32)),
        grid=(B // nb, n_cols // tn),
        in_specs=[pl.BlockSpec((tm, K), lambda i, j: (i, 0)),
                  _weight_spec(w, K, tn, lambda i, j: col_off + j * tn, True),
                  pl.BlockSpec((nb, CONV_W - 1, tn), lambda i, j: (i, 0, j + hb)),
                  pl.BlockSpec((CONV_W, tn), lambda i, j: (0, j + hb))],
        out_specs=(pl.BlockSpec((tm, tn), lambda i, j: (i, j)),
                   pl.BlockSpec((nb, CONV_PAD, tn), lambda i, j: (i, 0, j))),
        scratch_shapes=[pltpu.VMEM((min(2, tn // sub), nb, T + CONV_PAD, sub), F32)],
        compiler_params=_cparams(("parallel", "parallel")),
        name=name,
    )(x, w, hist, conv_w)


def _lam_value(lq1, lk1, lq2, lk2):
    a = jnp.sum(lq1 * lk1, axis=-1, keepdims=True)
    b = jnp.sum(lq2 * lk2, axis=-1, keepdims=True)
    return jnp.exp(a) - jnp.exp(b) + LAM_INIT


def _subln(o, g):
    ms = jnp.mean(o * o, axis=-1, keepdims=True)
    return o * lax.rsqrt(ms + NORM_EPS) * g * (1.0 - LAM_INIT)


def _head_slope(h):
    e = (h + 1).astype(F32) * (-8.0 / N_DIFF_HEADS)
    return jnp.exp2(jnp.full((1, 1), e, F32))


def _nt_dot(a, b):
    return lax.dot_general(a, b, (((1,), (1,)), ((), ())), preferred_element_type=F32)


def _lane_tile(x, n):
    return x if n == 1 else jnp.concatenate([x] * n, axis=1)


def _diff_prompt_kernel(lq1_ref, lk1_ref, lq2_ref, lk2_ref, g_ref, base_ref, q_ref, k_ref, v_ref, *rest, tq, hps,
                        n_cast):
    cast_in, o_ref, cast_out = rest[:n_cast], rest[n_cast], rest[n_cast + 1:2 * n_cast + 1]
    m_ref, l_ref, acc_ref = rest[2 * n_cast + 1:]
    for src, dst in zip(cast_in, cast_out):
        dst[...] = src[0].astype(dst.dtype)
    LANES = 128
    half = tq // 2
    hg = pl.program_id(1)
    qi = pl.program_id(2)
    slope = [_head_slope(hg * hps + hh) for hh in range(hps)]
    lam = _lam_value(lq1_ref[...], lk1_ref[...], lq2_ref[...], lk2_ref[...])
    col_iota = lax.broadcasted_iota(jnp.int32, (1, tq), 1).astype(F32)
    col_bias = [sl * col_iota for sl in slope]
    maps = range(2)

    m_ref[...] = jnp.full(m_ref.shape, -jnp.inf, F32)
    l_ref[...] = jnp.zeros(l_ref.shape, F32)
    acc_ref[...] = jnp.zeros(acc_ref.shape, F32)

    def cols(hh, m):
        c0 = hh * DIFF_VDIM + m * DIFF_HEAD_DIM
        return slice(c0, c0 + DIFF_HEAD_DIM)

    def update(*jobs):
        todo = [(job, m) for job in jobs for m in maps]
        t = [_nt_dot(q_ref[0, rows, cols(hh, m)], k_ref[0, pl.ds(start, nk), cols(hh, m)]) + bias
             for (hh, rows, start, nk, bias, _), m in todo]
        st = [2 * job[0] + m for job, m in todo]
        m_old = [m_ref[st[i], job[1]] for i, (job, m) in enumerate(todo)]
        m_new = [jnp.maximum(m_old[i], jnp.max(t[i], axis=-1, keepdims=True) + job[5])
                 for i, (job, m) in enumerate(todo)]
        a = [jnp.exp(m_old[i] - m_new[i]) for i in range(len(todo))]
        p = [jnp.exp(t[i] - _lane_tile(m_new[i] - job[5], job[3] // LANES)) for i, (job, m) in enumerate(todo)]
        pv = [jnp.dot(p[i].astype(BF16), v_ref[0, pl.ds(start, nk), hh * DIFF_VDIM:(hh + 1) * DIFF_VDIM],
                      preferred_element_type=F32) for i, ((hh, _, start, nk, _, _), m) in enumerate(todo)]
        for i, ((hh, rows, start, nk, _, _), m) in enumerate(todo):
            psum = p[i][:, :LANES]
            for c in range(1, nk // LANES):
                psum = psum + p[i][:, c * LANES:(c + 1) * LANES]
            l_ref[st[i], rows] = a[i] * l_ref[st[i], rows] + psum
            acc_ref[st[i], rows] = _lane_tile(a[i], DIFF_VDIM // LANES) * acc_ref[st[i], rows] + pv[i]
            m_ref[st[i], rows] = m_new[i]

    every = slice(0, tq)

    def body(j, carry):
        start = pl.multiple_of(j * tq, tq)
        gap = ((qi - j) * tq).astype(F32)
        update(*[(hh, every, start, tq, col_bias[hh], -slope[hh] * gap) for hh in range(hps)])
        return carry

    lax.fori_loop(0, qi, body, 0)

    lo, hi = slice(0, half), slice(half, tq)
    start = pl.multiple_of(qi * tq, tq)
    zero = jnp.zeros((1, 1), F32)
    jobs = []
    for hh in range(hps):
        jobs.append((hh, lo, start, half, slope[hh] * base_ref[:, half:], -slope[hh] * float(half)))
        jobs.append((hh, hi, start, tq, slope[hh] * base_ref[...], zero))
    update(*jobs)

    for hh in range(hps):
        outs = []
        for m in maps:
            inv = 1.0 / jnp.sum(l_ref[2 * hh + m], axis=-1, keepdims=True)
            outs.append(acc_ref[2 * hh + m] * inv)
        o = outs[0] - lam * outs[1]
        o_ref[0, :, hh * DIFF_VDIM:(hh + 1) * DIFF_VDIM] = _subln(o, g_ref[...]).astype(o_ref.dtype)


def _diff_attention_prompt(q, k, v, lam_rows, subln_g, cast=(), *, tq=512, hps=2):
    B, T, _ = q.shape
    tq = min(tq, T)
    half = tq // 2
    assert T % tq == 0 and half % CHUNK == 0 and N_DIFF_HEADS % hps == 0
    ii = lax.broadcasted_iota(jnp.int32, (half, tq), 0)
    jj = lax.broadcasted_iota(jnp.int32, (half, tq), 1)
    jh = jj - half
    diag = jnp.where((jh // CHUNK) <= (ii // CHUNK), (half + ii - jnp.abs(ii - jh)).astype(F32), -jnp.inf)
    base = jnp.where(jj < half, jj.astype(F32), diag)
    vec = pl.BlockSpec((1, DIFF_HEAD_DIM), lambda b, h, i: (0, 0))
    wide = hps * DIFF_VDIM
    nh, nq = N_DIFF_HEADS // hps, T // tq
    n_steps = B * nh * nq
    step = lambda b, h, i: (b * nh + h) * nq + i
    cast_in, cast_out, cast_shapes = [], [], []
    for w in cast:
        _, R, C = w.shape
        per = next(p for p in (1, 2, 4, 8) if R % (n_steps // p) == 0 and (R // (n_steps // p)) % 16 == 0)
        rows = R // (n_steps // per)
        cast_in.append(pl.BlockSpec((1, rows, C), lambda b, h, i, per=per: (0, step(b, h, i) // per, 0)))
        cast_out.append(pl.BlockSpec((rows, C), lambda b, h, i, per=per: (step(b, h, i) // per, 0)))
        cast_shapes.append(jax.ShapeDtypeStruct((R, C), BF16))
    outs = pl.pallas_call(
        functools.partial(_diff_prompt_kernel, tq=tq, hps=hps, n_cast=len(cast)),
        out_shape=(jax.ShapeDtypeStruct((B, T, DIFF_WIDTH), BF16), *cast_shapes),
        grid=(B, nh, nq),
        in_specs=[vec, vec, vec, vec,
                  pl.BlockSpec((1, DIFF_VDIM), lambda b, h, i: (0, 0)),
                  pl.BlockSpec((half, tq), lambda b, h, i: (0, 0)),
                  pl.BlockSpec((1, tq, wide), lambda b, h, i: (b, i, h)),
                  pl.BlockSpec((1, T, wide), lambda b, h, i: (b, 0, h)),
                  pl.BlockSpec((1, T, wide), lambda b, h, i: (b, 0, h)), *cast_in],
        out_specs=(pl.BlockSpec((1, tq, wide), lambda b, h, i: (b, i, h)), *cast_out),
        scratch_shapes=[pltpu.VMEM((2 * hps, tq, 128), F32), pltpu.VMEM((2 * hps, tq, 128), F32),
                        pltpu.VMEM((2 * hps, tq, DIFF_VDIM), F32)],
        compiler_params=_cparams(("arbitrary", "arbitrary", "arbitrary")),
        name="diff_attn_prompt",
    )(*lam_rows, subln_g, base, q, k, v, *cast)
    return outs[0], tuple(outs[1:])


def _diff_sample_kernel(lq1_ref, lk1_ref, lq2_ref, lk2_ref, g_ref, q_ref, kc_ref, vc_ref, kn_ref, vn_ref,
                        o_ref, *, past, pc):
    H = N_DIFF_HEADS
    T = q_ref.shape[1]
    L = H * T
    rows = pc * H
    n_chunks = past // pc
    lam = _lam_value(lq1_ref[...], lk1_ref[...], lq2_ref[...], lk2_ref[...])
    maps = range(2)

    def map_cols(m):
        return slice(m * DIFF_HEAD_DIM, (m + 1) * DIFF_HEAD_DIM)

    qm = [jnp.concatenate([q_ref[0, :, h * DIFF_VDIM + m * DIFF_HEAD_DIM:h * DIFF_VDIM + (m + 1) * DIFF_HEAD_DIM]
                           for h in range(H)], axis=0) for m in maps]
    lane = lax.broadcasted_iota(jnp.int32, (1, L), 1)
    h_lane = lane // T
    t_lane = lane % T
    slope = jnp.exp2((h_lane + 1).astype(F32) * (-8.0 / H))
    base = slope * (lax.broadcasted_iota(jnp.int32, (rows, L), 0) // H).astype(F32)
    valid = lax.broadcasted_iota(jnp.int32, (H, L), 0) == h_lane

    def chunk_shift(c):
        return slope * ((c * pc - past) - t_lane).astype(F32)

    zq = jnp.zeros((L, DIFF_HEAD_DIM), BF16)
    q_both = jnp.concatenate([jnp.concatenate([qm[0], zq], axis=1), jnp.concatenate([zq, qm[1]], axis=1)], axis=0)

    def both_maps(keys):
        s2 = _nt_dot(keys, q_both)
        return [s2[:, m * L:(m + 1) * L] for m in maps]

    def chunk_scores(c):
        r0 = c * rows if isinstance(c, int) else pl.multiple_of(c * rows, rows)
        s2 = both_maps(kc_ref[0, pl.ds(r0, rows), :].astype(BF16))
        return [(s2[m] + base).reshape(pc, H, L) for m in maps]

    shared = chunk_scores(0) if n_chunks == 1 else None

    def stats_step(c, carry):
        shift = chunk_shift(c)
        s = chunk_scores(c) if shared is None else shared
        out = []
        for m in maps:
            m_old, l_old = carry[2 * m], carry[2 * m + 1]
            m_new = jnp.maximum(m_old, jnp.max(s[m], axis=0) + shift)
            l_new = l_old * jnp.exp(m_old - m_new) + jnp.sum(jnp.exp(s[m] - (m_new - shift)[None]), axis=0)
            out += [m_new, l_new]
        return tuple(out)

    init = (jnp.full((H, L), -jnp.inf, F32), jnp.zeros((H, L), F32)) * 2
    stats = stats_step(0, init) if n_chunks == 1 else lax.fori_loop(0, n_chunks, stats_step, init)

    kn = kn_ref[0].astype(BF16)
    vn = vn_ref[0].astype(BF16)
    tk = lax.broadcasted_iota(jnp.int32, (T * H, L), 0) // H
    allowed = ((past + tk) // CHUNK) <= ((past + t_lane) // CHUNK)
    bias_n = jnp.where(allowed, -slope * jnp.abs(tk - t_lane).astype(F32), -jnp.inf)
    s_n = [(s + bias_n).reshape(T, H, L) for s in both_maps(kn)]
    m_fin, coef = [], []
    for m in maps:
        m_run, l_run = stats[2 * m], stats[2 * m + 1]
        mf = jnp.maximum(m_run, jnp.max(s_n[m], axis=0))
        lf = l_run * jnp.exp(m_run - mf) + jnp.sum(jnp.exp(s_n[m] - mf[None]), axis=0)
        m_fin.append(mf)
        coef.append(jnp.where(valid, (1.0 if m == 0 else -lam) / lf, 0.0))

    def tn_dot(w, v):
        return lax.dot_general(w, v, (((0,), (0,)), ((), ())), preferred_element_type=F32)

    def out_step(c, acc):
        shift = chunk_shift(c)
        s = chunk_scores(c) if shared is None else shared
        r0 = c * rows if isinstance(c, int) else pl.multiple_of(c * rows, rows)
        vblk = vc_ref[0, pl.ds(r0, rows), :].astype(BF16)
        w = sum(jnp.exp(s[m] - (m_fin[m] - shift)[None]) * coef[m][None] for m in maps)
        return acc + tn_dot(w.reshape(rows, L).astype(BF16), vblk)

    acc0 = jnp.zeros((L, DIFF_VDIM), F32)
    acc = out_step(0, acc0) if n_chunks == 1 else lax.fori_loop(0, n_chunks, out_step, acc0)
    w_n = sum(jnp.exp(s_n[m] - m_fin[m][None]) * coef[m][None] for m in maps)
    acc = acc + tn_dot(w_n.reshape(T * H, L).astype(BF16), vn)
    o = _subln(acc, g_ref[...]).astype(o_ref.dtype)
    for h in range(H):
        o_ref[0, :, h * DIFF_VDIM:(h + 1) * DIFF_VDIM] = o[h * T:(h + 1) * T]


def _diff_attention_sample(q, cache_k, cache_v, k_new, v_new, lam_rows, subln_g):
    B, T, _ = q.shape
    P = cache_k.shape[1]
    H = N_DIFF_HEADS
    pc = min(P, 1024)
    assert H * T == 128 and P > 0 and P % pc == 0
    vec = pl.BlockSpec((1, DIFF_HEAD_DIM), lambda b: (0, 0))
    tok = pl.BlockSpec((1, T, DIFF_WIDTH), lambda b: (b, 0, 0))
    new = pl.BlockSpec((1, T * H, DIFF_VDIM), lambda b: (b, 0, 0))
    old = pl.BlockSpec((1, P * H, DIFF_VDIM), lambda b: (b, 0, 0))
    flat = lambda a: a.reshape(B, a.shape[1] * H, DIFF_VDIM)
    return pl.pallas_call(
        functools.partial(_diff_sample_kernel, past=P, pc=pc),
        out_shape=jax.ShapeDtypeStruct((B, T, DIFF_WIDTH), BF16),
        grid=(B,),
        in_specs=[vec, vec, vec, vec, pl.BlockSpec((1, DIFF_VDIM), lambda b: (0, 0)),
                  tok, old, old, new, new],
        out_specs=tok,
        compiler_params=_cparams(("parallel",)),
        name="diff_attn_sample",
    )(*lam_rows, subln_g, q, flat(cache_k), flat(cache_v), flat(k_new), flat(v_new))


def _gdn_kernel(*refs, c, cps, has_state):
    if has_state:
        (xq_ref, xk_ref, xv_ref, ab_ref, alog_ref, dtb_ref, z_ref, ng_ref, s0_ref, o_ref, s_ref) = refs
    else:
        (xq_ref, xk_ref, xv_ref, ab_ref, alog_ref, dtb_ref, z_ref, ng_ref, o_ref, s_ref) = refs
        s0_ref = None
    n = pl.program_id(1)
    H = N_GDN_HEADS

    @pl.when(n == 0)
    def _init():
        if has_state:
            s_ref[0] = s0_ref[0]
        else:
            s_ref[...] = jnp.zeros(s_ref.shape, F32)

    ab = ab_ref[0]
    a_in = ab[:, 0:H] + dtb_ref[...]
    softplus = jnp.maximum(a_in, 0.0) + jnp.log1p(jnp.exp(-jnp.abs(a_in)))
    g_col = -jnp.exp(alog_ref[...]) * softplus
    beta_col = _sigmoid(ab[:, H:2 * H])
    r = lax.broadcasted_iota(jnp.int32, (c, c), 0)
    s = lax.broadcasted_iota(jnp.int32, (c, c), 1)
    tri = r >= s
    strict = r > s
    tri_f = tri.astype(F32)
    eye_c = (r == s).astype(F32)
    eye_h = (lax.broadcasted_iota(jnp.int32, (H, H), 0) == lax.broadcasted_iota(jnp.int32, (H, H), 1)).astype(F32)

    def bf(a):
        return a.astype(BF16)

    def split(a):
        hi = a.astype(BF16)
        return hi, (a - hi.astype(F32)).astype(BF16)

    def dot3(a, b):
        (ah, al), (bh, bl) = a, b
        return (jnp.dot(al, bh, preferred_element_type=F32) + jnp.dot(ah, bl, preferred_element_type=F32)
                + jnp.dot(ah, bh, preferred_element_type=F32))

    blocks = []
    b_ = 1
    while b_ < c:
        blocks.append((((r // (2 * b_)) == (s // (2 * b_))) & ((r // b_) != (s // b_)) & strict).astype(F32))
        b_ *= 2

    rows = [slice(ci * c, (ci + 1) * c) for ci in range(cps)]
    lanes = [slice(hh * GDN_DK, (hh + 1) * GDN_DK) for hh in range(H)]
    gc_col = [jnp.dot(tri_f, g_col[rows[ci]], preferred_element_type=F32, precision=HI) for ci in range(cps)]
    gc_row = [lax.dot_general(eye_h, gc_col[ci], (((1,), (1,)), ((), ())), preferred_element_type=F32,
                              precision=HI) for ci in range(cps)]
    items = [(ci, hh) for ci in range(cps) for hh in range(H)]
    every = range(len(items))
    gcc = [gc_col[ci][:, hh:hh + 1] for ci, hh in items]
    bet = [beta_col[rows[ci], hh:hh + 1] for ci, hh in items]
    gcr = [gc_row[ci][hh:hh + 1, :] for ci, hh in items]
    g_last = [g[c - 1:c, :] for g in gcc]

    q = [xq_ref[0, rows[ci], lanes[hh]].astype(F32) for ci, hh in items]
    k = [xk_ref[0, rows[ci], lanes[hh]].astype(F32) for ci, hh in items]
    v = [xv_ref[0, rows[ci], lanes[hh]].astype(F32) for ci, hh in items]

    decay = [jnp.exp(jnp.where(tri, gcc[i] - gcr[i], -jnp.inf)) for i in every]
    kb = [k[i] * bet[i] for i in every]
    k16 = [bf(k[i]) for i in every]
    mmat = [jnp.where(strict, _nt_dot(bf(kb[i]), k16[i]) * decay[i], 0.0) for i in every]
    qk = [jnp.where(tri, _nt_dot(bf(q[i]), k16[i]) * decay[i], 0.0) for i in every]

    tinv = [eye_c - mmat[i] * blocks[0] for i in every]
    for lvl in range(1, len(blocks)):
        d16 = [bf(t) for t in tinv]
        x = [jnp.dot(bf(mmat[i] * blocks[lvl]), d16[i], preferred_element_type=F32) for i in every]
        tinv = [tinv[i] - jnp.dot(d16[i], bf(x[i]), preferred_element_type=F32) for i in every]
    m_s = [split(m) for m in mmat]
    t_s = [split(t) for t in tinv]
    res = [(eye_c - tinv[i]) - dot3(m_s[i], t_s[i]) for i in every]
    tinv = [tinv[i] + jnp.dot(t_s[i][0], bf(res[i]), preferred_element_type=F32) for i in every]

    rhs = [jnp.concatenate([v[i] * bet[i], kb[i] * jnp.exp(gcc[i])], axis=1) for i in every]
    sol = [dot3(split(tinv[i]), split(rhs[i])) for i in every]
    u = [x_[:, :GDN_DV] for x_ in sol]
    w16 = [bf(x_[:, GDN_DV:]) for x_ in sol]
    qg16 = [bf(q[i] * jnp.exp(gcc[i])) for i in every]
    qk16 = [bf(x_) for x_ in qk]
    kd16 = [bf(k[i] * jnp.exp(g_last[i] - gcc[i])) for i in every]
    e_last = [jnp.exp(g) for g in g_last]

    S = [s_ref[0, hh] for hh in range(H)]
    for ci in range(cps):
        of = ci * H
        S16 = [bf(x_) for x_ in S]
        v_new = [u[of + hh] - jnp.dot(w16[of + hh], S16[hh], preferred_element_type=F32) for hh in range(H)]
        v16 = [bf(x_) for x_ in v_new]
        o = [jnp.dot(qg16[of + hh], S16[hh], preferred_element_type=F32)
             + jnp.dot(qk16[of + hh], v16[hh], preferred_element_type=F32) for hh in range(H)]
        S = [S[hh] * e_last[of + hh] + lax.dot_general(kd16[of + hh], v16[hh], (((0,), (0,)), ((), ())),
                                                       preferred_element_type=F32) for hh in range(H)]
        for hh in range(H):
            ms = jnp.mean(o[hh] * o[hh], axis=-1, keepdims=True)
            y = o[hh] * lax.rsqrt(ms + NORM_EPS) * ng_ref[...] * z_ref[0, rows[ci], lanes[hh]].astype(F32)
            o_ref[0, rows[ci], lanes[hh]] = y.astype(o_ref.dtype)
    for hh in range(H):
        s_ref[0, hh] = S[hh]


def _gdn(q, k, v, gab, a_log, dt_bias, z_silu, norm_g, s0, *, c):
    B, T, _ = q.shape
    H = N_GDN_HEADS
    assert T % c == 0 and c % 8 == 0 and (c & (c - 1)) == 0
    cps = 2 if (T // c) % 2 == 0 else 1
    R = cps * c
    stream = pl.BlockSpec((1, R, GDN_WIDTH), lambda b, n: (b, n, 0))
    small = pl.BlockSpec((1, H), lambda b, n: (0, 0))
    state = pl.BlockSpec((1, H, GDN_DK, GDN_DV), lambda b, n: (b, 0, 0, 0))
    in_specs = [stream, stream, stream,
                pl.BlockSpec((1, R, 128), lambda b, n: (b, n, 0)), small, small,
                stream, pl.BlockSpec((1, GDN_DV), lambda b, n: (0, 0))]
    args = [q, k, v, gab, a_log, dt_bias, z_silu, norm_g]
    if s0 is not None:
        in_specs.append(state)
        args.append(s0)
    return pl.pallas_call(
        functools.partial(_gdn_kernel, c=c, cps=cps, has_state=s0 is not None),
        out_shape=(jax.ShapeDtypeStruct((B, T, GDN_WIDTH), BF16),
                   jax.ShapeDtypeStruct((B, H, GDN_DK, GDN_DV), F32)),
        grid=(B, T // R),
        in_specs=in_specs,
        out_specs=(stream, state),
        compiler_params=_cparams(("parallel", "arbitrary")),
        name="gdn",
    )(*args)


def _mix_kernel(oa_ref, ob_ref, wa_ref, wb_ref, sa_ref, sb_ref, o_ref):
    a = jnp.dot(oa_ref[...], wa_ref[0].astype(BF16), preferred_element_type=F32)
    b = jnp.dot(ob_ref[...], wb_ref[0].astype(BF16), preferred_element_type=F32)
    o_ref[...] = (sa_ref[...].astype(F32) * a + sb_ref[...].astype(F32) * b).astype(o_ref.dtype)


def _mix(o_a, o_b, w_pa, w_pb, gates, *, tm=1024, tn=512):
    M = o_a.shape[0]
    tm = min(tm, M)
    nb = D_MODEL // tn
    assert w_pa.shape[0] == w_pb.shape[0] == DEPTH == 1 and M % tm == 0
    return pl.pallas_call(
        _mix_kernel,
        out_shape=jax.ShapeDtypeStruct((M, D_MODEL), BF16),
        grid=(nb, M // tm),
        in_specs=[pl.BlockSpec((tm, DIFF_WIDTH), lambda j, i: (i, 0)),
                  pl.BlockSpec((tm, GDN_WIDTH), lambda j, i: (i, 0)),
                  pl.BlockSpec((1, DIFF_WIDTH, tn), lambda j, i: (0, 0, j)),
                  pl.BlockSpec((1, GDN_WIDTH, tn), lambda j, i: (0, 0, j)),
                  pl.BlockSpec((tm, tn), lambda j, i: (i, j)),
                  pl.BlockSpec((tm, tn), lambda j, i: (i, j + nb))],
        out_specs=pl.BlockSpec((tm, tn), lambda j, i: (i, j)),
        compiler_params=_cparams(("parallel", "parallel")),
        name="mix",
    )(o_a, o_b, w_pa, w_pb, gates, gates)


def _post_kernel(x_ref, mix_ref, wo_ref, g1_ref, b1_ref, wxq_ref, mk_ref, mv_ref, wxo_ref, g2_ref, b2_ref,
                 h2_ref, h2b_ref):
    nb, tm, D = x_ref.shape
    x = x_ref[...].reshape(nb * tm, D)
    mix = mix_ref[...].reshape(nb * tm, D)
    h1 = ALPHA * x + jnp.dot(mix, wo_ref[...], preferred_element_type=F32)
    h1 = _layer_norm(h1, g1_ref[...], b1_ref[...])
    qx = jnp.dot(h1.astype(BF16), wxq_ref[...], preferred_element_type=F32) * (XHEAD_DIM ** -0.5)
    qx = qx.astype(BF16)
    units = [(b, slice(b * tm, (b + 1) * tm), slice(hh * XHEAD_DIM, (hh + 1) * XHEAD_DIM))
             for b in range(nb) for hh in range(N_XHEADS)]
    s = [_nt_dot(qx[rows, sl], mk_ref[b, :, sl]) for b, rows, sl in units]
    p = [jnp.exp(x_ - jnp.max(x_, axis=-1, keepdims=True)) for x_ in s]
    p = [x_ / jnp.sum(x_, axis=-1, keepdims=True) for x_ in p]
    o = [jnp.dot(p[i].astype(BF16), mv_ref[b, :, sl], preferred_element_type=F32)
         for i, (b, rows, sl) in enumerate(units)]
    seqs = [jnp.concatenate(o[b * N_XHEADS:(b + 1) * N_XHEADS], axis=1) for b in range(nb)]
    ox = jnp.concatenate(seqs, axis=0).astype(BF16)
    h2 = ALPHA * h1 + jnp.dot(ox, wxo_ref[...], preferred_element_type=F32)
    h2 = _layer_norm(h2, g2_ref[...], b2_ref[...])
    h2_ref[...] = h2.reshape(nb, tm, D)
    h2b_ref[...] = h2.astype(BF16).reshape(nb, tm, D)


def _post(x, mix, w_o, ln1_g, ln1_b, w_xq, mem_k, mem_v, w_xo, ln2_g, ln2_b, *, tm=512):
    B, T, D = x.shape
    tm = min(tm, T)
    nb = min(B, max(1, 256 // tm))
    assert B % nb == 0 and T % tm == 0 and tm % 16 == 0
    const = lambda shape: pl.BlockSpec(shape, lambda b, i: (0, 0), pipeline_mode=pl.Buffered(1))
    rows = lambda: pl.BlockSpec((nb, tm, D), lambda b, i: (b, i, 0))
    mem = lambda: pl.BlockSpec((nb, N_MEM, XWIDTH), lambda b, i: (b, 0, 0))
    return pl.pallas_call(
        _post_kernel,
        out_shape=(jax.ShapeDtypeStruct((B, T, D), F32), jax.ShapeDtypeStruct((B, T, D), BF16)),
        grid=(B // nb, T // tm),
        in_specs=[rows(), rows(), const((D, D)), const((1, D)), const((1, D)), const((D, XWIDTH)),
                  mem(), mem(), const((XWIDTH, D)), const((1, D)), const((1, D))],
        out_specs=(rows(), rows()),
        compiler_params=_cparams(("parallel", "parallel")),
        name="post_attn",
    )(x, mix, w_o, ln1_g, ln1_b, w_xq, mem_k, mem_v, w_xo, ln2_g, ln2_b)


def _ffn_kernel(hb_ref, h_ref, w1_ref, w3_ref, w2_ref, g_ref, b_ref, y_ref, acc_ref):
    f = pl.program_id(1)

    @pl.when(f == 0)
    def _():
        acc_ref[...] = jnp.zeros(acc_ref.shape, F32)

    hb = hb_ref[...]
    a = jnp.dot(hb, w1_ref[...], preferred_element_type=F32)
    b = jnp.dot(hb, w3_ref[...], preferred_element_type=F32)
    act = (_silu(a) * b).astype(BF16)
    acc_ref[...] += jnp.dot(act, w2_ref[...], preferred_element_type=F32)

    @pl.when(f == pl.num_programs(1) - 1)
    def _():
        y_ref[...] = _layer_norm(ALPHA * h_ref[...] + acc_ref[...], g_ref[...], b_ref[...])


def _ffn(h2b, h2, w1, w3, w2, ln_g, ln_b, *, tm=512, tf=512):
    M, D = h2.shape
    tm = min(tm, M)
    assert M % tm == 0 and D_FF % tf == 0
    return pl.pallas_call(
        _ffn_kernel,
        out_shape=jax.ShapeDtypeStruct((M, D), F32),
        grid=(M // tm, D_FF // tf),
        in_specs=[pl.BlockSpec((tm, D), lambda i, f: (i, 0)),
                  pl.BlockSpec((tm, D), lambda i, f: (i, 0)),
                  pl.BlockSpec((D, tf), lambda i, f: (0, f)),
                  pl.BlockSpec((D, tf), lambda i, f: (0, f)),
                  pl.BlockSpec((tf, D), lambda i, f: (f, 0)),
                  pl.BlockSpec((1, D), lambda i, f: (0, 0)),
                  pl.BlockSpec((1, D), lambda i, f: (0, 0))],
        out_specs=pl.BlockSpec((tm, D), lambda i, f: (i, 0)),
        scratch_shapes=[pltpu.VMEM((tm, D), F32)],
        compiler_params=_cparams(("parallel", "arbitrary")),
        name="ffn",
    )(h2b, h2, w1, w3, w2, ln_g, ln_b)


def _encoder_layer(x, mem_k, mem_v, past, W):
    B, T, D = x.shape
    M = B * T
    w_in = W["w_in_t"]
    xb, gab = _proj(x.reshape(M, D), w_in, OFF_GAB, 128, (F32,), tn=128, w_rows=True, emit_x=True, name="proj_gab")
    in_proj = functools.partial(_proj, xb, w_in, w_rows=True)
    (dq,) = in_proj(OFF_DQ, DIFF_WIDTH, (BF16,), scale=DIFF_HEAD_DIM ** -0.5, name="proj_dq")
    kv_dtypes = (F32, BF16) if past is None else (F32,)
    dk, *dkb = in_proj(OFF_DK, DIFF_WIDTH, kv_dtypes, rows_inner=True, name="proj_dk")
    dv, *dvb = in_proj(OFF_DV, DIFF_WIDTH, kv_dtypes, rows_inner=True, name="proj_dv")
    buf0 = jnp.zeros((B, CONV_W - 1, 3 * GDN_WIDTH), F32) if past is None else past[3]
    streams, tails = [], []
    for t, (norm, scale) in enumerate(((True, GDN_DK ** -0.5), (True, 1.0), (False, 1.0))):
        y, tail = _proj_conv(xb, w_in, OFF_GQKV + t * GDN_WIDTH, buf0, W["conv_w"], B=B, T=T, norm=norm,
                             scale=scale, name="proj_conv_" + "qkv"[t])
        streams.append(y.reshape(B, T, GDN_WIDTH))
        tails.append(tail[:, CONV_PAD - (CONV_W - 1):, :])
    (gz,) = in_proj(OFF_GZ, GDN_WIDTH, (BF16,), act="silu", name="proj_gz")
    (gates,) = in_proj(OFF_GATES, 2 * D_MODEL, (BF16,), act="sigmoid", name="proj_gates")

    dq = dq.reshape(B, T, DIFF_WIDTH)
    dk = dk.reshape(B, T, N_DIFF_HEADS, DIFF_VDIM)
    dv = dv.reshape(B, T, N_DIFF_HEADS, DIFF_VDIM)
    lam_rows = W["lam_rows"]
    if past is None:
        names = ("w_ff1", "w_ff3", "w_ff2", "w_o", "w_xq", "w_xo")
        o_a, copies = _diff_attention_prompt(dq, dkb[0].reshape(B, T, DIFF_WIDTH), dvb[0].reshape(B, T, DIFF_WIDTH),
                                             lam_rows, W["diff_subln_g"], cast=tuple(W["f32"][n] for n in names))
        W.update(zip(names, copies))
        s0 = None
        c = CHUNK
    else:
        cache_k, cache_v, s0, _ = past
        o_a = _diff_attention_sample(dq, cache_k, cache_v, dk, dv, lam_rows, W["diff_subln_g"])
        c = T
    o_b, s_new = _gdn(*streams, gab.reshape(B, T, 128), W["gdn_a_log"], W["gdn_dt_bias"],
                      gz.reshape(B, T, GDN_WIDTH), W["gdn_norm_g"], s0, c=c)
    new_buf = jnp.concatenate(tails, axis=-1)

    mix = _mix(o_a.reshape(M, DIFF_WIDTH), o_b.reshape(M, GDN_WIDTH), W["w_pa"], W["w_pb"], gates)
    h2, h2b = _post(x, mix.reshape(B, T, D), W["w_o"], W["ln1_g"], W["ln1_b"], W["w_xq"], mem_k, mem_v,
                    W["w_xo"], W["ln2_g"], W["ln2_b"])
    y = _ffn(h2b.reshape(M, D), h2.reshape(M, D), W["w_ff1"], W["w_ff3"], W["w_ff2"], W["ln3_g"], W["ln3_b"])
    return y.reshape(B, T, D), dk, dv, s_new, new_buf


def kernel(x_prompt, x_sample, mem_prompt, cache_diff_k, cache_diff_v, state_gdn, state_gdn_conv, cache_mem_k, cache_mem_v, w_in, conv_w, lam_q1, lam_k1, lam_q2, lam_k2, diff_subln_g, gdn_a_log, gdn_dt_bias, gdn_norm_g, w_pa, w_pb, w_o, ln1_g, ln1_b, w_xq, w_xk, w_xv, w_xo, ln2_g, ln2_b, w_ff1, w_ff3, w_ff2, ln3_g, ln3_b):
    l = 0
    W = {
        "w_in_t": jnp.swapaxes(w_in, 1, 2),
        "conv_w": conv_w[l],
        "lam_rows": tuple(v[l].reshape(1, DIFF_HEAD_DIM) for v in (lam_q1, lam_k1, lam_q2, lam_k2)),
        "diff_subln_g": diff_subln_g[l].reshape(1, DIFF_VDIM),
        "gdn_a_log": gdn_a_log[l].reshape(1, N_GDN_HEADS),
        "gdn_dt_bias": gdn_dt_bias[l].reshape(1, N_GDN_HEADS),
        "gdn_norm_g": gdn_norm_g[l].reshape(1, GDN_DV),
        "w_pa": w_pa, "w_pb": w_pb,
        "ln1_g": ln1_g[l].reshape(1, D_MODEL), "ln1_b": ln1_b[l].reshape(1, D_MODEL),
        "ln2_g": ln2_g[l].reshape(1, D_MODEL), "ln2_b": ln2_b[l].reshape(1, D_MODEL),
        "f32": {"w_ff1": w_ff1, "w_ff3": w_ff3, "w_ff2": w_ff2, "w_o": w_o, "w_xq": w_xq, "w_xo": w_xo},
        "ln3_g": ln3_g[l].reshape(1, D_MODEL), "ln3_b": ln3_b[l].reshape(1, D_MODEL),
    }
    Bp = x_prompt.shape[0]
    memb = mem_prompt.reshape(Bp * N_MEM, D_MODEL).astype(BF16)
    mem_k, mem_kb = _proj(memb, w_xk, 0, XWIDTH, (F32, BF16), tn=XWIDTH, name="proj_mem_k")
    mem_v, mem_vb = _proj(memb, w_xv, 0, XWIDTH, (F32, BF16), tn=XWIDTH, name="proj_mem_v")

    yp, pk, pv, ps, pc = _encoder_layer(x_prompt, mem_kb.reshape(Bp, N_MEM, XWIDTH),
                                        mem_vb.reshape(Bp, N_MEM, XWIDTH), None, W)
    Bs = x_sample.shape[0]
    past = (cache_diff_k[l], cache_diff_v[l], state_gdn[l], state_gdn_conv[l])
    ys, sk, sv, ss, sc = _encoder_layer(x_sample, cache_mem_k[l].reshape(Bs, N_MEM, XWIDTH).astype(BF16),
                                        cache_mem_v[l].reshape(Bs, N_MEM, XWIDTH).astype(BF16), past, W)
    st = lambda a: a[None]
    return (yp, ys, st(pk), st(pv), st(ps), st(pc),
            st(mem_k.reshape(Bp, N_MEM, N_XHEADS, XHEAD_DIM)), st(mem_v.reshape(Bp, N_MEM, N_XHEADS, XHEAD_DIM)),
            st(sk), st(sv), st(ss), st(sc))
```
